```python
import math
import jax, jax.numpy as jnp
from jax import lax
import numpy as np

D_MODEL = 1024
BATCH = 8
SEQ = 8192
DEPTH = 4

N_MIXERS = 2
N_HEADS = 16
HEAD_DIM = D_MODEL // N_HEADS
BLOCK_Q = 128
POOL_WINDOWS = (2, 4, 8, 16)
N_POOL_GROUPS = len(POOL_WINDOWS)
POOL_GROUP = D_MODEL // N_POOL_GROUPS
D_FF = 2816
PLE_DIM = 256
EPS = 1e-6

kernel_name = "hybrid_stickbreak_pool_macaron"


def rms_norm(x, g):
    xf = x.astype(jnp.float32)
    y = xf * lax.rsqrt(jnp.mean(xf * xf, axis=-1, keepdims=True) + EPS)
    return (y * g.astype(jnp.float32)).astype(x.dtype)


def swiglu(h, w_gu, w_down):
    gate, up = jnp.split(h @ w_gu, 2, axis=-1)
    return (jax.nn.silu(gate) * up) @ w_down


def stick_breaking_attention(h, w_qkv, q_gain, k_gain, w_o):
    B, S, _ = h.shape
    q, k, v = jnp.split(h @ w_qkv, 3, axis=-1)
    q = rms_norm(q.reshape(B, S, N_HEADS, HEAD_DIM), q_gain)
    k = rms_norm(k.reshape(B, S, N_HEADS, HEAD_DIM), k_gain)
    v = v.reshape(B, S, N_HEADS, HEAD_DIM)
    q, k, v = (t.transpose(0, 2, 1, 3) for t in (q, k, v))
    scale = 1.0 / math.sqrt(HEAD_DIM)
    outs = []
    for blk in range(S // BLOCK_Q):
        t0 = blk * BLOCK_Q
        t1 = t0 + BLOCK_Q
        qb = q[:, :, t0:t1]
        kb = k[:, :, :t1]
        vb = v[:, :, :t1]
        z = jnp.einsum('bhqd,bhkd->bhqk', qb, kb).astype(jnp.float32) * scale
        qpos = t0 + jnp.arange(BLOCK_Q)
        kpos = jnp.arange(t1)
        causal = kpos[None, :] < qpos[:, None]
        log_stay = jnp.where(causal, jax.nn.log_sigmoid(-z), 0.0)
        log_after = lax.cumsum(log_stay, axis=3, reverse=True) - log_stay
        weights = jnp.where(causal, jnp.exp(jax.nn.log_sigmoid(z) + log_after), 0.0)
        outs.append(jnp.einsum('bhqk,bhkd->bhqd', weights.astype(vb.dtype), vb))
    o = jnp.concatenate(outs, axis=2)
    o = o.transpose(0, 2, 1, 3).reshape(B, S, D_MODEL)
    return o @ w_o


def multiscale_pool_mixer(h, w_in, w_grp, scale):
    B, S, _ = h.shape
    u = (h @ w_in).reshape(B, S, N_POOL_GROUPS, POOL_GROUP)
    uf = u.astype(jnp.float32)
    c = jnp.cumsum(uf, axis=1)
    pos = jnp.arange(S)
    outs = []
    for gi, w in enumerate(POOL_WINDOWS):
        cg = c[:, :, gi]
        cpad = jnp.pad(cg, ((0, 0), (w, 0), (0, 0)))
        wsum = cpad[:, w:] - cpad[:, :S]
        cnt = jnp.minimum(pos + 1, w).astype(jnp.float32)
        outs.append(wsum / cnt[None, :, None] - uf[:, :, gi])
    pooled = jnp.stack(outs, axis=2).astype(h.dtype)
    y = jnp.einsum('bsgc,gcd->bsgd', pooled, w_grp).reshape(B, S, D_MODEL)
    return y * scale


def _fwd_setup_inputs(seed: int = 0) -> dict:
    key = jax.random.key(seed)
    ks = iter(jax.random.split(key, 32))
    n_a = (DEPTH + 1) // 2
    n_b = DEPTH // 2
    f32 = jnp.float32

    def w(shape, fan_in):
        return jax.random.normal(next(ks), shape, f32) * fan_in ** -0.5

    def gain(shape):
        return 1.0 + 0.05 * jax.random.normal(next(ks), shape, f32)

    return {
        "x": jax.random.normal(next(ks), (BATCH, SEQ, D_MODEL), f32),
        "p": jax.random.normal(next(ks), (DEPTH, BATCH, SEQ, PLE_DIM), f32),
        "norm_ffn1": gain((DEPTH, D_MODEL)),
        "w_ffn1_gu": w((DEPTH, D_MODEL, 2 * D_FF), D_MODEL),
        "w_ffn1_down": w((DEPTH, D_FF, D_MODEL), D_FF),
        "norm_mix": gain((DEPTH, D_MODEL)),
        "w_qkv": w((n_a, D_MODEL, 3 * D_MODEL), D_MODEL),
        "q_norm": gain((n_a, HEAD_DIM)),
        "k_norm": gain((n_a, HEAD_DIM)),
        "w_o": w((n_a, D_MODEL, D_MODEL), D_MODEL),
        "w_pool_in": w((n_b, D_MODEL, D_MODEL), D_MODEL),
        "w_pool_grp": w((n_b, N_POOL_GROUPS, POOL_GROUP, POOL_GROUP), POOL_GROUP),
        "pool_scale": gain((n_b, D_MODEL)),
        "norm_ffn2": gain((DEPTH, D_MODEL)),
        "w_ffn2_gu": w((DEPTH, D_MODEL, 2 * D_FF), D_MODEL),
        "w_ffn2_down": w((DEPTH, D_FF, D_MODEL), D_FF),
        "norm_ple": gain((DEPTH, D_MODEL)),
        "w_ple_gate": w((DEPTH, D_MODEL, D_MODEL), D_MODEL),
        "w_ple_proj": w((DEPTH, PLE_DIM, D_MODEL), PLE_DIM),
    }


def _fwd_reference(x, p, norm_ffn1, w_ffn1_gu, w_ffn1_down, norm_mix, w_qkv, q_norm,
              k_norm, w_o, w_pool_in, w_pool_grp, pool_scale, norm_ffn2,
              w_ffn2_gu, w_ffn2_down, norm_ple, w_ple_gate, w_ple_proj):
    for i in range(DEPTH):
        x = x + 0.5 * swiglu(rms_norm(x, norm_ffn1[i]), w_ffn1_gu[i], w_ffn1_down[i])
        h = rms_norm(x, norm_mix[i])
        j = i // N_MIXERS
        if i % N_MIXERS == 0:
            mix = stick_breaking_attention(h, w_qkv[j], q_norm[j], k_norm[j], w_o[j])
        else:
            mix = multiscale_pool_mixer(h, w_pool_in[j], w_pool_grp[j], pool_scale[j])
        x = x + mix
        x = x + 0.5 * swiglu(rms_norm(x, norm_ffn2[i]), w_ffn2_gu[i], w_ffn2_down[i])
        gate = jax.nn.sigmoid(rms_norm(x, norm_ple[i]) @ w_ple_gate[i])
        x = x + gate * (p[i] @ w_ple_proj[i])
    return x


import jax as _jax
import jax.numpy as _jnp

TWIN_FORMAT = 'train_step'
FWD_PARAMS = ['x', 'p', 'norm_ffn1', 'w_ffn1_gu', 'w_ffn1_down', 'norm_mix', 'w_qkv', 'q_norm', 'k_norm', 'w_o', 'w_pool_in', 'w_pool_grp', 'pool_scale', 'norm_ffn2', 'w_ffn2_gu', 'w_ffn2_down', 'norm_ple', 'w_ple_gate', 'w_ple_proj']
TWIN_WEIGHTS = ['norm_ffn1', 'w_ffn1_gu', 'w_ffn1_down', 'norm_mix', 'w_qkv', 'q_norm', 'k_norm', 'w_o', 'w_pool_in', 'w_pool_grp', 'pool_scale', 'norm_ffn2', 'w_ffn2_gu', 'w_ffn2_down', 'norm_ple', 'w_ple_gate', 'w_ple_proj']
TWIN_DIFF_INPUT = 'x'
TWIN_INPUTS = ['x', 'p', 'norm_ffn1', 'w_ffn1_gu', 'w_ffn1_down', 'norm_mix', 'w_qkv', 'q_norm', 'k_norm', 'w_o', 'w_pool_in', 'w_pool_grp', 'pool_scale', 'norm_ffn2', 'w_ffn2_gu', 'w_ffn2_down', 'norm_ple', 'w_ple_gate', 'w_ple_proj', 'loss_target', 'm_norm_ffn1', 'm_w_ffn1_gu', 'm_w_ffn1_down', 'm_norm_mix', 'm_w_qkv', 'm_q_norm', 'm_k_norm', 'm_w_o', 'm_w_pool_in', 'm_w_pool_grp', 'm_pool_scale', 'm_norm_ffn2', 'm_w_ffn2_gu', 'm_w_ffn2_down', 'm_norm_ple', 'm_w_ple_gate', 'm_w_ple_proj', 'v_norm_ffn1', 'v_w_ffn1_gu', 'v_w_ffn1_down', 'v_norm_mix', 'v_w_qkv', 'v_q_norm', 'v_k_norm', 'v_w_o', 'v_w_pool_in', 'v_w_pool_grp', 'v_pool_scale', 'v_norm_ffn2', 'v_w_ffn2_gu', 'v_w_ffn2_down', 'v_norm_ple', 'v_w_ple_gate', 'v_w_ple_proj']
TWIN_OUTPUTS = ['loss', 'grad_x', 'grad_norm_ffn1', 'grad_w_ffn1_gu', 'grad_w_ffn1_down', 'grad_norm_mix', 'grad_w_qkv', 'grad_q_norm', 'grad_k_norm', 'grad_w_o', 'grad_w_pool_in', 'grad_w_pool_grp', 'grad_pool_scale', 'grad_norm_ffn2', 'grad_w_ffn2_gu', 'grad_w_ffn2_down', 'grad_norm_ple', 'grad_w_ple_gate', 'grad_w_ple_proj', 'delta_norm_ffn1', 'delta_w_ffn1_gu', 'delta_w_ffn1_down', 'delta_norm_mix', 'delta_w_qkv', 'delta_q_norm', 'delta_k_norm', 'delta_w_o', 'delta_w_pool_in', 'delta_w_pool_grp', 'delta_pool_scale', 'delta_norm_ffn2', 'delta_w_ffn2_gu', 'delta_w_ffn2_down', 'delta_norm_ple', 'delta_w_ple_gate', 'delta_w_ple_proj', 'new_m_norm_ffn1', 'new_m_w_ffn1_gu', 'new_m_w_ffn1_down', 'new_m_norm_mix', 'new_m_w_qkv', 'new_m_q_norm', 'new_m_k_norm', 'new_m_w_o', 'new_m_w_pool_in', 'new_m_w_pool_grp', 'new_m_pool_scale', 'new_m_norm_ffn2', 'new_m_w_ffn2_gu', 'new_m_w_ffn2_down', 'new_m_norm_ple', 'new_m_w_ple_gate', 'new_m_w_ple_proj', 'new_v_norm_ffn1', 'new_v_w_ffn1_gu', 'new_v_w_ffn1_down', 'new_v_norm_mix', 'new_v_w_qkv', 'new_v_q_norm', 'new_v_k_norm', 'new_v_w_o', 'new_v_w_pool_in', 'new_v_w_pool_grp', 'new_v_pool_scale', 'new_v_norm_ffn2', 'new_v_w_ffn2_gu', 'new_v_w_ffn2_down', 'new_v_norm_ple', 'new_v_w_ple_gate', 'new_v_w_ple_proj']
TWIN_LEAF_KINDS = {'loss': 'loss', 'grad_x': 'grad_x', 'grad_norm_ffn1': 'grad_w', 'grad_w_ffn1_gu': 'grad_w', 'grad_w_ffn1_down': 'grad_w', 'grad_norm_mix': 'grad_w', 'grad_w_qkv': 'grad_w', 'grad_q_norm': 'grad_w', 'grad_k_norm': 'grad_w', 'grad_w_o': 'grad_w', 'grad_w_pool_in': 'grad_w', 'grad_w_pool_grp': 'grad_w', 'grad_pool_scale': 'grad_w', 'grad_norm_ffn2': 'grad_w', 'grad_w_ffn2_gu': 'grad_w', 'grad_w_ffn2_down': 'grad_w', 'grad_norm_ple': 'grad_w', 'grad_w_ple_gate': 'grad_w', 'grad_w_ple_proj': 'grad_w', 'delta_norm_ffn1': 'delta_w', 'delta_w_ffn1_gu': 'delta_w', 'delta_w_ffn1_down': 'delta_w', 'delta_norm_mix': 'delta_w', 'delta_w_qkv': 'delta_w', 'delta_q_norm': 'delta_w', 'delta_k_norm': 'delta_w', 'delta_w_o': 'delta_w', 'delta_w_pool_in': 'delta_w', 'delta_w_pool_grp': 'delta_w', 'delta_pool_scale': 'delta_w', 'delta_norm_ffn2': 'delta_w', 'delta_w_ffn2_gu': 'delta_w', 'delta_w_ffn2_down': 'delta_w', 'delta_norm_ple': 'delta_w', 'delta_w_ple_gate': 'delta_w', 'delta_w_ple_proj': 'delta_w', 'new_m_norm_ffn1': 'new_m', 'new_m_w_ffn1_gu': 'new_m', 'new_m_w_ffn1_down': 'new_m', 'new_m_norm_mix': 'new_m', 'new_m_w_qkv': 'new_m', 'new_m_q_norm': 'new_m', 'new_m_k_norm': 'new_m', 'new_m_w_o': 'new_m', 'new_m_w_pool_in': 'new_m', 'new_m_w_pool_grp': 'new_m', 'new_m_pool_scale': 'new_m', 'new_m_norm_ffn2': 'new_m', 'new_m_w_ffn2_gu': 'new_m', 'new_m_w_ffn2_down': 'new_m', 'new_m_norm_ple': 'new_m', 'new_m_w_ple_gate': 'new_m', 'new_m_w_ple_proj': 'new_m', 'new_v_norm_ffn1': 'new_v', 'new_v_w_ffn1_gu': 'new_v', 'new_v_w_ffn1_down': 'new_v', 'new_v_norm_mix': 'new_v', 'new_v_w_qkv': 'new_v', 'new_v_q_norm': 'new_v', 'new_v_k_norm': 'new_v', 'new_v_w_o': 'new_v', 'new_v_w_pool_in': 'new_v', 'new_v_w_pool_grp': 'new_v', 'new_v_pool_scale': 'new_v', 'new_v_norm_ffn2': 'new_v', 'new_v_w_ffn2_gu': 'new_v', 'new_v_w_ffn2_down': 'new_v', 'new_v_norm_ple': 'new_v', 'new_v_w_ple_gate': 'new_v', 'new_v_w_ple_proj': 'new_v'}


def _forward(args):
    return _fwd_reference(*[args[k] for k in FWD_PARAMS])


def _output_shape():
    def fwd():
        inp = _fwd_setup_inputs(0)
        return _fwd_reference(*[inp[k] for k in FWD_PARAMS])
    out = _jax.eval_shape(fwd)
    return out.shape, out.dtype

N_MICROBATCH = 1
ADAM_LR = 0.001
ADAM_B1 = 0.9
ADAM_B2 = 0.999
ADAM_EPS = 1e-08
ADAM_WD = 0.01
ADAM_STEP = 10
PER_EXAMPLE_BATCH_AXIS = {'x': 0, 'p': 1, 'loss_target': 0}
SHARED_INPUTS = []
_WEIGHT_DTYPES = {'norm_ffn1': _jnp.float32, 'w_ffn1_gu': _jnp.float32, 'w_ffn1_down': _jnp.float32, 'norm_mix': _jnp.float32, 'w_qkv': _jnp.float32, 'q_norm': _jnp.float32, 'k_norm': _jnp.float32, 'w_o': _jnp.float32, 'w_pool_in': _jnp.float32, 'w_pool_grp': _jnp.float32, 'pool_scale': _jnp.float32, 'norm_ffn2': _jnp.float32, 'w_ffn2_gu': _jnp.float32, 'w_ffn2_down': _jnp.float32, 'norm_ple': _jnp.float32, 'w_ple_gate': _jnp.float32, 'w_ple_proj': _jnp.float32}
MOMENT_SCALE = {'norm_ffn1': 1.242944e+01, 'w_ffn1_gu': 2.380552e-01, 'w_ffn1_down': 4.456005e-01, 'norm_mix': 3.891898e+01, 'w_qkv': 7.231962e-01, 'q_norm': 6.361330e+01, 'k_norm': 6.313214e+01, 'w_o': 1.244732e+00, 'w_pool_in': 3.450010e+00, 'w_pool_grp': 4.265475e+00, 'pool_scale': 4.945442e+01, 'norm_ffn2': 1.236952e+01, 'w_ffn2_gu': 2.112709e-01, 'w_ffn2_down': 4.154747e-01, 'norm_ple': 1.805422e+00, 'w_ple_gate': 1.940041e-01, 'w_ple_proj': 1.021899e+00}


def _to_microbatches(a, axis):
    t = _jnp.moveaxis(a, axis, 0)
    t = t.reshape((N_MICROBATCH, t.shape[0] // N_MICROBATCH) + t.shape[1:])
    return _jnp.moveaxis(t, 1, axis + 1)


def setup_inputs(seed: int = 0) -> dict:
    inp = _fwd_setup_inputs(seed)
    key = _jax.random.fold_in(_jax.random.key(seed), 7919)
    shape, _ = _output_shape()
    out = dict(inp)
    out["loss_target"] = _jax.random.normal(_jax.random.fold_in(key, 0), shape, _jnp.float32)
    for i, name in enumerate(TWIN_WEIGHTS):
        w = inp[name].astype(_jnp.float32)
        if MOMENT_SCALE is None:
            s = _jnp.sqrt(_jnp.mean(_jnp.square(w)) + 1e-30)
        else:
            s = MOMENT_SCALE[name]
        km, kv = _jax.random.split(_jax.random.fold_in(key, i + 1))
        out[name] = w
        out["m_" + name] = s * _jax.random.normal(km, w.shape, _jnp.float32)
        out["v_" + name] = (s * s) * _jax.random.uniform(kv, w.shape, _jnp.float32, 0.5, 1.5)
    if N_MICROBATCH > 1:
        for name, axis in PER_EXAMPLE_BATCH_AXIS.items():
            out[name] = _to_microbatches(out[name], axis)
    return {'x': out['x'], 'p': out['p'], 'norm_ffn1': out['norm_ffn1'], 'w_ffn1_gu': out['w_ffn1_gu'], 'w_ffn1_down': out['w_ffn1_down'], 'norm_mix': out['norm_mix'], 'w_qkv': out['w_qkv'], 'q_norm': out['q_norm'], 'k_norm': out['k_norm'], 'w_o': out['w_o'], 'w_pool_in': out['w_pool_in'], 'w_pool_grp': out['w_pool_grp'], 'pool_scale': out['pool_scale'], 'norm_ffn2': out['norm_ffn2'], 'w_ffn2_gu': out['w_ffn2_gu'], 'w_ffn2_down': out['w_ffn2_down'], 'norm_ple': out['norm_ple'], 'w_ple_gate': out['w_ple_gate'], 'w_ple_proj': out['w_ple_proj'], 'loss_target': out['loss_target'], 'm_norm_ffn1': out['m_norm_ffn1'], 'm_w_ffn1_gu': out['m_w_ffn1_gu'], 'm_w_ffn1_down': out['m_w_ffn1_down'], 'm_norm_mix': out['m_norm_mix'], 'm_w_qkv': out['m_w_qkv'], 'm_q_norm': out['m_q_norm'], 'm_k_norm': out['m_k_norm'], 'm_w_o': out['m_w_o'], 'm_w_pool_in': out['m_w_pool_in'], 'm_w_pool_grp': out['m_w_pool_grp'], 'm_pool_scale': out['m_pool_scale'], 'm_norm_ffn2': out['m_norm_ffn2'], 'm_w_ffn2_gu': out['m_w_ffn2_gu'], 'm_w_ffn2_down': out['m_w_ffn2_down'], 'm_norm_ple': out['m_norm_ple'], 'm_w_ple_gate': out['m_w_ple_gate'], 'm_w_ple_proj': out['m_w_ple_proj'], 'v_norm_ffn1': out['v_norm_ffn1'], 'v_w_ffn1_gu': out['v_w_ffn1_gu'], 'v_w_ffn1_down': out['v_w_ffn1_down'], 'v_norm_mix': out['v_norm_mix'], 'v_w_qkv': out['v_w_qkv'], 'v_q_norm': out['v_q_norm'], 'v_k_norm': out['v_k_norm'], 'v_w_o': out['v_w_o'], 'v_w_pool_in': out['v_w_pool_in'], 'v_w_pool_grp': out['v_w_pool_grp'], 'v_pool_scale': out['v_pool_scale'], 'v_norm_ffn2': out['v_norm_ffn2'], 'v_w_ffn2_gu': out['v_w_ffn2_gu'], 'v_w_ffn2_down': out['v_w_ffn2_down'], 'v_norm_ple': out['v_norm_ple'], 'v_w_ple_gate': out['v_w_ple_gate'], 'v_w_ple_proj': out['v_w_ple_proj']}


def _loss(weights, diff, rest, loss_target):
    with _jax.named_scope("forward"):
        args = {**rest, TWIN_DIFF_INPUT: diff, **{k: w.astype(_WEIGHT_DTYPES[k]) for k, w in weights.items()}}
        y = _forward(args)
    with _jax.named_scope("loss_head"):
        err = _jnp.square(y.astype(_jnp.float32) - loss_target)
        return 0.5 * _jnp.sum(_jnp.mean(err, axis=-1)) if err.ndim else 0.5 * err


def _adamw(w, g, m, v):
    m = ADAM_B1 * m + (1.0 - ADAM_B1) * g
    v = ADAM_B2 * v + (1.0 - ADAM_B2) * _jnp.square(g)
    m_hat = m / (1.0 - ADAM_B1 ** ADAM_STEP)
    v_hat = v / (1.0 - ADAM_B2 ** ADAM_STEP)
    delta = -ADAM_LR * (m_hat / (_jnp.sqrt(v_hat) + ADAM_EPS) + ADAM_WD * w)
    return delta, m, v


def reference(x, p, norm_ffn1, w_ffn1_gu, w_ffn1_down, norm_mix, w_qkv, q_norm, k_norm, w_o, w_pool_in, w_pool_grp, pool_scale, norm_ffn2, w_ffn2_gu, w_ffn2_down, norm_ple, w_ple_gate, w_ple_proj, loss_target, m_norm_ffn1, m_w_ffn1_gu, m_w_ffn1_down, m_norm_mix, m_w_qkv, m_q_norm, m_k_norm, m_w_o, m_w_pool_in, m_w_pool_grp, m_pool_scale, m_norm_ffn2, m_w_ffn2_gu, m_w_ffn2_down, m_norm_ple, m_w_ple_gate, m_w_ple_proj, v_norm_ffn1, v_w_ffn1_gu, v_w_ffn1_down, v_norm_mix, v_w_qkv, v_q_norm, v_k_norm, v_w_o, v_w_pool_in, v_w_pool_grp, v_pool_scale, v_norm_ffn2, v_w_ffn2_gu, v_w_ffn2_down, v_norm_ple, v_w_ple_gate, v_w_ple_proj):
    given = dict(x=x, p=p, norm_ffn1=norm_ffn1, w_ffn1_gu=w_ffn1_gu, w_ffn1_down=w_ffn1_down, norm_mix=norm_mix, w_qkv=w_qkv, q_norm=q_norm, k_norm=k_norm, w_o=w_o, w_pool_in=w_pool_in, w_pool_grp=w_pool_grp, pool_scale=pool_scale, norm_ffn2=norm_ffn2, w_ffn2_gu=w_ffn2_gu, w_ffn2_down=w_ffn2_down, norm_ple=norm_ple, w_ple_gate=w_ple_gate, w_ple_proj=w_ple_proj, loss_target=loss_target, m_norm_ffn1=m_norm_ffn1, m_w_ffn1_gu=m_w_ffn1_gu, m_w_ffn1_down=m_w_ffn1_down, m_norm_mix=m_norm_mix, m_w_qkv=m_w_qkv, m_q_norm=m_q_norm, m_k_norm=m_k_norm, m_w_o=m_w_o, m_w_pool_in=m_w_pool_in, m_w_pool_grp=m_w_pool_grp, m_pool_scale=m_pool_scale, m_norm_ffn2=m_norm_ffn2, m_w_ffn2_gu=m_w_ffn2_gu, m_w_ffn2_down=m_w_ffn2_down, m_norm_ple=m_norm_ple, m_w_ple_gate=m_w_ple_gate, m_w_ple_proj=m_w_ple_proj, v_norm_ffn1=v_norm_ffn1, v_w_ffn1_gu=v_w_ffn1_gu, v_w_ffn1_down=v_w_ffn1_down, v_norm_mix=v_norm_mix, v_w_qkv=v_w_qkv, v_q_norm=v_q_norm, v_k_norm=v_k_norm, v_w_o=v_w_o, v_w_pool_in=v_w_pool_in, v_w_pool_grp=v_w_pool_grp, v_pool_scale=v_pool_scale, v_norm_ffn2=v_norm_ffn2, v_w_ffn2_gu=v_w_ffn2_gu, v_w_ffn2_down=v_w_ffn2_down, v_norm_ple=v_norm_ple, v_w_ple_gate=v_w_ple_gate, v_w_ple_proj=v_w_ple_proj)
    weights = {n: given[n] for n in TWIN_WEIGHTS}
    shared = {n: given[n] for n in SHARED_INPUTS}
    per_example = {n: given[n] for n in ['x', 'p']}
    grad_fn = _jax.value_and_grad(_loss, argnums=(0, 1))

    def one_microbatch(ex, loss_target):
        ex = dict(ex)
        diff = ex.pop(TWIN_DIFF_INPUT)
        return grad_fn(weights, diff, {**shared, **ex}, loss_target)

    if N_MICROBATCH == 1:
        loss, (grad_w, grad_x) = one_microbatch(per_example, given["loss_target"])
    else:
        def body(carry, xs):
            loss_sum, grad_sum = carry
            l_k, (gw_k, gx_k) = one_microbatch(xs[0], xs[1])
            with _jax.named_scope("update"):
                return (loss_sum + l_k, _jax.tree.map(_jnp.add, grad_sum, gw_k)), gx_k

        init = (_jnp.zeros((), _jnp.float32), _jax.tree.map(_jnp.zeros_like, weights))
        (loss, grad_w), grad_x = _jax.lax.scan(body, init, (per_example, given["loss_target"]))
    with _jax.named_scope("update"):
        delta_w, new_m, new_v = {}, {}, {}
        for n in TWIN_WEIGHTS:
            delta_w[n], new_m[n], new_v[n] = _adamw(weights[n], grad_w[n], given["m_" + n], given["v_" + n])
    return (loss, grad_x, *[grad_w[n] for n in TWIN_WEIGHTS], *[delta_w[n] for n in TWIN_WEIGHTS],
            *[new_m[n] for n in TWIN_WEIGHTS], *[new_v[n] for n in TWIN_WEIGHTS])
```

```python
import math

import jax
import jax.numpy as jnp
from jax import lax
from jax.experimental import pallas as pl
from jax.experimental.pallas import tpu as pltpu

F32 = jnp.float32
BF16 = jnp.bfloat16

N_DEV = 8
DEPTH = 4
D_MODEL = 1024
N_UNITS = D_MODEL // 128
HEAD_DIM = 64
HEADS_PER_UNIT = 128 // HEAD_DIM
POOL_WINDOWS = (2, 4, 8, 16)
POOL_GROUP = 256
POOL_HALO = 128
EPS = 1e-6
ATTN_SCALE = 1.0 / math.sqrt(HEAD_DIM)
ATTN_BLOCK = 128
LOG_ZERO = -104.0

ADAM_LR = 0.001
ADAM_B1 = 0.9
ADAM_B2 = 0.999
ADAM_EPS = 1e-08
ADAM_WD = 0.01
ADAM_STEP = 10
ADAMW_BLOCK_ELEMS = 128 * 1024

VMEM_LIMIT = 56 * 1024 * 1024
MESH = pl.DeviceIdType.MESH

NT_DIMS = (((1,), (1,)), ((), ()))
TN_DIMS = (((0,), (0,)), ((), ()))


def _params(*sem):
    return pltpu.CompilerParams(dimension_semantics=sem, vmem_limit_bytes=VMEM_LIMIT)


def _row_tile(s):
    return min(512, s)


def _dot(a, b):
    return jnp.dot(a, b, preferred_element_type=F32)


def _dot_nt(a, b):
    return lax.dot_general(a, b, NT_DIMS, preferred_element_type=F32)


def _dot_tn(a, b):
    return lax.dot_general(a, b, TN_DIMS, preferred_element_type=F32)


def _dot_f32(a, b):
    return jnp.dot(a, b, precision=lax.Precision.HIGHEST, preferred_element_type=F32)


def _rms(x, g):
    r = lax.rsqrt(jnp.mean(x * x, axis=-1, keepdims=True) + EPS)
    return x * r * g


def _rms_bwd(dy, x, g):
    r = lax.rsqrt(jnp.mean(x * x, axis=-1, keepdims=True) + EPS)
    xh = x * r
    dg = jnp.sum(dy * xh, axis=0, keepdims=True)
    dxh = dy * g
    dx = r * (dxh - xh * jnp.mean(dxh * xh, axis=-1, keepdims=True))
    return dx, dg


def _cast_bf16(w):
    l, r, c = w.shape

    def body(w_ref, o_ref):
        o_ref[...] = w_ref[...].astype(BF16)

    return pl.pallas_call(
        body, grid=(l,), name="cast_bf16",
        in_specs=[pl.BlockSpec((1, r, c), lambda i: (i, 0, 0))],
        out_specs=pl.BlockSpec((1, r, c), lambda i: (i, 0, 0)),
        out_shape=jax.ShapeDtypeStruct(w.shape, BF16),
        compiler_params=_params("arbitrary"),
    )(w)


def _ffn_up(x, g, wgu):
    s, d = x.shape
    fs = wgu.shape[-1]
    tm = _row_tile(s)

    def body(x_ref, g_ref, wg_ref, wu_ref, h_ref, gu_ref, act_ref):
        @pl.when(pl.program_id(1) == 0)
        def _():
            h_ref[...] = _rms(x_ref[...], g_ref[...]).astype(BF16)

        h = h_ref[...]
        gate = _dot(h, wg_ref[0])
        up = _dot(h, wu_ref[0])
        gu_ref[0, 0] = gate.astype(BF16)
        gu_ref[1, 0] = up.astype(BF16)
        act_ref[0] = (gate * jax.nn.sigmoid(gate) * up).astype(BF16)

    return pl.pallas_call(
        body, grid=(s // tm, 4), name="ffn_up",
        in_specs=[
            pl.BlockSpec((tm, d), lambda i, j: (i, 0)),
            pl.BlockSpec((1, d), lambda i, j: (0, 0)),
            pl.BlockSpec((1, d, fs), lambda i, j: (j, 0, 0)),
            pl.BlockSpec((1, d, fs), lambda i, j: (j + 4, 0, 0)),
        ],
        out_specs=[
            pl.BlockSpec((tm, d), lambda i, j: (i, 0)),
            pl.BlockSpec((2, 1, tm, fs), lambda i, j: (0, j, i, 0)),
            pl.BlockSpec((1, tm, fs), lambda i, j: (j, i, 0)),
        ],
        out_shape=[
            jax.ShapeDtypeStruct((s, d), BF16),
            jax.ShapeDtypeStruct((2, 4, s, fs), BF16),
            jax.ShapeDtypeStruct((4, s, fs), BF16),
        ],
        compiler_params=_params("arbitrary", "arbitrary"),
    )(x, g, wgu, wgu)


def _ffn_down(act, wd, x):
    _, s, fs = act.shape
    d = wd.shape[-1]
    tm = _row_tile(s)

    def body(a_ref, w_ref, x_ref, o_ref):
        k = pl.program_id(1)

        @pl.when(k == 0)
        def _():
            o_ref[...] = jnp.zeros_like(o_ref)

        o_ref[...] += _dot(a_ref[0], w_ref[...])

        @pl.when(k == 3)
        def _():
            o_ref[...] = x_ref[...] + 0.5 * o_ref[...]

    return pl.pallas_call(
        body, grid=(s // tm, 4), name="ffn_down",
        in_specs=[
            pl.BlockSpec((1, tm, fs), lambda i, k: (k, i, 0)),
            pl.BlockSpec((fs, d), lambda i, k: (k, 0)),
            pl.BlockSpec((tm, d), lambda i, k: (i, 0)),
        ],
        out_specs=pl.BlockSpec((tm, d), lambda i, k: (i, 0)),
        out_shape=jax.ShapeDtypeStruct((s, d), F32),
        compiler_params=_params("arbitrary", "arbitrary"),
    )(act, wd, x)


def _ffn_bwd_act(dx, wd, gu):
    s, d = dx.shape
    fs = gu.shape[-1]
    tm = _row_tile(s)

    def body(dx_ref, w_ref, gu_ref, o_ref, dxb_ref):
        @pl.when(pl.program_id(1) == 0)
        def _():
            dxb_ref[...] = (0.5 * dx_ref[...]).astype(BF16)

        dact = _dot_nt(dxb_ref[...], w_ref[...])
        gate = gu_ref[0, 0].astype(F32)
        up = gu_ref[1, 0].astype(F32)
        sig = jax.nn.sigmoid(gate)
        o_ref[0, 0] = (dact * up * (sig * (1.0 + gate * (1.0 - sig)))).astype(BF16)
        o_ref[1, 0] = (dact * (gate * sig)).astype(BF16)

    return pl.pallas_call(
        body, grid=(s // tm, 4), name="ffn_bwd_act",
        in_specs=[
            pl.BlockSpec((tm, d), lambda i, j: (i, 0)),
            pl.BlockSpec((fs, d), lambda i, j: (j, 0)),
            pl.BlockSpec((2, 1, tm, fs), lambda i, j: (0, j, i, 0)),
        ],
        out_specs=pl.BlockSpec((2, 1, tm, fs), lambda i, j: (0, j, i, 0)),
        out_shape=jax.ShapeDtypeStruct(gu.shape, BF16),
        scratch_shapes=[pltpu.VMEM((tm, d), BF16)],
        compiler_params=_params("arbitrary", "arbitrary"),
    )(dx, wd, gu)


def _norm_bwd_matmul(name, pairs, nk, x, g, dres):
    s, d = x.shape
    tm = _row_tile(s)
    n = len(pairs)

    def body(*refs):
        a_refs = refs[0:2 * n:2]
        w_refs = refs[1:2 * n:2]
        x_ref, g_ref, dres_ref, dx_ref, dg_ref, acc_ref = refs[2 * n:]
        i, k = pl.program_id(0), pl.program_id(1)

        @pl.when(k == 0)
        def _():
            acc_ref[...] = jnp.zeros_like(acc_ref)

        @pl.when(jnp.logical_and(i == 0, k == 0))
        def _():
            dg_ref[...] = jnp.zeros_like(dg_ref)

        for a_ref, w_ref, pair in zip(a_refs, w_refs, pairs):
            acc_ref[...] += _dot_nt(pair[3](a_ref), pair[7](w_ref))

        @pl.when(k == nk - 1)
        def _():
            dx, dg = _rms_bwd(acc_ref[...], x_ref[...], g_ref[...])
            dx_ref[...] = dres_ref[...] + dx
            dg_ref[...] += dg

    in_specs, args = [], []
    for a, a_block, a_index, _, w, w_block, w_index, _ in pairs:
        in_specs += [pl.BlockSpec(a_block, a_index), pl.BlockSpec(w_block, w_index)]
        args += [a, w]
    row = pl.BlockSpec((tm, d), lambda i, k: (i, 0))
    vec = pl.BlockSpec((1, d), lambda i, k: (0, 0))
    return pl.pallas_call(
        body, grid=(s // tm, nk), name=name,
        in_specs=in_specs + [row, vec, row],
        out_specs=[row, vec],
        out_shape=[jax.ShapeDtypeStruct((s, d), F32), jax.ShapeDtypeStruct((1, d), F32)],
        scratch_shapes=[pltpu.VMEM((tm, d), F32)],
        compiler_params=_params("arbitrary", "arbitrary"),
    )(*args, x, g, dres)


def _ffn_dh(dgu, wgu, x, g, dres):
    s = x.shape[0]
    d, fs = wgu.shape[1:]
    tm = _row_tile(s)
    a_block, a_index = (2, 1, tm, fs), lambda i, k: (0, k, i, 0)
    pairs = [
        (dgu, a_block, a_index, lambda r: r[0, 0], wgu, (1, d, fs), lambda i, k: (k, 0, 0), lambda r: r[0]),
        (dgu, a_block, a_index, lambda r: r[1, 0], wgu, (1, d, fs), lambda i, k: (k + 4, 0, 0), lambda r: r[0]),
    ]
    return _norm_bwd_matmul("ffn_dh", pairs, 4, x, g, dres)


def _square_dh(name, a, w, x, g, dres):
    s, d = x.shape
    tm = _row_tile(s)
    pairs = [(a, (tm, d), lambda i, k: (i, 0), lambda r: r[...], w, (d, d), lambda i, k: (0, 0), lambda r: r[...])]
    return _norm_bwd_matmul(name, pairs, 1, x, g, dres)


def _qkv_dh(dqkv, wqkv, x, g, dres):
    s, d = x.shape
    n = wqkv.shape[-1]
    tm = _row_tile(s)
    pairs = [(dqkv, (tm, n), lambda i, k: (i, k), lambda r: r[...], wqkv, (1, d, n), lambda i, k: (k, 0, 0), lambda r: r[0])]
    return _norm_bwd_matmul("qkv_dh", pairs, N_DEV, x, g, dres)


def _grad_matmul(name, a, a_block, a_index, a_pick, b, b_block, b_index, b_picks, out_shape, out_block, out_index, out_stores, nj, scale=1.0):
    s = a.shape[-2]
    tk = _row_tile(s)
    nk = s // tk
    n_prod = len(b_picks)
    acc_shape = None

    def body(a_ref, b_ref, o_ref, *acc_refs):
        k = pl.program_id(1)
        av = a_pick(a_ref)
        if av.dtype != BF16:
            av = (scale * av).astype(BF16)
        for b_pick, acc_ref in zip(b_picks, acc_refs):
            bv = b_pick(b_ref)
            if bv.dtype != BF16:
                bv = bv.astype(BF16)
            prod = _dot_tn(av, bv)

            @pl.when(k == 0)
            def _():
                acc_ref[...] = prod

            @pl.when(k > 0)
            def _():
                acc_ref[...] += prod

        @pl.when(k == nk - 1)
        def _():
            for store, acc_ref in zip(out_stores, acc_refs):
                store(o_ref, acc_ref[...].astype(BF16))

    m = jax.eval_shape(a_pick, jax.ShapeDtypeStruct(a_block, a.dtype)).shape[-1]
    nn = jax.eval_shape(b_picks[0], jax.ShapeDtypeStruct(b_block, b.dtype)).shape[-1]
    acc_shape = (m, nn)
    return pl.pallas_call(
        body, grid=(nj, nk), name=name,
        in_specs=[pl.BlockSpec(a_block, a_index), pl.BlockSpec(b_block, b_index)],
        out_specs=pl.BlockSpec(out_block, out_index),
        out_shape=jax.ShapeDtypeStruct(out_shape, BF16),
        scratch_shapes=[pltpu.VMEM(acc_shape, F32) for _ in range(n_prod)],
        compiler_params=_params("arbitrary", "arbitrary"),
    )(a, b)


def _pick_all(r):
    return r[...]


def _pick0(r):
    return r[0]


def _store_all(r, v):
    r[...] = v


def _store0(r, v):
    r[0] = v


def _grad_ffn_down(act, dx):
    _, s, fs = act.shape
    d = dx.shape[-1]
    tk = _row_tile(s)
    return _grad_matmul(
        "grad_ffn_down", act, (1, tk, fs), lambda j, k: (j, k, 0), _pick0,
        dx, (tk, d), lambda j, k: (k, 0), [lambda r: 0.5 * r[...]],
        (4 * fs, d), (fs, d), lambda j, k: (j, 0), [_store_all], 4)


def _grad_ffn_gu(h, dgu):
    s, d = h.shape
    fs = dgu.shape[-1]
    tk = _row_tile(s)

    def store_gate(r, v):
        r[0, 0] = v

    def store_up(r, v):
        r[1, 0] = v

    return _grad_matmul(
        "grad_ffn_gu", h, (tk, d), lambda j, k: (k, 0), _pick_all,
        dgu, (2, 1, tk, fs), lambda j, k: (0, j, k, 0), [lambda r: r[0, 0], lambda r: r[1, 0]],
        (2, 4, d, fs), (2, 1, d, fs), lambda j, k: (0, j, 0, 0), [store_gate, store_up], 4)


def _grad_cols(name, a, b, n):
    s, m = a.shape
    tk = _row_tile(s)
    return _grad_matmul(
        name, a, (tk, m), lambda j, k: (k, 0), _pick_all,
        b, (tk, n), lambda j, k: (k, j), [_pick_all],
        (N_DEV, m, n), (1, m, n), lambda j, k: (j, 0, 0), [_store0], N_DEV)


def _grad_square(name, a, b):
    s, m = a.shape
    n = b.shape[-1]
    tk = _row_tile(s)
    return _grad_matmul(
        name, a, (tk, m), lambda j, k: (k, 0), _pick_all,
        b, (tk, n), lambda j, k: (k, 0), [_pick_all],
        (m, n), (m, n), lambda j, k: (0, 0), [_store_all], 1)


def _norm_matmul(name, x, g, w, w_block, w_index, w_pick, n_total, tn, nj):
    s, d = x.shape
    tm = _row_tile(s)

    def body(x_ref, g_ref, w_ref, h_ref, y_ref):
        @pl.when(pl.program_id(1) == 0)
        def _():
            h_ref[...] = _rms(x_ref[...], g_ref[...]).astype(BF16)

        y_ref[...] = _dot(h_ref[...], w_pick(w_ref))

    return pl.pallas_call(
        body, grid=(s // tm, nj), name=name,
        in_specs=[
            pl.BlockSpec((tm, d), lambda i, j: (i, 0)),
            pl.BlockSpec((1, d), lambda i, j: (0, 0)),
            pl.BlockSpec(w_block, w_index),
        ],
        out_specs=[pl.BlockSpec((tm, d), lambda i, j: (i, 0)), pl.BlockSpec((tm, tn), lambda i, j: (i, j))],
        out_shape=[jax.ShapeDtypeStruct((s, d), BF16), jax.ShapeDtypeStruct((s, n_total), F32)],
        compiler_params=_params("arbitrary", "arbitrary"),
    )(x, g, w)


def _head_mean_matrix():
    r = lax.broadcasted_iota(jnp.int32, (128, 128), 0) // HEAD_DIM
    c = lax.broadcasted_iota(jnp.int32, (128, 128), 1) // HEAD_DIM
    return jnp.where(r == c, 1.0 / HEAD_DIM, 0.0).astype(F32)


def _qk_norm(qkv, qg, kg):
    s = qkv.shape[0]
    d = D_MODEL
    tm = _row_tile(s)

    def body(q_ref, k_ref, v_ref, qg_ref, kg_ref, qo_ref, ko_ref, vo_ref):
        mean_m = _head_mean_matrix()
        for u in range(N_UNITS):
            cols = slice(128 * u, 128 * (u + 1))
            for src, gain, dst, scale in ((q_ref, qg_ref, qo_ref, ATTN_SCALE), (k_ref, kg_ref, ko_ref, 1.0)):
                xs = src[:, cols]
                r = lax.rsqrt(_dot_f32(xs * xs, mean_m) + EPS)
                y = xs * r * gain[:, cols]
                dst[:, cols] = (y * scale).astype(BF16) if scale != 1.0 else y.astype(BF16)
        vo_ref[...] = v_ref[...].astype(BF16)

    blk = lambda c: pl.BlockSpec((tm, d), lambda i: (i, c))
    vec = pl.BlockSpec((1, d), lambda i: (0, 0))
    return pl.pallas_call(
        body, grid=(s // tm,), name="qk_norm",
        in_specs=[blk(0), blk(1), blk(2), vec, vec],
        out_specs=[blk(0)] * 3,
        out_shape=[jax.ShapeDtypeStruct((s, d), BF16)] * 3,
        compiler_params=_params("arbitrary"),
    )(qkv, qkv, qkv, qg, kg)


def _qk_norm_bwd(qkv, dq, dk, dv, qg, kg):
    s = qkv.shape[0]
    d = D_MODEL
    tm = _row_tile(s)
    nsteps = s // tm

    def body(q_ref, k_ref, dq_ref, dk_ref, dv_ref, qg_ref, kg_ref, o_ref, dqg_ref, dkg_ref, acc_ref):
        i = pl.program_id(0)

        @pl.when(i == 0)
        def _():
            acc_ref[...] = jnp.zeros_like(acc_ref)

        mean_m = _head_mean_matrix()
        for u in range(N_UNITS):
            cols = slice(128 * u, 128 * (u + 1))
            for n, (src, dsrc, gain) in enumerate(((q_ref, dq_ref, qg_ref), (k_ref, dk_ref, kg_ref))):
                xs = src[:, cols]
                dy = dsrc[:, cols]
                r = lax.rsqrt(_dot_f32(xs * xs, mean_m) + EPS)
                xh = xs * r
                acc_ref[n:n + 1, :] += jnp.sum(dy * xh, axis=0, keepdims=True)
                dxh = dy * gain[:, cols]
                dx = r * (dxh - xh * _dot_f32(dxh * xh, mean_m))
                o_ref[:, 128 * (N_UNITS * n + u):128 * (N_UNITS * n + u + 1)] = dx.astype(BF16)
        o_ref[:, 2 * d:3 * d] = dv_ref[...].astype(BF16)

        @pl.when(i == nsteps - 1)
        def _():
            r = lax.broadcasted_iota(jnp.int32, (128, 128), 0) % HEAD_DIM
            c = lax.broadcasted_iota(jnp.int32, (128, 128), 1) % HEAD_DIM
            fold = jnp.where(r == c, 1.0, 0.0).astype(F32)
            folded = _dot_f32(acc_ref[...], fold)
            dqg_ref[...] = jnp.broadcast_to(folded[0:1], (8, 128))
            dkg_ref[...] = jnp.broadcast_to(folded[1:2], (8, 128))

    blk = lambda c: pl.BlockSpec((tm, d), lambda i: (i, c))
    row = pl.BlockSpec((tm, d), lambda i: (i, 0))
    vec = pl.BlockSpec((1, d), lambda i: (0, 0))
    small = pl.BlockSpec((8, 128), lambda i: (0, 0))
    return pl.pallas_call(
        body, grid=(nsteps,), name="qk_norm_bwd",
        in_specs=[blk(0), blk(1), row, row, row, vec, vec],
        out_specs=[pl.BlockSpec((tm, 3 * d), lambda i: (i, 0)), small, small],
        out_shape=[jax.ShapeDtypeStruct((s, 3 * d), BF16), jax.ShapeDtypeStruct((8, 128), F32), jax.ShapeDtypeStruct((8, 128), F32)],
        scratch_shapes=[pltpu.VMEM((8, 128), F32)],
        compiler_params=_params("arbitrary"),
    )(qkv, qkv, dq, dk, dv, qg, kg)


def _split_dot(x, m):
    hi = x.astype(BF16)
    lo = (x - hi.astype(F32)).astype(BF16)
    return _dot(hi, m) + _dot(lo, m)


def _attn_masks(t):
    row = lax.broadcasted_iota(jnp.int32, (t, t), 0)
    col = lax.broadcasted_iota(jnp.int32, (t, t), 1)
    return row, col


def _attn_scores(qh, kblk, after, causal):
    z = _dot_nt(qh, kblk)
    sp = jnp.maximum(z, 0.0) + jnp.log(1.0 + jnp.exp(-jnp.abs(z)))
    log_stay = -sp
    if causal is not None:
        log_stay = jnp.where(causal, log_stay, 0.0)
    return log_stay, z - sp, _split_dot(log_stay, after)


def _attention(q, k, v):
    s, d = q.shape
    t = min(ATTN_BLOCK, s)

    def body(q_ref, k_ref, v_ref, o_ref):
        i = pl.program_id(1)
        lane = lax.broadcasted_iota(jnp.int32, (t, 128), 1)
        row, col = _attn_masks(t)
        after = (row > col).astype(BF16)
        causal = col < row
        q = q_ref[...]
        outs = []
        for hd in range(HEADS_PER_UNIT):
            head = (lane // HEAD_DIM) == hd
            qh = jnp.where(head, q, jnp.zeros_like(q))

            def step(kb, carry, acc, diag):
                start = pl.multiple_of(kb * t, t)
                kblk = k_ref[pl.ds(start, t), :]
                vblk = v_ref[pl.ds(start, t), :]
                log_stay, log_beta, later = _attn_scores(qh, kblk, after, causal if diag else None)
                w = jnp.exp(log_beta + later + carry)
                if diag:
                    w = jnp.where(causal, w, 0.0)
                acc = acc + _dot(w.astype(BF16), vblk)
                return carry + jnp.sum(log_stay, axis=1, keepdims=True), acc

            carry, acc = step(i, jnp.zeros((t, 1), F32), jnp.zeros((t, 128), F32), True)

            def cond(st):
                return jnp.logical_and(st[0] >= 0, st[3] > LOG_ZERO)

            def loop(st):
                c, a = step(st[0], st[1], st[2], False)
                return st[0] - 1, c, a, jnp.max(c)

            outs.append(lax.while_loop(cond, loop, (i - 1, carry, acc, jnp.max(carry)))[2])
        o_ref[...] = jnp.where(lane < HEAD_DIM, outs[0], outs[1]).astype(BF16)

    return pl.pallas_call(
        body, grid=(N_UNITS, s // t), name="attention",
        in_specs=[
            pl.BlockSpec((t, 128), lambda h, i: (i, h)),
            pl.BlockSpec((s, 128), lambda h, i: (0, h)),
            pl.BlockSpec((s, 128), lambda h, i: (0, h)),
        ],
        out_specs=pl.BlockSpec((t, 128), lambda h, i: (i, h)),
        out_shape=jax.ShapeDtypeStruct((s, d), BF16),
        compiler_params=_params("arbitrary", "arbitrary"),
    )(q, k, v)


def _attention_bwd(q, k, v, do):
    s, d = q.shape
    t = min(ATTN_BLOCK, s)

    def body(q_ref, k_ref, v_ref, do_ref, dq_ref, dk_ref, dv_ref):
        i = pl.program_id(1)

        @pl.when(i == 0)
        def _():
            dk_ref[...] = jnp.zeros_like(dk_ref)
            dv_ref[...] = jnp.zeros_like(dv_ref)

        lane = lax.broadcasted_iota(jnp.int32, (t, 128), 1)
        row, col = _attn_masks(t)
        after = (row > col).astype(BF16)
        from_here = (row >= col).astype(BF16)
        causal = col < row
        q = q_ref[...]
        do = do_ref[...]
        dqs = []
        for hd in range(HEADS_PER_UNIT):
            head = (lane // HEAD_DIM) == hd
            qh = jnp.where(head, q, jnp.zeros_like(q))
            doh = jnp.where(head, do, jnp.zeros_like(do))

            def weights(kb, carry, diag):
                start = pl.multiple_of(kb * t, t)
                kblk = k_ref[pl.ds(start, t), :]
                vblk = v_ref[pl.ds(start, t), :]
                log_stay, log_beta, later = _attn_scores(qh, kblk, after, causal if diag else None)
                w = jnp.exp(log_beta + later + carry)
                if diag:
                    w = jnp.where(causal, w, 0.0)
                g = w * _dot_nt(doh, vblk)
                return start, kblk, log_stay, log_beta, w, g

            def total_step(kb, carry, tot, diag):
                _, _, log_stay, _, _, g = weights(kb, carry, diag)
                return carry + jnp.sum(log_stay, axis=1, keepdims=True), tot + jnp.sum(g, axis=1, keepdims=True)

            zero = jnp.zeros((t, 1), F32)
            carry, tot = total_step(i, zero, zero, True)

            def cond(st):
                return jnp.logical_and(st[0] >= 0, st[3] > LOG_ZERO)

            def total_loop(st):
                c, g = total_step(st[0], st[1], st[2], False)
                return st[0] - 1, c, g, jnp.max(c)

            total = lax.while_loop(cond, total_loop, (i - 1, carry, tot, jnp.max(carry)))[2]

            def grad_step(kb, carry, seen, dq, diag):
                start, kblk, log_stay, log_beta, w, g = weights(kb, carry, diag)
                before = total - (_split_dot(g, from_here) + seen)
                beta = jnp.exp(log_beta)
                da = g * (1.0 - beta) - before * beta
                if diag:
                    da = jnp.where(causal, da, 0.0)
                dab = da.astype(BF16)
                dk_ref[pl.ds(start, t), :] += _dot_tn(dab, qh)
                dv_ref[pl.ds(start, t), :] += _dot_tn(w.astype(BF16), doh)
                return (carry + jnp.sum(log_stay, axis=1, keepdims=True), seen + jnp.sum(g, axis=1, keepdims=True),
                        dq + _dot(dab, kblk))

            carry, seen, dq = grad_step(i, zero, zero, jnp.zeros((t, 128), F32), True)

            def grad_loop(st):
                c, sn, a = grad_step(st[0], st[1], st[2], st[4], False)
                return st[0] - 1, c, sn, jnp.max(c), a

            dqs.append(lax.while_loop(cond, grad_loop, (i - 1, carry, seen, jnp.max(carry), dq))[4])
        dq_ref[...] = ATTN_SCALE * jnp.where(lane < HEAD_DIM, dqs[0], dqs[1])

    blk = pl.BlockSpec((t, 128), lambda h, i: (i, h))
    full = pl.BlockSpec((s, 128), lambda h, i: (0, h))
    return pl.pallas_call(
        body, grid=(N_UNITS, s // t), name="attention_bwd",
        in_specs=[blk, full, full, blk],
        out_specs=[blk, full, full],
        out_shape=[jax.ShapeDtypeStruct((s, d), F32)] * 3,
        compiler_params=_params("arbitrary", "arbitrary"),
    )(q, k, v, do)


def _matmul_res(name, a, w, x, alpha):
    s, kd = a.shape
    d = w.shape[-1]
    tm = _row_tile(s)

    def body(a_ref, w_ref, x_ref, o_ref):
        o_ref[...] = x_ref[...] + alpha * _dot(a_ref[...].astype(BF16), w_ref[...])

    return pl.pallas_call(
        body, grid=(s // tm,), name=name,
        in_specs=[pl.BlockSpec((tm, kd), lambda i: (i, 0)), pl.BlockSpec((kd, d), lambda i: (0, 0)), pl.BlockSpec((tm, d), lambda i: (i, 0))],
        out_specs=pl.BlockSpec((tm, d), lambda i: (i, 0)),
        out_shape=jax.ShapeDtypeStruct((s, d), F32),
        compiler_params=_params("arbitrary"),
    )(a, w, x)


def _matmul_nt(name, a, w):
    s, n = a.shape
    kd = w.shape[0]
    tm = _row_tile(s)

    def body(a_ref, w_ref, o_ref):
        o_ref[...] = _dot_nt(a_ref[...].astype(BF16), w_ref[...]).astype(BF16)

    return pl.pallas_call(
        body, grid=(s // tm,), name=name,
        in_specs=[pl.BlockSpec((tm, n), lambda i: (i, 0)), pl.BlockSpec((kd, n), lambda i: (0, 0))],
        out_specs=pl.BlockSpec((tm, kd), lambda i: (i, 0)),
        out_shape=jax.ShapeDtypeStruct((s, kd), BF16),
        compiler_params=_params("arbitrary"),
    )(a, w)


def _pool_matrix(rows0, cols0, nr, nc, window, transpose):
    r = rows0 + lax.broadcasted_iota(jnp.int32, (nr, nc), 0)
    c = cols0 + lax.broadcasted_iota(jnp.int32, (nr, nc), 1)
    tt, ss = (c, r) if transpose else (r, c)
    inside = jnp.logical_and(tt - ss >= 0, tt - ss < window)
    cnt = jnp.minimum(tt + 1, window).astype(F32)
    return jnp.where(inside, 1.0 / cnt, 0.0) - jnp.where(tt == ss, 1.0, 0.0)


def _pool_tile(s):
    return min(256, s)


def _pool_fwd(u, wgrp, scale, x):
    s, d = u.shape
    tm = _pool_tile(s)
    halo = min(POOL_HALO, tm)
    ratio = tm // halo

    def body(u_ref, prev_ref, w_ref, sc_ref, x_ref, o_ref, p_ref):
        i = pl.program_id(0)
        t0 = i * tm
        for gi, window in enumerate(POOL_WINDOWS):
            cols = slice(POOL_GROUP * gi, POOL_GROUP * (gi + 1))
            pooled = _dot_f32(_pool_matrix(t0, t0, tm, tm, window, False), u_ref[:, cols])
            prev = jnp.where(i > 0, prev_ref[:, cols], 0.0)
            pooled += _dot_f32(_pool_matrix(t0, t0 - halo, tm, halo, window, False), prev)
            pb = pooled.astype(BF16)
            p_ref[:, cols] = pb
            o_ref[:, cols] = x_ref[:, cols] + _dot(pb, w_ref[gi]) * sc_ref[:, cols]

    row = pl.BlockSpec((tm, d), lambda i: (i, 0))
    return pl.pallas_call(
        body, grid=(s // tm,), name="pool_fwd",
        in_specs=[
            row,
            pl.BlockSpec((halo, d), lambda i: (jnp.maximum(i * ratio - 1, 0), 0)),
            pl.BlockSpec((4, POOL_GROUP, POOL_GROUP), lambda i: (0, 0, 0)),
            pl.BlockSpec((1, d), lambda i: (0, 0)),
            row,
        ],
        out_specs=[row, row],
        out_shape=[jax.ShapeDtypeStruct((s, d), F32), jax.ShapeDtypeStruct((s, d), BF16)],
        compiler_params=_params("arbitrary"),
    )(u, u, wgrp, scale, x)


def _pool_bwd_group(dx, pooled, wgrp, scale):
    s, d = dx.shape
    tm = _pool_tile(s)
    nsteps = s // tm

    def body(dx_ref, p_ref, w_ref, sc_ref, dp_ref, dw_ref, dsc_ref, acc_ref):
        i = pl.program_id(0)

        @pl.when(i == 0)
        def _():
            acc_ref[...] = jnp.zeros_like(acc_ref)
            dsc_ref[...] = jnp.zeros_like(dsc_ref)

        for gi in range(len(POOL_WINDOWS)):
            cols = slice(POOL_GROUP * gi, POOL_GROUP * (gi + 1))
            pb = p_ref[:, cols]
            dxg = dx_ref[:, cols]
            y = _dot(pb, w_ref[gi])
            dsc_ref[:, cols] += jnp.sum(dxg * y, axis=0, keepdims=True)
            dyb = (dxg * sc_ref[:, cols]).astype(BF16)
            dp_ref[:, cols] = _dot_nt(dyb, w_ref[gi])
            acc_ref[gi] += _dot_tn(pb, dyb)

        @pl.when(i == nsteps - 1)
        def _():
            dw_ref[...] = acc_ref[...].astype(BF16)

    row = pl.BlockSpec((tm, d), lambda i: (i, 0))
    grp = pl.BlockSpec((4, POOL_GROUP, POOL_GROUP), lambda i: (0, 0, 0))
    vec = pl.BlockSpec((1, d), lambda i: (0, 0))
    return pl.pallas_call(
        body, grid=(nsteps,), name="pool_bwd_group",
        in_specs=[row, row, grp, vec],
        out_specs=[row, grp, vec],
        out_shape=[jax.ShapeDtypeStruct((s, d), F32), jax.ShapeDtypeStruct((4, POOL_GROUP, POOL_GROUP), BF16), jax.ShapeDtypeStruct((1, d), F32)],
        scratch_shapes=[pltpu.VMEM((4, POOL_GROUP, POOL_GROUP), F32)],
        compiler_params=_params("arbitrary"),
    )(dx, pooled, wgrp, scale)


def _pool_bwd_window(dp):
    s, d = dp.shape
    tm = _pool_tile(s)
    halo = min(POOL_HALO, tm)
    ratio = tm // halo
    nsteps = s // tm

    def body(dp_ref, next_ref, o_ref):
        i = pl.program_id(0)
        t0 = i * tm
        for gi, window in enumerate(POOL_WINDOWS):
            cols = slice(POOL_GROUP * gi, POOL_GROUP * (gi + 1))
            du = _dot_f32(_pool_matrix(t0, t0, tm, tm, window, True), dp_ref[:, cols])
            nxt = jnp.where(i < nsteps - 1, next_ref[:, cols], 0.0)
            du += _dot_f32(_pool_matrix(t0, t0 + tm, tm, halo, window, True), nxt)
            o_ref[:, cols] = du.astype(BF16)

    row = pl.BlockSpec((tm, d), lambda i: (i, 0))
    return pl.pallas_call(
        body, grid=(nsteps,), name="pool_bwd_window",
        in_specs=[row, pl.BlockSpec((halo, d), lambda i: (jnp.minimum((i + 1) * ratio, s // halo - 1), 0))],
        out_specs=row,
        out_shape=jax.ShapeDtypeStruct((s, d), BF16),
        compiler_params=_params("arbitrary"),
    )(dp, dp)


def _ple_fwd(x, g, wgate, p, wproj):
    s, d = x.shape
    pd = p.shape[-1]
    tm = _row_tile(s)

    def body(x_ref, g_ref, wg_ref, p_ref, wp_ref, xc_ref, o_ref, h_ref, sig_ref, proj_ref, pb_ref):
        @pl.when(pl.program_id(1) == 0)
        def _():
            h_ref[...] = _rms(x_ref[...], g_ref[...]).astype(BF16)
            pb_ref[...] = p_ref[...].astype(BF16)

        sig = jax.nn.sigmoid(_dot(h_ref[...], wg_ref[...]))
        proj = _dot(pb_ref[...], wp_ref[0])
        o_ref[...] = xc_ref[...] + sig * proj
        sig_ref[...] = sig.astype(BF16)
        proj_ref[...] = proj.astype(BF16)

    row = pl.BlockSpec((tm, d), lambda i, j: (i, 0))
    col = pl.BlockSpec((tm, 128), lambda i, j: (i, j))
    return pl.pallas_call(
        body, grid=(s // tm, N_UNITS), name="ple_fwd",
        in_specs=[
            row,
            pl.BlockSpec((1, d), lambda i, j: (0, 0)),
            pl.BlockSpec((d, 128), lambda i, j: (0, j)),
            pl.BlockSpec((tm, pd), lambda i, j: (i, 0)),
            pl.BlockSpec((1, pd, 128), lambda i, j: (j, 0, 0)),
            col,
        ],
        out_specs=[col, row, col, col],
        out_shape=[jax.ShapeDtypeStruct((s, d), F32), jax.ShapeDtypeStruct((s, d), BF16), jax.ShapeDtypeStruct((s, d), BF16), jax.ShapeDtypeStruct((s, d), BF16)],
        scratch_shapes=[pltpu.VMEM((tm, pd), BF16)],
        compiler_params=_params("arbitrary", "arbitrary"),
    )(x, g, wgate, p, wproj, x)


def _ple_bwd_gate(dx, sig, proj):
    s, d = dx.shape
    tm = _row_tile(s)

    def body(dx_ref, sig_ref, proj_ref, dg_ref, dp_ref):
        dx = dx_ref[...]
        sig = sig_ref[...].astype(F32)
        dg_ref[...] = (dx * proj_ref[...].astype(F32) * (sig * (1.0 - sig))).astype(BF16)
        dp_ref[...] = (dx * sig).astype(BF16)

    row = pl.BlockSpec((tm, d), lambda i: (i, 0))
    return pl.pallas_call(
        body, grid=(s // tm,), name="ple_bwd_gate",
        in_specs=[row, row, row], out_specs=[row, row],
        out_shape=[jax.ShapeDtypeStruct((s, d), BF16)] * 2,
        compiler_params=_params("arbitrary"),
    )(dx, sig, proj)


def _loss_head(y, target):
    s, d = y.shape
    tm = _row_tile(s)
    nsteps = s // tm

    def body(y_ref, t_ref, dy_ref, loss_ref, acc_ref):
        i = pl.program_id(0)

        @pl.when(i == 0)
        def _():
            acc_ref[...] = jnp.zeros_like(acc_ref)

        err = y_ref[...] - t_ref[...]
        dy_ref[...] = err * (1.0 / d)
        acc_ref[...] += jnp.sum(jnp.mean(err * err, axis=-1, keepdims=True), axis=0, keepdims=True)

        @pl.when(i == nsteps - 1)
        def _():
            loss_ref[...] = 0.5 * acc_ref[...]

    row = pl.BlockSpec((tm, d), lambda i: (i, 0))
    return pl.pallas_call(
        body, grid=(nsteps,), name="loss_head",
        in_specs=[row, row], out_specs=[row, pl.BlockSpec((8, 128), lambda i: (0, 0))],
        out_shape=[jax.ShapeDtypeStruct((s, d), F32), jax.ShapeDtypeStruct((8, 128), F32)],
        scratch_shapes=[pltpu.VMEM((8, 128), F32)],
        compiler_params=_params("arbitrary"),
    )(y, target)


def _adamw_math(w, g, m, v):
    m = ADAM_B1 * m + (1.0 - ADAM_B1) * g
    v = ADAM_B2 * v + (1.0 - ADAM_B2) * (g * g)
    m_hat = m / (1.0 - ADAM_B1 ** ADAM_STEP)
    v_hat = v / (1.0 - ADAM_B2 ** ADAM_STEP)
    delta = -ADAM_LR * (m_hat / (jnp.sqrt(v_hat) + ADAM_EPS) + ADAM_WD * w)
    return delta, m, v


def _adamw_layer(parts, w, m, v, layer, outs):
    nl, r, c = w.shape
    tr = max(t for t in range(16, r + 1, 16) if r % t == 0 and t * c <= ADAMW_BLOCK_ELEMS)

    def body(p_ref, w_ref, m_ref, v_ref, *rest):
        g_ref, d_ref, nm_ref, nv_ref = rest[-4:]
        g = p_ref[0].astype(F32)
        for dev in range(1, N_DEV):
            g = g + p_ref[dev].astype(F32)
        delta, nm, nv = _adamw_math(w_ref[0], g, m_ref[0], v_ref[0])
        g_ref[0] = g
        d_ref[0] = delta
        nm_ref[0] = nm
        nv_ref[0] = nv

    slab = pl.BlockSpec((1, tr, c), lambda i: (layer, i, 0))
    any_spec = pl.BlockSpec(memory_space=pl.ANY)
    shape = jax.ShapeDtypeStruct(w.shape, F32)
    carried = [] if outs is None else list(outs)
    return pl.pallas_call(
        body, grid=(r // tr,), name="adamw",
        in_specs=[pl.BlockSpec((N_DEV, tr, c), lambda i: (0, i, 0)), slab, slab, slab] + [any_spec] * len(carried),
        out_specs=[slab] * 4,
        out_shape=[shape] * 4,
        input_output_aliases={4 + n: n for n in range(len(carried))},
        compiler_params=_params("arbitrary"),
    )(parts, w, m, v, *carried)


def _adamw_small(parts, w, m, v):
    r, c = w.shape

    def body(p_ref, w_ref, m_ref, v_ref, g_ref, d_ref, nm_ref, nv_ref):
        g = p_ref[0:r, :]
        for dev in range(1, N_DEV):
            g = g + p_ref[dev * r:(dev + 1) * r, :]
        delta, nm, nv = _adamw_math(w_ref[...], g, m_ref[...], v_ref[...])
        g_ref[...] = g
        d_ref[...] = delta
        nm_ref[...] = nm
        nv_ref[...] = nv

    shape = jax.ShapeDtypeStruct((r, c), F32)
    return pl.pallas_call(body, name="adamw_small", out_shape=[shape] * 4)(parts, w, m, v)


def _my_place():
    return lax.axis_index("x"), lax.axis_index("y"), lax.axis_index("c")


def _peer(k):
    x, y, c = _my_place()
    return (x ^ (k >> 2), y ^ ((k >> 1) & 1), c ^ (k & 1))


def _block_of(place):
    x, y, c = place
    return 4 * x + 2 * y + c


def _gather_small(block):
    m_per, n = block.shape

    def body(x_ref, out_ref, send_sems, recv_sems, local_sem):
        me = _block_of(_my_place())

        def rows(b):
            return out_ref.at[pl.ds(b * m_per, m_per), :]

        mine = pltpu.make_async_copy(x_ref, rows(me), local_sem)
        mine.start()
        sends = []
        for k in range(1, N_DEV):
            cp = pltpu.make_async_remote_copy(
                src_ref=x_ref, dst_ref=rows(me), send_sem=send_sems.at[k - 1], recv_sem=recv_sems.at[k - 1],
                device_id=_peer(k), device_id_type=MESH)
            cp.start()
            sends.append(cp)
        for k in range(1, N_DEV):
            src = rows(_block_of(_peer(k)))
            pltpu.make_async_remote_copy(
                src_ref=src, dst_ref=src, send_sem=send_sems.at[k - 1], recv_sem=recv_sems.at[k - 1],
                device_id=_peer(k), device_id_type=MESH).wait_recv()
        for cp in sends:
            cp.wait_send()
        mine.wait()

    return pl.pallas_call(
        body, name="gather_small",
        out_shape=jax.ShapeDtypeStruct((N_DEV * m_per, n), block.dtype),
        in_specs=[pl.BlockSpec(memory_space=pltpu.VMEM)],
        out_specs=pl.BlockSpec(memory_space=pltpu.VMEM),
        scratch_shapes=[pltpu.SemaphoreType.DMA((N_DEV - 1,)), pltpu.SemaphoreType.DMA((N_DEV - 1,)), pltpu.SemaphoreType.DMA],
    )(block)


def _gather_weights(name, stacks, layers):
    n_t = len(stacks)

    def body(*refs):
        srcs, outs = refs[:n_t], refs[n_t:2 * n_t]
        send_sems, recv_sems, local_sems = refs[2 * n_t:]
        me = _block_of(_my_place())
        local, sends = [], []
        for t in range(n_t):
            src = srcs[t].at[layers[t]]
            own = pltpu.make_async_copy(src, outs[t].at[me], local_sems.at[t])
            own.start()
            local.append(own)
            for k in range(1, N_DEV):
                cp = pltpu.make_async_remote_copy(
                    src_ref=src, dst_ref=outs[t].at[me], send_sem=send_sems.at[t, k - 1], recv_sem=recv_sems.at[t, k - 1],
                    device_id=_peer(k), device_id_type=MESH)
                cp.start()
                sends.append(cp)
        for t in range(n_t):
            for k in range(1, N_DEV):
                slot = outs[t].at[_block_of(_peer(k))]
                pltpu.make_async_remote_copy(
                    src_ref=slot, dst_ref=slot, send_sem=send_sems.at[t, k - 1], recv_sem=recv_sems.at[t, k - 1],
                    device_id=_peer(k), device_id_type=MESH).wait_recv()
        for cp in sends:
            cp.wait_send()
        for cp in local:
            cp.wait()

    any_spec = pl.BlockSpec(memory_space=pl.ANY)
    return pl.pallas_call(
        body, name=name,
        out_shape=[jax.ShapeDtypeStruct((N_DEV,) + st.shape[1:], st.dtype) for st in stacks],
        in_specs=[any_spec] * n_t, out_specs=[any_spec] * n_t,
        scratch_shapes=[pltpu.SemaphoreType.DMA((n_t, N_DEV - 1)), pltpu.SemaphoreType.DMA((n_t, N_DEV - 1)), pltpu.SemaphoreType.DMA((n_t,))],
    )(*stacks)


def _scatter_grads(name, grads):
    n_t = len(grads)

    def body(*refs):
        srcs, outs = refs[:n_t], refs[n_t:2 * n_t]
        send_sems, recv_sems, local_sems = refs[2 * n_t:]
        me = _block_of(_my_place())
        local, sends = [], []
        for t in range(n_t):
            own = pltpu.make_async_copy(srcs[t].at[me], outs[t].at[me], local_sems.at[t])
            own.start()
            local.append(own)
            for k in range(1, N_DEV):
                cp = pltpu.make_async_remote_copy(
                    src_ref=srcs[t].at[_block_of(_peer(k))], dst_ref=outs[t].at[me],
                    send_sem=send_sems.at[t, k - 1], recv_sem=recv_sems.at[t, k - 1],
                    device_id=_peer(k), device_id_type=MESH)
                cp.start()
                sends.append(cp)
        for t in range(n_t):
            for k in range(1, N_DEV):
                slot = outs[t].at[_block_of(_peer(k))]
                pltpu.make_async_remote_copy(
                    src_ref=slot, dst_ref=slot, send_sem=send_sems.at[t, k - 1], recv_sem=recv_sems.at[t, k - 1],
                    device_id=_peer(k), device_id_type=MESH).wait_recv()
        for cp in sends:
            cp.wait_send()
        for cp in local:
            cp.wait()

    any_spec = pl.BlockSpec(memory_space=pl.ANY)
    return pl.pallas_call(
        body, name=name,
        out_shape=[jax.ShapeDtypeStruct(g.shape, g.dtype) for g in grads],
        in_specs=[any_spec] * n_t, out_specs=[any_spec] * n_t,
        scratch_shapes=[pltpu.SemaphoreType.DMA((n_t, N_DEV - 1)), pltpu.SemaphoreType.DMA((n_t, N_DEV - 1)), pltpu.SemaphoreType.DMA((n_t,))],
    )(*grads)


def _ffn_forward(x, g, wgu, wd):
    h, gu, act = _ffn_up(x, g, wgu)
    return _ffn_down(act, wd, x), (x, g, h, gu, act)


def _ffn_backward(dx, saved, wgu, wd):
    x, g, h, gu, act = saved
    dgu = _ffn_bwd_act(dx, wd, gu)
    d_wd = _grad_ffn_down(act, dx)
    dx_in, dg = _ffn_dh(dgu, wgu, x, g, dx)
    d_wgu = _grad_ffn_gu(h, dgu)
    fs = wgu.shape[-1]
    return dx_in, dg, d_wgu.reshape(N_DEV, D_MODEL, fs), d_wd.reshape(N_DEV, fs // 2, D_MODEL)


def kernel(x, p, norm_ffn1, w_ffn1_gu, w_ffn1_down, norm_mix, w_qkv, q_norm, k_norm, w_o, w_pool_in, w_pool_grp, pool_scale, norm_ffn2, w_ffn2_gu, w_ffn2_down, norm_ple, w_ple_gate, w_ple_proj, loss_target, m_norm_ffn1, m_w_ffn1_gu, m_w_ffn1_down, m_norm_mix, m_w_qkv, m_q_norm, m_k_norm, m_w_o, m_w_pool_in, m_w_pool_grp, m_pool_scale, m_norm_ffn2, m_w_ffn2_gu, m_w_ffn2_down, m_norm_ple, m_w_ple_gate, m_w_ple_proj, v_norm_ffn1, v_w_ffn1_gu, v_w_ffn1_down, v_norm_mix, v_w_qkv, v_q_norm, v_k_norm, v_w_o, v_w_pool_in, v_w_pool_grp, v_pool_scale, v_norm_ffn2, v_w_ffn2_gu, v_w_ffn2_down, v_norm_ple, v_w_ple_gate, v_w_ple_proj):
    d = D_MODEL
    xs = x[0]
    target = loss_target[0]
    me = _block_of(_my_place())

    big = dict(w_ffn1_gu=w_ffn1_gu, w_ffn1_down=w_ffn1_down, w_qkv=w_qkv, w_o=w_o, w_pool_in=w_pool_in,
               w_pool_grp=w_pool_grp.reshape(2, 4 * 32, POOL_GROUP), w_ffn2_gu=w_ffn2_gu, w_ffn2_down=w_ffn2_down,
               w_ple_gate=w_ple_gate, w_ple_proj=w_ple_proj)
    moments = dict(
        w_ffn1_gu=(m_w_ffn1_gu, v_w_ffn1_gu), w_ffn1_down=(m_w_ffn1_down, v_w_ffn1_down), w_qkv=(m_w_qkv, v_w_qkv),
        w_o=(m_w_o, v_w_o), w_pool_in=(m_w_pool_in, v_w_pool_in),
        w_pool_grp=(m_w_pool_grp.reshape(2, 4 * 32, POOL_GROUP), v_w_pool_grp.reshape(2, 4 * 32, POOL_GROUP)),
        w_ffn2_gu=(m_w_ffn2_gu, v_w_ffn2_gu), w_ffn2_down=(m_w_ffn2_down, v_w_ffn2_down),
        w_ple_gate=(m_w_ple_gate, v_w_ple_gate), w_ple_proj=(m_w_ple_proj, v_w_ple_proj))
    half = {name: _cast_bf16(w) for name, w in big.items()}

    def layer_names(i):
        mixer = ["w_qkv", "w_o"] if i % 2 == 0 else ["w_pool_in", "w_pool_grp"]
        return ["w_ffn1_gu", "w_ffn1_down"] + mixer + ["w_ffn2_gu", "w_ffn2_down", "w_ple_gate", "w_ple_proj"]

    def layer_index(name, i):
        return i // 2 if name in ("w_qkv", "w_o", "w_pool_in", "w_pool_grp") else i

    scale_all = _gather_small(jnp.pad(pool_scale, ((0, 6), (0, 0))))
    scale_full = scale_all.reshape(N_DEV, 8, 128)[:, :2].transpose(1, 0, 2).reshape(2, d)

    saved = []
    weights = []
    cur = xs
    for i in range(DEPTH):
        names = layer_names(i)
        gathered = _gather_weights(f"gather_weights_{i}", [half[n] for n in names], [layer_index(n, i) for n in names])
        wl = dict(zip(names, gathered))
        for n in ("w_ffn1_down", "w_ffn2_down", "w_o", "w_pool_in", "w_ple_gate"):
            if n in wl:
                wl[n] = wl[n].reshape(-1, d)
        if "w_pool_grp" in wl:
            wl["w_pool_grp"] = wl["w_pool_grp"].reshape(N_DEV, 4, 32, POOL_GROUP).transpose(1, 0, 2, 3).reshape(4, POOL_GROUP, POOL_GROUP)
        weights.append(wl)
        j = i // 2
        rec = {}
        cur, rec["ffn1"] = _ffn_forward(cur, norm_ffn1[i][None], wl["w_ffn1_gu"], wl["w_ffn1_down"])
        x1 = cur
        gm = norm_mix[i][None]
        if i % 2 == 0:
            qg = jnp.tile(q_norm[j], d // HEAD_DIM)[None]
            kg = jnp.tile(k_norm[j], d // HEAD_DIM)[None]
            n_qkv = wl["w_qkv"].shape[-1]
            hm, qkv = _norm_matmul("qkv_proj", x1, gm, wl["w_qkv"], (1, d, n_qkv), lambda a, b: (b, 0, 0), _pick0, 3 * d, n_qkv, N_DEV)
            qn, kn, vb = _qk_norm(qkv, qg, kg)
            o = _attention(qn, kn, vb)
            cur = _matmul_res("attn_out", o, wl["w_o"], x1, 1.0)
            rec["mix"] = (x1, gm, hm, qkv, qg, kg, qn, kn, vb, o)
        else:
            sc = scale_full[j][None]
            hm, u = _norm_matmul("pool_in", x1, gm, wl["w_pool_in"], (d, d), lambda a, b: (0, 0), _pick_all, d, d, 1)
            cur, pooled = _pool_fwd(u, wl["w_pool_grp"], sc, x1)
            rec["mix"] = (x1, gm, hm, sc, pooled)
        cur, rec["ffn2"] = _ffn_forward(cur, norm_ffn2[i][None], wl["w_ffn2_gu"], wl["w_ffn2_down"])
        x3 = cur
        gp = norm_ple[i][None]
        cur, hp, sig, proj = _ple_fwd(x3, gp, wl["w_ple_gate"], p[i, 0], wl["w_ple_proj"])
        rec["ple"] = (x3, gp, hp, sig, proj)
        saved.append(rec)

    dy, loss_part = _loss_head(cur, target)
    loss = lax.psum(loss_part[0, 0], ("x", "y", "c"))

    small = {n: [None] * DEPTH for n in ("norm_ffn1", "norm_mix", "norm_ffn2", "norm_ple")}
    small.update(q_norm=[None] * 2, k_norm=[None] * 2, pool_scale=[None] * 2)
    results = {name: None for name in big}
    dcur = dy
    for i in reversed(range(DEPTH)):
        wl, rec, j = weights[i], saved[i], i // 2
        grads = {}
        x3, gp, hp, sig, proj = rec["ple"]
        dgate, dproj = _ple_bwd_gate(dcur, sig, proj)
        dx3, small["norm_ple"][i] = _square_dh("ple_dh", dgate, wl["w_ple_gate"], x3, gp, dcur)
        grads["w_ple_gate"] = _grad_square("grad_ple_gate", hp, dgate).reshape(N_DEV, d // N_DEV, d)
        grads["w_ple_proj"] = _grad_cols("grad_ple_proj", p[i, 0], dproj, 128)
        dx2, small["norm_ffn2"][i], grads["w_ffn2_gu"], grads["w_ffn2_down"] = _ffn_backward(dx3, rec["ffn2"], wl["w_ffn2_gu"], wl["w_ffn2_down"])
        if i % 2 == 0:
            x1, gm, hm, qkv, qg, kg, qn, kn, vb, o = rec["mix"]
            do = _matmul_nt("attn_out_bwd", dx2, wl["w_o"])
            grads["w_o"] = _grad_square("grad_attn_out", o, dx2).reshape(N_DEV, d // N_DEV, d)
            dq, dk, dv = _attention_bwd(qn, kn, vb, do)
            dqkv, dqg, dkg = _qk_norm_bwd(qkv, dq, dk, dv, qg, kg)
            small["q_norm"][j], small["k_norm"][j] = dqg[0:1], dkg[0:1]
            dx1, small["norm_mix"][i] = _qkv_dh(dqkv, wl["w_qkv"], x1, gm, dx2)
            grads["w_qkv"] = _grad_cols("grad_qkv", hm, dqkv, wl["w_qkv"].shape[-1])
        else:
            x1, gm, hm, sc, pooled = rec["mix"]
            dpool, dgrp, small["pool_scale"][j] = _pool_bwd_group(dx2, pooled, wl["w_pool_grp"], sc)
            grads["w_pool_grp"] = dgrp.reshape(4, N_DEV, 32, POOL_GROUP).transpose(1, 0, 2, 3).reshape(N_DEV, 4 * 32, POOL_GROUP)
            du = _pool_bwd_window(dpool)
            dx1, small["norm_mix"][i] = _square_dh("pool_dh", du, wl["w_pool_in"], x1, gm, dx2)
            grads["w_pool_in"] = _grad_square("grad_pool_in", hm, du).reshape(N_DEV, d // N_DEV, d)
        dcur, small["norm_ffn1"][i], grads["w_ffn1_gu"], grads["w_ffn1_down"] = _ffn_backward(dx1, rec["ffn1"], wl["w_ffn1_gu"], wl["w_ffn1_down"])

        names = layer_names(i)
        parts = _scatter_grads(f"scatter_grads_{i}", [grads[n] for n in names])
        for n, part in zip(names, parts):
            mm, vv = moments[n]
            results[n] = _adamw_layer(part, big[n], mm, vv, layer_index(n, i), results[n])

    def lanes(a):
        return jnp.pad(a, ((0, 0), (0, d - a.shape[-1])))

    order = [("norm_ffn1", DEPTH), ("norm_mix", DEPTH), ("norm_ffn2", DEPTH), ("norm_ple", DEPTH), ("pool_scale", 2), ("q_norm", 2), ("k_norm", 2)]
    rows = jnp.concatenate([lanes(g) for name, _ in order for g in small[name]], axis=0)
    n_rows = rows.shape[0]
    pad_rows = -n_rows % 8
    rows = jnp.pad(rows, ((0, pad_rows), (0, 0)))
    gathered = _gather_small(rows)

    def own_lanes(a):
        return lax.dynamic_update_slice(jnp.zeros((a.shape[0], d), F32), a, (0, me * 128))

    def pack(values):
        mats = [own_lanes(values[name]) if name == "pool_scale" else lanes(values[name]) for name, _ in order]
        return jnp.pad(jnp.concatenate(mats, axis=0), ((0, pad_rows), (0, 0)))

    small_w = dict(norm_ffn1=norm_ffn1, norm_mix=norm_mix, norm_ffn2=norm_ffn2, norm_ple=norm_ple, pool_scale=pool_scale, q_norm=q_norm, k_norm=k_norm)
    small_m = dict(norm_ffn1=m_norm_ffn1, norm_mix=m_norm_mix, norm_ffn2=m_norm_ffn2, norm_ple=m_norm_ple, pool_scale=m_pool_scale, q_norm=m_q_norm, k_norm=m_k_norm)
    small_v = dict(norm_ffn1=v_norm_ffn1, norm_mix=v_norm_mix, norm_ffn2=v_norm_ffn2, norm_ple=v_norm_ple, pool_scale=v_pool_scale, q_norm=v_q_norm, k_norm=v_k_norm)
    packed = _adamw_small(gathered, pack(small_w), pack(small_m), pack(small_v))

    def unpack(mat):
        out, at = {}, 0
        for name, n in order:
            blk = mat[at:at + n]
            at += n
            if name == "pool_scale":
                out[name] = lax.dynamic_slice(blk, (0, me * 128), (n, 128))
            elif name in ("q_norm", "k_norm"):
                out[name] = blk[:, :HEAD_DIM]
            else:
                out[name] = blk
        return out

    small_out = [unpack(mat) for mat in packed]

    def result(kind, name):
        if name in small_w:
            return small_out[kind][name]
        r = results[name][kind]
        return r.reshape(w_pool_grp.shape) if name == "w_pool_grp" else r

    weight_names = ["norm_ffn1", "w_ffn1_gu", "w_ffn1_down", "norm_mix", "w_qkv", "q_norm", "k_norm", "w_o", "w_pool_in", "w_pool_grp",
                    "pool_scale", "norm_ffn2", "w_ffn2_gu", "w_ffn2_down", "norm_ple", "w_ple_gate", "w_ple_proj"]
    outs = [loss, dcur[None]]
    for kind in range(4):
        outs += [result(kind, name) for name in weight_names]
    return tuple(outs)
```

```python
import math

import jax
import jax.numpy as jnp
from jax import lax
from jax.experimental import pallas as pl
from jax.experimental.pallas import tpu as pltpu

F32 = jnp.float32
BF16 = jnp.bfloat16

N_DEV = 8
DEPTH = 4
D_MODEL = 1024
N_UNITS = D_MODEL // 128
HEAD_DIM = 64
POOL_WINDOWS = (2, 4, 8, 16)
POOL_GROUP = 256
POOL_HALO = 128
EPS = 1e-6
ATTN_SCALE = 1.0 / math.sqrt(HEAD_DIM)
ATTN_BLOCK = 256
LOG_ZERO = -104.0

ADAM_LR = 0.001
ADAM_B1 = 0.9
ADAM_B2 = 0.999
ADAM_EPS = 1e-08
ADAM_WD = 0.01
ADAM_STEP = 10
ADAMW_BLOCK_ELEMS = 128 * 1024

VMEM_LIMIT = 56 * 1024 * 1024
MESH = pl.DeviceIdType.MESH

NT_DIMS = (((1,), (1,)), ((), ()))
TN_DIMS = (((0,), (0,)), ((), ()))


def _params(*sem):
    return pltpu.CompilerParams(dimension_semantics=sem, vmem_limit_bytes=VMEM_LIMIT)


def _row_tile(s):
    return min(512, s)


def _ffn_tile(s):
    return min(1024, s)


def _grad_tile(s):
    return min(2048, s)


def _dot(a, b):
    return jnp.dot(a, b, preferred_element_type=F32)


def _dot_nt(a, b):
    return lax.dot_general(a, b, NT_DIMS, preferred_element_type=F32)


def _dot_tn(a, b):
    return lax.dot_general(a, b, TN_DIMS, preferred_element_type=F32)


def _dot_f32(a, b):
    return jnp.dot(a, b, precision=lax.Precision.HIGHEST, preferred_element_type=F32)


def _rms(x, g):
    r = lax.rsqrt(jnp.mean(x * x, axis=-1, keepdims=True) + EPS)
    return x * r * g


def _rms_bwd(dy, x, g):
    r = lax.rsqrt(jnp.mean(x * x, axis=-1, keepdims=True) + EPS)
    xh = x * r
    dg = jnp.sum(dy * xh, axis=0, keepdims=True)
    dxh = dy * g
    dx = r * (dxh - xh * jnp.mean(dxh * xh, axis=-1, keepdims=True))
    return dx, dg


def _cast_bf16(w):
    l, r, c = w.shape

    def body(w_ref, o_ref):
        o_ref[...] = w_ref[...].astype(BF16)

    return pl.pallas_call(
        body, grid=(l,), name="cast_bf16",
        in_specs=[pl.BlockSpec((1, r, c), lambda i: (i, 0, 0))],
        out_specs=pl.BlockSpec((1, r, c), lambda i: (i, 0, 0)),
        out_shape=jax.ShapeDtypeStruct(w.shape, BF16),
        compiler_params=_params("arbitrary"),
    )(w)


def _ffn_up(x, g, wgu):
    s, d = x.shape
    fs = wgu.shape[-1]
    tm = _ffn_tile(s)

    def body(x_ref, g_ref, wg_ref, wu_ref, h_ref, gu_ref, act_ref):
        @pl.when(pl.program_id(1) == 0)
        def _():
            h_ref[...] = _rms(x_ref[...], g_ref[...]).astype(BF16)

        h = h_ref[...]
        gate = _dot(h, wg_ref[0])
        up = _dot(h, wu_ref[0])
        gu_ref[0, 0] = gate.astype(BF16)
        gu_ref[1, 0] = up.astype(BF16)
        act_ref[0] = (gate * jax.nn.sigmoid(gate) * up).astype(BF16)

    return pl.pallas_call(
        body, grid=(s // tm, 4), name="ffn_up",
        in_specs=[
            pl.BlockSpec((tm, d), lambda i, j: (i, 0)),
            pl.BlockSpec((1, d), lambda i, j: (0, 0)),
            pl.BlockSpec((1, d, fs), lambda i, j: (j, 0, 0)),
            pl.BlockSpec((1, d, fs), lambda i, j: (j + 4, 0, 0)),
        ],
        out_specs=[
            pl.BlockSpec((tm, d), lambda i, j: (i, 0)),
            pl.BlockSpec((2, 1, tm, fs), lambda i, j: (0, j, i, 0)),
            pl.BlockSpec((1, tm, fs), lambda i, j: (j, i, 0)),
        ],
        out_shape=[
            jax.ShapeDtypeStruct((s, d), BF16),
            jax.ShapeDtypeStruct((2, 4, s, fs), BF16),
            jax.ShapeDtypeStruct((4, s, fs), BF16),
        ],
        compiler_params=_params("arbitrary", "arbitrary"),
    )(x, g, wgu, wgu)


def _ffn_down(act, wd, x):
    _, s, fs = act.shape
    d = wd.shape[-1]
    tm = _ffn_tile(s)

    def body(a_ref, w_ref, x_ref, o_ref):
        k = pl.program_id(1)

        @pl.when(k == 0)
        def _():
            o_ref[...] = jnp.zeros_like(o_ref)

        o_ref[...] += _dot(a_ref[0], w_ref[...])

        @pl.when(k == 3)
        def _():
            o_ref[...] = x_ref[...] + 0.5 * o_ref[...]

    return pl.pallas_call(
        body, grid=(s // tm, 4), name="ffn_down",
        in_specs=[
            pl.BlockSpec((1, tm, fs), lambda i, k: (k, i, 0)),
            pl.BlockSpec((fs, d), lambda i, k: (k, 0)),
            pl.BlockSpec((tm, d), lambda i, k: (i, 0)),
        ],
        out_specs=pl.BlockSpec((tm, d), lambda i, k: (i, 0)),
        out_shape=jax.ShapeDtypeStruct((s, d), F32),
        compiler_params=_params("arbitrary", "arbitrary"),
    )(act, wd, x)


def _ffn_bwd_act(dx, wd, gu):
    s, d = dx.shape
    fs = gu.shape[-1]
    tm = _ffn_tile(s)

    def body(dx_ref, w_ref, gu_ref, o_ref, dxb_ref):
        @pl.when(pl.program_id(1) == 0)
        def _():
            dxb_ref[...] = (0.5 * dx_ref[...]).astype(BF16)

        dact = _dot_nt(dxb_ref[...], w_ref[...])
        gate = gu_ref[0, 0].astype(F32)
        up = gu_ref[1, 0].astype(F32)
        sig = jax.nn.sigmoid(gate)
        o_ref[0, 0] = (dact * up * (sig * (1.0 + gate * (1.0 - sig)))).astype(BF16)
        o_ref[1, 0] = (dact * (gate * sig)).astype(BF16)

    return pl.pallas_call(
        body, grid=(s // tm, 4), name="ffn_bwd_act",
        in_specs=[
            pl.BlockSpec((tm, d), lambda i, j: (i, 0)),
            pl.BlockSpec((fs, d), lambda i, j: (j, 0)),
            pl.BlockSpec((2, 1, tm, fs), lambda i, j: (0, j, i, 0)),
        ],
        out_specs=pl.BlockSpec((2, 1, tm, fs), lambda i, j: (0, j, i, 0)),
        out_shape=jax.ShapeDtypeStruct(gu.shape, BF16),
        scratch_shapes=[pltpu.VMEM((tm, d), BF16)],
        compiler_params=_params("arbitrary", "arbitrary"),
    )(dx, wd, gu)


def _norm_bwd_matmul(name, operands, products, nk, x, g, dres):
    s, d = x.shape
    tm = _row_tile(s)
    n = len(operands)

    def body(*refs):
        x_ref, g_ref, dres_ref, dx_ref, dg_ref, acc_ref = refs[n:]
        i, k = pl.program_id(0), pl.program_id(1)

        @pl.when(jnp.logical_and(i == 0, k == 0))
        def _():
            dg_ref[...] = jnp.zeros_like(dg_ref)

        total = None
        for a_at, a_pick, w_at, w_pick in products:
            prod = _dot_nt(a_pick(refs[a_at]), w_pick(refs[w_at]))
            total = prod if total is None else total + prod

        if nk > 1:
            @pl.when(k == 0)
            def _():
                acc_ref[...] = total

            @pl.when(jnp.logical_and(k > 0, k < nk - 1))
            def _():
                acc_ref[...] += total

        @pl.when(k == nk - 1)
        def _():
            dy = total if nk == 1 else acc_ref[...] + total
            dx, dg = _rms_bwd(dy, x_ref[...], g_ref[...])
            dx_ref[...] = dres_ref[...] + dx
            dg_ref[...] += dg

    row = pl.BlockSpec((tm, d), lambda i, k: (i, 0))
    vec = pl.BlockSpec((1, d), lambda i, k: (0, 0))
    return pl.pallas_call(
        body, grid=(s // tm, nk), name=name,
        in_specs=[pl.BlockSpec(block, index) for _, block, index in operands] + [row, vec, row],
        out_specs=[row, vec],
        out_shape=[jax.ShapeDtypeStruct((s, d), F32), jax.ShapeDtypeStruct((1, d), F32)],
        scratch_shapes=[pltpu.VMEM((tm, d), F32)],
        compiler_params=_params("arbitrary", "arbitrary"),
    )(*[op[0] for op in operands], x, g, dres)


def _ffn_dh(dgu, wgu, x, g, dres):
    s = x.shape[0]
    d, fs = wgu.shape[1:]
    tm = _row_tile(s)
    operands = [
        (dgu, (2, 1, tm, fs), lambda i, k: (0, k, i, 0)),
        (wgu, (1, d, fs), lambda i, k: (k, 0, 0)),
        (wgu, (1, d, fs), lambda i, k: (k + 4, 0, 0)),
    ]
    products = [(0, lambda r: r[0, 0], 1, _pick0), (0, lambda r: r[1, 0], 2, _pick0)]
    return _norm_bwd_matmul("ffn_dh", operands, products, 4, x, g, dres)


def _square_dh(name, a, w, x, g, dres):
    s, d = x.shape
    tm = _row_tile(s)
    operands = [(a, (tm, d), lambda i, k: (i, 0)), (w, (d, d), lambda i, k: (0, 0))]
    return _norm_bwd_matmul(name, operands, [(0, _pick_all, 1, _pick_all)], 1, x, g, dres)


def _qkv_dh(dqkv, wqkv, x, g, dres):
    s, d = x.shape
    tm = _row_tile(s)
    operands = [(dqkv, (tm, d), lambda i, k: (i, k)), (wqkv, (d, d), lambda i, k: (0, k))]
    return _norm_bwd_matmul("qkv_dh", operands, [(0, _pick_all, 1, _pick_all)], 3, x, g, dres)


def _grad_matmul(name, a, a_block, a_index, a_pick, b, b_block, b_index, b_picks, out_shape, out_block, out_index, out_stores, nj, scale=1.0):
    s = a.shape[-2]
    nk = s // _grad_tile(s)
    n_prod = len(b_picks)

    def body(a_ref, b_ref, o_ref, *acc_refs):
        k = pl.program_id(1)
        av = a_pick(a_ref)
        if av.dtype != BF16:
            av = (scale * av).astype(BF16)
        for b_pick, store, acc_ref in zip(b_picks, out_stores, acc_refs):
            bv = b_pick(b_ref)
            if bv.dtype != BF16:
                bv = bv.astype(BF16)
            prod = _dot_tn(av, bv)
            if nk == 1:
                store(o_ref, prod.astype(BF16))
                continue

            @pl.when(k == 0)
            def _():
                acc_ref[...] = prod

            @pl.when(jnp.logical_and(k > 0, k < nk - 1))
            def _():
                acc_ref[...] += prod

            @pl.when(k == nk - 1)
            def _():
                store(o_ref, (acc_ref[...] + prod).astype(BF16))

    m = jax.eval_shape(a_pick, jax.ShapeDtypeStruct(a_block, a.dtype)).shape[-1]
    nn = jax.eval_shape(b_picks[0], jax.ShapeDtypeStruct(b_block, b.dtype)).shape[-1]
    acc_shape = (m, nn)
    return pl.pallas_call(
        body, grid=(nj, nk), name=name,
        in_specs=[pl.BlockSpec(a_block, a_index), pl.BlockSpec(b_block, b_index)],
        out_specs=pl.BlockSpec(out_block, out_index),
        out_shape=jax.ShapeDtypeStruct(out_shape, BF16),
        scratch_shapes=[pltpu.VMEM(acc_shape, F32) for _ in range(n_prod)],
        compiler_params=_params("arbitrary", "arbitrary"),
    )(a, b)


def _pick_all(r):
    return r[...]


def _pick0(r):
    return r[0]


def _store_all(r, v):
    r[...] = v


def _grad_ffn_down(act, dx):
    _, s, fs = act.shape
    d = dx.shape[-1]
    tk = _grad_tile(s)
    return _grad_matmul(
        "grad_ffn_down", act, (1, tk, fs), lambda j, k: (j, k, 0), _pick0,
        dx, (tk, d), lambda j, k: (k, 0), [lambda r: 0.5 * r[...]],
        (4 * fs, d), (fs, d), lambda j, k: (j, 0), [_store_all], 4)


def _grad_ffn_gu(h, dgu):
    s, d = h.shape
    fs = dgu.shape[-1]
    tk = _grad_tile(s)

    def store_gate(r, v):
        r[0, 0] = v

    def store_up(r, v):
        r[1, 0] = v

    return _grad_matmul(
        "grad_ffn_gu", h, (tk, d), lambda j, k: (k, 0), _pick_all,
        dgu, (2, 1, tk, fs), lambda j, k: (0, j, k, 0), [lambda r: r[0, 0], lambda r: r[1, 0]],
        (2, 4, d, fs), (2, 1, d, fs), lambda j, k: (0, j, 0, 0), [store_gate, store_up], 4)


def _grad_square(name, a, b):
    s, m = a.shape
    n = b.shape[-1]
    tk = _grad_tile(s)
    tn = min(n, D_MODEL)
    return _grad_matmul(
        name, a, (tk, m), lambda j, k: (k, 0), _pick_all,
        b, (tk, tn), lambda j, k: (k, j), [_pick_all],
        (m, n), (m, tn), lambda j, k: (0, j), [_store_all], n // tn)


def _column_shards(w):
    m, n = w.shape
    return w.reshape(m, N_DEV, n // N_DEV).transpose(1, 0, 2)


def _from_column_shards(w):
    nd, m, n = w.shape
    return w.transpose(1, 0, 2).reshape(m, nd * n)


def _norm_matmul(name, x, g, w, w_block, w_index, w_pick, n_total, tn, nj):
    s, d = x.shape
    tm = _row_tile(s)

    def body(x_ref, g_ref, w_ref, h_ref, y_ref):
        @pl.when(pl.program_id(1) == 0)
        def _():
            h_ref[...] = _rms(x_ref[...], g_ref[...]).astype(BF16)

        y_ref[...] = _dot(h_ref[...], w_pick(w_ref))

    return pl.pallas_call(
        body, grid=(s // tm, nj), name=name,
        in_specs=[
            pl.BlockSpec((tm, d), lambda i, j: (i, 0)),
            pl.BlockSpec((1, d), lambda i, j: (0, 0)),
            pl.BlockSpec(w_block, w_index),
        ],
        out_specs=[pl.BlockSpec((tm, d), lambda i, j: (i, 0)), pl.BlockSpec((tm, tn), lambda i, j: (i, j))],
        out_shape=[jax.ShapeDtypeStruct((s, d), BF16), jax.ShapeDtypeStruct((s, n_total), F32)],
        compiler_params=_params("arbitrary", "arbitrary"),
    )(x, g, w)


def _head_mean_matrix():
    r = lax.broadcasted_iota(jnp.int32, (128, 128), 0) // HEAD_DIM
    c = lax.broadcasted_iota(jnp.int32, (128, 128), 1) // HEAD_DIM
    return jnp.where(r == c, 1.0 / HEAD_DIM, 0.0).astype(F32)


def _qk_norm(qkv, qg, kg):
    s = qkv.shape[0]
    d = D_MODEL
    tm = _row_tile(s)

    def body(q_ref, k_ref, v_ref, qg_ref, kg_ref, qo_ref, ko_ref, vo_ref):
        mean_m = _head_mean_matrix()
        for u in range(N_UNITS):
            cols = slice(128 * u, 128 * (u + 1))
            for src, gain, dst, scale in ((q_ref, qg_ref, qo_ref, ATTN_SCALE), (k_ref, kg_ref, ko_ref, 1.0)):
                xs = src[:, cols]
                r = lax.rsqrt(_dot_f32(xs * xs, mean_m) + EPS)
                y = xs * r * gain[:, cols]
                dst[:, cols] = (y * scale).astype(BF16) if scale != 1.0 else y.astype(BF16)
        vo_ref[...] = v_ref[...].astype(BF16)

    blk = lambda c: pl.BlockSpec((tm, d), lambda i: (i, c))
    vec = pl.BlockSpec((1, d), lambda i: (0, 0))
    return pl.pallas_call(
        body, grid=(s // tm,), name="qk_norm",
        in_specs=[blk(0), blk(1), blk(2), vec, vec],
        out_specs=[blk(0)] * 3,
        out_shape=[jax.ShapeDtypeStruct((s, d), BF16)] * 3,
        compiler_params=_params("arbitrary"),
    )(qkv, qkv, qkv, qg, kg)


def _qk_norm_bwd(qkv, dq, dk, dv, qg, kg):
    s = qkv.shape[0]
    d = D_MODEL
    tm = _row_tile(s)
    nsteps = s // tm

    def body(q_ref, k_ref, dq_ref, dk_ref, dv_ref, qg_ref, kg_ref, o_ref, dqg_ref, dkg_ref, acc_ref):
        i = pl.program_id(0)

        @pl.when(i == 0)
        def _():
            acc_ref[...] = jnp.zeros_like(acc_ref)

        mean_m = _head_mean_matrix()
        for u in range(N_UNITS):
            cols = slice(128 * u, 128 * (u + 1))
            for n, (src, dsrc, gain) in enumerate(((q_ref, dq_ref, qg_ref), (k_ref, dk_ref, kg_ref))):
                xs = src[:, cols]
                dy = dsrc[:, cols]
                r = lax.rsqrt(_dot_f32(xs * xs, mean_m) + EPS)
                xh = xs * r
                acc_ref[n:n + 1, :] += jnp.sum(dy * xh, axis=0, keepdims=True)
                dxh = dy * gain[:, cols]
                dx = r * (dxh - xh * _dot_f32(dxh * xh, mean_m))
                o_ref[:, 128 * (N_UNITS * n + u):128 * (N_UNITS * n + u + 1)] = dx.astype(BF16)
        o_ref[:, 2 * d:3 * d] = dv_ref[...].astype(BF16)

        @pl.when(i == nsteps - 1)
        def _():
            r = lax.broadcasted_iota(jnp.int32, (128, 128), 0) % HEAD_DIM
            c = lax.broadcasted_iota(jnp.int32, (128, 128), 1) % HEAD_DIM
            fold = jnp.where(r == c, 1.0, 0.0).astype(F32)
            folded = _dot_f32(acc_ref[...], fold)
            dqg_ref[...] = jnp.broadcast_to(folded[0:1], (8, 128))
            dkg_ref[...] = jnp.broadcast_to(folded[1:2], (8, 128))

    blk = lambda c: pl.BlockSpec((tm, d), lambda i: (i, c))
    row = pl.BlockSpec((tm, d), lambda i: (i, 0))
    vec = pl.BlockSpec((1, d), lambda i: (0, 0))
    small = pl.BlockSpec((8, 128), lambda i: (0, 0))
    return pl.pallas_call(
        body, grid=(nsteps,), name="qk_norm_bwd",
        in_specs=[blk(0), blk(1), row, row, row, vec, vec],
        out_specs=[pl.BlockSpec((tm, 3 * d), lambda i: (i, 0)), small, small],
        out_shape=[jax.ShapeDtypeStruct((s, 3 * d), BF16), jax.ShapeDtypeStruct((8, 128), F32), jax.ShapeDtypeStruct((8, 128), F32)],
        scratch_shapes=[pltpu.VMEM((8, 128), F32)],
        compiler_params=_params("arbitrary"),
    )(qkv, qkv, dq, dk, dv, qg, kg)


def _split_dot(x, m):
    hi = x.astype(BF16)
    lo = (x - hi.astype(F32)).astype(BF16)
    return _dot(hi, m) + _dot(lo, m)


def _stack_heads(x):
    lane = lax.broadcasted_iota(jnp.int32, x.shape, 1)
    zero = jnp.zeros_like(x)
    return jnp.concatenate([jnp.where(lane < HEAD_DIM, x, zero), jnp.where(lane < HEAD_DIM, zero, x)], axis=0)


def _unstack_heads(x2, t):
    lane = lax.broadcasted_iota(jnp.int32, (t, 128), 1)
    return jnp.where(lane < HEAD_DIM, x2[:t], x2[t:])


def _attn_masks(t):
    r = lax.broadcasted_iota(jnp.int32, (t, t), 0)
    c = lax.broadcasted_iota(jnp.int32, (t, t), 1)
    row = lax.broadcasted_iota(jnp.int32, (2 * t, t), 0)
    col = lax.broadcasted_iota(jnp.int32, (2 * t, t), 1)
    causal = col < jnp.where(row >= t, row - t, row)
    return (r > c).astype(BF16), (r >= c).astype(BF16), causal


def _attn_sweep_cond(st):
    return jnp.logical_and(st[0] >= 0, st[1] > LOG_ZERO)


def _attn_scores(q2, kblk, after, causal):
    z = _dot_nt(q2, kblk)
    sp = jnp.maximum(z, 0.0) + jnp.log(1.0 + jnp.exp(-jnp.abs(z)))
    log_stay = -sp
    if causal is not None:
        log_stay = jnp.where(causal, log_stay, 0.0)
    return log_stay, z - sp, _split_dot(log_stay, after)


def _attention(q, k, v):
    s, d = q.shape
    t = min(ATTN_BLOCK, s)

    def body(q_ref, k_ref, v_ref, o_ref):
        i = pl.program_id(1)
        after, _, causal = _attn_masks(t)
        q2 = _stack_heads(q_ref[...])

        def step(kb, carry, acc, diag):
            start = pl.multiple_of(kb * t, t)
            kblk = k_ref[pl.ds(start, t), :]
            vblk = v_ref[pl.ds(start, t), :]
            log_stay, log_beta, later = _attn_scores(q2, kblk, after, causal if diag else None)
            w = jnp.exp(log_beta + later + carry)
            if diag:
                w = jnp.where(causal, w, 0.0)
            return carry + jnp.sum(log_stay, axis=1, keepdims=True), acc + _dot(w.astype(BF16), vblk)

        carry, acc = step(i, jnp.zeros((2 * t, 1), F32), jnp.zeros((2 * t, 128), F32), True)

        def loop(st):
            c, a = step(st[0], st[2], st[3], False)
            return st[0] - 1, jnp.max(c), c, a

        acc = lax.while_loop(_attn_sweep_cond, loop, (i - 1, jnp.max(carry), carry, acc))[3]
        o_ref[...] = _unstack_heads(acc, t).astype(BF16)

    return pl.pallas_call(
        body, grid=(N_UNITS, s // t), name="attention",
        in_specs=[
            pl.BlockSpec((t, 128), lambda h, i: (i, h)),
            pl.BlockSpec((s, 128), lambda h, i: (0, h)),
            pl.BlockSpec((s, 128), lambda h, i: (0, h)),
        ],
        out_specs=pl.BlockSpec((t, 128), lambda h, i: (i, h)),
        out_shape=jax.ShapeDtypeStruct((s, d), BF16),
        compiler_params=_params("arbitrary", "arbitrary"),
    )(q, k, v)


def _attention_bwd(q, k, v, do):
    s, d = q.shape
    t = min(ATTN_BLOCK, s)

    def body(q_ref, k_ref, v_ref, do_ref, dq_ref, dk_ref, dv_ref):
        i = pl.program_id(1)

        @pl.when(i == 0)
        def _():
            dk_ref[...] = jnp.zeros_like(dk_ref)
            dv_ref[...] = jnp.zeros_like(dv_ref)

        after, from_here, causal = _attn_masks(t)
        q2 = _stack_heads(q_ref[...])
        do2 = _stack_heads(do_ref[...])

        def weights(kb, carry, diag):
            start = pl.multiple_of(kb * t, t)
            kblk = k_ref[pl.ds(start, t), :]
            vblk = v_ref[pl.ds(start, t), :]
            log_stay, log_beta, later = _attn_scores(q2, kblk, after, causal if diag else None)
            w = jnp.exp(log_beta + later + carry)
            if diag:
                w = jnp.where(causal, w, 0.0)
            g = w * _dot_nt(do2, vblk)
            return start, kblk, log_stay, log_beta, w, g

        def total_step(kb, carry, tot, diag):
            _, _, log_stay, _, _, g = weights(kb, carry, diag)
            return carry + jnp.sum(log_stay, axis=1, keepdims=True), tot + jnp.sum(g, axis=1, keepdims=True)

        zero = jnp.zeros((2 * t, 1), F32)
        carry, tot = total_step(i, zero, zero, True)

        def total_loop(st):
            c, g = total_step(st[0], st[2], st[3], False)
            return st[0] - 1, jnp.max(c), c, g

        total = lax.while_loop(_attn_sweep_cond, total_loop, (i - 1, jnp.max(carry), carry, tot))[3]

        def grad_step(kb, carry, seen, dq, diag):
            start, kblk, log_stay, log_beta, w, g = weights(kb, carry, diag)
            before = total - (_split_dot(g, from_here) + seen)
            beta = jnp.exp(log_beta)
            da = g * (1.0 - beta) - before * beta
            if diag:
                da = jnp.where(causal, da, 0.0)
            dab = da.astype(BF16)
            dk_ref[pl.ds(start, t), :] += _dot_tn(dab, q2)
            dv_ref[pl.ds(start, t), :] += _dot_tn(w.astype(BF16), do2)
            return (carry + jnp.sum(log_stay, axis=1, keepdims=True), seen + jnp.sum(g, axis=1, keepdims=True),
                    dq + _dot(dab, kblk))

        carry, seen, dq = grad_step(i, zero, zero, jnp.zeros((2 * t, 128), F32), True)

        def grad_loop(st):
            c, sn, a = grad_step(st[0], st[2], st[3], st[4], False)
            return st[0] - 1, jnp.max(c), c, sn, a

        dq = lax.while_loop(_attn_sweep_cond, grad_loop, (i - 1, jnp.max(carry), carry, seen, dq))[4]
        dq_ref[...] = ATTN_SCALE * _unstack_heads(dq, t)

    blk = pl.BlockSpec((t, 128), lambda h, i: (i, h))
    full = pl.BlockSpec((s, 128), lambda h, i: (0, h))
    return pl.pallas_call(
        body, grid=(N_UNITS, s // t), name="attention_bwd",
        in_specs=[blk, full, full, blk],
        out_specs=[blk, full, full],
        out_shape=[jax.ShapeDtypeStruct((s, d), F32)] * 3,
        compiler_params=_params("arbitrary", "arbitrary"),
    )(q, k, v, do)


def _matmul_res(name, a, w, x, alpha):
    s, kd = a.shape
    d = w.shape[-1]
    tm = _row_tile(s)

    def body(a_ref, w_ref, x_ref, o_ref):
        o_ref[...] = x_ref[...] + alpha * _dot(a_ref[...].astype(BF16), w_ref[...])

    return pl.pallas_call(
        body, grid=(s // tm,), name=name,
        in_specs=[pl.BlockSpec((tm, kd), lambda i: (i, 0)), pl.BlockSpec((kd, d), lambda i: (0, 0)), pl.BlockSpec((tm, d), lambda i: (i, 0))],
        out_specs=pl.BlockSpec((tm, d), lambda i: (i, 0)),
        out_shape=jax.ShapeDtypeStruct((s, d), F32),
        compiler_params=_params("arbitrary"),
    )(a, w, x)


def _matmul_nt(name, a, w):
    s, n = a.shape
    kd = w.shape[0]
    tm = _row_tile(s)

    def body(a_ref, w_ref, o_ref):
        o_ref[...] = _dot_nt(a_ref[...].astype(BF16), w_ref[...]).astype(BF16)

    return pl.pallas_call(
        body, grid=(s // tm,), name=name,
        in_specs=[pl.BlockSpec((tm, n), lambda i: (i, 0)), pl.BlockSpec((kd, n), lambda i: (0, 0))],
        out_specs=pl.BlockSpec((tm, kd), lambda i: (i, 0)),
        out_shape=jax.ShapeDtypeStruct((s, kd), BF16),
        compiler_params=_params("arbitrary"),
    )(a, w)


def _pool_matrix(rows0, cols0, nr, nc, window, transpose):
    r = rows0 + lax.broadcasted_iota(jnp.int32, (nr, nc), 0)
    c = cols0 + lax.broadcasted_iota(jnp.int32, (nr, nc), 1)
    tt, ss = (c, r) if transpose else (r, c)
    inside = jnp.logical_and(tt - ss >= 0, tt - ss < window)
    cnt = jnp.minimum(tt + 1, window).astype(F32)
    return jnp.where(inside, 1.0 / cnt, 0.0) - jnp.where(tt == ss, 1.0, 0.0)


def _pool_tile(s):
    return min(256, s)


def _pool_fwd(u, wgrp, scale, x):
    s, d = u.shape
    tm = _pool_tile(s)
    halo = min(POOL_HALO, tm)
    ratio = tm // halo

    def body(u_ref, prev_ref, w_ref, sc_ref, x_ref, o_ref, p_ref):
        i = pl.program_id(0)
        t0 = i * tm
        for gi, window in enumerate(POOL_WINDOWS):
            cols = slice(POOL_GROUP * gi, POOL_GROUP * (gi + 1))
            pooled = _dot_f32(_pool_matrix(t0, t0, tm, tm, window, False), u_ref[:, cols])
            prev = jnp.where(i > 0, prev_ref[:, cols], 0.0)
            pooled += _dot_f32(_pool_matrix(t0, t0 - halo, tm, halo, window, False), prev)
            pb = pooled.astype(BF16)
            p_ref[:, cols] = pb
            o_ref[:, cols] = x_ref[:, cols] + _dot(pb, w_ref[gi]) * sc_ref[:, cols]

    row = pl.BlockSpec((tm, d), lambda i: (i, 0))
    return pl.pallas_call(
        body, grid=(s // tm,), name="pool_fwd",
        in_specs=[
            row,
            pl.BlockSpec((halo, d), lambda i: (jnp.maximum(i * ratio - 1, 0), 0)),
            pl.BlockSpec((4, POOL_GROUP, POOL_GROUP), lambda i: (0, 0, 0)),
            pl.BlockSpec((1, d), lambda i: (0, 0)),
            row,
        ],
        out_specs=[row, row],
        out_shape=[jax.ShapeDtypeStruct((s, d), F32), jax.ShapeDtypeStruct((s, d), BF16)],
        compiler_params=_params("arbitrary"),
    )(u, u, wgrp, scale, x)


def _pool_bwd_group(dx, pooled, wgrp, scale):
    s, d = dx.shape
    tm = _pool_tile(s)
    nsteps = s // tm

    def body(dx_ref, p_ref, w_ref, sc_ref, dp_ref, dw_ref, dsc_ref, acc_ref):
        i = pl.program_id(0)

        @pl.when(i == 0)
        def _():
            acc_ref[...] = jnp.zeros_like(acc_ref)
            dsc_ref[...] = jnp.zeros_like(dsc_ref)

        for gi in range(len(POOL_WINDOWS)):
            cols = slice(POOL_GROUP * gi, POOL_GROUP * (gi + 1))
            pb = p_ref[:, cols]
            dxg = dx_ref[:, cols]
            y = _dot(pb, w_ref[gi])
            dsc_ref[:, cols] += jnp.sum(dxg * y, axis=0, keepdims=True)
            dyb = (dxg * sc_ref[:, cols]).astype(BF16)
            dp_ref[:, cols] = _dot_nt(dyb, w_ref[gi])
            acc_ref[gi] += _dot_tn(pb, dyb)

        @pl.when(i == nsteps - 1)
        def _():
            dw_ref[...] = acc_ref[...].astype(BF16)

    row = pl.BlockSpec((tm, d), lambda i: (i, 0))
    grp = pl.BlockSpec((4, POOL_GROUP, POOL_GROUP), lambda i: (0, 0, 0))
    vec = pl.BlockSpec((1, d), lambda i: (0, 0))
    return pl.pallas_call(
        body, grid=(nsteps,), name="pool_bwd_group",
        in_specs=[row, row, grp, vec],
        out_specs=[row, grp, vec],
        out_shape=[jax.ShapeDtypeStruct((s, d), F32), jax.ShapeDtypeStruct((4, POOL_GROUP, POOL_GROUP), BF16), jax.ShapeDtypeStruct((1, d), F32)],
        scratch_shapes=[pltpu.VMEM((4, POOL_GROUP, POOL_GROUP), F32)],
        compiler_params=_params("arbitrary"),
    )(dx, pooled, wgrp, scale)


def _pool_bwd_window(dp):
    s, d = dp.shape
    tm = _pool_tile(s)
    halo = min(POOL_HALO, tm)
    ratio = tm // halo
    nsteps = s // tm

    def body(dp_ref, next_ref, o_ref):
        i = pl.program_id(0)
        t0 = i * tm
        for gi, window in enumerate(POOL_WINDOWS):
            cols = slice(POOL_GROUP * gi, POOL_GROUP * (gi + 1))
            du = _dot_f32(_pool_matrix(t0, t0, tm, tm, window, True), dp_ref[:, cols])
            nxt = jnp.where(i < nsteps - 1, next_ref[:, cols], 0.0)
            du += _dot_f32(_pool_matrix(t0, t0 + tm, tm, halo, window, True), nxt)
            o_ref[:, cols] = du.astype(BF16)

    row = pl.BlockSpec((tm, d), lambda i: (i, 0))
    return pl.pallas_call(
        body, grid=(nsteps,), name="pool_bwd_window",
        in_specs=[row, pl.BlockSpec((halo, d), lambda i: (jnp.minimum((i + 1) * ratio, s // halo - 1), 0))],
        out_specs=row,
        out_shape=jax.ShapeDtypeStruct((s, d), BF16),
        compiler_params=_params("arbitrary"),
    )(dp, dp)


def _ple_fwd(x, g, wgate, p, wproj):
    s, d = x.shape
    pd = p.shape[-1]
    tm = _row_tile(s)

    def body(x_ref, g_ref, wg_ref, p_ref, wp_ref, o_ref, h_ref, sig_ref, proj_ref):
        x = x_ref[...]
        h = _rms(x, g_ref[...]).astype(BF16)
        sig = jax.nn.sigmoid(_dot(h, wg_ref[...]))
        proj = _dot(p_ref[...].astype(BF16), wp_ref[...])
        o_ref[...] = x + sig * proj
        h_ref[...] = h
        sig_ref[...] = sig.astype(BF16)
        proj_ref[...] = proj.astype(BF16)

    row = pl.BlockSpec((tm, d), lambda i: (i, 0))
    return pl.pallas_call(
        body, grid=(s // tm,), name="ple_fwd",
        in_specs=[
            row,
            pl.BlockSpec((1, d), lambda i: (0, 0)),
            pl.BlockSpec((d, d), lambda i: (0, 0)),
            pl.BlockSpec((tm, pd), lambda i: (i, 0)),
            pl.BlockSpec((pd, d), lambda i: (0, 0)),
        ],
        out_specs=[row] * 4,
        out_shape=[jax.ShapeDtypeStruct((s, d), F32)] + [jax.ShapeDtypeStruct((s, d), BF16)] * 3,
        compiler_params=_params("arbitrary"),
    )(x, g, wgate, p, wproj)


def _ple_bwd_gate(dx, sig, proj):
    s, d = dx.shape
    tm = _row_tile(s)

    def body(dx_ref, sig_ref, proj_ref, dg_ref, dp_ref):
        dx = dx_ref[...]
        sig = sig_ref[...].astype(F32)
        dg_ref[...] = (dx * proj_ref[...].astype(F32) * (sig * (1.0 - sig))).astype(BF16)
        dp_ref[...] = (dx * sig).astype(BF16)

    row = pl.BlockSpec((tm, d), lambda i: (i, 0))
    return pl.pallas_call(
        body, grid=(s // tm,), name="ple_bwd_gate",
        in_specs=[row, row, row], out_specs=[row, row],
        out_shape=[jax.ShapeDtypeStruct((s, d), BF16)] * 2,
        compiler_params=_params("arbitrary"),
    )(dx, sig, proj)


def _loss_head(y, target):
    s, d = y.shape
    tm = _row_tile(s)
    nsteps = s // tm

    def body(y_ref, t_ref, dy_ref, loss_ref, acc_ref):
        i = pl.program_id(0)

        @pl.when(i == 0)
        def _():
            acc_ref[...] = jnp.zeros_like(acc_ref)

        err = y_ref[...] - t_ref[...]
        dy_ref[...] = err * (1.0 / d)
        acc_ref[...] += jnp.sum(jnp.mean(err * err, axis=-1, keepdims=True), axis=0, keepdims=True)

        @pl.when(i == nsteps - 1)
        def _():
            loss_ref[...] = 0.5 * acc_ref[...]

    row = pl.BlockSpec((tm, d), lambda i: (i, 0))
    return pl.pallas_call(
        body, grid=(nsteps,), name="loss_head",
        in_specs=[row, row], out_specs=[row, pl.BlockSpec((8, 128), lambda i: (0, 0))],
        out_shape=[jax.ShapeDtypeStruct((s, d), F32), jax.ShapeDtypeStruct((8, 128), F32)],
        scratch_shapes=[pltpu.VMEM((8, 128), F32)],
        compiler_params=_params("arbitrary"),
    )(y, target)


def _adamw_math(w, g, m, v):
    m = ADAM_B1 * m + (1.0 - ADAM_B1) * g
    v = ADAM_B2 * v + (1.0 - ADAM_B2) * (g * g)
    m_hat = m / (1.0 - ADAM_B1 ** ADAM_STEP)
    v_hat = v / (1.0 - ADAM_B2 ** ADAM_STEP)
    delta = -ADAM_LR * (m_hat / (jnp.sqrt(v_hat) + ADAM_EPS) + ADAM_WD * w)
    return delta, m, v


def _adamw_layer(parts, w, m, v, layer, outs):
    nl, r, c = w.shape
    tr = max(t for t in range(16, r + 1, 16) if r % t == 0 and t * c <= ADAMW_BLOCK_ELEMS)

    def body(p_ref, w_ref, m_ref, v_ref, *rest):
        g_ref, d_ref, nm_ref, nv_ref = rest[-4:]
        g = p_ref[0].astype(F32)
        for dev in range(1, N_DEV):
            g = g + p_ref[dev].astype(F32)
        delta, nm, nv = _adamw_math(w_ref[0], g, m_ref[0], v_ref[0])
        g_ref[0] = g
        d_ref[0] = delta
        nm_ref[0] = nm
        nv_ref[0] = nv

    slab = pl.BlockSpec((1, tr, c), lambda i: (layer, i, 0))
    any_spec = pl.BlockSpec(memory_space=pl.ANY)
    shape = jax.ShapeDtypeStruct(w.shape, F32)
    carried = [] if outs is None else list(outs)
    return pl.pallas_call(
        body, grid=(r // tr,), name="adamw",
        in_specs=[pl.BlockSpec((N_DEV, tr, c), lambda i: (0, i, 0)), slab, slab, slab] + [any_spec] * len(carried),
        out_specs=[slab] * 4,
        out_shape=[shape] * 4,
        input_output_aliases={4 + n: n for n in range(len(carried))},
        compiler_params=_params("arbitrary"),
    )(parts, w, m, v, *carried)


def _adamw_small(parts, w, m, v):
    r, c = w.shape

    def body(p_ref, w_ref, m_ref, v_ref, g_ref, d_ref, nm_ref, nv_ref):
        g = p_ref[0:r, :]
        for dev in range(1, N_DEV):
            g = g + p_ref[dev * r:(dev + 1) * r, :]
        delta, nm, nv = _adamw_math(w_ref[...], g, m_ref[...], v_ref[...])
        g_ref[...] = g
        d_ref[...] = delta
        nm_ref[...] = nm
        nv_ref[...] = nv

    shape = jax.ShapeDtypeStruct((r, c), F32)
    return pl.pallas_call(body, name="adamw_small", out_shape=[shape] * 4)(parts, w, m, v)


def _my_place():
    return lax.axis_index("x"), lax.axis_index("y"), lax.axis_index("c")


def _peer(k):
    x, y, c = _my_place()
    return (x ^ (k >> 2), y ^ ((k >> 1) & 1), c ^ (k & 1))


def _block_of(place):
    x, y, c = place
    return 4 * x + 2 * y + c


def _gather_small(block):
    m_per, n = block.shape

    def body(x_ref, out_ref, send_sems, recv_sems, local_sem):
        me = _block_of(_my_place())

        def rows(b):
            return out_ref.at[pl.ds(b * m_per, m_per), :]

        mine = pltpu.make_async_copy(x_ref, rows(me), local_sem)
        mine.start()
        sends = []
        for k in range(1, N_DEV):
            cp = pltpu.make_async_remote_copy(
                src_ref=x_ref, dst_ref=rows(me), send_sem=send_sems.at[k - 1], recv_sem=recv_sems.at[k - 1],
                device_id=_peer(k), device_id_type=MESH)
            cp.start()
            sends.append(cp)
        for k in range(1, N_DEV):
            src = rows(_block_of(_peer(k)))
            pltpu.make_async_remote_copy(
                src_ref=src, dst_ref=src, send_sem=send_sems.at[k - 1], recv_sem=recv_sems.at[k - 1],
                device_id=_peer(k), device_id_type=MESH).wait_recv()
        for cp in sends:
            cp.wait_send()
        mine.wait()

    return pl.pallas_call(
        body, name="gather_small",
        out_shape=jax.ShapeDtypeStruct((N_DEV * m_per, n), block.dtype),
        in_specs=[pl.BlockSpec(memory_space=pltpu.VMEM)],
        out_specs=pl.BlockSpec(memory_space=pltpu.VMEM),
        scratch_shapes=[pltpu.SemaphoreType.DMA((N_DEV - 1,)), pltpu.SemaphoreType.DMA((N_DEV - 1,)), pltpu.SemaphoreType.DMA],
    )(block)


def _gather_weights(name, stacks, layers):
    n_t = len(stacks)

    def body(*refs):
        srcs, outs = refs[:n_t], refs[n_t:2 * n_t]
        send_sems, recv_sems, local_sems = refs[2 * n_t:]
        x, y, c = _my_place()
        me, sibling = (x, y, c), (x, y, 1 - c)
        chips = [(1 - x, y), (x, 1 - y), (1 - x, 1 - y)]

        def copy(t, k, block, to, src=None):
            slot = outs[t].at[_block_of(block)]
            return pltpu.make_async_remote_copy(
                src_ref=slot if src is None else src, dst_ref=slot,
                send_sem=send_sems.at[t, k], recv_sem=recv_sems.at[t, k], device_id=to, device_id_type=MESH)

        local, sends = [], []
        for t in range(n_t):
            src = srcs[t].at[layers[t]]
            own = pltpu.make_async_copy(src, outs[t].at[_block_of(me)], local_sems.at[t])
            own.start()
            local.append(own)
            sends.append(copy(t, 0, me, sibling, src=src))
            sends += [copy(t, 1 + j, me, (*chip, c), src=src) for j, chip in enumerate(chips)]
        for cp in sends:
            cp.start()
        for t in range(n_t):
            for j, chip in enumerate(chips):
                copy(t, 1 + j, (*chip, c), me).wait_recv()
                passed = copy(t, 4 + j, (*chip, c), sibling)
                passed.start()
                sends.append(passed)
        for t in range(n_t):
            copy(t, 0, sibling, me).wait_recv()
            for j, chip in enumerate(chips):
                copy(t, 4 + j, (*chip, 1 - c), me).wait_recv()
        for cp in sends:
            cp.wait_send()
        for cp in local:
            cp.wait()

    any_spec = pl.BlockSpec(memory_space=pl.ANY)
    return pl.pallas_call(
        body, name=name,
        out_shape=[jax.ShapeDtypeStruct((N_DEV,) + st.shape[1:], st.dtype) for st in stacks],
        in_specs=[any_spec] * n_t, out_specs=[any_spec] * n_t,
        scratch_shapes=[pltpu.SemaphoreType.DMA((n_t, N_DEV - 1)), pltpu.SemaphoreType.DMA((n_t, N_DEV - 1)), pltpu.SemaphoreType.DMA((n_t,))],
    )(*stacks)


def _scatter_grads(name, grads):
    n_t = len(grads)

    def body(*refs):
        srcs, outs = refs[:n_t], refs[n_t:2 * n_t]
        send_sems, recv_sems, local_sems = refs[2 * n_t:]
        me = _block_of(_my_place())
        local, sends = [], []
        for t in range(n_t):
            own = pltpu.make_async_copy(srcs[t].at[me], outs[t].at[me], local_sems.at[t])
            own.start()
            local.append(own)
            for k in range(1, N_DEV):
                cp = pltpu.make_async_remote_copy(
                    src_ref=srcs[t].at[_block_of(_peer(k))], dst_ref=outs[t].at[me],
                    send_sem=send_sems.at[t, k - 1], recv_sem=recv_sems.at[t, k - 1],
                    device_id=_peer(k), device_id_type=MESH)
                cp.start()
                sends.append(cp)
        for t in range(n_t):
            for k in range(1, N_DEV):
                slot = outs[t].at[_block_of(_peer(k))]
                pltpu.make_async_remote_copy(
                    src_ref=slot, dst_ref=slot, send_sem=send_sems.at[t, k - 1], recv_sem=recv_sems.at[t, k - 1],
                    device_id=_peer(k), device_id_type=MESH).wait_recv()
        for cp in sends:
            cp.wait_send()
        for cp in local:
            cp.wait()

    any_spec = pl.BlockSpec(memory_space=pl.ANY)
    return pl.pallas_call(
        body, name=name,
        out_shape=[jax.ShapeDtypeStruct(g.shape, g.dtype) for g in grads],
        in_specs=[any_spec] * n_t, out_specs=[any_spec] * n_t,
        scratch_shapes=[pltpu.SemaphoreType.DMA((n_t, N_DEV - 1)), pltpu.SemaphoreType.DMA((n_t, N_DEV - 1)), pltpu.SemaphoreType.DMA((n_t,))],
    )(*grads)


def _ffn_forward(x, g, wgu, wd):
    h, gu, act = _ffn_up(x, g, wgu)
    return _ffn_down(act, wd, x), (x, g, h, gu, act)


def _ffn_backward(dx, saved, wgu, wd):
    x, g, h, gu, act = saved
    dgu = _ffn_bwd_act(dx, wd, gu)
    d_wd = _grad_ffn_down(act, dx)
    dx_in, dg = _ffn_dh(dgu, wgu, x, g, dx)
    d_wgu = _grad_ffn_gu(h, dgu)
    fs = wgu.shape[-1]
    return dx_in, dg, d_wgu.reshape(N_DEV, D_MODEL, fs), d_wd.reshape(N_DEV, fs // 2, D_MODEL)


def kernel(x, p, norm_ffn1, w_ffn1_gu, w_ffn1_down, norm_mix, w_qkv, q_norm, k_norm, w_o, w_pool_in, w_pool_grp, pool_scale, norm_ffn2, w_ffn2_gu, w_ffn2_down, norm_ple, w_ple_gate, w_ple_proj, loss_target, m_norm_ffn1, m_w_ffn1_gu, m_w_ffn1_down, m_norm_mix, m_w_qkv, m_q_norm, m_k_norm, m_w_o, m_w_pool_in, m_w_pool_grp, m_pool_scale, m_norm_ffn2, m_w_ffn2_gu, m_w_ffn2_down, m_norm_ple, m_w_ple_gate, m_w_ple_proj, v_norm_ffn1, v_w_ffn1_gu, v_w_ffn1_down, v_norm_mix, v_w_qkv, v_q_norm, v_k_norm, v_w_o, v_w_pool_in, v_w_pool_grp, v_pool_scale, v_norm_ffn2, v_w_ffn2_gu, v_w_ffn2_down, v_norm_ple, v_w_ple_gate, v_w_ple_proj):
    d = D_MODEL
    xs = x[0]
    target = loss_target[0]
    me = _block_of(_my_place())

    big = dict(w_ffn1_gu=w_ffn1_gu, w_ffn1_down=w_ffn1_down, w_qkv=w_qkv, w_o=w_o, w_pool_in=w_pool_in,
               w_pool_grp=w_pool_grp.reshape(2, 4 * 32, POOL_GROUP), w_ffn2_gu=w_ffn2_gu, w_ffn2_down=w_ffn2_down,
               w_ple_gate=w_ple_gate, w_ple_proj=w_ple_proj)
    moments = dict(
        w_ffn1_gu=(m_w_ffn1_gu, v_w_ffn1_gu), w_ffn1_down=(m_w_ffn1_down, v_w_ffn1_down), w_qkv=(m_w_qkv, v_w_qkv),
        w_o=(m_w_o, v_w_o), w_pool_in=(m_w_pool_in, v_w_pool_in),
        w_pool_grp=(m_w_pool_grp.reshape(2, 4 * 32, POOL_GROUP), v_w_pool_grp.reshape(2, 4 * 32, POOL_GROUP)),
        w_ffn2_gu=(m_w_ffn2_gu, v_w_ffn2_gu), w_ffn2_down=(m_w_ffn2_down, v_w_ffn2_down),
        w_ple_gate=(m_w_ple_gate, v_w_ple_gate), w_ple_proj=(m_w_ple_proj, v_w_ple_proj))
    half = {name: _cast_bf16(w) for name, w in big.items()}

    def layer_names(i):
        mixer = ["w_qkv", "w_o"] if i % 2 == 0 else ["w_pool_in", "w_pool_grp"]
        return ["w_ffn1_gu", "w_ffn1_down"] + mixer + ["w_ffn2_gu", "w_ffn2_down", "w_ple_gate", "w_ple_proj"]

    def layer_index(name, i):
        return i // 2 if name in ("w_qkv", "w_o", "w_pool_in", "w_pool_grp") else i

    scale_all = _gather_small(jnp.pad(pool_scale, ((0, 6), (0, 0))))
    scale_full = scale_all.reshape(N_DEV, 8, 128)[:, :2].transpose(1, 0, 2).reshape(2, d)

    saved = []
    weights = []
    cur = xs
    for i in range(DEPTH):
        names = layer_names(i)
        gathered = _gather_weights(f"gather_weights_{i}", [half[n] for n in names], [layer_index(n, i) for n in names])
        wl = dict(zip(names, gathered))
        for n in ("w_ffn1_down", "w_ffn2_down", "w_o", "w_pool_in", "w_ple_gate"):
            if n in wl:
                wl[n] = wl[n].reshape(-1, d)
        if "w_pool_grp" in wl:
            wl["w_pool_grp"] = wl["w_pool_grp"].reshape(N_DEV, 4, 32, POOL_GROUP).transpose(1, 0, 2, 3).reshape(4, POOL_GROUP, POOL_GROUP)
        for n in ("w_qkv", "w_ple_proj"):
            if n in wl:
                wl[n] = _from_column_shards(wl[n])
        weights.append(wl)
        j = i // 2
        rec = {}
        cur, rec["ffn1"] = _ffn_forward(cur, norm_ffn1[i][None], wl["w_ffn1_gu"], wl["w_ffn1_down"])
        x1 = cur
        gm = norm_mix[i][None]
        if i % 2 == 0:
            qg = jnp.tile(q_norm[j], d // HEAD_DIM)[None]
            kg = jnp.tile(k_norm[j], d // HEAD_DIM)[None]
            hm, qkv = _norm_matmul("qkv_proj", x1, gm, wl["w_qkv"], (d, d), lambda a, b: (0, b), _pick_all, 3 * d, d, 3)
            qn, kn, vb = _qk_norm(qkv, qg, kg)
            o = _attention(qn, kn, vb)
            cur = _matmul_res("attn_out", o, wl["w_o"], x1, 1.0)
            rec["mix"] = (x1, gm, hm, qkv, qg, kg, qn, kn, vb, o)
        else:
            sc = scale_full[j][None]
            hm, u = _norm_matmul("pool_in", x1, gm, wl["w_pool_in"], (d, d), lambda a, b: (0, 0), _pick_all, d, d, 1)
            cur, pooled = _pool_fwd(u, wl["w_pool_grp"], sc, x1)
            rec["mix"] = (x1, gm, hm, sc, pooled)
        cur, rec["ffn2"] = _ffn_forward(cur, norm_ffn2[i][None], wl["w_ffn2_gu"], wl["w_ffn2_down"])
        x3 = cur
        gp = norm_ple[i][None]
        cur, hp, sig, proj = _ple_fwd(x3, gp, wl["w_ple_gate"], p[i, 0], wl["w_ple_proj"])
        rec["ple"] = (x3, gp, hp, sig, proj)
        saved.append(rec)

    dy, loss_part = _loss_head(cur, target)
    loss = lax.psum(loss_part[0, 0], ("x", "y", "c"))

    small = {n: [None] * DEPTH for n in ("norm_ffn1", "norm_mix", "norm_ffn2", "norm_ple")}
    small.update(q_norm=[None] * 2, k_norm=[None] * 2, pool_scale=[None] * 2)
    results = {name: None for name in big}
    dcur = dy
    for i in reversed(range(DEPTH)):
        wl, rec, j = weights[i], saved[i], i // 2
        grads = {}
        x3, gp, hp, sig, proj = rec["ple"]
        dgate, dproj = _ple_bwd_gate(dcur, sig, proj)
        dx3, small["norm_ple"][i] = _square_dh("ple_dh", dgate, wl["w_ple_gate"], x3, gp, dcur)
        grads["w_ple_gate"] = _grad_square("grad_ple_gate", hp, dgate).reshape(N_DEV, d // N_DEV, d)
        grads["w_ple_proj"] = _column_shards(_grad_square("grad_ple_proj", p[i, 0], dproj))
        dx2, small["norm_ffn2"][i], grads["w_ffn2_gu"], grads["w_ffn2_down"] = _ffn_backward(dx3, rec["ffn2"], wl["w_ffn2_gu"], wl["w_ffn2_down"])
        if i % 2 == 0:
            x1, gm, hm, qkv, qg, kg, qn, kn, vb, o = rec["mix"]
            do = _matmul_nt("attn_out_bwd", dx2, wl["w_o"])
            grads["w_o"] = _grad_square("grad_attn_out", o, dx2).reshape(N_DEV, d // N_DEV, d)
            dq, dk, dv = _attention_bwd(qn, kn, vb, do)
            dqkv, dqg, dkg = _qk_norm_bwd(qkv, dq, dk, dv, qg, kg)
            small["q_norm"][j], small["k_norm"][j] = dqg[0:1], dkg[0:1]
            dx1, small["norm_mix"][i] = _qkv_dh(dqkv, wl["w_qkv"], x1, gm, dx2)
            grads["w_qkv"] = _column_shards(_grad_square("grad_qkv", hm, dqkv))
        else:
            x1, gm, hm, sc, pooled = rec["mix"]
            dpool, dgrp, small["pool_scale"][j] = _pool_bwd_group(dx2, pooled, wl["w_pool_grp"], sc)
            grads["w_pool_grp"] = dgrp.reshape(4, N_DEV, 32, POOL_GROUP).transpose(1, 0, 2, 3).reshape(N_DEV, 4 * 32, POOL_GROUP)
            du = _pool_bwd_window(dpool)
            dx1, small["norm_mix"][i] = _square_dh("pool_dh", du, wl["w_pool_in"], x1, gm, dx2)
            grads["w_pool_in"] = _grad_square("grad_pool_in", hm, du).reshape(N_DEV, d // N_DEV, d)
        dcur, small["norm_ffn1"][i], grads["w_ffn1_gu"], grads["w_ffn1_down"] = _ffn_backward(dx1, rec["ffn1"], wl["w_ffn1_gu"], wl["w_ffn1_down"])

        names = layer_names(i)
        parts = _scatter_grads(f"scatter_grads_{i}", [grads[n] for n in names])
        for n, part in zip(names, parts):
            mm, vv = moments[n]
            results[n] = _adamw_layer(part, big[n], mm, vv, layer_index(n, i), results[n])

    def lanes(a):
        return jnp.pad(a, ((0, 0), (0, d - a.shape[-1])))

    order = [("norm_ffn1", DEPTH), ("norm_mix", DEPTH), ("norm_ffn2", DEPTH), ("norm_ple", DEPTH), ("pool_scale", 2), ("q_norm", 2), ("k_norm", 2)]
    rows = jnp.concatenate([lanes(g) for name, _ in order for g in small[name]], axis=0)
    n_rows = rows.shape[0]
    pad_rows = -n_rows % 8
    rows = jnp.pad(rows, ((0, pad_rows), (0, 0)))
    gathered = _gather_small(rows)

    def own_lanes(a):
        return lax.dynamic_update_slice(jnp.zeros((a.shape[0], d), F32), a, (0, me * 128))

    def pack(values):
        mats = [own_lanes(values[name]) if name == "pool_scale" else lanes(values[name]) for name, _ in order]
        return jnp.pad(jnp.concatenate(mats, axis=0), ((0, pad_rows), (0, 0)))

    small_w = dict(norm_ffn1=norm_ffn1, norm_mix=norm_mix, norm_ffn2=norm_ffn2, norm_ple=norm_ple, pool_scale=pool_scale, q_norm=q_norm, k_norm=k_norm)
    small_m = dict(norm_ffn1=m_norm_ffn1, norm_mix=m_norm_mix, norm_ffn2=m_norm_ffn2, norm_ple=m_norm_ple, pool_scale=m_pool_scale, q_norm=m_q_norm, k_norm=m_k_norm)
    small_v = dict(norm_ffn1=v_norm_ffn1, norm_mix=v_norm_mix, norm_ffn2=v_norm_ffn2, norm_ple=v_norm_ple, pool_scale=v_pool_scale, q_norm=v_q_norm, k_norm=v_k_norm)
    packed = _adamw_small(gathered, pack(small_w), pack(small_m), pack(small_v))

    def unpack(mat):
        out, at = {}, 0
        for name, n in order:
            blk = mat[at:at + n]
            at += n
            if name == "pool_scale":
                out[name] = lax.dynamic_slice(blk, (0, me * 128), (n, 128))
            elif name in ("q_norm", "k_norm"):
                out[name] = blk[:, :HEAD_DIM]
            else:
                out[name] = blk
        return out

    small_out = [unpack(mat) for mat in packed]

    def result(kind, name):
        if name in small_w:
            return small_out[kind][name]
        r = results[name][kind]
        return r.reshape(w_pool_grp.shape) if name == "w_pool_grp" else r

    weight_names = ["norm_ffn1", "w_ffn1_gu", "w_ffn1_down", "norm_mix", "w_qkv", "q_norm", "k_norm", "w_o", "w_pool_in", "w_pool_grp",
                    "pool_scale", "norm_ffn2", "w_ffn2_gu", "w_ffn2_down", "norm_ple", "w_ple_gate", "w_ple_proj"]
    outs = [loss, dcur[None]]
    for kind in range(4):
        outs += [result(kind, name) for name in weight_names]
    return tuple(outs)
```

```python
import math

import jax
import jax.numpy as jnp
from jax import lax
from jax.experimental import pallas as pl
from jax.experimental.pallas import tpu as pltpu

F32 = jnp.float32
BF16 = jnp.bfloat16

N_DEV = 8
DEPTH = 4
D_MODEL = 1024
N_UNITS = D_MODEL // 128
HEAD_DIM = 64
POOL_WINDOWS = (2, 4, 8, 16)
POOL_GROUP = 256
POOL_HALO = 128
EPS = 1e-6
ATTN_SCALE = 1.0 / math.sqrt(HEAD_DIM)
ATTN_BLOCK = 256
LOG_ZERO = -104.0

ADAM_LR = 0.001
ADAM_B1 = 0.9
ADAM_B2 = 0.999
ADAM_EPS = 1e-08
ADAM_WD = 0.01
ADAM_STEP = 10
ADAMW_BLOCK_ELEMS = 128 * 1024

VMEM_LIMIT = 56 * 1024 * 1024
MESH = pl.DeviceIdType.MESH

NT_DIMS = (((1,), (1,)), ((), ()))
TN_DIMS = (((0,), (0,)), ((), ()))


def _params(*sem):
    return pltpu.CompilerParams(dimension_semantics=sem, vmem_limit_bytes=VMEM_LIMIT)


def _row_tile(s):
    return min(512, s)


def _grad_tile(s):
    return min(2048, s)


def _block_spec(block, index, mode=None):
    return pl.BlockSpec(block, index) if mode is None else pl.BlockSpec(block, index, pipeline_mode=mode)


def _dot(a, b):
    return jnp.dot(a, b, preferred_element_type=F32)


def _dot_nt(a, b):
    return lax.dot_general(a, b, NT_DIMS, preferred_element_type=F32)


def _dot_tn(a, b):
    return lax.dot_general(a, b, TN_DIMS, preferred_element_type=F32)


def _dot_f32(a, b):
    return jnp.dot(a, b, precision=lax.Precision.HIGHEST, preferred_element_type=F32)


def _rms(x, g):
    r = lax.rsqrt(jnp.mean(x * x, axis=-1, keepdims=True) + EPS)
    return x * r * g


def _rms_bwd(dy, x, g):
    r = lax.rsqrt(jnp.mean(x * x, axis=-1, keepdims=True) + EPS)
    xh = x * r
    dg = jnp.sum(dy * xh, axis=0, keepdims=True)
    dxh = dy * g
    dx = r * (dxh - xh * jnp.mean(dxh * xh, axis=-1, keepdims=True))
    return dx, dg


def _cast_bf16(w):
    l, r, c = w.shape

    def body(w_ref, o_ref):
        o_ref[...] = w_ref[...].astype(BF16)

    return pl.pallas_call(
        body, grid=(l,), name="cast_bf16",
        in_specs=[pl.BlockSpec((1, r, c), lambda i: (i, 0, 0))],
        out_specs=pl.BlockSpec((1, r, c), lambda i: (i, 0, 0)),
        out_shape=jax.ShapeDtypeStruct(w.shape, BF16),
        compiler_params=_params("arbitrary"),
    )(w)


def _ffn_up(x, g, wt):
    s, d = x.shape
    f = wt.shape[0] // 2
    tn = f // 2
    tm = _row_tile(s)

    last_row_block = s // tm - 1

    def body(x_ref, g_ref, wg_ref, wu_ref, h_ref, gate_ref, up_ref, act_ref):
        h = _rms(x_ref[...], g_ref[...]).astype(BF16)

        @pl.when(pl.program_id(0) == 0)
        def _():
            h_ref[...] = h

        gate = _dot_nt(h, wg_ref[...])
        up = _dot_nt(h, wu_ref[...])
        gate_ref[...] = gate.astype(BF16)
        up_ref[...] = up.astype(BF16)
        act_ref[...] = (gate * jax.nn.sigmoid(gate) * up).astype(BF16)

    col = pl.BlockSpec((tm, tn), lambda j, i: (i, j))
    hidden = jax.ShapeDtypeStruct((s, f), BF16)
    return pl.pallas_call(
        body, grid=(2, s // tm), name="ffn_up",
        in_specs=[
            pl.BlockSpec((tm, d), lambda j, i: (i, 0)),
            pl.BlockSpec((1, d), lambda j, i: (0, 0)),
            pl.BlockSpec((tn, d), lambda j, i: (j, 0)),
            pl.BlockSpec((tn, d), lambda j, i: (j + 2, 0)),
        ],
        out_specs=[pl.BlockSpec((tm, d), lambda j, i: (jnp.where(j == 0, i, last_row_block), 0)), col, col, col],
        out_shape=[jax.ShapeDtypeStruct((s, d), BF16), hidden, hidden, hidden],
        compiler_params=_params("arbitrary", "arbitrary"),
    )(x, g, wt, wt)


def _ffn_down(act, wd, x):
    s, f = act.shape
    d = wd.shape[-1]
    tm = _row_tile(s)

    def body(a_ref, w_ref, x_ref, o_ref):
        o_ref[...] = x_ref[...] + 0.5 * _dot(a_ref[...], w_ref[...])

    row = pl.BlockSpec((tm, d), lambda i: (i, 0))
    return pl.pallas_call(
        body, grid=(s // tm,), name="ffn_down",
        in_specs=[pl.BlockSpec((tm, f), lambda i: (i, 0)), pl.BlockSpec((f, d), lambda i: (0, 0)), row],
        out_specs=row,
        out_shape=jax.ShapeDtypeStruct((s, d), F32),
        compiler_params=_params("arbitrary"),
    )(act, wd, x)


def _ffn_bwd_act(dx, wd, gate, up, side=None):
    s, d = dx.shape
    f = gate.shape[-1]
    tn = f // 2
    tm = _row_tile(s)

    def body(dx_ref, w_ref, gate_ref, up_ref, dgate_ref, dup_ref):
        dact = _dot_nt((0.5 * dx_ref[...]).astype(BF16), w_ref[...])
        gate = gate_ref[...].astype(F32)
        sig = jax.nn.sigmoid(gate)
        silu = gate * sig
        dgate_ref[...] = (dact * up_ref[...].astype(F32) * (sig + silu * (1.0 - sig))).astype(BF16)
        dup_ref[...] = (dact * silu).astype(BF16)

    col = pl.BlockSpec((tm, tn), lambda j, i: (i, j))
    hidden = jax.ShapeDtypeStruct((s, f), BF16)
    return _host_call(
        body, (dx, wd, gate, up), side, grid=(2, s // tm), name="ffn_bwd_act",
        in_specs=[pl.BlockSpec((tm, d), lambda j, i: (i, 0)), pl.BlockSpec((tn, d), lambda j, i: (j, 0)), col, col],
        out_specs=[col, col],
        out_shape=[hidden, hidden],
        compiler_params=_params("arbitrary", "arbitrary"),
    )


def _norm_bwd_matmul(name, operands, products, nk, x, g, dres, w_rows_contract=False, side=None):
    s, d = x.shape
    tm = _row_tile(s)
    n = len(operands)

    def body(*refs):
        x_ref, g_ref, dres_ref, dx_ref, dg_ref, acc_ref = refs[n:]
        i, k = pl.program_id(0), pl.program_id(1)

        @pl.when(jnp.logical_and(i == 0, k == 0))
        def _():
            dg_ref[...] = jnp.zeros_like(dg_ref)

        total = None
        for a_at, a_pick, w_at, w_pick in products:
            prod = (_dot if w_rows_contract else _dot_nt)(a_pick(refs[a_at]), w_pick(refs[w_at]))
            total = prod if total is None else total + prod

        if nk > 1:
            @pl.when(k == 0)
            def _():
                acc_ref[...] = total

            @pl.when(jnp.logical_and(k > 0, k < nk - 1))
            def _():
                acc_ref[...] += total

        @pl.when(k == nk - 1)
        def _():
            dy = total if nk == 1 else acc_ref[...] + total
            dx, dg = _rms_bwd(dy, x_ref[...], g_ref[...])
            dx_ref[...] = dres_ref[...] + dx
            dg_ref[...] += dg

    row = pl.BlockSpec((tm, d), lambda i, k: (i, 0))
    vec = pl.BlockSpec((1, d), lambda i, k: (0, 0))
    return _host_call(
        body, [op[0] for op in operands] + [x, g, dres], side, grid=(s // tm, nk), name=name,
        in_specs=[_block_spec(*op[1:]) for op in operands] + [row, vec, row],
        out_specs=[row, vec],
        out_shape=[jax.ShapeDtypeStruct((s, d), F32), jax.ShapeDtypeStruct((1, d), F32)],
        scratch_shapes=[pltpu.VMEM((tm, d), F32)],
        compiler_params=_params("arbitrary", "arbitrary"),
    )


def _ffn_dh(dgate, dup, wt, x, g, dres, side=None):
    s, d = x.shape
    f = dgate.shape[-1]
    tm = _row_tile(s)
    once = pl.Buffered(1)
    operands = [
        (dgate, (tm, f), lambda i, k: (i, 0)),
        (dup, (tm, f), lambda i, k: (i, 0)),
        (wt, (f, d), lambda i, k: (0, 0), once),
        (wt, (f, d), lambda i, k: (1, 0), once),
    ]
    products = [(0, _pick_all, 2, _pick_all), (1, _pick_all, 3, _pick_all)]
    return _norm_bwd_matmul("ffn_dh", operands, products, 1, x, g, dres, w_rows_contract=True, side=side)


def _square_dh(name, a, w, x, g, dres):
    s, d = x.shape
    tm = _row_tile(s)
    operands = [(a, (tm, d), lambda i, k: (i, 0)), (w, (d, d), lambda i, k: (0, 0))]
    return _norm_bwd_matmul(name, operands, [(0, _pick_all, 1, _pick_all)], 1, x, g, dres)[0]


def _qkv_dh(dqkv, wqkv, x, g, dres):
    s, d = x.shape
    tm = _row_tile(s)
    operands = [(dqkv, (tm, d), lambda i, k: (i, k)), (wqkv, (d, d), lambda i, k: (0, k))]
    return _norm_bwd_matmul("qkv_dh", operands, [(0, _pick_all, 1, _pick_all)], 3, x, g, dres)[0]


def _grad_matmul(name, a, a_block, a_index, a_pick, b, b_block, b_index, b_picks, out_shape, out_block, out_index, out_stores, nj, scale=1.0):
    s = a.shape[-2]
    nk = s // _grad_tile(s)
    n_prod = len(b_picks)

    def body(a_ref, b_ref, o_ref, *acc_refs):
        k = pl.program_id(1)
        av = a_pick(a_ref)
        if av.dtype != BF16:
            av = (scale * av).astype(BF16)
        for b_pick, store, acc_ref in zip(b_picks, out_stores, acc_refs):
            bv = b_pick(b_ref)
            if bv.dtype != BF16:
                bv = bv.astype(BF16)
            prod = _dot_tn(av, bv)
            if nk == 1:
                store(o_ref, prod.astype(BF16))
                continue

            @pl.when(k == 0)
            def _():
                acc_ref[...] = prod

            @pl.when(jnp.logical_and(k > 0, k < nk - 1))
            def _():
                acc_ref[...] += prod

            @pl.when(k == nk - 1)
            def _():
                store(o_ref, (acc_ref[...] + prod).astype(BF16))

    m = jax.eval_shape(a_pick, jax.ShapeDtypeStruct(a_block, a.dtype)).shape[-1]
    nn = jax.eval_shape(b_picks[0], jax.ShapeDtypeStruct(b_block, b.dtype)).shape[-1]
    acc_shape = (m, nn)
    return pl.pallas_call(
        body, grid=(nj, nk), name=name,
        in_specs=[pl.BlockSpec(a_block, a_index), pl.BlockSpec(b_block, b_index)],
        out_specs=pl.BlockSpec(out_block, out_index),
        out_shape=jax.ShapeDtypeStruct(out_shape, BF16),
        scratch_shapes=[pltpu.VMEM(acc_shape, F32) for _ in range(n_prod)],
        compiler_params=_params("arbitrary", "arbitrary"),
    )(a, b)


def _pick_all(r):
    return r[...]


def _store_all(r, v):
    r[...] = v


def _grad_ffn_down(act, dx):
    s, f = act.shape
    d = dx.shape[-1]
    tk = _grad_tile(s)
    tn = f // 2
    return _grad_matmul(
        "grad_ffn_down", act, (tk, tn), lambda j, k: (k, j), _pick_all,
        dx, (tk, d), lambda j, k: (k, 0), [lambda r: 0.5 * r[...]],
        (f, d), (tn, d), lambda j, k: (j, 0), [_store_all], 2)


def _grad_ffn_gu(h, dgate, dup, side=None):
    s, d = h.shape
    f = dgate.shape[-1]
    tn = f // 2
    tk = min(1024, s)
    nk = s // tk

    def body(h_ref, dgate_ref, dup_ref, o_ref, acc_ref):
        j, k = pl.program_id(0), pl.program_id(1)

        def accumulate(a_ref):
            prod = _dot_tn(a_ref[...], h_ref[...])
            if nk == 1:
                o_ref[...] = prod.astype(BF16)
                return

            @pl.when(k == 0)
            def _():
                acc_ref[...] = prod

            @pl.when(jnp.logical_and(k > 0, k < nk - 1))
            def _():
                acc_ref[...] += prod

            @pl.when(k == nk - 1)
            def _():
                o_ref[...] = (acc_ref[...] + prod).astype(BF16)

        @pl.when(j < 2)
        def _():
            accumulate(dgate_ref)

        @pl.when(j >= 2)
        def _():
            accumulate(dup_ref)

    (out,), parts = _host_call(
        body, (h, dgate, dup), side, grid=(4, nk), name="grad_ffn_gu",
        in_specs=[
            pl.BlockSpec((tk, d), lambda j, k: (k, 0)),
            pl.BlockSpec((tk, tn), lambda j, k: (jnp.where(j < 2, k, 0), jnp.minimum(j, 1))),
            pl.BlockSpec((tk, tn), lambda j, k: (jnp.where(j >= 2, k, 0), jnp.maximum(j - 2, 0))),
        ],
        out_specs=[pl.BlockSpec((tn, d), lambda j, k: (j, 0))],
        out_shape=[jax.ShapeDtypeStruct((2 * f, d), BF16)],
        scratch_shapes=[pltpu.VMEM((tn, d), F32)],
        compiler_params=_params("arbitrary", "arbitrary"),
    )
    return out, parts


def _grad_square(name, a, b):
    s, m = a.shape
    n = b.shape[-1]
    tk = _grad_tile(s)
    tn = min(n, D_MODEL)
    return _grad_matmul(
        name, a, (tk, m), lambda j, k: (k, 0), _pick_all,
        b, (tk, tn), lambda j, k: (k, j), [_pick_all],
        (m, n), (m, tn), lambda j, k: (0, j), [_store_all], n // tn)


def _column_shards(w):
    m, n = w.shape
    return w.reshape(m, N_DEV, n // N_DEV).transpose(1, 0, 2)


def _from_column_shards(w):
    nd, m, n = w.shape
    return w.transpose(1, 0, 2).reshape(m, nd * n)


def _norm_matmul(name, x, g, w, w_block, w_index, w_pick, n_total, tn, nj):
    s, d = x.shape
    tm = _row_tile(s)

    def body(x_ref, g_ref, w_ref, h_ref, y_ref):
        @pl.when(pl.program_id(1) == 0)
        def _():
            h_ref[...] = _rms(x_ref[...], g_ref[...]).astype(BF16)

        y_ref[...] = _dot(h_ref[...], w_pick(w_ref))

    return pl.pallas_call(
        body, grid=(s // tm, nj), name=name,
        in_specs=[
            pl.BlockSpec((tm, d), lambda i, j: (i, 0)),
            pl.BlockSpec((1, d), lambda i, j: (0, 0)),
            pl.BlockSpec(w_block, w_index),
        ],
        out_specs=[pl.BlockSpec((tm, d), lambda i, j: (i, 0)), pl.BlockSpec((tm, tn), lambda i, j: (i, j))],
        out_shape=[jax.ShapeDtypeStruct((s, d), BF16), jax.ShapeDtypeStruct((s, n_total), F32)],
        compiler_params=_params("arbitrary", "arbitrary"),
    )(x, g, w)


def _head_mean_matrix():
    r = lax.broadcasted_iota(jnp.int32, (128, 128), 0) // HEAD_DIM
    c = lax.broadcasted_iota(jnp.int32, (128, 128), 1) // HEAD_DIM
    return jnp.where(r == c, 1.0 / HEAD_DIM, 0.0).astype(F32)


def _qk_norm(qkv, qg, kg):
    s = qkv.shape[0]
    d = D_MODEL
    tm = _row_tile(s)

    def body(q_ref, k_ref, v_ref, qg_ref, kg_ref, qo_ref, ko_ref, vo_ref):
        mean_m = _head_mean_matrix()
        for u in range(N_UNITS):
            cols = slice(128 * u, 128 * (u + 1))
            for src, gain, dst, scale in ((q_ref, qg_ref, qo_ref, ATTN_SCALE), (k_ref, kg_ref, ko_ref, 1.0)):
                xs = src[:, cols]
                r = lax.rsqrt(_dot_f32(xs * xs, mean_m) + EPS)
                y = xs * r * gain[:, cols]
                dst[:, cols] = (y * scale).astype(BF16) if scale != 1.0 else y.astype(BF16)
        vo_ref[...] = v_ref[...].astype(BF16)

    blk = lambda c: pl.BlockSpec((tm, d), lambda i: (i, c))
    vec = pl.BlockSpec((1, d), lambda i: (0, 0))
    return pl.pallas_call(
        body, grid=(s // tm,), name="qk_norm",
        in_specs=[blk(0), blk(1), blk(2), vec, vec],
        out_specs=[blk(0)] * 3,
        out_shape=[jax.ShapeDtypeStruct((s, d), BF16)] * 3,
        compiler_params=_params("arbitrary"),
    )(qkv, qkv, qkv, qg, kg)


def _qk_norm_bwd(qkv, dq, dk, dv, qg, kg):
    s = qkv.shape[0]
    d = D_MODEL
    tm = _row_tile(s)
    nsteps = s // tm

    def body(q_ref, k_ref, dq_ref, dk_ref, dv_ref, qg_ref, kg_ref, o_ref, dqg_ref, dkg_ref, acc_ref):
        i = pl.program_id(0)

        @pl.when(i == 0)
        def _():
            acc_ref[...] = jnp.zeros_like(acc_ref)

        mean_m = _head_mean_matrix()
        for u in range(N_UNITS):
            cols = slice(128 * u, 128 * (u + 1))
            for n, (src, dsrc, gain) in enumerate(((q_ref, dq_ref, qg_ref), (k_ref, dk_ref, kg_ref))):
                xs = src[:, cols]
                dy = dsrc[:, cols]
                r = lax.rsqrt(_dot_f32(xs * xs, mean_m) + EPS)
                xh = xs * r
                acc_ref[n:n + 1, :] += jnp.sum(dy * xh, axis=0, keepdims=True)
                dxh = dy * gain[:, cols]
                dx = r * (dxh - xh * _dot_f32(dxh * xh, mean_m))
                o_ref[:, 128 * (N_UNITS * n + u):128 * (N_UNITS * n + u + 1)] = dx.astype(BF16)
        o_ref[:, 2 * d:3 * d] = dv_ref[...].astype(BF16)

        @pl.when(i == nsteps - 1)
        def _():
            r = lax.broadcasted_iota(jnp.int32, (128, 128), 0) % HEAD_DIM
            c = lax.broadcasted_iota(jnp.int32, (128, 128), 1) % HEAD_DIM
            fold = jnp.where(r == c, 1.0, 0.0).astype(F32)
            folded = _dot_f32(acc_ref[...], fold)
            dqg_ref[...] = jnp.broadcast_to(folded[0:1], (8, 128))
            dkg_ref[...] = jnp.broadcast_to(folded[1:2], (8, 128))

    blk = lambda c: pl.BlockSpec((tm, d), lambda i: (i, c))
    row = pl.BlockSpec((tm, d), lambda i: (i, 0))
    vec = pl.BlockSpec((1, d), lambda i: (0, 0))
    small = pl.BlockSpec((8, 128), lambda i: (0, 0))
    return pl.pallas_call(
        body, grid=(nsteps,), name="qk_norm_bwd",
        in_specs=[blk(0), blk(1), row, row, row, vec, vec],
        out_specs=[pl.BlockSpec((tm, 3 * d), lambda i: (i, 0)), small, small],
        out_shape=[jax.ShapeDtypeStruct((s, 3 * d), BF16), jax.ShapeDtypeStruct((8, 128), F32), jax.ShapeDtypeStruct((8, 128), F32)],
        scratch_shapes=[pltpu.VMEM((8, 128), F32)],
        compiler_params=_params("arbitrary"),
    )(qkv, qkv, dq, dk, dv, qg, kg)


def _split_dot(x, m):
    hi = x.astype(BF16)
    lo = (x - hi.astype(F32)).astype(BF16)
    return _dot(hi, m) + _dot(lo, m)


def _stack_heads(x):
    lane = lax.broadcasted_iota(jnp.int32, x.shape, 1)
    zero = jnp.zeros_like(x)
    return jnp.concatenate([jnp.where(lane < HEAD_DIM, x, zero), jnp.where(lane < HEAD_DIM, zero, x)], axis=0)


def _unstack_heads(x2, t):
    lane = lax.broadcasted_iota(jnp.int32, (t, 128), 1)
    return jnp.where(lane < HEAD_DIM, x2[:t], x2[t:])


def _attn_masks(t):
    r = lax.broadcasted_iota(jnp.int32, (t, t), 0)
    c = lax.broadcasted_iota(jnp.int32, (t, t), 1)
    row = lax.broadcasted_iota(jnp.int32, (2 * t, t), 0)
    col = lax.broadcasted_iota(jnp.int32, (2 * t, t), 1)
    causal = col < jnp.where(row >= t, row - t, row)
    return (r > c).astype(BF16), (r >= c).astype(BF16), causal


def _attn_sweep_cond(st):
    return jnp.logical_and(st[0] >= 0, st[1] > LOG_ZERO)


def _attn_scores(q2, kblk, after, causal):
    z = _dot_nt(q2, kblk)
    sp = jnp.maximum(z, 0.0) + jnp.log(1.0 + jnp.exp(-jnp.abs(z)))
    log_stay = -sp
    if causal is not None:
        log_stay = jnp.where(causal, log_stay, 0.0)
    return log_stay, z - sp, _split_dot(log_stay, after)


def _attention(q, k, v):
    s, d = q.shape
    t = min(ATTN_BLOCK, s)

    def body(q_ref, k_ref, v_ref, o_ref):
        i = pl.program_id(1)
        after, _, causal = _attn_masks(t)
        q2 = _stack_heads(q_ref[...])

        def step(kb, carry, acc, diag):
            start = pl.multiple_of(kb * t, t)
            kblk = k_ref[pl.ds(start, t), :]
            vblk = v_ref[pl.ds(start, t), :]
            log_stay, log_beta, later = _attn_scores(q2, kblk, after, causal if diag else None)
            w = jnp.exp(log_beta + later + carry)
            if diag:
                w = jnp.where(causal, w, 0.0)
            return carry + jnp.sum(log_stay, axis=1, keepdims=True), acc + _dot(w.astype(BF16), vblk)

        carry, acc = step(i, jnp.zeros((2 * t, 1), F32), jnp.zeros((2 * t, 128), F32), True)

        def loop(st):
            c, a = step(st[0], st[2], st[3], False)
            return st[0] - 1, jnp.max(c), c, a

        acc = lax.while_loop(_attn_sweep_cond, loop, (i - 1, jnp.max(carry), carry, acc))[3]
        o_ref[...] = _unstack_heads(acc, t).astype(BF16)

    return pl.pallas_call(
        body, grid=(N_UNITS, s // t), name="attention",
        in_specs=[
            pl.BlockSpec((t, 128), lambda h, i: (i, h)),
            pl.BlockSpec((s, 128), lambda h, i: (0, h)),
            pl.BlockSpec((s, 128), lambda h, i: (0, h)),
        ],
        out_specs=pl.BlockSpec((t, 128), lambda h, i: (i, h)),
        out_shape=jax.ShapeDtypeStruct((s, d), BF16),
        compiler_params=_params("arbitrary", "arbitrary"),
    )(q, k, v)


def _attention_bwd(q, k, v, do):
    s, d = q.shape
    t = min(ATTN_BLOCK, s)

    def body(q_ref, k_ref, v_ref, do_ref, dq_ref, dk_ref, dv_ref):
        i = pl.program_id(1)

        @pl.when(i == 0)
        def _():
            dk_ref[...] = jnp.zeros_like(dk_ref)
            dv_ref[...] = jnp.zeros_like(dv_ref)

        after, from_here, causal = _attn_masks(t)
        q2 = _stack_heads(q_ref[...])
        do2 = _stack_heads(do_ref[...])

        def weights(kb, carry, diag):
            start = pl.multiple_of(kb * t, t)
            kblk = k_ref[pl.ds(start, t), :]
            vblk = v_ref[pl.ds(start, t), :]
            log_stay, log_beta, later = _attn_scores(q2, kblk, after, causal if diag else None)
            w = jnp.exp(log_beta + later + carry)
            if diag:
                w = jnp.where(causal, w, 0.0)
            g = w * _dot_nt(do2, vblk)
            return start, kblk, log_stay, log_beta, w, g

        def total_step(kb, carry, tot, diag):
            _, _, log_stay, _, _, g = weights(kb, carry, diag)
            return carry + jnp.sum(log_stay, axis=1, keepdims=True), tot + jnp.sum(g, axis=1, keepdims=True)

        zero = jnp.zeros((2 * t, 1), F32)
        carry, tot = total_step(i, zero, zero, True)

        def total_loop(st):
            c, g = total_step(st[0], st[2], st[3], False)
            return st[0] - 1, jnp.max(c), c, g

        total = lax.while_loop(_attn_sweep_cond, total_loop, (i - 1, jnp.max(carry), carry, tot))[3]

        def grad_step(kb, carry, seen, dq, diag):
            start, kblk, log_stay, log_beta, w, g = weights(kb, carry, diag)
            before = total - (_split_dot(g, from_here) + seen)
            beta = jnp.exp(log_beta)
            da = g * (1.0 - beta) - before * beta
            if diag:
                da = jnp.where(causal, da, 0.0)
            dab = da.astype(BF16)
            dk_ref[pl.ds(start, t), :] += _dot_tn(dab, q2)
            dv_ref[pl.ds(start, t), :] += _dot_tn(w.astype(BF16), do2)
            return (carry + jnp.sum(log_stay, axis=1, keepdims=True), seen + jnp.sum(g, axis=1, keepdims=True),
                    dq + _dot(dab, kblk))

        carry, seen, dq = grad_step(i, zero, zero, jnp.zeros((2 * t, 128), F32), True)

        def grad_loop(st):
            c, sn, a = grad_step(st[0], st[2], st[3], st[4], False)
            return st[0] - 1, jnp.max(c), c, sn, a

        dq = lax.while_loop(_attn_sweep_cond, grad_loop, (i - 1, jnp.max(carry), carry, seen, dq))[4]
        dq_ref[...] = ATTN_SCALE * _unstack_heads(dq, t)

    blk = pl.BlockSpec((t, 128), lambda h, i: (i, h))
    full = pl.BlockSpec((s, 128), lambda h, i: (0, h))
    return pl.pallas_call(
        body, grid=(N_UNITS, s // t), name="attention_bwd",
        in_specs=[blk, full, full, blk],
        out_specs=[blk, full, full],
        out_shape=[jax.ShapeDtypeStruct((s, d), F32)] * 3,
        compiler_params=_params("arbitrary", "arbitrary"),
    )(q, k, v, do)


def _matmul_res(name, a, w, x, alpha):
    s, kd = a.shape
    d = w.shape[-1]
    tm = _row_tile(s)

    def body(a_ref, w_ref, x_ref, o_ref):
        o_ref[...] = x_ref[...] + alpha * _dot(a_ref[...].astype(BF16), w_ref[...])

    return pl.pallas_call(
        body, grid=(s // tm,), name=name,
        in_specs=[pl.BlockSpec((tm, kd), lambda i: (i, 0)), pl.BlockSpec((kd, d), lambda i: (0, 0)), pl.BlockSpec((tm, d), lambda i: (i, 0))],
        out_specs=pl.BlockSpec((tm, d), lambda i: (i, 0)),
        out_shape=jax.ShapeDtypeStruct((s, d), F32),
        compiler_params=_params("arbitrary"),
    )(a, w, x)


def _matmul_nt(name, a, w):
    s, n = a.shape
    kd = w.shape[0]
    tm = _row_tile(s)

    def body(a_ref, w_ref, o_ref):
        o_ref[...] = _dot_nt(a_ref[...].astype(BF16), w_ref[...]).astype(BF16)

    return pl.pallas_call(
        body, grid=(s // tm,), name=name,
        in_specs=[pl.BlockSpec((tm, n), lambda i: (i, 0)), pl.BlockSpec((kd, n), lambda i: (0, 0))],
        out_specs=pl.BlockSpec((tm, kd), lambda i: (i, 0)),
        out_shape=jax.ShapeDtypeStruct((s, kd), BF16),
        compiler_params=_params("arbitrary"),
    )(a, w)


def _pool_matrix(rows0, cols0, nr, nc, window, transpose):
    r = rows0 + lax.broadcasted_iota(jnp.int32, (nr, nc), 0)
    c = cols0 + lax.broadcasted_iota(jnp.int32, (nr, nc), 1)
    tt, ss = (c, r) if transpose else (r, c)
    inside = jnp.logical_and(tt - ss >= 0, tt - ss < window)
    cnt = jnp.minimum(tt + 1, window).astype(F32)
    return jnp.where(inside, 1.0 / cnt, 0.0) - jnp.where(tt == ss, 1.0, 0.0)


def _pool_tile(s):
    return min(256, s)


def _pool_fwd(u, wgrp, scale, x):
    s, d = u.shape
    tm = _pool_tile(s)
    halo = min(POOL_HALO, tm)
    ratio = tm // halo

    def body(u_ref, prev_ref, w_ref, sc_ref, x_ref, o_ref, p_ref):
        i = pl.program_id(0)
        t0 = i * tm
        for gi, window in enumerate(POOL_WINDOWS):
            cols = slice(POOL_GROUP * gi, POOL_GROUP * (gi + 1))
            pooled = _dot_f32(_pool_matrix(t0, t0, tm, tm, window, False), u_ref[:, cols])
            prev = jnp.where(i > 0, prev_ref[:, cols], 0.0)
            pooled += _dot_f32(_pool_matrix(t0, t0 - halo, tm, halo, window, False), prev)
            pb = pooled.astype(BF16)
            p_ref[:, cols] = pb
            o_ref[:, cols] = x_ref[:, cols] + _dot(pb, w_ref[gi]) * sc_ref[:, cols]

    row = pl.BlockSpec((tm, d), lambda i: (i, 0))
    return pl.pallas_call(
        body, grid=(s // tm,), name="pool_fwd",
        in_specs=[
            row,
            pl.BlockSpec((halo, d), lambda i: (jnp.maximum(i * ratio - 1, 0), 0)),
            pl.BlockSpec((4, POOL_GROUP, POOL_GROUP), lambda i: (0, 0, 0)),
            pl.BlockSpec((1, d), lambda i: (0, 0)),
            row,
        ],
        out_specs=[row, row],
        out_shape=[jax.ShapeDtypeStruct((s, d), F32), jax.ShapeDtypeStruct((s, d), BF16)],
        compiler_params=_params("arbitrary"),
    )(u, u, wgrp, scale, x)


def _pool_bwd_group(dx, pooled, wgrp, scale):
    s, d = dx.shape
    tm = _pool_tile(s)
    nsteps = s // tm

    def body(dx_ref, p_ref, w_ref, sc_ref, dp_ref, dw_ref, dsc_ref, acc_ref):
        i = pl.program_id(0)

        @pl.when(i == 0)
        def _():
            acc_ref[...] = jnp.zeros_like(acc_ref)
            dsc_ref[...] = jnp.zeros_like(dsc_ref)

        for gi in range(len(POOL_WINDOWS)):
            cols = slice(POOL_GROUP * gi, POOL_GROUP * (gi + 1))
            pb = p_ref[:, cols]
            dxg = dx_ref[:, cols]
            y = _dot(pb, w_ref[gi])
            dsc_ref[:, cols] += jnp.sum(dxg * y, axis=0, keepdims=True)
            dyb = (dxg * sc_ref[:, cols]).astype(BF16)
            dp_ref[:, cols] = _dot_nt(dyb, w_ref[gi])
            acc_ref[gi] += _dot_tn(pb, dyb)

        @pl.when(i == nsteps - 1)
        def _():
            dw_ref[...] = acc_ref[...].astype(BF16)

    row = pl.BlockSpec((tm, d), lambda i: (i, 0))
    grp = pl.BlockSpec((4, POOL_GROUP, POOL_GROUP), lambda i: (0, 0, 0))
    vec = pl.BlockSpec((1, d), lambda i: (0, 0))
    return pl.pallas_call(
        body, grid=(nsteps,), name="pool_bwd_group",
        in_specs=[row, row, grp, vec],
        out_specs=[row, grp, vec],
        out_shape=[jax.ShapeDtypeStruct((s, d), F32), jax.ShapeDtypeStruct((4, POOL_GROUP, POOL_GROUP), BF16), jax.ShapeDtypeStruct((1, d), F32)],
        scratch_shapes=[pltpu.VMEM((4, POOL_GROUP, POOL_GROUP), F32)],
        compiler_params=_params("arbitrary"),
    )(dx, pooled, wgrp, scale)


def _pool_bwd_window(dp):
    s, d = dp.shape
    tm = _pool_tile(s)
    halo = min(POOL_HALO, tm)
    ratio = tm // halo
    nsteps = s // tm

    def body(dp_ref, next_ref, o_ref):
        i = pl.program_id(0)
        t0 = i * tm
        for gi, window in enumerate(POOL_WINDOWS):
            cols = slice(POOL_GROUP * gi, POOL_GROUP * (gi + 1))
            du = _dot_f32(_pool_matrix(t0, t0, tm, tm, window, True), dp_ref[:, cols])
            nxt = jnp.where(i < nsteps - 1, next_ref[:, cols], 0.0)
            du += _dot_f32(_pool_matrix(t0, t0 + tm, tm, halo, window, True), nxt)
            o_ref[:, cols] = du.astype(BF16)

    row = pl.BlockSpec((tm, d), lambda i: (i, 0))
    return pl.pallas_call(
        body, grid=(nsteps,), name="pool_bwd_window",
        in_specs=[row, pl.BlockSpec((halo, d), lambda i: (jnp.minimum((i + 1) * ratio, s // halo - 1), 0))],
        out_specs=row,
        out_shape=jax.ShapeDtypeStruct((s, d), BF16),
        compiler_params=_params("arbitrary"),
    )(dp, dp)


def _ple_fwd(x, g, wgate, p, wproj):
    s, d = x.shape
    pd = p.shape[-1]
    tm = _row_tile(s)

    def body(x_ref, g_ref, wg_ref, p_ref, wp_ref, o_ref, h_ref, sig_ref, proj_ref):
        x = x_ref[...]
        h = _rms(x, g_ref[...]).astype(BF16)
        sig = jax.nn.sigmoid(_dot(h, wg_ref[...]))
        proj = _dot(p_ref[...].astype(BF16), wp_ref[...])
        o_ref[...] = x + sig * proj
        h_ref[...] = h
        sig_ref[...] = sig.astype(BF16)
        proj_ref[...] = proj.astype(BF16)

    row = pl.BlockSpec((tm, d), lambda i: (i, 0))
    return pl.pallas_call(
        body, grid=(s // tm,), name="ple_fwd",
        in_specs=[
            row,
            pl.BlockSpec((1, d), lambda i: (0, 0)),
            pl.BlockSpec((d, d), lambda i: (0, 0)),
            pl.BlockSpec((tm, pd), lambda i: (i, 0)),
            pl.BlockSpec((pd, d), lambda i: (0, 0)),
        ],
        out_specs=[row] * 4,
        out_shape=[jax.ShapeDtypeStruct((s, d), F32)] + [jax.ShapeDtypeStruct((s, d), BF16)] * 3,
        compiler_params=_params("arbitrary"),
    )(x, g, wgate, p, wproj)


def _ple_bwd_gate(dx, sig, proj):
    s, d = dx.shape
    tm = _row_tile(s)

    def body(dx_ref, sig_ref, proj_ref, dg_ref, dp_ref):
        dx = dx_ref[...]
        sig = sig_ref[...].astype(F32)
        dg_ref[...] = (dx * proj_ref[...].astype(F32) * (sig * (1.0 - sig))).astype(BF16)
        dp_ref[...] = (dx * sig).astype(BF16)

    row = pl.BlockSpec((tm, d), lambda i: (i, 0))
    return pl.pallas_call(
        body, grid=(s // tm,), name="ple_bwd_gate",
        in_specs=[row, row, row], out_specs=[row, row],
        out_shape=[jax.ShapeDtypeStruct((s, d), BF16)] * 2,
        compiler_params=_params("arbitrary"),
    )(dx, sig, proj)


def _loss_head(y, target):
    s, d = y.shape
    tm = _row_tile(s)
    nsteps = s // tm

    def body(y_ref, t_ref, dy_ref, loss_ref, acc_ref):
        i = pl.program_id(0)

        @pl.when(i == 0)
        def _():
            acc_ref[...] = jnp.zeros_like(acc_ref)

        err = y_ref[...] - t_ref[...]
        dy_ref[...] = err * (1.0 / d)
        acc_ref[...] += jnp.sum(jnp.mean(err * err, axis=-1, keepdims=True), axis=0, keepdims=True)

        @pl.when(i == nsteps - 1)
        def _():
            loss_ref[...] = 0.5 * acc_ref[...]

    row = pl.BlockSpec((tm, d), lambda i: (i, 0))
    return pl.pallas_call(
        body, grid=(nsteps,), name="loss_head",
        in_specs=[row, row], out_specs=[row, pl.BlockSpec((8, 128), lambda i: (0, 0))],
        out_shape=[jax.ShapeDtypeStruct((s, d), F32), jax.ShapeDtypeStruct((8, 128), F32)],
        scratch_shapes=[pltpu.VMEM((8, 128), F32)],
        compiler_params=_params("arbitrary"),
    )(y, target)


def _adamw_math(w, g, m, v):
    m = ADAM_B1 * m + (1.0 - ADAM_B1) * g
    v = ADAM_B2 * v + (1.0 - ADAM_B2) * (g * g)
    m_hat = m / (1.0 - ADAM_B1 ** ADAM_STEP)
    v_hat = v / (1.0 - ADAM_B2 ** ADAM_STEP)
    delta = -ADAM_LR * (m_hat / (jnp.sqrt(v_hat) + ADAM_EPS) + ADAM_WD * w)
    return delta, m, v


def _adamw_layer(parts, w, m, v, layer, outs):
    nl, r, c = w.shape
    tr = max(t for t in range(16, r + 1, 16) if r % t == 0 and t * c <= ADAMW_BLOCK_ELEMS)

    def body(p_ref, w_ref, m_ref, v_ref, *rest):
        g_ref, d_ref, nm_ref, nv_ref = rest[-4:]
        g = p_ref[0].astype(F32)
        for dev in range(1, N_DEV):
            g = g + p_ref[dev].astype(F32)
        delta, nm, nv = _adamw_math(w_ref[0], g, m_ref[0], v_ref[0])
        g_ref[0] = g
        d_ref[0] = delta
        nm_ref[0] = nm
        nv_ref[0] = nv

    slab = pl.BlockSpec((1, tr, c), lambda i: (layer, i, 0))
    any_spec = pl.BlockSpec(memory_space=pl.ANY)
    shape = jax.ShapeDtypeStruct(w.shape, F32)
    carried = [] if outs is None else list(outs)
    return pl.pallas_call(
        body, grid=(r // tr,), name="adamw",
        in_specs=[pl.BlockSpec((N_DEV, tr, c), lambda i: (0, i, 0)), slab, slab, slab] + [any_spec] * len(carried),
        out_specs=[slab] * 4,
        out_shape=[shape] * 4,
        input_output_aliases={4 + n: n for n in range(len(carried))},
        compiler_params=_params("arbitrary"),
    )(parts, w, m, v, *carried)


def _adamw_small(parts, w, m, v):
    r, c = w.shape

    def body(p_ref, w_ref, m_ref, v_ref, g_ref, d_ref, nm_ref, nv_ref):
        g = p_ref[0:r, :]
        for dev in range(1, N_DEV):
            g = g + p_ref[dev * r:(dev + 1) * r, :]
        delta, nm, nv = _adamw_math(w_ref[...], g, m_ref[...], v_ref[...])
        g_ref[...] = g
        d_ref[...] = delta
        nm_ref[...] = nm
        nv_ref[...] = nv

    shape = jax.ShapeDtypeStruct((r, c), F32)
    return pl.pallas_call(body, name="adamw_small", out_shape=[shape] * 4)(parts, w, m, v)


def _my_place():
    return lax.axis_index("x"), lax.axis_index("y"), lax.axis_index("c")


def _peer(k):
    x, y, c = _my_place()
    return (x ^ (k >> 2), y ^ ((k >> 1) & 1), c ^ (k & 1))


def _block_of(place):
    x, y, c = place
    return 4 * x + 2 * y + c


def _gather_small(block):
    m_per, n = block.shape

    def body(x_ref, out_ref, send_sems, recv_sems, local_sem):
        me = _block_of(_my_place())

        def rows(b):
            return out_ref.at[pl.ds(b * m_per, m_per), :]

        mine = pltpu.make_async_copy(x_ref, rows(me), local_sem)
        mine.start()
        sends = []
        for k in range(1, N_DEV):
            cp = pltpu.make_async_remote_copy(
                src_ref=x_ref, dst_ref=rows(me), send_sem=send_sems.at[k - 1], recv_sem=recv_sems.at[k - 1],
                device_id=_peer(k), device_id_type=MESH)
            cp.start()
            sends.append(cp)
        for k in range(1, N_DEV):
            src = rows(_block_of(_peer(k)))
            pltpu.make_async_remote_copy(
                src_ref=src, dst_ref=src, send_sem=send_sems.at[k - 1], recv_sem=recv_sems.at[k - 1],
                device_id=_peer(k), device_id_type=MESH).wait_recv()
        for cp in sends:
            cp.wait_send()
        mine.wait()

    return pl.pallas_call(
        body, name="gather_small",
        out_shape=jax.ShapeDtypeStruct((N_DEV * m_per, n), block.dtype),
        in_specs=[pl.BlockSpec(memory_space=pltpu.VMEM)],
        out_specs=pl.BlockSpec(memory_space=pltpu.VMEM),
        scratch_shapes=[pltpu.SemaphoreType.DMA((N_DEV - 1,)), pltpu.SemaphoreType.DMA((N_DEV - 1,)), pltpu.SemaphoreType.DMA],
    )(block)


def _gather_weights(name, stacks, layers):
    n_t = len(stacks)

    def body(*refs):
        srcs, outs = refs[:n_t], refs[n_t:2 * n_t]
        send_sems, recv_sems, local_sems = refs[2 * n_t:]
        x, y, c = _my_place()
        me, sibling = (x, y, c), (x, y, 1 - c)
        chips = [(1 - x, y), (x, 1 - y), (1 - x, 1 - y)]

        def copy(t, k, block, to, src=None):
            slot = outs[t].at[_block_of(block)]
            return pltpu.make_async_remote_copy(
                src_ref=slot if src is None else src, dst_ref=slot,
                send_sem=send_sems.at[t, k], recv_sem=recv_sems.at[t, k], device_id=to, device_id_type=MESH)

        local, sends = [], []
        for t in range(n_t):
            src = srcs[t].at[layers[t]]
            own = pltpu.make_async_copy(src, outs[t].at[_block_of(me)], local_sems.at[t])
            own.start()
            local.append(own)
            sends.append(copy(t, 0, me, sibling, src=src))
            sends += [copy(t, 1 + j, me, (*chip, c), src=src) for j, chip in enumerate(chips)]
        for cp in sends:
            cp.start()
        for t in range(n_t):
            for j, chip in enumerate(chips):
                copy(t, 1 + j, (*chip, c), me).wait_recv()
                passed = copy(t, 4 + j, (*chip, c), sibling)
                passed.start()
                sends.append(passed)
        for t in range(n_t):
            copy(t, 0, sibling, me).wait_recv()
            for j, chip in enumerate(chips):
                copy(t, 4 + j, (*chip, 1 - c), me).wait_recv()
        for cp in sends:
            cp.wait_send()
        for cp in local:
            cp.wait()

    any_spec = pl.BlockSpec(memory_space=pl.ANY)
    return pl.pallas_call(
        body, name=name,
        out_shape=[jax.ShapeDtypeStruct((N_DEV,) + st.shape[1:], st.dtype) for st in stacks],
        in_specs=[any_spec] * n_t, out_specs=[any_spec] * n_t,
        scratch_shapes=[pltpu.SemaphoreType.DMA((n_t, N_DEV - 1)), pltpu.SemaphoreType.DMA((n_t, N_DEV - 1)), pltpu.SemaphoreType.DMA((n_t,))],
    )(*stacks)


class _ScatterSide:
    def __init__(self, grads):
        self.grads = list(grads)
        self.n = len(self.grads)

    def out_shape(self):
        return [jax.ShapeDtypeStruct(g.shape, g.dtype) for g in self.grads]

    def scratch(self):
        return [pltpu.SemaphoreType.DMA((self.n, N_DEV - 1)), pltpu.SemaphoreType.DMA((self.n, N_DEV - 1)), pltpu.SemaphoreType.DMA((self.n,))]

    def _copies(self, srcs, outs, send_sems, recv_sems, local_sems):
        me = _block_of(_my_place())
        local, sends, arrivals = [], [], []
        for t in range(self.n):
            local.append(pltpu.make_async_copy(srcs[t].at[me], outs[t].at[me], local_sems.at[t]))
            for k in range(1, N_DEV):
                sems = dict(send_sem=send_sems.at[t, k - 1], recv_sem=recv_sems.at[t, k - 1], device_id=_peer(k), device_id_type=MESH)
                sends.append(pltpu.make_async_remote_copy(src_ref=srcs[t].at[_block_of(_peer(k))], dst_ref=outs[t].at[me], **sems))
                slot = outs[t].at[_block_of(_peer(k))]
                arrivals.append(pltpu.make_async_remote_copy(src_ref=slot, dst_ref=slot, **sems))
        return local, sends, arrivals

    def start(self, *refs):
        local, sends, _ = self._copies(*refs)
        for cp in local + sends:
            cp.start()

    def finish(self, *refs):
        local, sends, arrivals = self._copies(*refs)
        for cp in arrivals:
            cp.wait_recv()
        for cp in sends:
            cp.wait_send()
        for cp in local:
            cp.wait()


def _scatter_grads(name, grads):
    side = _ScatterSide(grads)
    n_t = side.n

    def body(*refs):
        parts = (refs[:n_t], refs[n_t:2 * n_t]) + tuple(refs[2 * n_t:])
        side.start(*parts)
        side.finish(*parts)

    any_spec = pl.BlockSpec(memory_space=pl.ANY)
    return pl.pallas_call(
        body, name=name, out_shape=side.out_shape(), in_specs=[any_spec] * n_t, out_specs=[any_spec] * n_t,
        scratch_shapes=side.scratch(),
    )(*grads)


def _host_call(body, args, side, *, grid, in_specs, out_specs, out_shape, scratch_shapes=(), **kw):
    if side is None:
        return pl.pallas_call(body, grid=grid, in_specs=in_specs, out_specs=out_specs, out_shape=out_shape,
                              scratch_shapes=list(scratch_shapes), **kw)(*args), None
    n_in, n_out, n_scr, n_side = len(in_specs), len(out_specs), len(scratch_shapes), side.n

    def hosted(*refs):
        cuts = [n_in, n_side, n_out, n_side, n_scr, 3]
        groups, at = [], 0
        for c in cuts:
            groups.append(refs[at:at + c])
            at += c
        ins, side_in, outs, side_out, scr, sems = groups
        first, last = None, None
        for axis, size in enumerate(grid):
            at_start, at_end = pl.program_id(axis) == 0, pl.program_id(axis) == size - 1
            first = at_start if first is None else jnp.logical_and(first, at_start)
            last = at_end if last is None else jnp.logical_and(last, at_end)

        @pl.when(first)
        def _():
            side.start(side_in, side_out, *sems)

        body(*ins, *outs, *scr)

        @pl.when(last)
        def _():
            side.finish(side_in, side_out, *sems)

    any_spec = pl.BlockSpec(memory_space=pl.ANY)
    res = pl.pallas_call(
        hosted, grid=grid, in_specs=list(in_specs) + [any_spec] * n_side, out_specs=list(out_specs) + [any_spec] * n_side,
        out_shape=list(out_shape) + side.out_shape(), scratch_shapes=list(scratch_shapes) + side.scratch(), **kw,
    )(*args, *side.grads)
    return res[:n_out], res[n_out:]


def _ffn_forward(x, g, wt, wd):
    h, gate, up, act = _ffn_up(x, g, wt)
    return _ffn_down(act, wd, x), (x, g, h, gate, up, act)


def _ffn_backward(dx, saved, wt, wd, carried):
    x, g, h, gate, up, act = saved
    sides = {k: _ScatterSide(v) if v else None for k, v in carried.items()}
    parts = {}
    (dgate, dup), parts["act"] = _ffn_bwd_act(dx, wd, gate, up, sides.get("act"))
    d_wd = _grad_ffn_down(act, dx)
    (dx_in, dg), parts["dh"] = _ffn_dh(dgate, dup, wt, x, g, dx, sides.get("dh"))
    d_wt, parts["gu"] = _grad_ffn_gu(h, dgate, dup, sides.get("gu"))
    return dx_in, dg, d_wt.reshape(N_DEV, -1, D_MODEL), d_wd.reshape(N_DEV, -1, D_MODEL), parts


def kernel(x, p, norm_ffn1, w_ffn1_gu, w_ffn1_down, norm_mix, w_qkv, q_norm, k_norm, w_o, w_pool_in, w_pool_grp, pool_scale, norm_ffn2, w_ffn2_gu, w_ffn2_down, norm_ple, w_ple_gate, w_ple_proj, loss_target, m_norm_ffn1, m_w_ffn1_gu, m_w_ffn1_down, m_norm_mix, m_w_qkv, m_q_norm, m_k_norm, m_w_o, m_w_pool_in, m_w_pool_grp, m_pool_scale, m_norm_ffn2, m_w_ffn2_gu, m_w_ffn2_down, m_norm_ple, m_w_ple_gate, m_w_ple_proj, v_norm_ffn1, v_w_ffn1_gu, v_w_ffn1_down, v_norm_mix, v_w_qkv, v_q_norm, v_k_norm, v_w_o, v_w_pool_in, v_w_pool_grp, v_pool_scale, v_norm_ffn2, v_w_ffn2_gu, v_w_ffn2_down, v_norm_ple, v_w_ple_gate, v_w_ple_proj):
    d = D_MODEL
    xs = x[0]
    target = loss_target[0]
    me = _block_of(_my_place())

    def tr(w):
        return jnp.swapaxes(w, 1, 2)

    big = dict(w_ffn1_gu=tr(w_ffn1_gu), w_ffn1_down=w_ffn1_down, w_qkv=w_qkv, w_o=w_o, w_pool_in=w_pool_in,
               w_pool_grp=w_pool_grp.reshape(2, 4 * 32, POOL_GROUP), w_ffn2_gu=tr(w_ffn2_gu), w_ffn2_down=w_ffn2_down,
               w_ple_gate=w_ple_gate, w_ple_proj=w_ple_proj)
    moments = dict(
        w_ffn1_gu=(tr(m_w_ffn1_gu), tr(v_w_ffn1_gu)), w_ffn1_down=(m_w_ffn1_down, v_w_ffn1_down), w_qkv=(m_w_qkv, v_w_qkv),
        w_o=(m_w_o, v_w_o), w_pool_in=(m_w_pool_in, v_w_pool_in),
        w_pool_grp=(m_w_pool_grp.reshape(2, 4 * 32, POOL_GROUP), v_w_pool_grp.reshape(2, 4 * 32, POOL_GROUP)),
        w_ffn2_gu=(tr(m_w_ffn2_gu), tr(v_w_ffn2_gu)), w_ffn2_down=(m_w_ffn2_down, v_w_ffn2_down),
        w_ple_gate=(m_w_ple_gate, v_w_ple_gate), w_ple_proj=(m_w_ple_proj, v_w_ple_proj))
    half = {name: _cast_bf16(w) for name, w in big.items()}

    def layer_names(i):
        mixer = ["w_qkv", "w_o"] if i % 2 == 0 else ["w_pool_in", "w_pool_grp"]
        return ["w_ffn1_gu", "w_ffn1_down"] + mixer + ["w_ffn2_gu", "w_ffn2_down", "w_ple_gate", "w_ple_proj"]

    def layer_index(name, i):
        return i // 2 if name in ("w_qkv", "w_o", "w_pool_in", "w_pool_grp") else i

    scale_all = _gather_small(jnp.pad(pool_scale, ((0, 6), (0, 0))))
    scale_full = scale_all.reshape(N_DEV, 8, 128)[:, :2].transpose(1, 0, 2).reshape(2, d)

    saved = []
    weights = []
    cur = xs
    for i in range(DEPTH):
        names = layer_names(i)
        gathered = _gather_weights(f"gather_weights_{i}", [half[n] for n in names], [layer_index(n, i) for n in names])
        wl = dict(zip(names, gathered))
        for n in ("w_ffn1_gu", "w_ffn2_gu", "w_ffn1_down", "w_ffn2_down", "w_o", "w_pool_in", "w_ple_gate"):
            if n in wl:
                wl[n] = wl[n].reshape(-1, d)
        if "w_pool_grp" in wl:
            wl["w_pool_grp"] = wl["w_pool_grp"].reshape(N_DEV, 4, 32, POOL_GROUP).transpose(1, 0, 2, 3).reshape(4, POOL_GROUP, POOL_GROUP)
        for n in ("w_qkv", "w_ple_proj"):
            if n in wl:
                wl[n] = _from_column_shards(wl[n])
        weights.append(wl)
        j = i // 2
        rec = {}
        cur, rec["ffn1"] = _ffn_forward(cur, norm_ffn1[i][None], wl["w_ffn1_gu"], wl["w_ffn1_down"])
        x1 = cur
        gm = norm_mix[i][None]
        if i % 2 == 0:
            qg = jnp.tile(q_norm[j], d // HEAD_DIM)[None]
            kg = jnp.tile(k_norm[j], d // HEAD_DIM)[None]
            hm, qkv = _norm_matmul("qkv_proj", x1, gm, wl["w_qkv"], (d, d), lambda a, b: (0, b), _pick_all, 3 * d, d, 3)
            qn, kn, vb = _qk_norm(qkv, qg, kg)
            o = _attention(qn, kn, vb)
            cur = _matmul_res("attn_out", o, wl["w_o"], x1, 1.0)
            rec["mix"] = (x1, gm, hm, qkv, qg, kg, qn, kn, vb, o)
        else:
            sc = scale_full[j][None]
            hm, u = _norm_matmul("pool_in", x1, gm, wl["w_pool_in"], (d, d), lambda a, b: (0, 0), _pick_all, d, d, 1)
            cur, pooled = _pool_fwd(u, wl["w_pool_grp"], sc, x1)
            rec["mix"] = (x1, gm, hm, sc, pooled)
        cur, rec["ffn2"] = _ffn_forward(cur, norm_ffn2[i][None], wl["w_ffn2_gu"], wl["w_ffn2_down"])
        x3 = cur
        gp = norm_ple[i][None]
        cur, hp, sig, proj = _ple_fwd(x3, gp, wl["w_ple_gate"], p[i, 0], wl["w_ple_proj"])
        rec["ple"] = (x3, gp, hp, sig, proj)
        saved.append(rec)

    dy, loss_part = _loss_head(cur, target)
    loss = lax.psum(loss_part[0, 0], ("x", "y", "c"))

    small = {n: [None] * DEPTH for n in ("norm_ffn1", "norm_mix", "norm_ffn2", "norm_ple")}
    small.update(q_norm=[None] * 2, k_norm=[None] * 2, pool_scale=[None] * 2)
    results = {name: None for name in big}

    def update(layer, parts):
        for n, part in parts.items():
            mm, vv = moments[n]
            results[n] = _adamw_layer(part, big[n], mm, vv, layer_index(n, layer), results[n])

    carriers = {
        ("ffn2", "gu"): ["w_ffn1_gu"], ("ffn1", "gu"): ["w_ffn2_gu"],
        ("ffn2", "dh"): ["w_ffn1_down", "w_ple_gate", "w_ple_proj"],
        ("ffn1", "dh"): ["w_ffn2_down", "w_o", "w_pool_in", "w_pool_grp"],
        ("ffn2", "act"): ["w_qkv"],
    }
    above = {}

    def carried(ffn):
        return {host: [above[n] for n in names if n in above] for (f, host), names in carriers.items() if f == ffn}

    def received(ffn, parts, into):
        for (f, host), names in carriers.items():
            if f == ffn and parts.get(host) is not None:
                into.update(zip([n for n in names if n in above], parts[host]))

    dcur = dy
    for i in reversed(range(DEPTH)):
        wl, rec, j = weights[i], saved[i], i // 2
        grads = {}
        arrived = {}
        x3, gp, hp, sig, proj = rec["ple"]
        dgate, dproj = _ple_bwd_gate(dcur, sig, proj)
        dx3, small["norm_ple"][i] = _square_dh("ple_dh", dgate, wl["w_ple_gate"], x3, gp, dcur)
        grads["w_ple_gate"] = _grad_square("grad_ple_gate", hp, dgate).reshape(N_DEV, d // N_DEV, d)
        grads["w_ple_proj"] = _column_shards(_grad_square("grad_ple_proj", p[i, 0], dproj))
        dx2, small["norm_ffn2"][i], grads["w_ffn2_gu"], grads["w_ffn2_down"], parts = _ffn_backward(
            dx3, rec["ffn2"], wl["w_ffn2_gu"], wl["w_ffn2_down"], carried("ffn2"))
        received("ffn2", parts, arrived)
        if i % 2 == 0:
            x1, gm, hm, qkv, qg, kg, qn, kn, vb, o = rec["mix"]
            do = _matmul_nt("attn_out_bwd", dx2, wl["w_o"])
            grads["w_o"] = _grad_square("grad_attn_out", o, dx2).reshape(N_DEV, d // N_DEV, d)
            dq, dk, dv = _attention_bwd(qn, kn, vb, do)
            dqkv, dqg, dkg = _qk_norm_bwd(qkv, dq, dk, dv, qg, kg)
            small["q_norm"][j], small["k_norm"][j] = dqg[0:1], dkg[0:1]
            dx1, small["norm_mix"][i] = _qkv_dh(dqkv, wl["w_qkv"], x1, gm, dx2)
            grads["w_qkv"] = _column_shards(_grad_square("grad_qkv", hm, dqkv))
        else:
            x1, gm, hm, sc, pooled = rec["mix"]
            dpool, dgrp, small["pool_scale"][j] = _pool_bwd_group(dx2, pooled, wl["w_pool_grp"], sc)
            grads["w_pool_grp"] = dgrp.reshape(4, N_DEV, 32, POOL_GROUP).transpose(1, 0, 2, 3).reshape(N_DEV, 4 * 32, POOL_GROUP)
            du = _pool_bwd_window(dpool)
            dx1, small["norm_mix"][i] = _square_dh("pool_dh", du, wl["w_pool_in"], x1, gm, dx2)
            grads["w_pool_in"] = _grad_square("grad_pool_in", hm, du).reshape(N_DEV, d // N_DEV, d)
        dcur, small["norm_ffn1"][i], grads["w_ffn1_gu"], grads["w_ffn1_down"], parts = _ffn_backward(
            dx1, rec["ffn1"], wl["w_ffn1_gu"], wl["w_ffn1_down"], carried("ffn1"))
        received("ffn1", parts, arrived)
        if above:
            assert set(arrived) == set(above), (sorted(arrived), sorted(above))
            update(i + 1, arrived)
        above = grads

    names = layer_names(0)
    update(0, dict(zip(names, _scatter_grads("scatter_grads_0", [above[n] for n in names]))))

    def lanes(a):
        return jnp.pad(a, ((0, 0), (0, d - a.shape[-1])))

    order = [("norm_ffn1", DEPTH), ("norm_mix", DEPTH), ("norm_ffn2", DEPTH), ("norm_ple", DEPTH), ("pool_scale", 2), ("q_norm", 2), ("k_norm", 2)]
    rows = jnp.concatenate([lanes(g) for name, _ in order for g in small[name]], axis=0)
    n_rows = rows.shape[0]
    pad_rows = -n_rows % 8
    rows = jnp.pad(rows, ((0, pad_rows), (0, 0)))
    gathered = _gather_small(rows)

    def own_lanes(a):
        return lax.dynamic_update_slice(jnp.zeros((a.shape[0], d), F32), a, (0, me * 128))

    def pack(values):
        mats = [own_lanes(values[name]) if name == "pool_scale" else lanes(values[name]) for name, _ in order]
        return jnp.pad(jnp.concatenate(mats, axis=0), ((0, pad_rows), (0, 0)))

    small_w = dict(norm_ffn1=norm_ffn1, norm_mix=norm_mix, norm_ffn2=norm_ffn2, norm_ple=norm_ple, pool_scale=pool_scale, q_norm=q_norm, k_norm=k_norm)
    small_m = dict(norm_ffn1=m_norm_ffn1, norm_mix=m_norm_mix, norm_ffn2=m_norm_ffn2, norm_ple=m_norm_ple, pool_scale=m_pool_scale, q_norm=m_q_norm, k_norm=m_k_norm)
    small_v = dict(norm_ffn1=v_norm_ffn1, norm_mix=v_norm_mix, norm_ffn2=v_norm_ffn2, norm_ple=v_norm_ple, pool_scale=v_pool_scale, q_norm=v_q_norm, k_norm=v_k_norm)
    packed = _adamw_small(gathered, pack(small_w), pack(small_m), pack(small_v))

    def unpack(mat):
        out, at = {}, 0
        for name, n in order:
            blk = mat[at:at + n]
            at += n
            if name == "pool_scale":
                out[name] = lax.dynamic_slice(blk, (0, me * 128), (n, 128))
            elif name in ("q_norm", "k_norm"):
                out[name] = blk[:, :HEAD_DIM]
            else:
                out[name] = blk
        return out

    small_out = [unpack(mat) for mat in packed]

    def result(kind, name):
        if name in small_w:
            return small_out[kind][name]
        r = results[name][kind]
        if name in ("w_ffn1_gu", "w_ffn2_gu"):
            return tr(r)
        return r.reshape(w_pool_grp.shape) if name == "w_pool_grp" else r

    weight_names = ["norm_ffn1", "w_ffn1_gu", "w_ffn1_down", "norm_mix", "w_qkv", "q_norm", "k_norm", "w_o", "w_pool_in", "w_pool_grp",
                    "pool_scale", "norm_ffn2", "w_ffn2_gu", "w_ffn2_down", "norm_ple", "w_ple_gate", "w_ple_proj"]
    outs = [loss, dcur[None]]
    for kind in range(4):
        outs += [result(kind, name) for name in weight_names]
    return tuple(outs)
```

```python
import math

import jax
import jax.numpy as jnp
from jax import lax
from jax.experimental import pallas as pl
from jax.experimental.pallas import tpu as pltpu

F32 = jnp.float32
BF16 = jnp.bfloat16

N_DEV = 8
DEPTH = 4
D_MODEL = 1024
N_UNITS = D_MODEL // 128
HEAD_DIM = 64
POOL_WINDOWS = (2, 4, 8, 16)
POOL_GROUP = 256
POOL_HALO = 128
EPS = 1e-6
ATTN_SCALE = 1.0 / math.sqrt(HEAD_DIM)
ATTN_BLOCK = 256
LOG_ZERO = -104.0

ADAM_LR = 0.001
ADAM_B1 = 0.9
ADAM_B2 = 0.999
ADAM_EPS = 1e-08
ADAM_WD = 0.01
ADAM_STEP = 10
ADAMW_BLOCK_ELEMS = 128 * 1024

VMEM_LIMIT = 56 * 1024 * 1024
MESH = pl.DeviceIdType.MESH

NT_DIMS = (((1,), (1,)), ((), ()))
TN_DIMS = (((0,), (0,)), ((), ()))


def _params(*sem):
    return pltpu.CompilerParams(dimension_semantics=sem, vmem_limit_bytes=VMEM_LIMIT)


def _row_tile(s):
    return min(512, s)


def _grad_tile(s):
    return min(2048, s)


def _block_spec(block, index, mode=None):
    return pl.BlockSpec(block, index) if mode is None else pl.BlockSpec(block, index, pipeline_mode=mode)


def _dot(a, b):
    return jnp.dot(a, b, preferred_element_type=F32)


def _dot_nt(a, b):
    return lax.dot_general(a, b, NT_DIMS, preferred_element_type=F32)


def _dot_tn(a, b):
    return lax.dot_general(a, b, TN_DIMS, preferred_element_type=F32)


def _dot_f32(a, b):
    return jnp.dot(a, b, precision=lax.Precision.HIGHEST, preferred_element_type=F32)


def _rms(x, g):
    r = lax.rsqrt(jnp.mean(x * x, axis=-1, keepdims=True) + EPS)
    return x * r * g


def _rms_bwd(dy, x, g):
    r = lax.rsqrt(jnp.mean(x * x, axis=-1, keepdims=True) + EPS)
    xh = x * r
    dg = jnp.sum(dy * xh, axis=0, keepdims=True)
    dxh = dy * g
    dx = r * (dxh - xh * jnp.mean(dxh * xh, axis=-1, keepdims=True))
    return dx, dg


def _cast_bf16(w):
    l, r, c = w.shape

    def body(w_ref, o_ref):
        o_ref[...] = w_ref[...].astype(BF16)

    return pl.pallas_call(
        body, grid=(l,), name="cast_bf16",
        in_specs=[pl.BlockSpec((1, r, c), lambda i: (i, 0, 0))],
        out_specs=pl.BlockSpec((1, r, c), lambda i: (i, 0, 0)),
        out_shape=jax.ShapeDtypeStruct(w.shape, BF16),
        compiler_params=_params("arbitrary"),
    )(w)


def _ffn_up(x, g, wt, side=None):
    s, d = x.shape
    f = wt.shape[0] // 2
    tn = f // 2
    tm = _row_tile(s)

    def body(x_ref, g_ref, w_ref, h_ref, gate_ref, up_ref, act_ref):
        h = _rms(x_ref[...], g_ref[...]).astype(BF16)
        h_ref[...] = h
        for half in range(2):
            cols = slice(half * tn, (half + 1) * tn)
            gate = _dot_nt(h, w_ref[half * tn:(half + 1) * tn, :])
            up = _dot_nt(h, w_ref[f + half * tn:f + (half + 1) * tn, :])
            gate_ref[:, cols] = gate.astype(BF16)
            up_ref[:, cols] = up.astype(BF16)
            act_ref[:, cols] = (gate * jax.nn.sigmoid(gate) * up).astype(BF16)

    wide = pl.BlockSpec((tm, f), lambda i: (i, 0))
    hidden = jax.ShapeDtypeStruct((s, f), BF16)
    return _host_call(
        body, (x, g, wt), side, grid=(s // tm,), name="ffn_up",
        in_specs=[
            pl.BlockSpec((tm, d), lambda i: (i, 0)),
            pl.BlockSpec((1, d), lambda i: (0, 0)),
            _block_spec((2 * f, d), lambda i: (0, 0), pl.Buffered(1)),
        ],
        out_specs=[pl.BlockSpec((tm, d), lambda i: (i, 0)), wide, wide, wide],
        out_shape=[jax.ShapeDtypeStruct((s, d), BF16), hidden, hidden, hidden],
        compiler_params=_params("arbitrary"),
    )


def _ffn_down(act, wd, x, side=None):
    s, f = act.shape
    d = wd.shape[-1]
    tm = _row_tile(s)

    def body(a_ref, w_ref, x_ref, o_ref):
        o_ref[...] = x_ref[...] + 0.5 * _dot(a_ref[...], w_ref[...])

    row = pl.BlockSpec((tm, d), lambda i: (i, 0))
    return _host_call(
        body, (act, wd, x), side, grid=(s // tm,), name="ffn_down",
        in_specs=[pl.BlockSpec((tm, f), lambda i: (i, 0)), _block_spec((f, d), lambda i: (0, 0), pl.Buffered(1)), row],
        out_specs=[row],
        out_shape=[jax.ShapeDtypeStruct((s, d), F32)],
        compiler_params=_params("arbitrary"),
    )


def _ffn_bwd_act(dx, wd, gate, up, side=None):
    s, d = dx.shape
    f = gate.shape[-1]
    tn = f // 2
    tm = _row_tile(s)

    def body(dx_ref, w_ref, gate_ref, up_ref, dgate_ref, dup_ref):
        dact = _dot_nt((0.5 * dx_ref[...]).astype(BF16), w_ref[...])
        gate = gate_ref[...].astype(F32)
        sig = jax.nn.sigmoid(gate)
        silu = gate * sig
        dgate_ref[...] = (dact * up_ref[...].astype(F32) * (sig + silu * (1.0 - sig))).astype(BF16)
        dup_ref[...] = (dact * silu).astype(BF16)

    col = pl.BlockSpec((tm, tn), lambda j, i: (i, j))
    hidden = jax.ShapeDtypeStruct((s, f), BF16)
    return _host_call(
        body, (dx, wd, gate, up), side, grid=(2, s // tm), name="ffn_bwd_act",
        in_specs=[pl.BlockSpec((tm, d), lambda j, i: (i, 0)), pl.BlockSpec((tn, d), lambda j, i: (j, 0)), col, col],
        out_specs=[col, col],
        out_shape=[hidden, hidden],
        compiler_params=_params("arbitrary", "arbitrary"),
    )


def _norm_bwd_matmul(name, operands, products, nk, x, g, dres, w_rows_contract=False, side=None):
    s, d = x.shape
    tm = _row_tile(s)
    n = len(operands)

    def body(*refs):
        x_ref, g_ref, dres_ref, dx_ref, dg_ref, acc_ref = refs[n:]
        i, k = pl.program_id(0), pl.program_id(1)

        @pl.when(jnp.logical_and(i == 0, k == 0))
        def _():
            dg_ref[...] = jnp.zeros_like(dg_ref)

        total = None
        for a_at, a_pick, w_at, w_pick in products:
            prod = (_dot if w_rows_contract else _dot_nt)(a_pick(refs[a_at]), w_pick(refs[w_at]))
            total = prod if total is None else total + prod

        if nk > 1:
            @pl.when(k == 0)
            def _():
                acc_ref[...] = total

            @pl.when(jnp.logical_and(k > 0, k < nk - 1))
            def _():
                acc_ref[...] += total

        @pl.when(k == nk - 1)
        def _():
            dy = total if nk == 1 else acc_ref[...] + total
            dx, dg = _rms_bwd(dy, x_ref[...], g_ref[...])
            dx_ref[...] = dres_ref[...] + dx
            dg_ref[...] += dg

    row = pl.BlockSpec((tm, d), lambda i, k: (i, 0))
    vec = pl.BlockSpec((1, d), lambda i, k: (0, 0))
    return _host_call(
        body, [op[0] for op in operands] + [x, g, dres], side, grid=(s // tm, nk), name=name,
        in_specs=[_block_spec(*op[1:]) for op in operands] + [row, vec, row],
        out_specs=[row, vec],
        out_shape=[jax.ShapeDtypeStruct((s, d), F32), jax.ShapeDtypeStruct((1, d), F32)],
        scratch_shapes=[pltpu.VMEM((tm, d), F32)],
        compiler_params=_params("arbitrary", "arbitrary"),
    )


def _ffn_dh(dgate, dup, wt, x, g, dres, side=None):
    s, d = x.shape
    f = dgate.shape[-1]
    tm = _row_tile(s)
    once = pl.Buffered(1)
    operands = [
        (dgate, (tm, f), lambda i, k: (i, 0)),
        (dup, (tm, f), lambda i, k: (i, 0)),
        (wt, (f, d), lambda i, k: (0, 0), once),
        (wt, (f, d), lambda i, k: (1, 0), once),
    ]
    products = [(0, _pick_all, 2, _pick_all), (1, _pick_all, 3, _pick_all)]
    return _norm_bwd_matmul("ffn_dh", operands, products, 1, x, g, dres, w_rows_contract=True, side=side)


def _square_dh(name, a, w, x, g, dres):
    s, d = x.shape
    tm = _row_tile(s)
    operands = [(a, (tm, d), lambda i, k: (i, 0)), (w, (d, d), lambda i, k: (0, 0))]
    return _norm_bwd_matmul(name, operands, [(0, _pick_all, 1, _pick_all)], 1, x, g, dres)[0]


def _qkv_dh(dqkv, wqkv, x, g, dres):
    s, d = x.shape
    tm = _row_tile(s)
    operands = [(dqkv, (tm, d), lambda i, k: (i, k)), (wqkv, (d, d), lambda i, k: (0, k))]
    return _norm_bwd_matmul("qkv_dh", operands, [(0, _pick_all, 1, _pick_all)], 3, x, g, dres)[0]


def _grad_matmul(name, a, a_block, a_index, a_pick, b, b_block, b_index, b_picks, out_shape, out_block, out_index, out_stores, nj, scale=1.0):
    s = a.shape[-2]
    nk = s // _grad_tile(s)
    n_prod = len(b_picks)

    def body(a_ref, b_ref, o_ref, *acc_refs):
        k = pl.program_id(1)
        av = a_pick(a_ref)
        if av.dtype != BF16:
            av = (scale * av).astype(BF16)
        for b_pick, store, acc_ref in zip(b_picks, out_stores, acc_refs):
            bv = b_pick(b_ref)
            if bv.dtype != BF16:
                bv = bv.astype(BF16)
            prod = _dot_tn(av, bv)
            if nk == 1:
                store(o_ref, prod.astype(BF16))
                continue

            @pl.when(k == 0)
            def _():
                acc_ref[...] = prod

            @pl.when(jnp.logical_and(k > 0, k < nk - 1))
            def _():
                acc_ref[...] += prod

            @pl.when(k == nk - 1)
            def _():
                store(o_ref, (acc_ref[...] + prod).astype(BF16))

    m = jax.eval_shape(a_pick, jax.ShapeDtypeStruct(a_block, a.dtype)).shape[-1]
    nn = jax.eval_shape(b_picks[0], jax.ShapeDtypeStruct(b_block, b.dtype)).shape[-1]
    acc_shape = (m, nn)
    return pl.pallas_call(
        body, grid=(nj, nk), name=name,
        in_specs=[pl.BlockSpec(a_block, a_index), pl.BlockSpec(b_block, b_index)],
        out_specs=pl.BlockSpec(out_block, out_index),
        out_shape=jax.ShapeDtypeStruct(out_shape, BF16),
        scratch_shapes=[pltpu.VMEM(acc_shape, F32) for _ in range(n_prod)],
        compiler_params=_params("arbitrary", "arbitrary"),
    )(a, b)


def _pick_all(r):
    return r[...]


def _store_all(r, v):
    r[...] = v


def _grad_ffn_down(act, dx):
    s, f = act.shape
    d = dx.shape[-1]
    tk = _grad_tile(s)
    tn = f // 2
    return _grad_matmul(
        "grad_ffn_down", act, (tk, tn), lambda j, k: (k, j), _pick_all,
        dx, (tk, d), lambda j, k: (k, 0), [lambda r: 0.5 * r[...]],
        (f, d), (tn, d), lambda j, k: (j, 0), [_store_all], 2)


def _grad_ffn_gu(h, dgate, dup, side=None):
    s, d = h.shape
    f = dgate.shape[-1]
    tn = f // 2
    tk = min(1024, s)
    nk = s // tk

    def body(h_ref, dgate_ref, dup_ref, o_ref, acc_ref):
        j, k = pl.program_id(0), pl.program_id(1)

        def accumulate(a_ref):
            prod = _dot_tn(a_ref[...], h_ref[...])
            if nk == 1:
                o_ref[...] = prod.astype(BF16)
                return

            @pl.when(k == 0)
            def _():
                acc_ref[...] = prod

            @pl.when(jnp.logical_and(k > 0, k < nk - 1))
            def _():
                acc_ref[...] += prod

            @pl.when(k == nk - 1)
            def _():
                o_ref[...] = (acc_ref[...] + prod).astype(BF16)

        @pl.when(j < 2)
        def _():
            accumulate(dgate_ref)

        @pl.when(j >= 2)
        def _():
            accumulate(dup_ref)

    (out,), parts = _host_call(
        body, (h, dgate, dup), side, grid=(4, nk), name="grad_ffn_gu",
        in_specs=[
            pl.BlockSpec((tk, d), lambda j, k: (k, 0)),
            pl.BlockSpec((tk, tn), lambda j, k: (jnp.where(j < 2, k, 0), jnp.minimum(j, 1))),
            pl.BlockSpec((tk, tn), lambda j, k: (jnp.where(j >= 2, k, 0), jnp.maximum(j - 2, 0))),
        ],
        out_specs=[pl.BlockSpec((tn, d), lambda j, k: (j, 0))],
        out_shape=[jax.ShapeDtypeStruct((2 * f, d), BF16)],
        scratch_shapes=[pltpu.VMEM((tn, d), F32)],
        compiler_params=_params("arbitrary", "arbitrary"),
    )
    return out, parts


def _grad_square(name, a, b):
    s, m = a.shape
    n = b.shape[-1]
    tk = _grad_tile(s)
    tn = min(n, D_MODEL)
    return _grad_matmul(
        name, a, (tk, m), lambda j, k: (k, 0), _pick_all,
        b, (tk, tn), lambda j, k: (k, j), [_pick_all],
        (m, n), (m, tn), lambda j, k: (0, j), [_store_all], n // tn)


def _column_shards(w):
    m, n = w.shape
    return w.reshape(m, N_DEV, n // N_DEV).transpose(1, 0, 2)


def _from_column_shards(w):
    nd, m, n = w.shape
    return w.transpose(1, 0, 2).reshape(m, nd * n)


def _norm_matmul(name, x, g, w, w_block, w_index, w_pick, n_total, tn, nj):
    s, d = x.shape
    tm = _row_tile(s)

    def body(x_ref, g_ref, w_ref, h_ref, y_ref):
        @pl.when(pl.program_id(1) == 0)
        def _():
            h_ref[...] = _rms(x_ref[...], g_ref[...]).astype(BF16)

        y_ref[...] = _dot(h_ref[...], w_pick(w_ref))

    return pl.pallas_call(
        body, grid=(s // tm, nj), name=name,
        in_specs=[
            pl.BlockSpec((tm, d), lambda i, j: (i, 0)),
            pl.BlockSpec((1, d), lambda i, j: (0, 0)),
            pl.BlockSpec(w_block, w_index),
        ],
        out_specs=[pl.BlockSpec((tm, d), lambda i, j: (i, 0)), pl.BlockSpec((tm, tn), lambda i, j: (i, j))],
        out_shape=[jax.ShapeDtypeStruct((s, d), BF16), jax.ShapeDtypeStruct((s, n_total), F32)],
        compiler_params=_params("arbitrary", "arbitrary"),
    )(x, g, w)


def _head_mean_matrix():
    r = lax.broadcasted_iota(jnp.int32, (128, 128), 0) // HEAD_DIM
    c = lax.broadcasted_iota(jnp.int32, (128, 128), 1) // HEAD_DIM
    return jnp.where(r == c, 1.0 / HEAD_DIM, 0.0).astype(F32)


def _qk_norm(qkv, qg, kg):
    s = qkv.shape[0]
    d = D_MODEL
    tm = _row_tile(s)

    def body(q_ref, k_ref, v_ref, qg_ref, kg_ref, qo_ref, ko_ref, vo_ref):
        mean_m = _head_mean_matrix()
        for u in range(N_UNITS):
            cols = slice(128 * u, 128 * (u + 1))
            for src, gain, dst, scale in ((q_ref, qg_ref, qo_ref, ATTN_SCALE), (k_ref, kg_ref, ko_ref, 1.0)):
                xs = src[:, cols]
                r = lax.rsqrt(_dot_f32(xs * xs, mean_m) + EPS)
                y = xs * r * gain[:, cols]
                dst[:, cols] = (y * scale).astype(BF16) if scale != 1.0 else y.astype(BF16)
        vo_ref[...] = v_ref[...].astype(BF16)

    blk = lambda c: pl.BlockSpec((tm, d), lambda i: (i, c))
    vec = pl.BlockSpec((1, d), lambda i: (0, 0))
    return pl.pallas_call(
        body, grid=(s // tm,), name="qk_norm",
        in_specs=[blk(0), blk(1), blk(2), vec, vec],
        out_specs=[blk(0)] * 3,
        out_shape=[jax.ShapeDtypeStruct((s, d), BF16)] * 3,
        compiler_params=_params("arbitrary"),
    )(qkv, qkv, qkv, qg, kg)


def _qk_norm_bwd(qkv, dq, dk, dv, qg, kg):
    s = qkv.shape[0]
    d = D_MODEL
    tm = _row_tile(s)
    nsteps = s // tm

    def body(q_ref, k_ref, dq_ref, dk_ref, dv_ref, qg_ref, kg_ref, o_ref, dqg_ref, dkg_ref, acc_ref):
        i = pl.program_id(0)

        @pl.when(i == 0)
        def _():
            acc_ref[...] = jnp.zeros_like(acc_ref)

        mean_m = _head_mean_matrix()
        for u in range(N_UNITS):
            cols = slice(128 * u, 128 * (u + 1))
            for n, (src, dsrc, gain) in enumerate(((q_ref, dq_ref, qg_ref), (k_ref, dk_ref, kg_ref))):
                xs = src[:, cols]
                dy = dsrc[:, cols]
                r = lax.rsqrt(_dot_f32(xs * xs, mean_m) + EPS)
                xh = xs * r
                acc_ref[n:n + 1, :] += jnp.sum(dy * xh, axis=0, keepdims=True)
                dxh = dy * gain[:, cols]
                dx = r * (dxh - xh * _dot_f32(dxh * xh, mean_m))
                o_ref[:, 128 * (N_UNITS * n + u):128 * (N_UNITS * n + u + 1)] = dx.astype(BF16)
        o_ref[:, 2 * d:3 * d] = dv_ref[...].astype(BF16)

        @pl.when(i == nsteps - 1)
        def _():
            r = lax.broadcasted_iota(jnp.int32, (128, 128), 0) % HEAD_DIM
            c = lax.broadcasted_iota(jnp.int32, (128, 128), 1) % HEAD_DIM
            fold = jnp.where(r == c, 1.0, 0.0).astype(F32)
            folded = _dot_f32(acc_ref[...], fold)
            dqg_ref[...] = jnp.broadcast_to(folded[0:1], (8, 128))
            dkg_ref[...] = jnp.broadcast_to(folded[1:2], (8, 128))

    blk = lambda c: pl.BlockSpec((tm, d), lambda i: (i, c))
    row = pl.BlockSpec((tm, d), lambda i: (i, 0))
    vec = pl.BlockSpec((1, d), lambda i: (0, 0))
    small = pl.BlockSpec((8, 128), lambda i: (0, 0))
    return pl.pallas_call(
        body, grid=(nsteps,), name="qk_norm_bwd",
        in_specs=[blk(0), blk(1), row, row, row, vec, vec],
        out_specs=[pl.BlockSpec((tm, 3 * d), lambda i: (i, 0)), small, small],
        out_shape=[jax.ShapeDtypeStruct((s, 3 * d), BF16), jax.ShapeDtypeStruct((8, 128), F32), jax.ShapeDtypeStruct((8, 128), F32)],
        scratch_shapes=[pltpu.VMEM((8, 128), F32)],
        compiler_params=_params("arbitrary"),
    )(qkv, qkv, dq, dk, dv, qg, kg)


def _split_dot(x, m):
    hi = x.astype(BF16)
    lo = (x - hi.astype(F32)).astype(BF16)
    return _dot(hi, m) + _dot(lo, m)


def _stack_heads(x):
    lane = lax.broadcasted_iota(jnp.int32, x.shape, 1)
    zero = jnp.zeros_like(x)
    return jnp.concatenate([jnp.where(lane < HEAD_DIM, x, zero), jnp.where(lane < HEAD_DIM, zero, x)], axis=0)


def _unstack_heads(x2, t):
    lane = lax.broadcasted_iota(jnp.int32, (t, 128), 1)
    return jnp.where(lane < HEAD_DIM, x2[:t], x2[t:])


def _attn_masks(t):
    r = lax.broadcasted_iota(jnp.int32, (t, t), 0)
    c = lax.broadcasted_iota(jnp.int32, (t, t), 1)
    row = lax.broadcasted_iota(jnp.int32, (2 * t, t), 0)
    col = lax.broadcasted_iota(jnp.int32, (2 * t, t), 1)
    causal = col < jnp.where(row >= t, row - t, row)
    return (r > c).astype(BF16), (r >= c).astype(BF16), causal


def _attn_sweep_cond(st):
    return jnp.logical_and(st[0] >= 0, st[1] > LOG_ZERO)


def _attn_scores(q2, kblk, after, causal):
    z = _dot_nt(q2, kblk)
    sp = jnp.maximum(z, 0.0) + jnp.log(1.0 + jnp.exp(-jnp.abs(z)))
    log_stay = -sp
    if causal is not None:
        log_stay = jnp.where(causal, log_stay, 0.0)
    return log_stay, z - sp, _split_dot(log_stay, after)


def _attention(q, k, v):
    s, d = q.shape
    t = min(ATTN_BLOCK, s)

    def body(q_ref, k_ref, v_ref, o_ref):
        i = pl.program_id(1)
        after, _, causal = _attn_masks(t)
        q2 = _stack_heads(q_ref[...])

        def step(kb, carry, acc, diag):
            start = pl.multiple_of(kb * t, t)
            kblk = k_ref[pl.ds(start, t), :]
            vblk = v_ref[pl.ds(start, t), :]
            log_stay, log_beta, later = _attn_scores(q2, kblk, after, causal if diag else None)
            w = jnp.exp(log_beta + later + carry)
            if diag:
                w = jnp.where(causal, w, 0.0)
            return carry + jnp.sum(log_stay, axis=1, keepdims=True), acc + _split_dot(w, vblk)

        carry, acc = step(i, jnp.zeros((2 * t, 1), F32), jnp.zeros((2 * t, 128), F32), True)

        def loop(st):
            c, a = step(st[0], st[2], st[3], False)
            return st[0] - 1, jnp.max(c), c, a

        acc = lax.while_loop(_attn_sweep_cond, loop, (i - 1, jnp.max(carry), carry, acc))[3]
        o_ref[...] = _unstack_heads(acc, t)

    return pl.pallas_call(
        body, grid=(N_UNITS, s // t), name="attention",
        in_specs=[
            pl.BlockSpec((t, 128), lambda h, i: (i, h)),
            pl.BlockSpec((s, 128), lambda h, i: (0, h)),
            pl.BlockSpec((s, 128), lambda h, i: (0, h)),
        ],
        out_specs=pl.BlockSpec((t, 128), lambda h, i: (i, h)),
        out_shape=jax.ShapeDtypeStruct((s, d), F32),
        compiler_params=_params("arbitrary", "arbitrary"),
    )(q, k, v)


def _attention_bwd(q, k, v, do, o):
    s, d = q.shape
    t = min(ATTN_BLOCK, s)

    def body(q_ref, k_ref, v_ref, do_ref, o_ref, dq_ref, dk_ref, dv_ref):
        i = pl.program_id(1)

        @pl.when(i == 0)
        def _():
            dk_ref[...] = jnp.zeros_like(dk_ref)
            dv_ref[...] = jnp.zeros_like(dv_ref)

        after, from_here, causal = _attn_masks(t)
        q2 = _stack_heads(q_ref[...])
        do2 = _stack_heads(do_ref[...])

        def weights(kb, carry, diag):
            start = pl.multiple_of(kb * t, t)
            kblk = k_ref[pl.ds(start, t), :]
            vblk = v_ref[pl.ds(start, t), :]
            log_stay, log_beta, later = _attn_scores(q2, kblk, after, causal if diag else None)
            w = jnp.exp(log_beta + later + carry)
            if diag:
                w = jnp.where(causal, w, 0.0)
            g = w * _dot_nt(do2, vblk)
            return start, kblk, log_stay, log_beta, w, g

        lane = lax.broadcasted_iota(jnp.int32, (t, 128), 1)
        prod = do_ref[...].astype(F32) * o_ref[...]
        total = jnp.concatenate([
            jnp.sum(jnp.where(lane < HEAD_DIM, prod, 0.0), axis=1, keepdims=True),
            jnp.sum(jnp.where(lane < HEAD_DIM, 0.0, prod), axis=1, keepdims=True)], axis=0)
        zero = jnp.zeros((2 * t, 1), F32)

        def grad_step(kb, carry, seen, dq, diag):
            start, kblk, log_stay, log_beta, w, g = weights(kb, carry, diag)
            before = total - (_split_dot(g, from_here) + seen)
            beta = jnp.exp(log_beta)
            da = g * (1.0 - beta) - before * beta
            if diag:
                da = jnp.where(causal, da, 0.0)
            dab = da.astype(BF16)
            dk_ref[pl.ds(start, t), :] += _dot_tn(dab, q2)
            dv_ref[pl.ds(start, t), :] += _dot_tn(w.astype(BF16), do2)
            return (carry + jnp.sum(log_stay, axis=1, keepdims=True), seen + jnp.sum(g, axis=1, keepdims=True),
                    dq + _dot(dab, kblk))

        carry, seen, dq = grad_step(i, zero, zero, jnp.zeros((2 * t, 128), F32), True)

        def grad_loop(st):
            c, sn, a = grad_step(st[0], st[2], st[3], st[4], False)
            return st[0] - 1, jnp.max(c), c, sn, a

        dq = lax.while_loop(_attn_sweep_cond, grad_loop, (i - 1, jnp.max(carry), carry, seen, dq))[4]
        dq_ref[...] = ATTN_SCALE * _unstack_heads(dq, t)

    blk = pl.BlockSpec((t, 128), lambda h, i: (i, h))
    full = pl.BlockSpec((s, 128), lambda h, i: (0, h))
    return pl.pallas_call(
        body, grid=(N_UNITS, s // t), name="attention_bwd",
        in_specs=[blk, full, full, blk, blk],
        out_specs=[blk, full, full],
        out_shape=[jax.ShapeDtypeStruct((s, d), F32)] * 3,
        compiler_params=_params("arbitrary", "arbitrary"),
    )(q, k, v, do, o)


def _matmul_res(name, a, w, x, alpha):
    s, kd = a.shape
    d = w.shape[-1]
    tm = _row_tile(s)

    def body(a_ref, w_ref, x_ref, o_ref):
        o_ref[...] = x_ref[...] + alpha * _dot(a_ref[...].astype(BF16), w_ref[...])

    return pl.pallas_call(
        body, grid=(s // tm,), name=name,
        in_specs=[pl.BlockSpec((tm, kd), lambda i: (i, 0)), pl.BlockSpec((kd, d), lambda i: (0, 0)), pl.BlockSpec((tm, d), lambda i: (i, 0))],
        out_specs=pl.BlockSpec((tm, d), lambda i: (i, 0)),
        out_shape=jax.ShapeDtypeStruct((s, d), F32),
        compiler_params=_params("arbitrary"),
    )(a, w, x)


def _matmul_nt(name, a, w):
    s, n = a.shape
    kd = w.shape[0]
    tm = _row_tile(s)

    def body(a_ref, w_ref, o_ref):
        o_ref[...] = _dot_nt(a_ref[...].astype(BF16), w_ref[...]).astype(BF16)

    return pl.pallas_call(
        body, grid=(s // tm,), name=name,
        in_specs=[pl.BlockSpec((tm, n), lambda i: (i, 0)), pl.BlockSpec((kd, n), lambda i: (0, 0))],
        out_specs=pl.BlockSpec((tm, kd), lambda i: (i, 0)),
        out_shape=jax.ShapeDtypeStruct((s, kd), BF16),
        compiler_params=_params("arbitrary"),
    )(a, w)


def _pool_matrix(rows0, cols0, nr, nc, window, transpose):
    r = rows0 + lax.broadcasted_iota(jnp.int32, (nr, nc), 0)
    c = cols0 + lax.broadcasted_iota(jnp.int32, (nr, nc), 1)
    tt, ss = (c, r) if transpose else (r, c)
    inside = jnp.logical_and(tt - ss >= 0, tt - ss < window)
    cnt = jnp.minimum(tt + 1, window).astype(F32)
    return jnp.where(inside, 1.0 / cnt, 0.0) - jnp.where(tt == ss, 1.0, 0.0)


def _pool_tile(s):
    return min(256, s)


def _pool_fwd(u, wgrp, scale, x):
    s, d = u.shape
    tm = _pool_tile(s)
    halo = min(POOL_HALO, tm)
    ratio = tm // halo

    def body(u_ref, prev_ref, w_ref, sc_ref, x_ref, o_ref, p_ref):
        i = pl.program_id(0)
        t0 = i * tm
        for gi, window in enumerate(POOL_WINDOWS):
            cols = slice(POOL_GROUP * gi, POOL_GROUP * (gi + 1))
            pooled = _dot_f32(_pool_matrix(t0, t0, tm, tm, window, False), u_ref[:, cols])
            prev = jnp.where(i > 0, prev_ref[:, cols], 0.0)
            pooled += _dot_f32(_pool_matrix(t0, t0 - halo, tm, halo, window, False), prev)
            pb = pooled.astype(BF16)
            p_ref[:, cols] = pb
            o_ref[:, cols] = x_ref[:, cols] + _dot(pb, w_ref[gi]) * sc_ref[:, cols]

    row = pl.BlockSpec((tm, d), lambda i: (i, 0))
    return pl.pallas_call(
        body, grid=(s // tm,), name="pool_fwd",
        in_specs=[
            row,
            pl.BlockSpec((halo, d), lambda i: (jnp.maximum(i * ratio - 1, 0), 0)),
            pl.BlockSpec((4, POOL_GROUP, POOL_GROUP), lambda i: (0, 0, 0)),
            pl.BlockSpec((1, d), lambda i: (0, 0)),
            row,
        ],
        out_specs=[row, row],
        out_shape=[jax.ShapeDtypeStruct((s, d), F32), jax.ShapeDtypeStruct((s, d), BF16)],
        compiler_params=_params("arbitrary"),
    )(u, u, wgrp, scale, x)


def _pool_bwd_group(dx, pooled, wgrp, scale):
    s, d = dx.shape
    tm = _pool_tile(s)
    nsteps = s // tm

    def body(dx_ref, p_ref, w_ref, sc_ref, dp_ref, dw_ref, dsc_ref, acc_ref):
        i = pl.program_id(0)

        @pl.when(i == 0)
        def _():
            acc_ref[...] = jnp.zeros_like(acc_ref)
            dsc_ref[...] = jnp.zeros_like(dsc_ref)

        for gi in range(len(POOL_WINDOWS)):
            cols = slice(POOL_GROUP * gi, POOL_GROUP * (gi + 1))
            pb = p_ref[:, cols]
            dxg = dx_ref[:, cols]
            y = _dot(pb, w_ref[gi])
            dsc_ref[:, cols] += jnp.sum(dxg * y, axis=0, keepdims=True)
            dyb = (dxg * sc_ref[:, cols]).astype(BF16)
            dp_ref[:, cols] = _dot_nt(dyb, w_ref[gi])
            acc_ref[gi] += _dot_tn(pb, dyb)

        @pl.when(i == nsteps - 1)
        def _():
            dw_ref[...] = acc_ref[...].astype(BF16)

    row = pl.BlockSpec((tm, d), lambda i: (i, 0))
    grp = pl.BlockSpec((4, POOL_GROUP, POOL_GROUP), lambda i: (0, 0, 0))
    vec = pl.BlockSpec((1, d), lambda i: (0, 0))
    return pl.pallas_call(
        body, grid=(nsteps,), name="pool_bwd_group",
        in_specs=[row, row, grp, vec],
        out_specs=[row, grp, vec],
        out_shape=[jax.ShapeDtypeStruct((s, d), F32), jax.ShapeDtypeStruct((4, POOL_GROUP, POOL_GROUP), BF16), jax.ShapeDtypeStruct((1, d), F32)],
        scratch_shapes=[pltpu.VMEM((4, POOL_GROUP, POOL_GROUP), F32)],
        compiler_params=_params("arbitrary"),
    )(dx, pooled, wgrp, scale)


def _pool_bwd_window(dp):
    s, d = dp.shape
    tm = _pool_tile(s)
    halo = min(POOL_HALO, tm)
    ratio = tm // halo
    nsteps = s // tm

    def body(dp_ref, next_ref, o_ref):
        i = pl.program_id(0)
        t0 = i * tm
        for gi, window in enumerate(POOL_WINDOWS):
            cols = slice(POOL_GROUP * gi, POOL_GROUP * (gi + 1))
            du = _dot_f32(_pool_matrix(t0, t0, tm, tm, window, True), dp_ref[:, cols])
            nxt = jnp.where(i < nsteps - 1, next_ref[:, cols], 0.0)
            du += _dot_f32(_pool_matrix(t0, t0 + tm, tm, halo, window, True), nxt)
            o_ref[:, cols] = du.astype(BF16)

    row = pl.BlockSpec((tm, d), lambda i: (i, 0))
    return pl.pallas_call(
        body, grid=(nsteps,), name="pool_bwd_window",
        in_specs=[row, pl.BlockSpec((halo, d), lambda i: (jnp.minimum((i + 1) * ratio, s // halo - 1), 0))],
        out_specs=row,
        out_shape=jax.ShapeDtypeStruct((s, d), BF16),
        compiler_params=_params("arbitrary"),
    )(dp, dp)


def _ple_fwd(x, g, wgate, p, wproj, side=None):
    s, d = x.shape
    pd = p.shape[-1]
    tm = _row_tile(s)

    def body(x_ref, g_ref, wg_ref, p_ref, wp_ref, o_ref, h_ref, sig_ref, proj_ref):
        x = x_ref[...]
        h = _rms(x, g_ref[...]).astype(BF16)
        sig = jax.nn.sigmoid(_dot(h, wg_ref[...]))
        proj = _dot(p_ref[...].astype(BF16), wp_ref[...])
        o_ref[...] = x + sig * proj
        h_ref[...] = h
        sig_ref[...] = sig.astype(BF16)
        proj_ref[...] = proj.astype(BF16)

    row = pl.BlockSpec((tm, d), lambda i: (i, 0))
    return _host_call(
        body, (x, g, wgate, p, wproj), side, grid=(s // tm,), name="ple_fwd",
        in_specs=[
            row,
            pl.BlockSpec((1, d), lambda i: (0, 0)),
            pl.BlockSpec((d, d), lambda i: (0, 0)),
            pl.BlockSpec((tm, pd), lambda i: (i, 0)),
            pl.BlockSpec((pd, d), lambda i: (0, 0)),
        ],
        out_specs=[row] * 4,
        out_shape=[jax.ShapeDtypeStruct((s, d), F32)] + [jax.ShapeDtypeStruct((s, d), BF16)] * 3,
        compiler_params=_params("arbitrary"),
    )


def _ple_bwd_gate(dx, sig, proj):
    s, d = dx.shape
    tm = _row_tile(s)

    def body(dx_ref, sig_ref, proj_ref, dg_ref, dp_ref):
        dx = dx_ref[...]
        sig = sig_ref[...].astype(F32)
        dg_ref[...] = (dx * proj_ref[...].astype(F32) * (sig * (1.0 - sig))).astype(BF16)
        dp_ref[...] = (dx * sig).astype(BF16)

    row = pl.BlockSpec((tm, d), lambda i: (i, 0))
    return pl.pallas_call(
        body, grid=(s // tm,), name="ple_bwd_gate",
        in_specs=[row, row, row], out_specs=[row, row],
        out_shape=[jax.ShapeDtypeStruct((s, d), BF16)] * 2,
        compiler_params=_params("arbitrary"),
    )(dx, sig, proj)


def _loss_head(y, target):
    s, d = y.shape
    tm = _row_tile(s)
    nsteps = s // tm

    def body(y_ref, t_ref, dy_ref, loss_ref, acc_ref):
        i = pl.program_id(0)

        @pl.when(i == 0)
        def _():
            acc_ref[...] = jnp.zeros_like(acc_ref)

        err = y_ref[...] - t_ref[...]
        dy_ref[...] = err * (1.0 / d)
        acc_ref[...] += jnp.sum(jnp.mean(err * err, axis=-1, keepdims=True), axis=0, keepdims=True)

        @pl.when(i == nsteps - 1)
        def _():
            loss_ref[...] = 0.5 * acc_ref[...]

    row = pl.BlockSpec((tm, d), lambda i: (i, 0))
    return pl.pallas_call(
        body, grid=(nsteps,), name="loss_head",
        in_specs=[row, row], out_specs=[row, pl.BlockSpec((8, 128), lambda i: (0, 0))],
        out_shape=[jax.ShapeDtypeStruct((s, d), F32), jax.ShapeDtypeStruct((8, 128), F32)],
        scratch_shapes=[pltpu.VMEM((8, 128), F32)],
        compiler_params=_params("arbitrary"),
    )(y, target)


def _adamw_math(w, g, m, v):
    m = ADAM_B1 * m + (1.0 - ADAM_B1) * g
    v = ADAM_B2 * v + (1.0 - ADAM_B2) * (g * g)
    m_hat = m / (1.0 - ADAM_B1 ** ADAM_STEP)
    v_hat = v / (1.0 - ADAM_B2 ** ADAM_STEP)
    delta = -ADAM_LR * (m_hat / (jnp.sqrt(v_hat) + ADAM_EPS) + ADAM_WD * w)
    return delta, m, v


def _adamw_layer(parts, w, m, v, layer, outs):
    nl, r, c = w.shape
    tr = max(t for t in range(16, r + 1, 16) if r % t == 0 and t * c <= ADAMW_BLOCK_ELEMS)

    def body(p_ref, w_ref, m_ref, v_ref, *rest):
        g_ref, d_ref, nm_ref, nv_ref = rest[-4:]
        g = p_ref[0].astype(F32)
        for dev in range(1, N_DEV):
            g = g + p_ref[dev].astype(F32)
        delta, nm, nv = _adamw_math(w_ref[0], g, m_ref[0], v_ref[0])
        g_ref[0] = g
        d_ref[0] = delta
        nm_ref[0] = nm
        nv_ref[0] = nv

    slab = pl.BlockSpec((1, tr, c), lambda i: (layer, i, 0))
    any_spec = pl.BlockSpec(memory_space=pl.ANY)
    shape = jax.ShapeDtypeStruct(w.shape, F32)
    carried = [] if outs is None else list(outs)
    return pl.pallas_call(
        body, grid=(r // tr,), name="adamw",
        in_specs=[pl.BlockSpec((N_DEV, tr, c), lambda i: (0, i, 0)), slab, slab, slab] + [any_spec] * len(carried),
        out_specs=[slab] * 4,
        out_shape=[shape] * 4,
        input_output_aliases={4 + n: n for n in range(len(carried))},
        compiler_params=_params("arbitrary"),
    )(parts, w, m, v, *carried)


def _adamw_small(parts, w, m, v):
    r, c = w.shape

    def body(p_ref, w_ref, m_ref, v_ref, g_ref, d_ref, nm_ref, nv_ref):
        g = p_ref[0:r, :]
        for dev in range(1, N_DEV):
            g = g + p_ref[dev * r:(dev + 1) * r, :]
        delta, nm, nv = _adamw_math(w_ref[...], g, m_ref[...], v_ref[...])
        g_ref[...] = g
        d_ref[...] = delta
        nm_ref[...] = nm
        nv_ref[...] = nv

    shape = jax.ShapeDtypeStruct((r, c), F32)
    return pl.pallas_call(body, name="adamw_small", out_shape=[shape] * 4)(parts, w, m, v)


def _my_place():
    return lax.axis_index("x"), lax.axis_index("y"), lax.axis_index("c")


def _peer(k):
    x, y, c = _my_place()
    return (x ^ (k >> 2), y ^ ((k >> 1) & 1), c ^ (k & 1))


def _block_of(place):
    x, y, c = place
    return 4 * x + 2 * y + c


def _gather_small(block):
    m_per, n = block.shape

    def body(x_ref, out_ref, send_sems, recv_sems, local_sem):
        me = _block_of(_my_place())

        def rows(b):
            return out_ref.at[pl.ds(b * m_per, m_per), :]

        mine = pltpu.make_async_copy(x_ref, rows(me), local_sem)
        mine.start()
        sends = []
        for k in range(1, N_DEV):
            cp = pltpu.make_async_remote_copy(
                src_ref=x_ref, dst_ref=rows(me), send_sem=send_sems.at[k - 1], recv_sem=recv_sems.at[k - 1],
                device_id=_peer(k), device_id_type=MESH)
            cp.start()
            sends.append(cp)
        for k in range(1, N_DEV):
            src = rows(_block_of(_peer(k)))
            pltpu.make_async_remote_copy(
                src_ref=src, dst_ref=src, send_sem=send_sems.at[k - 1], recv_sem=recv_sems.at[k - 1],
                device_id=_peer(k), device_id_type=MESH).wait_recv()
        for cp in sends:
            cp.wait_send()
        mine.wait()

    return pl.pallas_call(
        body, name="gather_small",
        out_shape=jax.ShapeDtypeStruct((N_DEV * m_per, n), block.dtype),
        in_specs=[pl.BlockSpec(memory_space=pltpu.VMEM)],
        out_specs=pl.BlockSpec(memory_space=pltpu.VMEM),
        scratch_shapes=[pltpu.SemaphoreType.DMA((N_DEV - 1,)), pltpu.SemaphoreType.DMA((N_DEV - 1,)), pltpu.SemaphoreType.DMA],
    )(block)


def _gather_weights(name, stacks, layers):
    n_t = len(stacks)

    def body(*refs):
        srcs, outs = refs[:n_t], refs[n_t:2 * n_t]
        send_sems, recv_sems, local_sems = refs[2 * n_t:]
        x, y, c = _my_place()
        me, sibling = (x, y, c), (x, y, 1 - c)
        chips = [(1 - x, y), (x, 1 - y), (1 - x, 1 - y)]

        def copy(t, k, block, to, src=None):
            slot = outs[t].at[_block_of(block)]
            return pltpu.make_async_remote_copy(
                src_ref=slot if src is None else src, dst_ref=slot,
                send_sem=send_sems.at[t, k], recv_sem=recv_sems.at[t, k], device_id=to, device_id_type=MESH)

        local, sends = [], []
        for t in range(n_t):
            src = srcs[t].at[layers[t]]
            own = pltpu.make_async_copy(src, outs[t].at[_block_of(me)], local_sems.at[t])
            own.start()
            local.append(own)
            sends.append(copy(t, 0, me, sibling, src=src))
            sends += [copy(t, 1 + j, me, (*chip, c), src=src) for j, chip in enumerate(chips)]
        for cp in sends:
            cp.start()
        for t in range(n_t):
            for j, chip in enumerate(chips):
                copy(t, 1 + j, (*chip, c), me).wait_recv()
                passed = copy(t, 4 + j, (*chip, c), sibling)
                passed.start()
                sends.append(passed)
        for t in range(n_t):
            copy(t, 0, sibling, me).wait_recv()
            for j, chip in enumerate(chips):
                copy(t, 4 + j, (*chip, 1 - c), me).wait_recv()
        for cp in sends:
            cp.wait_send()
        for cp in local:
            cp.wait()

    any_spec = pl.BlockSpec(memory_space=pl.ANY)
    return pl.pallas_call(
        body, name=name,
        out_shape=[jax.ShapeDtypeStruct((N_DEV,) + st.shape[1:], st.dtype) for st in stacks],
        in_specs=[any_spec] * n_t, out_specs=[any_spec] * n_t,
        scratch_shapes=[pltpu.SemaphoreType.DMA((n_t, N_DEV - 1)), pltpu.SemaphoreType.DMA((n_t, N_DEV - 1)), pltpu.SemaphoreType.DMA((n_t,))],
    )(*stacks)


class _ScatterSide:
    def __init__(self, grads):
        self.operands = list(grads)
        self.n = len(self.operands)

    def out_shape(self):
        return [jax.ShapeDtypeStruct(g.shape, g.dtype) for g in self.operands]

    def scratch(self):
        return [pltpu.SemaphoreType.DMA((self.n, N_DEV - 1)), pltpu.SemaphoreType.DMA((self.n, N_DEV - 1)), pltpu.SemaphoreType.DMA((self.n,))]

    def _copies(self, srcs, outs, send_sems, recv_sems, local_sems):
        me = _block_of(_my_place())
        local, sends, arrivals = [], [], []
        for t in range(self.n):
            local.append(pltpu.make_async_copy(srcs[t].at[me], outs[t].at[me], local_sems.at[t]))
            for k in range(1, N_DEV):
                sems = dict(send_sem=send_sems.at[t, k - 1], recv_sem=recv_sems.at[t, k - 1], device_id=_peer(k), device_id_type=MESH)
                sends.append(pltpu.make_async_remote_copy(src_ref=srcs[t].at[_block_of(_peer(k))], dst_ref=outs[t].at[me], **sems))
                slot = outs[t].at[_block_of(_peer(k))]
                arrivals.append(pltpu.make_async_remote_copy(src_ref=slot, dst_ref=slot, **sems))
        return local, sends, arrivals

    def start(self, *refs):
        local, sends, _ = self._copies(*refs)
        for cp in local + sends:
            cp.start()

    def finish(self, *refs):
        local, sends, arrivals = self._copies(*refs)
        for cp in arrivals:
            cp.wait_recv()
        for cp in sends:
            cp.wait_send()
        for cp in local:
            cp.wait()


class _GatherSide(_ScatterSide):
    def __init__(self, stacks, layers):
        super().__init__(stacks)
        self.layers = list(layers)

    def out_shape(self):
        return [jax.ShapeDtypeStruct((N_DEV,) + st.shape[1:], st.dtype) for st in self.operands]

    def _copies(self, srcs, outs, send_sems, recv_sems, local_sems):
        me = _block_of(_my_place())
        local, sends, arrivals = [], [], []
        for t in range(self.n):
            src = srcs[t].at[self.layers[t]]
            local.append(pltpu.make_async_copy(src, outs[t].at[me], local_sems.at[t]))
            for k in range(1, N_DEV):
                sems = dict(send_sem=send_sems.at[t, k - 1], recv_sem=recv_sems.at[t, k - 1], device_id=_peer(k), device_id_type=MESH)
                sends.append(pltpu.make_async_remote_copy(src_ref=src, dst_ref=outs[t].at[me], **sems))
                slot = outs[t].at[_block_of(_peer(k))]
                arrivals.append(pltpu.make_async_remote_copy(src_ref=slot, dst_ref=slot, **sems))
        return local, sends, arrivals


def _scatter_grads(name, grads):
    side = _ScatterSide(grads)
    n_t = side.n

    def body(*refs):
        parts = (refs[:n_t], refs[n_t:2 * n_t]) + tuple(refs[2 * n_t:])
        side.start(*parts)
        side.finish(*parts)

    any_spec = pl.BlockSpec(memory_space=pl.ANY)
    return pl.pallas_call(
        body, name=name, out_shape=side.out_shape(), in_specs=[any_spec] * n_t, out_specs=[any_spec] * n_t,
        scratch_shapes=side.scratch(),
    )(*grads)


def _host_call(body, args, side, *, grid, in_specs, out_specs, out_shape, scratch_shapes=(), **kw):
    if side is None:
        return pl.pallas_call(body, grid=grid, in_specs=in_specs, out_specs=out_specs, out_shape=out_shape,
                              scratch_shapes=list(scratch_shapes), **kw)(*args), None
    n_in, n_out, n_scr, n_side = len(in_specs), len(out_specs), len(scratch_shapes), side.n

    def hosted(*refs):
        cuts = [n_in, n_side, n_out, n_side, n_scr, 3]
        groups, at = [], 0
        for c in cuts:
            groups.append(refs[at:at + c])
            at += c
        ins, side_in, outs, side_out, scr, sems = groups
        first, last = None, None
        for axis, size in enumerate(grid):
            at_start, at_end = pl.program_id(axis) == 0, pl.program_id(axis) == size - 1
            first = at_start if first is None else jnp.logical_and(first, at_start)
            last = at_end if last is None else jnp.logical_and(last, at_end)

        @pl.when(first)
        def _():
            side.start(side_in, side_out, *sems)

        body(*ins, *outs, *scr)

        @pl.when(last)
        def _():
            side.finish(side_in, side_out, *sems)

    any_spec = pl.BlockSpec(memory_space=pl.ANY)
    res = pl.pallas_call(
        hosted, grid=grid, in_specs=list(in_specs) + [any_spec] * n_side, out_specs=list(out_specs) + [any_spec] * n_side,
        out_shape=list(out_shape) + side.out_shape(), scratch_shapes=list(scratch_shapes) + side.scratch(), **kw,
    )(*args, *side.operands)
    return res[:n_out], res[n_out:]


def _ffn_forward(x, g, wt, wd, carried):
    sides = {k: _GatherSide(*v) if v[0] else None for k, v in carried.items()}
    parts = {}
    (h, gate, up, act), parts["up"] = _ffn_up(x, g, wt, sides.get("up"))
    (x_new,), parts["down"] = _ffn_down(act, wd, x, sides.get("down"))
    return x_new, (x, g, h, gate, up, act), parts


def _ffn_backward(dx, saved, wt, wd, carried):
    x, g, h, gate, up, act = saved
    sides = {k: _ScatterSide(v) if v else None for k, v in carried.items()}
    parts = {}
    (dgate, dup), parts["act"] = _ffn_bwd_act(dx, wd, gate, up, sides.get("act"))
    d_wd = _grad_ffn_down(act, dx)
    (dx_in, dg), parts["dh"] = _ffn_dh(dgate, dup, wt, x, g, dx, sides.get("dh"))
    d_wt, parts["gu"] = _grad_ffn_gu(h, dgate, dup, sides.get("gu"))
    return dx_in, dg, d_wt.reshape(N_DEV, -1, D_MODEL), d_wd.reshape(N_DEV, -1, D_MODEL), parts


def kernel(x, p, norm_ffn1, w_ffn1_gu, w_ffn1_down, norm_mix, w_qkv, q_norm, k_norm, w_o, w_pool_in, w_pool_grp, pool_scale, norm_ffn2, w_ffn2_gu, w_ffn2_down, norm_ple, w_ple_gate, w_ple_proj, loss_target, m_norm_ffn1, m_w_ffn1_gu, m_w_ffn1_down, m_norm_mix, m_w_qkv, m_q_norm, m_k_norm, m_w_o, m_w_pool_in, m_w_pool_grp, m_pool_scale, m_norm_ffn2, m_w_ffn2_gu, m_w_ffn2_down, m_norm_ple, m_w_ple_gate, m_w_ple_proj, v_norm_ffn1, v_w_ffn1_gu, v_w_ffn1_down, v_norm_mix, v_w_qkv, v_q_norm, v_k_norm, v_w_o, v_w_pool_in, v_w_pool_grp, v_pool_scale, v_norm_ffn2, v_w_ffn2_gu, v_w_ffn2_down, v_norm_ple, v_w_ple_gate, v_w_ple_proj):
    d = D_MODEL
    xs = x[0]
    target = loss_target[0]
    me = _block_of(_my_place())

    def tr(w):
        return jnp.swapaxes(w, 1, 2)

    big = dict(w_ffn1_gu=tr(w_ffn1_gu), w_ffn1_down=w_ffn1_down, w_qkv=w_qkv, w_o=w_o, w_pool_in=w_pool_in,
               w_pool_grp=w_pool_grp.reshape(2, 4 * 32, POOL_GROUP), w_ffn2_gu=tr(w_ffn2_gu), w_ffn2_down=w_ffn2_down,
               w_ple_gate=w_ple_gate, w_ple_proj=w_ple_proj)
    moments = dict(
        w_ffn1_gu=(tr(m_w_ffn1_gu), tr(v_w_ffn1_gu)), w_ffn1_down=(m_w_ffn1_down, v_w_ffn1_down), w_qkv=(m_w_qkv, v_w_qkv),
        w_o=(m_w_o, v_w_o), w_pool_in=(m_w_pool_in, v_w_pool_in),
        w_pool_grp=(m_w_pool_grp.reshape(2, 4 * 32, POOL_GROUP), v_w_pool_grp.reshape(2, 4 * 32, POOL_GROUP)),
        w_ffn2_gu=(tr(m_w_ffn2_gu), tr(v_w_ffn2_gu)), w_ffn2_down=(m_w_ffn2_down, v_w_ffn2_down),
        w_ple_gate=(m_w_ple_gate, v_w_ple_gate), w_ple_proj=(m_w_ple_proj, v_w_ple_proj))
    half = {name: _cast_bf16(w) for name, w in big.items()}

    def layer_names(i):
        mixer = ["w_qkv", "w_o"] if i % 2 == 0 else ["w_pool_in", "w_pool_grp"]
        return ["w_ffn1_gu", "w_ffn1_down"] + mixer + ["w_ffn2_gu", "w_ffn2_down", "w_ple_gate", "w_ple_proj"]

    def layer_index(name, i):
        return i // 2 if name in ("w_qkv", "w_o", "w_pool_in", "w_pool_grp") else i

    scale_all = _gather_small(jnp.pad(pool_scale, ((0, 6), (0, 0))))
    scale_full = scale_all.reshape(N_DEV, 8, 128)[:, :2].transpose(1, 0, 2).reshape(2, d)

    saved = []
    weights = []
    cur = xs

    fwd_carriers = {
        ("ffn1", "up"): ["w_ffn1_gu"], ("ffn2", "up"): ["w_ffn2_gu"],
        ("ffn1", "down"): ["w_ffn1_down"], ("ffn2", "down"): ["w_ffn2_down"],
        ("ple", "ple"): ["w_ple_gate", "w_ple_proj", "w_qkv", "w_o", "w_pool_in", "w_pool_grp"],
    }

    def next_weights(i, stage):
        nxt = layer_names(i + 1) if i + 1 < DEPTH else []
        out = {}
        for (st, host), names in fwd_carriers.items():
            if st == stage:
                take = [n for n in names if n in nxt]
                out[host] = ([half[n] for n in take], [layer_index(n, i + 1) for n in take])
        return out

    def arrived_weights(i, stage, parts, into):
        nxt = layer_names(i + 1) if i + 1 < DEPTH else []
        for (st, host), names in fwd_carriers.items():
            if st == stage and parts.get(host) is not None:
                into.update(zip([n for n in names if n in nxt], parts[host]))

    names = layer_names(0)
    coming = dict(zip(names, _gather_weights("gather_weights_0", [half[n] for n in names], [layer_index(n, 0) for n in names])))
    for i in range(DEPTH):
        wl, coming = coming, {}
        assert set(wl) == set(layer_names(i)), sorted(wl)
        for n in ("w_ffn1_gu", "w_ffn2_gu", "w_ffn1_down", "w_ffn2_down", "w_o", "w_pool_in", "w_ple_gate"):
            if n in wl:
                wl[n] = wl[n].reshape(-1, d)
        if "w_pool_grp" in wl:
            wl["w_pool_grp"] = wl["w_pool_grp"].reshape(N_DEV, 4, 32, POOL_GROUP).transpose(1, 0, 2, 3).reshape(4, POOL_GROUP, POOL_GROUP)
        for n in ("w_qkv", "w_ple_proj"):
            if n in wl:
                wl[n] = _from_column_shards(wl[n])
        weights.append(wl)
        j = i // 2
        rec = {}
        cur, rec["ffn1"], parts = _ffn_forward(cur, norm_ffn1[i][None], wl["w_ffn1_gu"], wl["w_ffn1_down"], next_weights(i, "ffn1"))
        arrived_weights(i, "ffn1", parts, coming)
        x1 = cur
        gm = norm_mix[i][None]
        if i % 2 == 0:
            qg = jnp.tile(q_norm[j], d // HEAD_DIM)[None]
            kg = jnp.tile(k_norm[j], d // HEAD_DIM)[None]
            hm, qkv = _norm_matmul("qkv_proj", x1, gm, wl["w_qkv"], (d, d), lambda a, b: (0, b), _pick_all, 3 * d, d, 3)
            qn, kn, vb = _qk_norm(qkv, qg, kg)
            o = _attention(qn, kn, vb)
            cur = _matmul_res("attn_out", o, wl["w_o"], x1, 1.0)
            rec["mix"] = (x1, gm, hm, qkv, qg, kg, qn, kn, vb, o)
        else:
            sc = scale_full[j][None]
            hm, u = _norm_matmul("pool_in", x1, gm, wl["w_pool_in"], (d, d), lambda a, b: (0, 0), _pick_all, d, d, 1)
            cur, pooled = _pool_fwd(u, wl["w_pool_grp"], sc, x1)
            rec["mix"] = (x1, gm, hm, sc, pooled)
        cur, rec["ffn2"], parts = _ffn_forward(cur, norm_ffn2[i][None], wl["w_ffn2_gu"], wl["w_ffn2_down"], next_weights(i, "ffn2"))
        arrived_weights(i, "ffn2", parts, coming)
        x3 = cur
        gp = norm_ple[i][None]
        stacks, layers = next_weights(i, "ple")["ple"]
        (cur, hp, sig, proj), parts = _ple_fwd(x3, gp, wl["w_ple_gate"], p[i, 0], wl["w_ple_proj"],
                                               _GatherSide(stacks, layers) if stacks else None)
        arrived_weights(i, "ple", {"ple": parts}, coming)
        rec["ple"] = (x3, gp, hp, sig, proj)
        saved.append(rec)

    dy, loss_part = _loss_head(cur, target)
    loss = lax.psum(loss_part[0, 0], ("x", "y", "c"))

    small = {n: [None] * DEPTH for n in ("norm_ffn1", "norm_mix", "norm_ffn2", "norm_ple")}
    small.update(q_norm=[None] * 2, k_norm=[None] * 2, pool_scale=[None] * 2)
    results = {name: None for name in big}

    def update(layer, parts):
        for n, part in parts.items():
            mm, vv = moments[n]
            results[n] = _adamw_layer(part, big[n], mm, vv, layer_index(n, layer), results[n])

    carriers = {
        ("ffn2", "gu"): ["w_ffn1_gu"], ("ffn1", "gu"): ["w_ffn2_gu"],
        ("ffn2", "dh"): ["w_ffn1_down", "w_ple_gate", "w_ple_proj"],
        ("ffn1", "dh"): ["w_ffn2_down", "w_o", "w_pool_in", "w_pool_grp"],
        ("ffn2", "act"): ["w_qkv"],
    }
    above = {}

    def carried(ffn):
        return {host: [above[n] for n in names if n in above] for (f, host), names in carriers.items() if f == ffn}

    def received(ffn, parts, into):
        for (f, host), names in carriers.items():
            if f == ffn and parts.get(host) is not None:
                into.update(zip([n for n in names if n in above], parts[host]))

    dcur = dy
    for i in reversed(range(DEPTH)):
        wl, rec, j = weights[i], saved[i], i // 2
        grads = {}
        arrived = {}
        x3, gp, hp, sig, proj = rec["ple"]
        dgate, dproj = _ple_bwd_gate(dcur, sig, proj)
        dx3, small["norm_ple"][i] = _square_dh("ple_dh", dgate, wl["w_ple_gate"], x3, gp, dcur)
        grads["w_ple_gate"] = _grad_square("grad_ple_gate", hp, dgate).reshape(N_DEV, d // N_DEV, d)
        grads["w_ple_proj"] = _column_shards(_grad_square("grad_ple_proj", p[i, 0], dproj))
        dx2, small["norm_ffn2"][i], grads["w_ffn2_gu"], grads["w_ffn2_down"], parts = _ffn_backward(
            dx3, rec["ffn2"], wl["w_ffn2_gu"], wl["w_ffn2_down"], carried("ffn2"))
        received("ffn2", parts, arrived)
        if i % 2 == 0:
            x1, gm, hm, qkv, qg, kg, qn, kn, vb, o = rec["mix"]
            do = _matmul_nt("attn_out_bwd", dx2, wl["w_o"])
            grads["w_o"] = _grad_square("grad_attn_out", o, dx2).reshape(N_DEV, d // N_DEV, d)
            dq, dk, dv = _attention_bwd(qn, kn, vb, do, o)
            dqkv, dqg, dkg = _qk_norm_bwd(qkv, dq, dk, dv, qg, kg)
            small["q_norm"][j], small["k_norm"][j] = dqg[0:1], dkg[0:1]
            dx1, small["norm_mix"][i] = _qkv_dh(dqkv, wl["w_qkv"], x1, gm, dx2)
            grads["w_qkv"] = _column_shards(_grad_square("grad_qkv", hm, dqkv))
        else:
            x1, gm, hm, sc, pooled = rec["mix"]
            dpool, dgrp, small["pool_scale"][j] = _pool_bwd_group(dx2, pooled, wl["w_pool_grp"], sc)
            grads["w_pool_grp"] = dgrp.reshape(4, N_DEV, 32, POOL_GROUP).transpose(1, 0, 2, 3).reshape(N_DEV, 4 * 32, POOL_GROUP)
            du = _pool_bwd_window(dpool)
            dx1, small["norm_mix"][i] = _square_dh("pool_dh", du, wl["w_pool_in"], x1, gm, dx2)
            grads["w_pool_in"] = _grad_square("grad_pool_in", hm, du).reshape(N_DEV, d // N_DEV, d)
        dcur, small["norm_ffn1"][i], grads["w_ffn1_gu"], grads["w_ffn1_down"], parts = _ffn_backward(
            dx1, rec["ffn1"], wl["w_ffn1_gu"], wl["w_ffn1_down"], carried("ffn1"))
        received("ffn1", parts, arrived)
        if above:
            assert set(arrived) == set(above), (sorted(arrived), sorted(above))
            update(i + 1, arrived)
        above = grads

    names = layer_names(0)
    update(0, dict(zip(names, _scatter_grads("scatter_grads_0", [above[n] for n in names]))))

    def lanes(a):
        return jnp.pad(a, ((0, 0), (0, d - a.shape[-1])))

    order = [("norm_ffn1", DEPTH), ("norm_mix", DEPTH), ("norm_ffn2", DEPTH), ("norm_ple", DEPTH), ("pool_scale", 2), ("q_norm", 2), ("k_norm", 2)]
    rows = jnp.concatenate([lanes(g) for name, _ in order for g in small[name]], axis=0)
    n_rows = rows.shape[0]
    pad_rows = -n_rows % 8
    rows = jnp.pad(rows, ((0, pad_rows), (0, 0)))
    gathered = _gather_small(rows)

    def own_lanes(a):
        return lax.dynamic_update_slice(jnp.zeros((a.shape[0], d), F32), a, (0, me * 128))

    def pack(values):
        mats = [own_lanes(values[name]) if name == "pool_scale" else lanes(values[name]) for name, _ in order]
        return jnp.pad(jnp.concatenate(mats, axis=0), ((0, pad_rows), (0, 0)))

    small_w = dict(norm_ffn1=norm_ffn1, norm_mix=norm_mix, norm_ffn2=norm_ffn2, norm_ple=norm_ple, pool_scale=pool_scale, q_norm=q_norm, k_norm=k_norm)
    small_m = dict(norm_ffn1=m_norm_ffn1, norm_mix=m_norm_mix, norm_ffn2=m_norm_ffn2, norm_ple=m_norm_ple, pool_scale=m_pool_scale, q_norm=m_q_norm, k_norm=m_k_norm)
    small_v = dict(norm_ffn1=v_norm_ffn1, norm_mix=v_norm_mix, norm_ffn2=v_norm_ffn2, norm_ple=v_norm_ple, pool_scale=v_pool_scale, q_norm=v_q_norm, k_norm=v_k_norm)
    packed = _adamw_small(gathered, pack(small_w), pack(small_m), pack(small_v))

    def unpack(mat):
        out, at = {}, 0
        for name, n in order:
            blk = mat[at:at + n]
            at += n
            if name == "pool_scale":
                out[name] = lax.dynamic_slice(blk, (0, me * 128), (n, 128))
            elif name in ("q_norm", "k_norm"):
                out[name] = blk[:, :HEAD_DIM]
            else:
                out[name] = blk
        return out

    small_out = [unpack(mat) for mat in packed]

    def result(kind, name):
        if name in small_w:
            return small_out[kind][name]
        r = results[name][kind]
        if name in ("w_ffn1_gu", "w_ffn2_gu"):
            return tr(r)
        return r.reshape(w_pool_grp.shape) if name == "w_pool_grp" else r

    weight_names = ["norm_ffn1", "w_ffn1_gu", "w_ffn1_down", "norm_mix", "w_qkv", "q_norm", "k_norm", "w_o", "w_pool_in", "w_pool_grp",
                    "pool_scale", "norm_ffn2", "w_ffn2_gu", "w_ffn2_down", "norm_ple", "w_ple_gate", "w_ple_proj"]
    outs = [loss, dcur[None]]
    for kind in range(4):
        outs += [result(kind, name) for name in weight_names]
    return tuple(outs)
```

```python
import math

import jax
import jax.numpy as jnp
from jax import lax
from jax.experimental import pallas as pl
from jax.experimental.pallas import tpu as pltpu

F32 = jnp.float32
BF16 = jnp.bfloat16

N_DEV = 8
DEPTH = 4
D_MODEL = 1024
N_UNITS = D_MODEL // 128
HEAD_DIM = 64
POOL_WINDOWS = (2, 4, 8, 16)
POOL_GROUP = 256
POOL_HALO = 128
EPS = 1e-6
ATTN_SCALE = 1.0 / math.sqrt(HEAD_DIM)
ATTN_BLOCK = 256
LOG_ZERO = -104.0

ADAM_LR = 0.001
ADAM_B1 = 0.9
ADAM_B2 = 0.999
ADAM_EPS = 1e-08
ADAM_WD = 0.01
ADAM_STEP = 10
ADAMW_BLOCK_ELEMS = 128 * 1024

VMEM_LIMIT = 56 * 1024 * 1024
MESH = pl.DeviceIdType.MESH

NT_DIMS = (((1,), (1,)), ((), ()))
TN_DIMS = (((0,), (0,)), ((), ()))


def _params(*sem):
    return pltpu.CompilerParams(dimension_semantics=sem, vmem_limit_bytes=VMEM_LIMIT)


def _row_tile(s):
    return min(512, s)


def _grad_tile(s):
    return min(2048, s)


def _block_spec(block, index, mode=None):
    return pl.BlockSpec(block, index) if mode is None else pl.BlockSpec(block, index, pipeline_mode=mode)


def _dot(a, b):
    return jnp.dot(a, b, preferred_element_type=F32)


def _dot_nt(a, b):
    return lax.dot_general(a, b, NT_DIMS, preferred_element_type=F32)


def _dot_tn(a, b):
    return lax.dot_general(a, b, TN_DIMS, preferred_element_type=F32)


def _dot_f32(a, b):
    return jnp.dot(a, b, precision=lax.Precision.HIGHEST, preferred_element_type=F32)


def _dot_3x(a, b):
    return jnp.dot(a, b, precision=lax.Precision.HIGH, preferred_element_type=F32)


def _rms(x, g):
    r = lax.rsqrt(jnp.mean(x * x, axis=-1, keepdims=True) + EPS)
    return x * r * g


def _rms_bwd(dy, x, g):
    r = lax.rsqrt(jnp.mean(x * x, axis=-1, keepdims=True) + EPS)
    xh = x * r
    dg = jnp.sum(dy * xh, axis=0, keepdims=True)
    dxh = dy * g
    dx = r * (dxh - xh * jnp.mean(dxh * xh, axis=-1, keepdims=True))
    return dx, dg


def _cast_bf16(w):
    l, r, c = w.shape

    def body(w_ref, o_ref):
        o_ref[...] = w_ref[...].astype(BF16)

    return pl.pallas_call(
        body, grid=(l,), name="cast_bf16",
        in_specs=[pl.BlockSpec((1, r, c), lambda i: (i, 0, 0))],
        out_specs=pl.BlockSpec((1, r, c), lambda i: (i, 0, 0)),
        out_shape=jax.ShapeDtypeStruct(w.shape, BF16),
        compiler_params=_params("arbitrary"),
    )(w)


def _ffn_up(x, g, wt, side=None):
    s, d = x.shape
    f = wt.shape[0] // 2
    tn = f // 2
    tm = _row_tile(s)

    def body(x_ref, g_ref, w_ref, h_ref, gate_ref, up_ref, act_ref):
        h = _rms(x_ref[...], g_ref[...]).astype(BF16)
        h_ref[...] = h
        for half in range(2):
            cols = slice(half * tn, (half + 1) * tn)
            gate = _dot_nt(h, w_ref[half * tn:(half + 1) * tn, :])
            up = _dot_nt(h, w_ref[f + half * tn:f + (half + 1) * tn, :])
            gate_ref[:, cols] = gate.astype(BF16)
            up_ref[:, cols] = up.astype(BF16)
            act_ref[:, cols] = (gate * jax.nn.sigmoid(gate) * up).astype(BF16)

    wide = pl.BlockSpec((tm, f), lambda i: (i, 0))
    hidden = jax.ShapeDtypeStruct((s, f), BF16)
    return _host_call(
        body, (x, g, wt), side, grid=(s // tm,), name="ffn_up",
        in_specs=[
            pl.BlockSpec((tm, d), lambda i: (i, 0)),
            pl.BlockSpec((1, d), lambda i: (0, 0)),
            _block_spec((2 * f, d), lambda i: (0, 0), pl.Buffered(1)),
        ],
        out_specs=[pl.BlockSpec((tm, d), lambda i: (i, 0)), wide, wide, wide],
        out_shape=[jax.ShapeDtypeStruct((s, d), BF16), hidden, hidden, hidden],
        compiler_params=_params("arbitrary"),
    )


def _ffn_down(act, wd, x, side=None):
    s, f = act.shape
    d = wd.shape[-1]
    tm = _row_tile(s)

    def body(a_ref, w_ref, x_ref, o_ref):
        o_ref[...] = x_ref[...] + 0.5 * _dot(a_ref[...], w_ref[...])

    row = pl.BlockSpec((tm, d), lambda i: (i, 0))
    return _host_call(
        body, (act, wd, x), side, grid=(s // tm,), name="ffn_down",
        in_specs=[pl.BlockSpec((tm, f), lambda i: (i, 0)), _block_spec((f, d), lambda i: (0, 0), pl.Buffered(1)), row],
        out_specs=[row],
        out_shape=[jax.ShapeDtypeStruct((s, d), F32)],
        compiler_params=_params("arbitrary"),
    )


def _ffn_bwd_act(dx, wd, gate, up, side=None):
    s, d = dx.shape
    f = gate.shape[-1]
    tn = f // 2
    tm = _row_tile(s)

    def body(dx_ref, w_ref, gate_ref, up_ref, dgate_ref, dup_ref):
        dact = _dot_nt((0.5 * dx_ref[...]).astype(BF16), w_ref[...])
        gate = gate_ref[...].astype(F32)
        sig = jax.nn.sigmoid(gate)
        silu = gate * sig
        dgate_ref[...] = (dact * up_ref[...].astype(F32) * (sig + silu * (1.0 - sig))).astype(BF16)
        dup_ref[...] = (dact * silu).astype(BF16)

    col = pl.BlockSpec((tm, tn), lambda j, i: (i, j))
    hidden = jax.ShapeDtypeStruct((s, f), BF16)
    return _host_call(
        body, (dx, wd, gate, up), side, grid=(2, s // tm), name="ffn_bwd_act",
        in_specs=[pl.BlockSpec((tm, d), lambda j, i: (i, 0)), pl.BlockSpec((tn, d), lambda j, i: (j, 0)), col, col],
        out_specs=[col, col],
        out_shape=[hidden, hidden],
        compiler_params=_params("arbitrary", "arbitrary"),
    )


def _norm_bwd_matmul(name, operands, products, nk, x, g, dres, w_rows_contract=False, side=None):
    s, d = x.shape
    tm = _row_tile(s)
    n = len(operands)

    def body(*refs):
        x_ref, g_ref, dres_ref, dx_ref, dg_ref, acc_ref = refs[n:]
        i, k = pl.program_id(0), pl.program_id(1)

        @pl.when(jnp.logical_and(i == 0, k == 0))
        def _():
            dg_ref[...] = jnp.zeros_like(dg_ref)

        total = None
        for a_at, a_pick, w_at, w_pick in products:
            prod = (_dot if w_rows_contract else _dot_nt)(a_pick(refs[a_at]), w_pick(refs[w_at]))
            total = prod if total is None else total + prod

        if nk > 1:
            @pl.when(k == 0)
            def _():
                acc_ref[...] = total

            @pl.when(jnp.logical_and(k > 0, k < nk - 1))
            def _():
                acc_ref[...] += total

        @pl.when(k == nk - 1)
        def _():
            dy = total if nk == 1 else acc_ref[...] + total
            dx, dg = _rms_bwd(dy, x_ref[...], g_ref[...])
            dx_ref[...] = dres_ref[...] + dx
            dg_ref[...] += dg

    row = pl.BlockSpec((tm, d), lambda i, k: (i, 0))
    vec = pl.BlockSpec((1, d), lambda i, k: (0, 0))
    return _host_call(
        body, [op[0] for op in operands] + [x, g, dres], side, grid=(s // tm, nk), name=name,
        in_specs=[_block_spec(*op[1:]) for op in operands] + [row, vec, row],
        out_specs=[row, vec],
        out_shape=[jax.ShapeDtypeStruct((s, d), F32), jax.ShapeDtypeStruct((1, d), F32)],
        scratch_shapes=[pltpu.VMEM((tm, d), F32)],
        compiler_params=_params("arbitrary", "arbitrary"),
    )


def _ffn_dh(dgate, dup, wt, x, g, dres, side=None):
    s, d = x.shape
    f = dgate.shape[-1]
    tm = _row_tile(s)
    once = pl.Buffered(1)
    operands = [
        (dgate, (tm, f), lambda i, k: (i, 0)),
        (dup, (tm, f), lambda i, k: (i, 0)),
        (wt, (f, d), lambda i, k: (0, 0), once),
        (wt, (f, d), lambda i, k: (1, 0), once),
    ]
    products = [(0, _pick_all, 2, _pick_all), (1, _pick_all, 3, _pick_all)]
    return _norm_bwd_matmul("ffn_dh", operands, products, 1, x, g, dres, w_rows_contract=True, side=side)


def _square_dh(name, a, w, x, g, dres):
    s, d = x.shape
    tm = _row_tile(s)
    operands = [(a, (tm, d), lambda i, k: (i, 0)), (w, (d, d), lambda i, k: (0, 0))]
    return _norm_bwd_matmul(name, operands, [(0, _pick_all, 1, _pick_all)], 1, x, g, dres)[0]


def _qkv_dh(dqkv, wqkv, x, g, dres):
    s, d = x.shape
    tm = _row_tile(s)
    operands = [(dqkv, (tm, d), lambda i, k: (i, k)), (wqkv, (d, d), lambda i, k: (0, k))]
    return _norm_bwd_matmul("qkv_dh", operands, [(0, _pick_all, 1, _pick_all)], 3, x, g, dres)[0]


def _grad_matmul(name, a, a_block, a_index, a_pick, b, b_block, b_index, b_picks, out_shape, out_block, out_index, out_stores, nj, scale=1.0):
    s = a.shape[-2]
    nk = s // _grad_tile(s)
    n_prod = len(b_picks)

    def body(a_ref, b_ref, o_ref, *acc_refs):
        k = pl.program_id(1)
        av = a_pick(a_ref)
        if av.dtype != BF16:
            av = (scale * av).astype(BF16)
        for b_pick, store, acc_ref in zip(b_picks, out_stores, acc_refs):
            bv = b_pick(b_ref)
            if bv.dtype != BF16:
                bv = bv.astype(BF16)
            prod = _dot_tn(av, bv)
            if nk == 1:
                store(o_ref, prod.astype(BF16))
                continue

            @pl.when(k == 0)
            def _():
                acc_ref[...] = prod

            @pl.when(jnp.logical_and(k > 0, k < nk - 1))
            def _():
                acc_ref[...] += prod

            @pl.when(k == nk - 1)
            def _():
                store(o_ref, (acc_ref[...] + prod).astype(BF16))

    m = jax.eval_shape(a_pick, jax.ShapeDtypeStruct(a_block, a.dtype)).shape[-1]
    nn = jax.eval_shape(b_picks[0], jax.ShapeDtypeStruct(b_block, b.dtype)).shape[-1]
    acc_shape = (m, nn)
    return pl.pallas_call(
        body, grid=(nj, nk), name=name,
        in_specs=[pl.BlockSpec(a_block, a_index), pl.BlockSpec(b_block, b_index)],
        out_specs=pl.BlockSpec(out_block, out_index),
        out_shape=jax.ShapeDtypeStruct(out_shape, BF16),
        scratch_shapes=[pltpu.VMEM(acc_shape, F32) for _ in range(n_prod)],
        compiler_params=_params("arbitrary", "arbitrary"),
    )(a, b)


def _pick_all(r):
    return r[...]


def _store_all(r, v):
    r[...] = v


def _grad_ffn_down(act, dx):
    s, f = act.shape
    d = dx.shape[-1]
    tk = _grad_tile(s)
    tn = f // 2
    return _grad_matmul(
        "grad_ffn_down", act, (tk, tn), lambda j, k: (k, j), _pick_all,
        dx, (tk, d), lambda j, k: (k, 0), [lambda r: 0.5 * r[...]],
        (f, d), (tn, d), lambda j, k: (j, 0), [_store_all], 2)


def _grad_ffn_gu(h, dgate, dup, side=None):
    s, d = h.shape
    f = dgate.shape[-1]
    tn = f // 2
    tk = min(1024, s)
    nk = s // tk

    def body(h_ref, dgate_ref, dup_ref, o_ref, acc_ref):
        j, k = pl.program_id(0), pl.program_id(1)

        def accumulate(a_ref):
            prod = _dot_tn(a_ref[...], h_ref[...])
            if nk == 1:
                o_ref[...] = prod.astype(BF16)
                return

            @pl.when(k == 0)
            def _():
                acc_ref[...] = prod

            @pl.when(jnp.logical_and(k > 0, k < nk - 1))
            def _():
                acc_ref[...] += prod

            @pl.when(k == nk - 1)
            def _():
                o_ref[...] = (acc_ref[...] + prod).astype(BF16)

        @pl.when(j < 2)
        def _():
            accumulate(dgate_ref)

        @pl.when(j >= 2)
        def _():
            accumulate(dup_ref)

    (out,), parts = _host_call(
        body, (h, dgate, dup), side, grid=(4, nk), name="grad_ffn_gu",
        in_specs=[
            pl.BlockSpec((tk, d), lambda j, k: (k, 0)),
            pl.BlockSpec((tk, tn), lambda j, k: (jnp.where(j < 2, k, 0), jnp.minimum(j, 1))),
            pl.BlockSpec((tk, tn), lambda j, k: (jnp.where(j >= 2, k, 0), jnp.maximum(j - 2, 0))),
        ],
        out_specs=[pl.BlockSpec((tn, d), lambda j, k: (j, 0))],
        out_shape=[jax.ShapeDtypeStruct((2 * f, d), BF16)],
        scratch_shapes=[pltpu.VMEM((tn, d), F32)],
        compiler_params=_params("arbitrary", "arbitrary"),
    )
    return out, parts


def _grad_square(name, a, b):
    s, m = a.shape
    n = b.shape[-1]
    tk = _grad_tile(s)
    tn = min(n, D_MODEL)
    return _grad_matmul(
        name, a, (tk, m), lambda j, k: (k, 0), _pick_all,
        b, (tk, tn), lambda j, k: (k, j), [_pick_all],
        (m, n), (m, tn), lambda j, k: (0, j), [_store_all], n // tn)


def _column_shards(w):
    m, n = w.shape
    return w.reshape(m, N_DEV, n // N_DEV).transpose(1, 0, 2)


def _from_column_shards(w):
    nd, m, n = w.shape
    return w.transpose(1, 0, 2).reshape(m, nd * n)


def _norm_matmul(name, x, g, w, w_block, w_index, w_pick, n_total, tn, nj):
    s, d = x.shape
    tm = _row_tile(s)

    def body(x_ref, g_ref, w_ref, h_ref, y_ref):
        @pl.when(pl.program_id(1) == 0)
        def _():
            h_ref[...] = _rms(x_ref[...], g_ref[...]).astype(BF16)

        y_ref[...] = _dot(h_ref[...], w_pick(w_ref))

    return pl.pallas_call(
        body, grid=(s // tm, nj), name=name,
        in_specs=[
            pl.BlockSpec((tm, d), lambda i, j: (i, 0)),
            pl.BlockSpec((1, d), lambda i, j: (0, 0)),
            pl.BlockSpec(w_block, w_index),
        ],
        out_specs=[pl.BlockSpec((tm, d), lambda i, j: (i, 0)), pl.BlockSpec((tm, tn), lambda i, j: (i, j))],
        out_shape=[jax.ShapeDtypeStruct((s, d), BF16), jax.ShapeDtypeStruct((s, n_total), F32)],
        compiler_params=_params("arbitrary", "arbitrary"),
    )(x, g, w)


def _head_mean_matrix():
    r = lax.broadcasted_iota(jnp.int32, (128, 128), 0) // HEAD_DIM
    c = lax.broadcasted_iota(jnp.int32, (128, 128), 1) // HEAD_DIM
    return jnp.where(r == c, 1.0 / HEAD_DIM, 0.0).astype(F32)


def _qk_norm(qkv, qg, kg):
    s = qkv.shape[0]
    d = D_MODEL
    tm = _row_tile(s)

    def body(q_ref, k_ref, v_ref, qg_ref, kg_ref, qo_ref, ko_ref, vo_ref):
        mean_m = _head_mean_matrix()
        for u in range(N_UNITS):
            cols = slice(128 * u, 128 * (u + 1))
            for src, gain, dst, scale in ((q_ref, qg_ref, qo_ref, ATTN_SCALE), (k_ref, kg_ref, ko_ref, 1.0)):
                xs = src[:, cols]
                r = lax.rsqrt(_dot_3x(xs * xs, mean_m) + EPS)
                y = xs * r * gain[:, cols]
                dst[:, cols] = (y * scale).astype(BF16) if scale != 1.0 else y.astype(BF16)
        vo_ref[...] = v_ref[...].astype(BF16)

    blk = lambda c: pl.BlockSpec((tm, d), lambda i: (i, c))
    vec = pl.BlockSpec((1, d), lambda i: (0, 0))
    return pl.pallas_call(
        body, grid=(s // tm,), name="qk_norm",
        in_specs=[blk(0), blk(1), blk(2), vec, vec],
        out_specs=[blk(0)] * 3,
        out_shape=[jax.ShapeDtypeStruct((s, d), BF16)] * 3,
        compiler_params=_params("arbitrary"),
    )(qkv, qkv, qkv, qg, kg)


def _qk_norm_bwd(qkv, dq, dk, dv, qg, kg):
    s = qkv.shape[0]
    d = D_MODEL
    tm = _row_tile(s)
    nsteps = s // tm

    def body(q_ref, k_ref, dq_ref, dk_ref, dv_ref, qg_ref, kg_ref, o_ref, dqg_ref, dkg_ref, acc_ref):
        i = pl.program_id(0)

        @pl.when(i == 0)
        def _():
            acc_ref[...] = jnp.zeros_like(acc_ref)

        mean_m = _head_mean_matrix()
        for u in range(N_UNITS):
            cols = slice(128 * u, 128 * (u + 1))
            for n, (src, dsrc, gain) in enumerate(((q_ref, dq_ref, qg_ref), (k_ref, dk_ref, kg_ref))):
                xs = src[:, cols]
                dy = dsrc[:, cols]
                r = lax.rsqrt(_dot_3x(xs * xs, mean_m) + EPS)
                xh = xs * r
                acc_ref[n:n + 1, :] += jnp.sum(dy * xh, axis=0, keepdims=True)
                dxh = dy * gain[:, cols]
                dx = r * (dxh - xh * _dot_3x(dxh * xh, mean_m))
                o_ref[:, 128 * (N_UNITS * n + u):128 * (N_UNITS * n + u + 1)] = dx.astype(BF16)
        o_ref[:, 2 * d:3 * d] = dv_ref[...].astype(BF16)

        @pl.when(i == nsteps - 1)
        def _():
            r = lax.broadcasted_iota(jnp.int32, (128, 128), 0) % HEAD_DIM
            c = lax.broadcasted_iota(jnp.int32, (128, 128), 1) % HEAD_DIM
            fold = jnp.where(r == c, 1.0, 0.0).astype(F32)
            folded = _dot_f32(acc_ref[...], fold)
            dqg_ref[...] = jnp.broadcast_to(folded[0:1], (8, 128))
            dkg_ref[...] = jnp.broadcast_to(folded[1:2], (8, 128))

    blk = lambda c: pl.BlockSpec((tm, d), lambda i: (i, c))
    row = pl.BlockSpec((tm, d), lambda i: (i, 0))
    vec = pl.BlockSpec((1, d), lambda i: (0, 0))
    small = pl.BlockSpec((8, 128), lambda i: (0, 0))
    return pl.pallas_call(
        body, grid=(nsteps,), name="qk_norm_bwd",
        in_specs=[blk(0), blk(1), row, row, row, vec, vec],
        out_specs=[pl.BlockSpec((tm, 3 * d), lambda i: (i, 0)), small, small],
        out_shape=[jax.ShapeDtypeStruct((s, 3 * d), BF16), jax.ShapeDtypeStruct((8, 128), F32), jax.ShapeDtypeStruct((8, 128), F32)],
        scratch_shapes=[pltpu.VMEM((8, 128), F32)],
        compiler_params=_params("arbitrary"),
    )(qkv, qkv, dq, dk, dv, qg, kg)


def _split_dot(x, m):
    hi = x.astype(BF16)
    lo = (x - hi.astype(F32)).astype(BF16)
    return _dot(hi, m) + _dot(lo, m)


def _stack_heads(x):
    lane = lax.broadcasted_iota(jnp.int32, x.shape, 1)
    zero = jnp.zeros_like(x)
    return jnp.concatenate([jnp.where(lane < HEAD_DIM, x, zero), jnp.where(lane < HEAD_DIM, zero, x)], axis=0)


def _unstack_heads(x2, t):
    lane = lax.broadcasted_iota(jnp.int32, (t, 128), 1)
    return jnp.where(lane < HEAD_DIM, x2[:t], x2[t:])


def _attn_masks(t):
    r = lax.broadcasted_iota(jnp.int32, (t, t), 0)
    c = lax.broadcasted_iota(jnp.int32, (t, t), 1)
    row = lax.broadcasted_iota(jnp.int32, (2 * t, t), 0)
    col = lax.broadcasted_iota(jnp.int32, (2 * t, t), 1)
    causal = col < jnp.where(row >= t, row - t, row)
    return (r > c).astype(BF16), (r >= c).astype(BF16), causal


def _attn_sweep_cond(st):
    return jnp.logical_and(st[0] >= 0, st[1] > LOG_ZERO)


def _attn_scores(q2, kblk, after, causal):
    z = _dot_nt(q2, kblk)
    sp = jnp.maximum(z, 0.0) + jnp.log(1.0 + jnp.exp(-jnp.abs(z)))
    log_stay = -sp
    if causal is not None:
        log_stay = jnp.where(causal, log_stay, 0.0)
    return log_stay, z - sp, _split_dot(log_stay, after)


def _attention(q, k, v):
    s, d = q.shape
    t = min(ATTN_BLOCK, s)

    def body(q_ref, k_ref, v_ref, o_ref):
        i = pl.program_id(1)
        after, _, causal = _attn_masks(t)
        q2 = _stack_heads(q_ref[...])

        def step(kb, carry, acc, mask):
            start = pl.multiple_of(kb * t, t)
            kblk = k_ref[pl.ds(start, t), :]
            vblk = v_ref[pl.ds(start, t), :]
            log_stay, log_beta, later = _attn_scores(q2, kblk, after, mask)
            w = jnp.exp(log_beta + later + carry)
            if mask is not None:
                w = jnp.where(mask, w, 0.0)
            return carry + jnp.sum(log_stay, axis=1, keepdims=True), acc + _split_dot(w, vblk)

        carry, acc = step(i, jnp.zeros((2 * t, 1), F32), jnp.zeros((2 * t, 128), F32), causal)
        carry, acc = step(jnp.maximum(i - 1, 0), carry, acc, jnp.broadcast_to(i > 0, causal.shape))

        def loop(st):
            c, a = step(st[0], st[2], st[3], None)
            return st[0] - 1, jnp.max(c), c, a

        acc = lax.while_loop(_attn_sweep_cond, loop, (i - 2, jnp.max(carry), carry, acc))[3]
        o_ref[...] = _unstack_heads(acc, t)

    return pl.pallas_call(
        body, grid=(N_UNITS, s // t), name="attention",
        in_specs=[
            pl.BlockSpec((t, 128), lambda h, i: (i, h)),
            pl.BlockSpec((s, 128), lambda h, i: (0, h)),
            pl.BlockSpec((s, 128), lambda h, i: (0, h)),
        ],
        out_specs=pl.BlockSpec((t, 128), lambda h, i: (i, h)),
        out_shape=jax.ShapeDtypeStruct((s, d), F32),
        compiler_params=_params("arbitrary", "arbitrary"),
    )(q, k, v)


def _attention_bwd(q, k, v, do, o):
    s, d = q.shape
    t = min(ATTN_BLOCK, s)

    def body(q_ref, k_ref, v_ref, do_ref, o_ref, dq_ref, dk_ref, dv_ref):
        i = pl.program_id(1)

        @pl.when(i == 0)
        def _():
            dk_ref[...] = jnp.zeros_like(dk_ref)
            dv_ref[...] = jnp.zeros_like(dv_ref)

        after, from_here, causal = _attn_masks(t)
        q2 = _stack_heads(q_ref[...])
        do2 = _stack_heads(do_ref[...])

        def weights(kb, carry, mask):
            start = pl.multiple_of(kb * t, t)
            kblk = k_ref[pl.ds(start, t), :]
            vblk = v_ref[pl.ds(start, t), :]
            log_stay, log_beta, later = _attn_scores(q2, kblk, after, mask)
            w = jnp.exp(log_beta + later + carry)
            if mask is not None:
                w = jnp.where(mask, w, 0.0)
            g = w * _dot_nt(do2, vblk)
            return start, kblk, log_stay, log_beta, w, g

        lane = lax.broadcasted_iota(jnp.int32, (t, 128), 1)
        prod = do_ref[...].astype(F32) * o_ref[...]
        total = jnp.concatenate([
            jnp.sum(jnp.where(lane < HEAD_DIM, prod, 0.0), axis=1, keepdims=True),
            jnp.sum(jnp.where(lane < HEAD_DIM, 0.0, prod), axis=1, keepdims=True)], axis=0)
        zero = jnp.zeros((2 * t, 1), F32)

        def grad_step(kb, carry, seen, dq, mask):
            start, kblk, log_stay, log_beta, w, g = weights(kb, carry, mask)
            before = total - (_split_dot(g, from_here) + seen)
            beta = jnp.exp(log_beta)
            da = g * (1.0 - beta) - before * beta
            if mask is not None:
                da = jnp.where(mask, da, 0.0)
            dab = da.astype(BF16)
            dk_ref[pl.ds(start, t), :] += _dot_tn(dab, q2)
            dv_ref[pl.ds(start, t), :] += _dot_tn(w.astype(BF16), do2)
            return (carry + jnp.sum(log_stay, axis=1, keepdims=True), seen + jnp.sum(g, axis=1, keepdims=True),
                    dq + _dot(dab, kblk))

        carry, seen, dq = grad_step(i, zero, zero, jnp.zeros((2 * t, 128), F32), causal)
        carry, seen, dq = grad_step(jnp.maximum(i - 1, 0), carry, seen, dq, jnp.broadcast_to(i > 0, causal.shape))

        def grad_loop(st):
            c, sn, a = grad_step(st[0], st[2], st[3], st[4], None)
            return st[0] - 1, jnp.max(c), c, sn, a

        dq = lax.while_loop(_attn_sweep_cond, grad_loop, (i - 2, jnp.max(carry), carry, seen, dq))[4]
        dq_ref[...] = ATTN_SCALE * _unstack_heads(dq, t)

    blk = pl.BlockSpec((t, 128), lambda h, i: (i, h))
    full = pl.BlockSpec((s, 128), lambda h, i: (0, h))
    return pl.pallas_call(
        body, grid=(N_UNITS, s // t), name="attention_bwd",
        in_specs=[blk, full, full, blk, blk],
        out_specs=[blk, full, full],
        out_shape=[jax.ShapeDtypeStruct((s, d), F32)] * 3,
        compiler_params=_params("arbitrary", "arbitrary"),
    )(q, k, v, do, o)


def _matmul_res(name, a, w, x, alpha):
    s, kd = a.shape
    d = w.shape[-1]
    tm = _row_tile(s)

    def body(a_ref, w_ref, x_ref, o_ref):
        o_ref[...] = x_ref[...] + alpha * _dot(a_ref[...].astype(BF16), w_ref[...])

    return pl.pallas_call(
        body, grid=(s // tm,), name=name,
        in_specs=[pl.BlockSpec((tm, kd), lambda i: (i, 0)), pl.BlockSpec((kd, d), lambda i: (0, 0)), pl.BlockSpec((tm, d), lambda i: (i, 0))],
        out_specs=pl.BlockSpec((tm, d), lambda i: (i, 0)),
        out_shape=jax.ShapeDtypeStruct((s, d), F32),
        compiler_params=_params("arbitrary"),
    )(a, w, x)


def _matmul_nt(name, a, w):
    s, n = a.shape
    kd = w.shape[0]
    tm = _row_tile(s)

    def body(a_ref, w_ref, o_ref):
        o_ref[...] = _dot_nt(a_ref[...].astype(BF16), w_ref[...]).astype(BF16)

    return pl.pallas_call(
        body, grid=(s // tm,), name=name,
        in_specs=[pl.BlockSpec((tm, n), lambda i: (i, 0)), pl.BlockSpec((kd, n), lambda i: (0, 0))],
        out_specs=pl.BlockSpec((tm, kd), lambda i: (i, 0)),
        out_shape=jax.ShapeDtypeStruct((s, kd), BF16),
        compiler_params=_params("arbitrary"),
    )(a, w)


def _pool_matrix(rows0, cols0, nr, nc, window, transpose):
    r = rows0 + lax.broadcasted_iota(jnp.int32, (nr, nc), 0)
    c = cols0 + lax.broadcasted_iota(jnp.int32, (nr, nc), 1)
    tt, ss = (c, r) if transpose else (r, c)
    inside = jnp.logical_and(tt - ss >= 0, tt - ss < window)
    cnt = jnp.minimum(tt + 1, window).astype(F32)
    return jnp.where(inside, 1.0 / cnt, 0.0) - jnp.where(tt == ss, 1.0, 0.0)


def _pool_tile(s):
    return min(256, s)


def _pool_fwd(u, wgrp, scale, x, side=None):
    s, d = u.shape
    tm = _pool_tile(s)
    halo = min(POOL_HALO, tm)
    ratio = tm // halo

    def body(u_ref, prev_ref, w_ref, sc_ref, x_ref, o_ref, p_ref):
        i = pl.program_id(0)
        t0 = i * tm
        for gi, window in enumerate(POOL_WINDOWS):
            cols = slice(POOL_GROUP * gi, POOL_GROUP * (gi + 1))
            pooled = _dot_3x(_pool_matrix(t0, t0, tm, tm, window, False), u_ref[:, cols])
            prev = jnp.where(i > 0, prev_ref[:, cols], 0.0)
            pooled += _dot_3x(_pool_matrix(t0, t0 - halo, tm, halo, window, False), prev)
            pb = pooled.astype(BF16)
            p_ref[:, cols] = pb
            o_ref[:, cols] = x_ref[:, cols] + _dot(pb, w_ref[gi]) * sc_ref[:, cols]

    row = pl.BlockSpec((tm, d), lambda i: (i, 0))
    return _host_call(
        body, (u, u, wgrp, scale, x), side, grid=(s // tm,), name="pool_fwd",
        in_specs=[
            row,
            pl.BlockSpec((halo, d), lambda i: (jnp.maximum(i * ratio - 1, 0), 0)),
            pl.BlockSpec((4, POOL_GROUP, POOL_GROUP), lambda i: (0, 0, 0)),
            pl.BlockSpec((1, d), lambda i: (0, 0)),
            row,
        ],
        out_specs=[row, row],
        out_shape=[jax.ShapeDtypeStruct((s, d), F32), jax.ShapeDtypeStruct((s, d), BF16)],
        compiler_params=_params("arbitrary"),
    )


def _pool_bwd_group(dx, pooled, wgrp, scale):
    s, d = dx.shape
    tm = _pool_tile(s)
    nsteps = s // tm

    def body(dx_ref, p_ref, w_ref, sc_ref, dp_ref, dw_ref, dsc_ref, acc_ref):
        i = pl.program_id(0)

        @pl.when(i == 0)
        def _():
            acc_ref[...] = jnp.zeros_like(acc_ref)
            dsc_ref[...] = jnp.zeros_like(dsc_ref)

        for gi in range(len(POOL_WINDOWS)):
            cols = slice(POOL_GROUP * gi, POOL_GROUP * (gi + 1))
            pb = p_ref[:, cols]
            dxg = dx_ref[:, cols]
            y = _dot(pb, w_ref[gi])
            dsc_ref[:, cols] += jnp.sum(dxg * y, axis=0, keepdims=True)
            dyb = (dxg * sc_ref[:, cols]).astype(BF16)
            dp_ref[:, cols] = _dot_nt(dyb, w_ref[gi])
            acc_ref[gi] += _dot_tn(pb, dyb)

        @pl.when(i == nsteps - 1)
        def _():
            dw_ref[...] = acc_ref[...].astype(BF16)

    row = pl.BlockSpec((tm, d), lambda i: (i, 0))
    grp = pl.BlockSpec((4, POOL_GROUP, POOL_GROUP), lambda i: (0, 0, 0))
    vec = pl.BlockSpec((1, d), lambda i: (0, 0))
    return pl.pallas_call(
        body, grid=(nsteps,), name="pool_bwd_group",
        in_specs=[row, row, grp, vec],
        out_specs=[row, grp, vec],
        out_shape=[jax.ShapeDtypeStruct((s, d), F32), jax.ShapeDtypeStruct((4, POOL_GROUP, POOL_GROUP), BF16), jax.ShapeDtypeStruct((1, d), F32)],
        scratch_shapes=[pltpu.VMEM((4, POOL_GROUP, POOL_GROUP), F32)],
        compiler_params=_params("arbitrary"),
    )(dx, pooled, wgrp, scale)


def _pool_bwd_window(dp):
    s, d = dp.shape
    tm = _pool_tile(s)
    halo = min(POOL_HALO, tm)
    ratio = tm // halo
    nsteps = s // tm

    def body(dp_ref, next_ref, o_ref):
        i = pl.program_id(0)
        t0 = i * tm
        for gi, window in enumerate(POOL_WINDOWS):
            cols = slice(POOL_GROUP * gi, POOL_GROUP * (gi + 1))
            du = _dot_3x(_pool_matrix(t0, t0, tm, tm, window, True), dp_ref[:, cols])
            nxt = jnp.where(i < nsteps - 1, next_ref[:, cols], 0.0)
            du += _dot_3x(_pool_matrix(t0, t0 + tm, tm, halo, window, True), nxt)
            o_ref[:, cols] = du.astype(BF16)

    row = pl.BlockSpec((tm, d), lambda i: (i, 0))
    return pl.pallas_call(
        body, grid=(nsteps,), name="pool_bwd_window",
        in_specs=[row, pl.BlockSpec((halo, d), lambda i: (jnp.minimum((i + 1) * ratio, s // halo - 1), 0))],
        out_specs=row,
        out_shape=jax.ShapeDtypeStruct((s, d), BF16),
        compiler_params=_params("arbitrary"),
    )(dp, dp)


def _ple_fwd(x, g, wgate, p, wproj, side=None):
    s, d = x.shape
    pd = p.shape[-1]
    tm = _row_tile(s)

    def body(x_ref, g_ref, wg_ref, p_ref, wp_ref, o_ref, h_ref, sig_ref, proj_ref):
        x = x_ref[...]
        h = _rms(x, g_ref[...]).astype(BF16)
        sig = jax.nn.sigmoid(_dot(h, wg_ref[...]))
        proj = _dot(p_ref[...].astype(BF16), wp_ref[...])
        o_ref[...] = x + sig * proj
        h_ref[...] = h
        sig_ref[...] = sig.astype(BF16)
        proj_ref[...] = proj.astype(BF16)

    row = pl.BlockSpec((tm, d), lambda i: (i, 0))
    return _host_call(
        body, (x, g, wgate, p, wproj), side, grid=(s // tm,), name="ple_fwd",
        in_specs=[
            row,
            pl.BlockSpec((1, d), lambda i: (0, 0)),
            pl.BlockSpec((d, d), lambda i: (0, 0)),
            pl.BlockSpec((tm, pd), lambda i: (i, 0)),
            pl.BlockSpec((pd, d), lambda i: (0, 0)),
        ],
        out_specs=[row] * 4,
        out_shape=[jax.ShapeDtypeStruct((s, d), F32)] + [jax.ShapeDtypeStruct((s, d), BF16)] * 3,
        compiler_params=_params("arbitrary"),
    )


def _ple_bwd_gate(dx, sig, proj):
    s, d = dx.shape
    tm = _row_tile(s)

    def body(dx_ref, sig_ref, proj_ref, dg_ref, dp_ref):
        dx = dx_ref[...]
        sig = sig_ref[...].astype(F32)
        dg_ref[...] = (dx * proj_ref[...].astype(F32) * (sig * (1.0 - sig))).astype(BF16)
        dp_ref[...] = (dx * sig).astype(BF16)

    row = pl.BlockSpec((tm, d), lambda i: (i, 0))
    return pl.pallas_call(
        body, grid=(s // tm,), name="ple_bwd_gate",
        in_specs=[row, row, row], out_specs=[row, row],
        out_shape=[jax.ShapeDtypeStruct((s, d), BF16)] * 2,
        compiler_params=_params("arbitrary"),
    )(dx, sig, proj)


def _loss_head(y, target):
    s, d = y.shape
    tm = _row_tile(s)
    nsteps = s // tm

    def body(y_ref, t_ref, dy_ref, loss_ref, acc_ref):
        i = pl.program_id(0)

        @pl.when(i == 0)
        def _():
            acc_ref[...] = jnp.zeros_like(acc_ref)

        err = y_ref[...] - t_ref[...]
        dy_ref[...] = err * (1.0 / d)
        acc_ref[...] += jnp.sum(jnp.mean(err * err, axis=-1, keepdims=True), axis=0, keepdims=True)

        @pl.when(i == nsteps - 1)
        def _():
            loss_ref[...] = 0.5 * acc_ref[...]

    row = pl.BlockSpec((tm, d), lambda i: (i, 0))
    return pl.pallas_call(
        body, grid=(nsteps,), name="loss_head",
        in_specs=[row, row], out_specs=[row, pl.BlockSpec((8, 128), lambda i: (0, 0))],
        out_shape=[jax.ShapeDtypeStruct((s, d), F32), jax.ShapeDtypeStruct((8, 128), F32)],
        scratch_shapes=[pltpu.VMEM((8, 128), F32)],
        compiler_params=_params("arbitrary"),
    )(y, target)


def _adamw_math(w, g, m, v):
    m = ADAM_B1 * m + (1.0 - ADAM_B1) * g
    v = ADAM_B2 * v + (1.0 - ADAM_B2) * (g * g)
    m_hat = m / (1.0 - ADAM_B1 ** ADAM_STEP)
    v_hat = v / (1.0 - ADAM_B2 ** ADAM_STEP)
    delta = -ADAM_LR * (m_hat / (jnp.sqrt(v_hat) + ADAM_EPS) + ADAM_WD * w)
    return delta, m, v


def _adamw_layer(parts, w, m, v, layer, outs):
    nl, r, c = w.shape
    tr = max(t for t in range(16, r + 1, 16) if r % t == 0 and t * c <= ADAMW_BLOCK_ELEMS)

    def body(p_ref, w_ref, m_ref, v_ref, *rest):
        g_ref, d_ref, nm_ref, nv_ref = rest[-4:]
        g = p_ref[0].astype(F32)
        for dev in range(1, N_DEV):
            g = g + p_ref[dev].astype(F32)
        delta, nm, nv = _adamw_math(w_ref[0], g, m_ref[0], v_ref[0])
        g_ref[0] = g
        d_ref[0] = delta
        nm_ref[0] = nm
        nv_ref[0] = nv

    slab = pl.BlockSpec((1, tr, c), lambda i: (layer, i, 0))
    any_spec = pl.BlockSpec(memory_space=pl.ANY)
    shape = jax.ShapeDtypeStruct(w.shape, F32)
    carried = [] if outs is None else list(outs)
    return pl.pallas_call(
        body, grid=(r // tr,), name="adamw",
        in_specs=[pl.BlockSpec((N_DEV, tr, c), lambda i: (0, i, 0)), slab, slab, slab] + [any_spec] * len(carried),
        out_specs=[slab] * 4,
        out_shape=[shape] * 4,
        input_output_aliases={4 + n: n for n in range(len(carried))},
        compiler_params=_params("arbitrary"),
    )(parts, w, m, v, *carried)


def _adamw_small(parts, w, m, v):
    r, c = w.shape

    def body(p_ref, w_ref, m_ref, v_ref, g_ref, d_ref, nm_ref, nv_ref):
        g = p_ref[0:r, :]
        for dev in range(1, N_DEV):
            g = g + p_ref[dev * r:(dev + 1) * r, :]
        delta, nm, nv = _adamw_math(w_ref[...], g, m_ref[...], v_ref[...])
        g_ref[...] = g
        d_ref[...] = delta
        nm_ref[...] = nm
        nv_ref[...] = nv

    shape = jax.ShapeDtypeStruct((r, c), F32)
    return pl.pallas_call(body, name="adamw_small", out_shape=[shape] * 4)(parts, w, m, v)


def _my_place():
    return lax.axis_index("x"), lax.axis_index("y"), lax.axis_index("c")


def _peer(k):
    x, y, c = _my_place()
    return (x ^ (k >> 2), y ^ ((k >> 1) & 1), c ^ (k & 1))


def _block_of(place):
    x, y, c = place
    return 4 * x + 2 * y + c


def _gather_small(block):
    m_per, n = block.shape

    def body(x_ref, out_ref, send_sems, recv_sems, local_sem):
        me = _block_of(_my_place())

        def rows(b):
            return out_ref.at[pl.ds(b * m_per, m_per), :]

        mine = pltpu.make_async_copy(x_ref, rows(me), local_sem)
        mine.start()
        sends = []
        for k in range(1, N_DEV):
            cp = pltpu.make_async_remote_copy(
                src_ref=x_ref, dst_ref=rows(me), send_sem=send_sems.at[k - 1], recv_sem=recv_sems.at[k - 1],
                device_id=_peer(k), device_id_type=MESH)
            cp.start()
            sends.append(cp)
        for k in range(1, N_DEV):
            src = rows(_block_of(_peer(k)))
            pltpu.make_async_remote_copy(
                src_ref=src, dst_ref=src, send_sem=send_sems.at[k - 1], recv_sem=recv_sems.at[k - 1],
                device_id=_peer(k), device_id_type=MESH).wait_recv()
        for cp in sends:
            cp.wait_send()
        mine.wait()

    return pl.pallas_call(
        body, name="gather_small",
        out_shape=jax.ShapeDtypeStruct((N_DEV * m_per, n), block.dtype),
        in_specs=[pl.BlockSpec(memory_space=pltpu.VMEM)],
        out_specs=pl.BlockSpec(memory_space=pltpu.VMEM),
        scratch_shapes=[pltpu.SemaphoreType.DMA((N_DEV - 1,)), pltpu.SemaphoreType.DMA((N_DEV - 1,)), pltpu.SemaphoreType.DMA],
    )(block)


def _gather_weights(name, stacks, layers):
    n_t = len(stacks)

    def body(*refs):
        srcs, outs = refs[:n_t], refs[n_t:2 * n_t]
        send_sems, recv_sems, local_sems = refs[2 * n_t:]
        x, y, c = _my_place()
        me, sibling = (x, y, c), (x, y, 1 - c)
        chips = [(1 - x, y), (x, 1 - y), (1 - x, 1 - y)]

        def copy(t, k, block, to, src=None):
            slot = outs[t].at[_block_of(block)]
            return pltpu.make_async_remote_copy(
                src_ref=slot if src is None else src, dst_ref=slot,
                send_sem=send_sems.at[t, k], recv_sem=recv_sems.at[t, k], device_id=to, device_id_type=MESH)

        local, sends = [], []
        for t in range(n_t):
            src = srcs[t].at[layers[t]]
            own = pltpu.make_async_copy(src, outs[t].at[_block_of(me)], local_sems.at[t])
            own.start()
            local.append(own)
            sends.append(copy(t, 0, me, sibling, src=src))
            sends += [copy(t, 1 + j, me, (*chip, c), src=src) for j, chip in enumerate(chips)]
        for cp in sends:
            cp.start()
        for t in range(n_t):
            for j, chip in enumerate(chips):
                copy(t, 1 + j, (*chip, c), me).wait_recv()
                passed = copy(t, 4 + j, (*chip, c), sibling)
                passed.start()
                sends.append(passed)
        for t in range(n_t):
            copy(t, 0, sibling, me).wait_recv()
            for j, chip in enumerate(chips):
                copy(t, 4 + j, (*chip, 1 - c), me).wait_recv()
        for cp in sends:
            cp.wait_send()
        for cp in local:
            cp.wait()

    any_spec = pl.BlockSpec(memory_space=pl.ANY)
    return pl.pallas_call(
        body, name=name,
        out_shape=[jax.ShapeDtypeStruct((N_DEV,) + st.shape[1:], st.dtype) for st in stacks],
        in_specs=[any_spec] * n_t, out_specs=[any_spec] * n_t,
        scratch_shapes=[pltpu.SemaphoreType.DMA((n_t, N_DEV - 1)), pltpu.SemaphoreType.DMA((n_t, N_DEV - 1)), pltpu.SemaphoreType.DMA((n_t,))],
    )(*stacks)


class _ScatterSide:
    def __init__(self, grads):
        self.operands = list(grads)
        self.n = len(self.operands)

    def out_shape(self):
        return [jax.ShapeDtypeStruct(g.shape, g.dtype) for g in self.operands]

    def scratch(self):
        return [pltpu.SemaphoreType.DMA((self.n, N_DEV - 1)), pltpu.SemaphoreType.DMA((self.n, N_DEV - 1)), pltpu.SemaphoreType.DMA((self.n,))]

    def _copies(self, srcs, outs, send_sems, recv_sems, local_sems):
        me = _block_of(_my_place())
        local, sends, arrivals = [], [], []
        for t in range(self.n):
            local.append(pltpu.make_async_copy(srcs[t].at[me], outs[t].at[me], local_sems.at[t]))
            for k in range(1, N_DEV):
                sems = dict(send_sem=send_sems.at[t, k - 1], recv_sem=recv_sems.at[t, k - 1], device_id=_peer(k), device_id_type=MESH)
                sends.append(pltpu.make_async_remote_copy(src_ref=srcs[t].at[_block_of(_peer(k))], dst_ref=outs[t].at[me], **sems))
                slot = outs[t].at[_block_of(_peer(k))]
                arrivals.append(pltpu.make_async_remote_copy(src_ref=slot, dst_ref=slot, **sems))
        return local, sends, arrivals

    def start(self, *refs):
        local, sends, _ = self._copies(*refs)
        for cp in local + sends:
            cp.start()

    def finish(self, *refs):
        local, sends, arrivals = self._copies(*refs)
        for cp in arrivals:
            cp.wait_recv()
        for cp in sends:
            cp.wait_send()
        for cp in local:
            cp.wait()


class _GatherSide(_ScatterSide):
    def __init__(self, stacks, layers):
        super().__init__(stacks)
        self.layers = list(layers)

    def out_shape(self):
        return [jax.ShapeDtypeStruct((N_DEV,) + st.shape[1:], st.dtype) for st in self.operands]

    def _copies(self, srcs, outs, send_sems, recv_sems, local_sems):
        me = _block_of(_my_place())
        local, sends, arrivals = [], [], []
        for t in range(self.n):
            src = srcs[t].at[self.layers[t]]
            local.append(pltpu.make_async_copy(src, outs[t].at[me], local_sems.at[t]))
            for k in range(1, N_DEV):
                sems = dict(send_sem=send_sems.at[t, k - 1], recv_sem=recv_sems.at[t, k - 1], device_id=_peer(k), device_id_type=MESH)
                sends.append(pltpu.make_async_remote_copy(src_ref=src, dst_ref=outs[t].at[me], **sems))
                slot = outs[t].at[_block_of(_peer(k))]
                arrivals.append(pltpu.make_async_remote_copy(src_ref=slot, dst_ref=slot, **sems))
        return local, sends, arrivals


def _scatter_grads(name, grads):
    side = _ScatterSide(grads)
    n_t = side.n

    def body(*refs):
        parts = (refs[:n_t], refs[n_t:2 * n_t]) + tuple(refs[2 * n_t:])
        side.start(*parts)
        side.finish(*parts)

    any_spec = pl.BlockSpec(memory_space=pl.ANY)
    return pl.pallas_call(
        body, name=name, out_shape=side.out_shape(), in_specs=[any_spec] * n_t, out_specs=[any_spec] * n_t,
        scratch_shapes=side.scratch(),
    )(*grads)


def _host_call(body, args, side, *, grid, in_specs, out_specs, out_shape, scratch_shapes=(), **kw):
    if side is None:
        return pl.pallas_call(body, grid=grid, in_specs=in_specs, out_specs=out_specs, out_shape=out_shape,
                              scratch_shapes=list(scratch_shapes), **kw)(*args), None
    n_in, n_out, n_scr, n_side = len(in_specs), len(out_specs), len(scratch_shapes), side.n

    def hosted(*refs):
        cuts = [n_in, n_side, n_out, n_side, n_scr, 3]
        groups, at = [], 0
        for c in cuts:
            groups.append(refs[at:at + c])
            at += c
        ins, side_in, outs, side_out, scr, sems = groups
        first, last = None, None
        for axis, size in enumerate(grid):
            at_start, at_end = pl.program_id(axis) == 0, pl.program_id(axis) == size - 1
            first = at_start if first is None else jnp.logical_and(first, at_start)
            last = at_end if last is None else jnp.logical_and(last, at_end)

        @pl.when(first)
        def _():
            side.start(side_in, side_out, *sems)

        body(*ins, *outs, *scr)

        @pl.when(last)
        def _():
            side.finish(side_in, side_out, *sems)

    any_spec = pl.BlockSpec(memory_space=pl.ANY)
    res = pl.pallas_call(
        hosted, grid=grid, in_specs=list(in_specs) + [any_spec] * n_side, out_specs=list(out_specs) + [any_spec] * n_side,
        out_shape=list(out_shape) + side.out_shape(), scratch_shapes=list(scratch_shapes) + side.scratch(), **kw,
    )(*args, *side.operands)
    return res[:n_out], res[n_out:]


def _ffn_forward(x, g, wt, wd, carried):
    sides = {k: _GatherSide(*v) if v[0] else None for k, v in carried.items()}
    parts = {}
    (h, gate, up, act), parts["up"] = _ffn_up(x, g, wt, sides.get("up"))
    (x_new,), parts["down"] = _ffn_down(act, wd, x, sides.get("down"))
    return x_new, (x, g, h, gate, up, act), parts


def _ffn_backward(dx, saved, wt, wd, carried):
    x, g, h, gate, up, act = saved
    sides = {k: _ScatterSide(v) if v else None for k, v in carried.items()}
    parts = {}
    (dgate, dup), parts["act"] = _ffn_bwd_act(dx, wd, gate, up, sides.get("act"))
    d_wd = _grad_ffn_down(act, dx)
    (dx_in, dg), parts["dh"] = _ffn_dh(dgate, dup, wt, x, g, dx, sides.get("dh"))
    d_wt, parts["gu"] = _grad_ffn_gu(h, dgate, dup, sides.get("gu"))
    return dx_in, dg, d_wt.reshape(N_DEV, -1, D_MODEL), d_wd.reshape(N_DEV, -1, D_MODEL), parts


def kernel(x, p, norm_ffn1, w_ffn1_gu, w_ffn1_down, norm_mix, w_qkv, q_norm, k_norm, w_o, w_pool_in, w_pool_grp, pool_scale, norm_ffn2, w_ffn2_gu, w_ffn2_down, norm_ple, w_ple_gate, w_ple_proj, loss_target, m_norm_ffn1, m_w_ffn1_gu, m_w_ffn1_down, m_norm_mix, m_w_qkv, m_q_norm, m_k_norm, m_w_o, m_w_pool_in, m_w_pool_grp, m_pool_scale, m_norm_ffn2, m_w_ffn2_gu, m_w_ffn2_down, m_norm_ple, m_w_ple_gate, m_w_ple_proj, v_norm_ffn1, v_w_ffn1_gu, v_w_ffn1_down, v_norm_mix, v_w_qkv, v_q_norm, v_k_norm, v_w_o, v_w_pool_in, v_w_pool_grp, v_pool_scale, v_norm_ffn2, v_w_ffn2_gu, v_w_ffn2_down, v_norm_ple, v_w_ple_gate, v_w_ple_proj):
    d = D_MODEL
    xs = x[0]
    target = loss_target[0]
    me = _block_of(_my_place())

    def tr(w):
        return jnp.swapaxes(w, 1, 2)

    big = dict(w_ffn1_gu=tr(w_ffn1_gu), w_ffn1_down=w_ffn1_down, w_qkv=w_qkv, w_o=w_o, w_pool_in=w_pool_in,
               w_pool_grp=w_pool_grp.reshape(2, 4 * 32, POOL_GROUP), w_ffn2_gu=tr(w_ffn2_gu), w_ffn2_down=w_ffn2_down,
               w_ple_gate=w_ple_gate, w_ple_proj=w_ple_proj)
    moments = dict(
        w_ffn1_gu=(tr(m_w_ffn1_gu), tr(v_w_ffn1_gu)), w_ffn1_down=(m_w_ffn1_down, v_w_ffn1_down), w_qkv=(m_w_qkv, v_w_qkv),
        w_o=(m_w_o, v_w_o), w_pool_in=(m_w_pool_in, v_w_pool_in),
        w_pool_grp=(m_w_pool_grp.reshape(2, 4 * 32, POOL_GROUP), v_w_pool_grp.reshape(2, 4 * 32, POOL_GROUP)),
        w_ffn2_gu=(tr(m_w_ffn2_gu), tr(v_w_ffn2_gu)), w_ffn2_down=(m_w_ffn2_down, v_w_ffn2_down),
        w_ple_gate=(m_w_ple_gate, v_w_ple_gate), w_ple_proj=(m_w_ple_proj, v_w_ple_proj))
    half = {name: _cast_bf16(w) for name, w in big.items()}

    def layer_names(i):
        mixer = ["w_qkv", "w_o"] if i % 2 == 0 else ["w_pool_in", "w_pool_grp"]
        return ["w_ffn1_gu", "w_ffn1_down"] + mixer + ["w_ffn2_gu", "w_ffn2_down", "w_ple_gate", "w_ple_proj"]

    def layer_index(name, i):
        return i // 2 if name in ("w_qkv", "w_o", "w_pool_in", "w_pool_grp") else i

    scale_all = _gather_small(jnp.pad(pool_scale, ((0, 6), (0, 0))))
    scale_full = scale_all.reshape(N_DEV, 8, 128)[:, :2].transpose(1, 0, 2).reshape(2, d)

    saved = []
    weights = []
    cur = xs

    fwd_carriers = {
        ("ffn1", "up"): ["w_ffn1_gu"], ("ffn2", "up"): ["w_ffn2_gu"],
        ("ffn1", "down"): ["w_ffn1_down"], ("ffn2", "down"): ["w_ffn2_down"],
        ("mix", "pool"): ["w_qkv", "w_o"],
        ("ple", "ple"): ["w_ple_gate", "w_ple_proj", "w_pool_in", "w_pool_grp"],
    }

    def next_weights(i, stage):
        nxt = layer_names(i + 1) if i + 1 < DEPTH else []
        out = {}
        for (st, host), names in fwd_carriers.items():
            if st == stage:
                take = [n for n in names if n in nxt]
                out[host] = ([half[n] for n in take], [layer_index(n, i + 1) for n in take])
        return out

    def arrived_weights(i, stage, parts, into):
        nxt = layer_names(i + 1) if i + 1 < DEPTH else []
        for (st, host), names in fwd_carriers.items():
            if st == stage and parts.get(host) is not None:
                into.update(zip([n for n in names if n in nxt], parts[host]))

    names = layer_names(0)
    coming = dict(zip(names, _gather_weights("gather_weights_0", [half[n] for n in names], [layer_index(n, 0) for n in names])))
    for i in range(DEPTH):
        wl, coming = coming, {}
        assert set(wl) == set(layer_names(i)), sorted(wl)
        for n in ("w_ffn1_gu", "w_ffn2_gu", "w_ffn1_down", "w_ffn2_down", "w_o", "w_pool_in", "w_ple_gate"):
            if n in wl:
                wl[n] = wl[n].reshape(-1, d)
        if "w_pool_grp" in wl:
            wl["w_pool_grp"] = wl["w_pool_grp"].reshape(N_DEV, 4, 32, POOL_GROUP).transpose(1, 0, 2, 3).reshape(4, POOL_GROUP, POOL_GROUP)
        for n in ("w_qkv", "w_ple_proj"):
            if n in wl:
                wl[n] = _from_column_shards(wl[n])
        weights.append(wl)
        j = i // 2
        rec = {}
        cur, rec["ffn1"], parts = _ffn_forward(cur, norm_ffn1[i][None], wl["w_ffn1_gu"], wl["w_ffn1_down"], next_weights(i, "ffn1"))
        arrived_weights(i, "ffn1", parts, coming)
        x1 = cur
        gm = norm_mix[i][None]
        if i % 2 == 0:
            qg = jnp.tile(q_norm[j], d // HEAD_DIM)[None]
            kg = jnp.tile(k_norm[j], d // HEAD_DIM)[None]
            hm, qkv = _norm_matmul("qkv_proj", x1, gm, wl["w_qkv"], (d, d), lambda a, b: (0, b), _pick_all, 3 * d, d, 3)
            qn, kn, vb = _qk_norm(qkv, qg, kg)
            o = _attention(qn, kn, vb)
            cur = _matmul_res("attn_out", o, wl["w_o"], x1, 1.0)
            rec["mix"] = (x1, gm, hm, qkv, qg, kg, qn, kn, vb, o)
        else:
            sc = scale_full[j][None]
            hm, u = _norm_matmul("pool_in", x1, gm, wl["w_pool_in"], (d, d), lambda a, b: (0, 0), _pick_all, d, d, 1)
            stacks, layers = next_weights(i, "mix")["pool"]
            (cur, pooled), parts = _pool_fwd(u, wl["w_pool_grp"], sc, x1, _GatherSide(stacks, layers) if stacks else None)
            arrived_weights(i, "mix", {"pool": parts}, coming)
            rec["mix"] = (x1, gm, hm, sc, pooled)
        cur, rec["ffn2"], parts = _ffn_forward(cur, norm_ffn2[i][None], wl["w_ffn2_gu"], wl["w_ffn2_down"], next_weights(i, "ffn2"))
        arrived_weights(i, "ffn2", parts, coming)
        x3 = cur
        gp = norm_ple[i][None]
        stacks, layers = next_weights(i, "ple")["ple"]
        (cur, hp, sig, proj), parts = _ple_fwd(x3, gp, wl["w_ple_gate"], p[i, 0], wl["w_ple_proj"],
                                               _GatherSide(stacks, layers) if stacks else None)
        arrived_weights(i, "ple", {"ple": parts}, coming)
        rec["ple"] = (x3, gp, hp, sig, proj)
        saved.append(rec)

    dy, loss_part = _loss_head(cur, target)
    loss = lax.psum(loss_part[0, 0], ("x", "y", "c"))

    small = {n: [None] * DEPTH for n in ("norm_ffn1", "norm_mix", "norm_ffn2", "norm_ple")}
    small.update(q_norm=[None] * 2, k_norm=[None] * 2, pool_scale=[None] * 2)
    results = {name: None for name in big}

    def update(layer, parts):
        for n, part in parts.items():
            mm, vv = moments[n]
            results[n] = _adamw_layer(part, big[n], mm, vv, layer_index(n, layer), results[n])

    carriers = {
        ("ffn2", "gu"): ["w_ffn1_gu"], ("ffn1", "gu"): ["w_ffn2_gu"],
        ("ffn2", "dh"): ["w_ffn1_down", "w_ple_gate", "w_ple_proj"],
        ("ffn1", "dh"): ["w_ffn2_down", "w_o", "w_pool_in", "w_pool_grp"],
        ("ffn2", "act"): ["w_qkv"],
    }
    above = {}

    def carried(ffn):
        return {host: [above[n] for n in names if n in above] for (f, host), names in carriers.items() if f == ffn}

    def received(ffn, parts, into):
        for (f, host), names in carriers.items():
            if f == ffn and parts.get(host) is not None:
                into.update(zip([n for n in names if n in above], parts[host]))

    dcur = dy
    for i in reversed(range(DEPTH)):
        wl, rec, j = weights[i], saved[i], i // 2
        grads = {}
        arrived = {}
        x3, gp, hp, sig, proj = rec["ple"]
        dgate, dproj = _ple_bwd_gate(dcur, sig, proj)
        dx3, small["norm_ple"][i] = _square_dh("ple_dh", dgate, wl["w_ple_gate"], x3, gp, dcur)
        grads["w_ple_gate"] = _grad_square("grad_ple_gate", hp, dgate).reshape(N_DEV, d // N_DEV, d)
        grads["w_ple_proj"] = _column_shards(_grad_square("grad_ple_proj", p[i, 0], dproj))
        dx2, small["norm_ffn2"][i], grads["w_ffn2_gu"], grads["w_ffn2_down"], parts = _ffn_backward(
            dx3, rec["ffn2"], wl["w_ffn2_gu"], wl["w_ffn2_down"], carried("ffn2"))
        received("ffn2", parts, arrived)
        if i % 2 == 0:
            x1, gm, hm, qkv, qg, kg, qn, kn, vb, o = rec["mix"]
            do = _matmul_nt("attn_out_bwd", dx2, wl["w_o"])
            grads["w_o"] = _grad_square("grad_attn_out", o, dx2).reshape(N_DEV, d // N_DEV, d)
            dq, dk, dv = _attention_bwd(qn, kn, vb, do, o)
            dqkv, dqg, dkg = _qk_norm_bwd(qkv, dq, dk, dv, qg, kg)
            small["q_norm"][j], small["k_norm"][j] = dqg[0:1], dkg[0:1]
            dx1, small["norm_mix"][i] = _qkv_dh(dqkv, wl["w_qkv"], x1, gm, dx2)
            grads["w_qkv"] = _column_shards(_grad_square("grad_qkv", hm, dqkv))
        else:
            x1, gm, hm, sc, pooled = rec["mix"]
            dpool, dgrp, small["pool_scale"][j] = _pool_bwd_group(dx2, pooled, wl["w_pool_grp"], sc)
            grads["w_pool_grp"] = dgrp.reshape(4, N_DEV, 32, POOL_GROUP).transpose(1, 0, 2, 3).reshape(N_DEV, 4 * 32, POOL_GROUP)
            du = _pool_bwd_window(dpool)
            dx1, small["norm_mix"][i] = _square_dh("pool_dh", du, wl["w_pool_in"], x1, gm, dx2)
            grads["w_pool_in"] = _grad_square("grad_pool_in", hm, du).reshape(N_DEV, d // N_DEV, d)
        dcur, small["norm_ffn1"][i], grads["w_ffn1_gu"], grads["w_ffn1_down"], parts = _ffn_backward(
            dx1, rec["ffn1"], wl["w_ffn1_gu"], wl["w_ffn1_down"], carried("ffn1"))
        received("ffn1", parts, arrived)
        if above:
            assert set(arrived) == set(above), (sorted(arrived), sorted(above))
            update(i + 1, arrived)
        above = grads

    names = layer_names(0)
    update(0, dict(zip(names, _scatter_grads("scatter_grads_0", [above[n] for n in names]))))

    def lanes(a):
        return jnp.pad(a, ((0, 0), (0, d - a.shape[-1])))

    order = [("norm_ffn1", DEPTH), ("norm_mix", DEPTH), ("norm_ffn2", DEPTH), ("norm_ple", DEPTH), ("pool_scale", 2), ("q_norm", 2), ("k_norm", 2)]
    rows = jnp.concatenate([lanes(g) for name, _ in order for g in small[name]], axis=0)
    n_rows = rows.shape[0]
    pad_rows = -n_rows % 8
    rows = jnp.pad(rows, ((0, pad_rows), (0, 0)))
    gathered = _gather_small(rows)

    def own_lanes(a):
        return lax.dynamic_update_slice(jnp.zeros((a.shape[0], d), F32), a, (0, me * 128))

    def pack(values):
        mats = [own_lanes(values[name]) if name == "pool_scale" else lanes(values[name]) for name, _ in order]
        return jnp.pad(jnp.concatenate(mats, axis=0), ((0, pad_rows), (0, 0)))

    small_w = dict(norm_ffn1=norm_ffn1, norm_mix=norm_mix, norm_ffn2=norm_ffn2, norm_ple=norm_ple, pool_scale=pool_scale, q_norm=q_norm, k_norm=k_norm)
    small_m = dict(norm_ffn1=m_norm_ffn1, norm_mix=m_norm_mix, norm_ffn2=m_norm_ffn2, norm_ple=m_norm_ple, pool_scale=m_pool_scale, q_norm=m_q_norm, k_norm=m_k_norm)
    small_v = dict(norm_ffn1=v_norm_ffn1, norm_mix=v_norm_mix, norm_ffn2=v_norm_ffn2, norm_ple=v_norm_ple, pool_scale=v_pool_scale, q_norm=v_q_norm, k_norm=v_k_norm)
    packed = _adamw_small(gathered, pack(small_w), pack(small_m), pack(small_v))

    def unpack(mat):
        out, at = {}, 0
        for name, n in order:
            blk = mat[at:at + n]
            at += n
            if name == "pool_scale":
                out[name] = lax.dynamic_slice(blk, (0, me * 128), (n, 128))
            elif name in ("q_norm", "k_norm"):
                out[name] = blk[:, :HEAD_DIM]
            else:
                out[name] = blk
        return out

    small_out = [unpack(mat) for mat in packed]

    def result(kind, name):
        if name in small_w:
            return small_out[kind][name]
        r = results[name][kind]
        if name in ("w_ffn1_gu", "w_ffn2_gu"):
            return tr(r)
        return r.reshape(w_pool_grp.shape) if name == "w_pool_grp" else r

    weight_names = ["norm_ffn1", "w_ffn1_gu", "w_ffn1_down", "norm_mix", "w_qkv", "q_norm", "k_norm", "w_o", "w_pool_in", "w_pool_grp",
                    "pool_scale", "norm_ffn2", "w_ffn2_gu", "w_ffn2_down", "norm_ple", "w_ple_gate", "w_ple_proj"]
    outs = [loss, dcur[None]]
    for kind in range(4):
        outs += [result(kind, name) for name in weight_names]
    return tuple(outs)
```

```python
import math

import jax
import jax.numpy as jnp
from jax import lax
from jax.experimental import pallas as pl
from jax.experimental.pallas import tpu as pltpu

F32 = jnp.float32
BF16 = jnp.bfloat16

N_DEV = 8
DEPTH = 4
D_MODEL = 1024
N_UNITS = D_MODEL // 128
HEAD_DIM = 64
POOL_WINDOWS = (2, 4, 8, 16)
POOL_GROUP = 256
POOL_HALO = 128
EPS = 1e-6
ATTN_SCALE = 1.0 / math.sqrt(HEAD_DIM)
ATTN_BLOCK = 256
LOG_ZERO = -104.0

ADAM_LR = 0.001
ADAM_B1 = 0.9
ADAM_B2 = 0.999
ADAM_EPS = 1e-08
ADAM_WD = 0.01
ADAM_STEP = 10
ADAMW_BLOCK_ELEMS = 128 * 1024

VMEM_LIMIT = 56 * 1024 * 1024
MESH = pl.DeviceIdType.MESH

NT_DIMS = (((1,), (1,)), ((), ()))
TN_DIMS = (((0,), (0,)), ((), ()))


def _params(*sem):
    return pltpu.CompilerParams(dimension_semantics=sem, vmem_limit_bytes=VMEM_LIMIT)


def _row_tile(s):
    return min(512, s)


def _grad_tile(s):
    return min(2048, s)


def _block_spec(block, index, mode=None):
    return pl.BlockSpec(block, index) if mode is None else pl.BlockSpec(block, index, pipeline_mode=mode)


def _dot(a, b):
    return jnp.dot(a, b, preferred_element_type=F32)


def _dot_nt(a, b):
    return lax.dot_general(a, b, NT_DIMS, preferred_element_type=F32)


def _dot_tn(a, b):
    return lax.dot_general(a, b, TN_DIMS, preferred_element_type=F32)


def _dot_f32(a, b):
    return jnp.dot(a, b, precision=lax.Precision.HIGHEST, preferred_element_type=F32)


def _dot_3x(a, b):
    return jnp.dot(a, b, precision=lax.Precision.HIGH, preferred_element_type=F32)


def _rms(x, g):
    r = lax.rsqrt(jnp.mean(x * x, axis=-1, keepdims=True) + EPS)
    return x * r * g


def _rms_bwd(dy, x, g):
    r = lax.rsqrt(jnp.mean(x * x, axis=-1, keepdims=True) + EPS)
    xh = x * r
    dg = jnp.sum(dy * xh, axis=0, keepdims=True)
    dxh = dy * g
    dx = r * (dxh - xh * jnp.mean(dxh * xh, axis=-1, keepdims=True))
    return dx, dg


def _cast_bf16(w):
    l, r, c = w.shape

    def body(w_ref, o_ref):
        o_ref[...] = w_ref[...].astype(BF16)

    return pl.pallas_call(
        body, grid=(l,), name="cast_bf16",
        in_specs=[pl.BlockSpec((1, r, c), lambda i: (i, 0, 0))],
        out_specs=pl.BlockSpec((1, r, c), lambda i: (i, 0, 0)),
        out_shape=jax.ShapeDtypeStruct(w.shape, BF16),
        compiler_params=_params("arbitrary"),
    )(w)


def _ffn_up(x, g, wt, side=None):
    s, d = x.shape
    f = wt.shape[0] // 2
    tn = f // 2
    tm = _row_tile(s)

    def body(x_ref, g_ref, w_ref, h_ref, gate_ref, up_ref, act_ref):
        h = _rms(x_ref[...], g_ref[...]).astype(BF16)
        h_ref[...] = h
        for half in range(2):
            cols = slice(half * tn, (half + 1) * tn)
            gate = _dot_nt(h, w_ref[half * tn:(half + 1) * tn, :])
            up = _dot_nt(h, w_ref[f + half * tn:f + (half + 1) * tn, :])
            gate_ref[:, cols] = gate.astype(BF16)
            up_ref[:, cols] = up.astype(BF16)
            act_ref[:, cols] = (gate * jax.nn.sigmoid(gate) * up).astype(BF16)

    wide = pl.BlockSpec((tm, f), lambda i: (i, 0))
    hidden = jax.ShapeDtypeStruct((s, f), BF16)
    return _host_call(
        body, (x, g, wt), side, grid=(s // tm,), name="ffn_up",
        in_specs=[
            pl.BlockSpec((tm, d), lambda i: (i, 0)),
            pl.BlockSpec((1, d), lambda i: (0, 0)),
            _block_spec((2 * f, d), lambda i: (0, 0), pl.Buffered(1)),
        ],
        out_specs=[pl.BlockSpec((tm, d), lambda i: (i, 0)), wide, wide, wide],
        out_shape=[jax.ShapeDtypeStruct((s, d), BF16), hidden, hidden, hidden],
        compiler_params=_params("arbitrary"),
    )


def _ffn_down(act, wd, x, side=None):
    s, f = act.shape
    d = wd.shape[-1]
    tm = _row_tile(s)

    def body(a_ref, w_ref, x_ref, o_ref):
        o_ref[...] = x_ref[...] + 0.5 * _dot(a_ref[...], w_ref[...])

    row = pl.BlockSpec((tm, d), lambda i: (i, 0))
    return _host_call(
        body, (act, wd, x), side, grid=(s // tm,), name="ffn_down",
        in_specs=[pl.BlockSpec((tm, f), lambda i: (i, 0)), _block_spec((f, d), lambda i: (0, 0), pl.Buffered(1)), row],
        out_specs=[row],
        out_shape=[jax.ShapeDtypeStruct((s, d), F32)],
        compiler_params=_params("arbitrary"),
    )


def _ffn_bwd_act(dx, wd, gate, up, side=None):
    s, d = dx.shape
    f = gate.shape[-1]
    tn = f // 2
    tm = _row_tile(s)

    def body(dx_ref, w_ref, gate_ref, up_ref, dgate_ref, dup_ref):
        dact = _dot_nt((0.5 * dx_ref[...]).astype(BF16), w_ref[...])
        gate = gate_ref[...].astype(F32)
        sig = jax.nn.sigmoid(gate)
        silu = gate * sig
        dgate_ref[...] = (dact * up_ref[...].astype(F32) * (sig + silu * (1.0 - sig))).astype(BF16)
        dup_ref[...] = (dact * silu).astype(BF16)

    col = pl.BlockSpec((tm, tn), lambda j, i: (i, j))
    hidden = jax.ShapeDtypeStruct((s, f), BF16)
    return _host_call(
        body, (dx, wd, gate, up), side, grid=(2, s // tm), name="ffn_bwd_act",
        in_specs=[pl.BlockSpec((tm, d), lambda j, i: (i, 0)), pl.BlockSpec((tn, d), lambda j, i: (j, 0)), col, col],
        out_specs=[col, col],
        out_shape=[hidden, hidden],
        compiler_params=_params("arbitrary", "arbitrary"),
    )


def _norm_bwd_matmul(name, operands, products, nk, x, g, dres, w_rows_contract=False, side=None):
    s, d = x.shape
    tm = _row_tile(s)
    n = len(operands)

    def body(*refs):
        x_ref, g_ref, dres_ref, dx_ref, dg_ref, acc_ref = refs[n:]
        i, k = pl.program_id(0), pl.program_id(1)

        @pl.when(jnp.logical_and(i == 0, k == 0))
        def _():
            dg_ref[...] = jnp.zeros_like(dg_ref)

        total = None
        for a_at, a_pick, w_at, w_pick in products:
            prod = (_dot if w_rows_contract else _dot_nt)(a_pick(refs[a_at]), w_pick(refs[w_at]))
            total = prod if total is None else total + prod

        if nk > 1:
            @pl.when(k == 0)
            def _():
                acc_ref[...] = total

            @pl.when(jnp.logical_and(k > 0, k < nk - 1))
            def _():
                acc_ref[...] += total

        @pl.when(k == nk - 1)
        def _():
            dy = total if nk == 1 else acc_ref[...] + total
            dx, dg = _rms_bwd(dy, x_ref[...], g_ref[...])
            dx_ref[...] = dres_ref[...] + dx
            dg_ref[...] += dg

    row = pl.BlockSpec((tm, d), lambda i, k: (i, 0))
    vec = pl.BlockSpec((1, d), lambda i, k: (0, 0))
    return _host_call(
        body, [op[0] for op in operands] + [x, g, dres], side, grid=(s // tm, nk), name=name,
        in_specs=[_block_spec(*op[1:]) for op in operands] + [row, vec, row],
        out_specs=[row, vec],
        out_shape=[jax.ShapeDtypeStruct((s, d), F32), jax.ShapeDtypeStruct((1, d), F32)],
        scratch_shapes=[pltpu.VMEM((tm, d), F32)],
        compiler_params=_params("arbitrary", "arbitrary"),
    )


def _ffn_dh(dgate, dup, wt, x, g, dres, side=None):
    s, d = x.shape
    f = dgate.shape[-1]
    tm = _row_tile(s)
    once = pl.Buffered(1)
    operands = [
        (dgate, (tm, f), lambda i, k: (i, 0)),
        (dup, (tm, f), lambda i, k: (i, 0)),
        (wt, (f, d), lambda i, k: (0, 0), once),
        (wt, (f, d), lambda i, k: (1, 0), once),
    ]
    products = [(0, _pick_all, 2, _pick_all), (1, _pick_all, 3, _pick_all)]
    return _norm_bwd_matmul("ffn_dh", operands, products, 1, x, g, dres, w_rows_contract=True, side=side)


def _square_dh(name, a, w, x, g, dres):
    s, d = x.shape
    tm = _row_tile(s)
    operands = [(a, (tm, d), lambda i, k: (i, 0)), (w, (d, d), lambda i, k: (0, 0))]
    return _norm_bwd_matmul(name, operands, [(0, _pick_all, 1, _pick_all)], 1, x, g, dres)[0]


def _qkv_dh(dqkv, wqkv, x, g, dres):
    s, d = x.shape
    tm = _row_tile(s)
    operands = [(dqkv, (tm, 3 * d), lambda i, k: (i, 0)), (wqkv, (d, 3 * d), lambda i, k: (0, 0), pl.Buffered(1))]
    return _norm_bwd_matmul("qkv_dh", operands, [(0, _pick_all, 1, _pick_all)], 1, x, g, dres)[0]


def _grad_matmul(name, a, a_block, a_index, a_pick, b, b_block, b_index, b_picks, out_shape, out_block, out_index, out_stores, nj, scale=1.0):
    s = a.shape[-2]
    nk = s // _grad_tile(s)
    n_prod = len(b_picks)

    def body(a_ref, b_ref, o_ref, *acc_refs):
        k = pl.program_id(1)
        av = a_pick(a_ref)
        if av.dtype != BF16:
            av = (scale * av).astype(BF16)
        for b_pick, store, acc_ref in zip(b_picks, out_stores, acc_refs):
            bv = b_pick(b_ref)
            if bv.dtype != BF16:
                bv = bv.astype(BF16)
            prod = _dot_tn(av, bv)
            if nk == 1:
                store(o_ref, prod.astype(BF16))
                continue

            @pl.when(k == 0)
            def _():
                acc_ref[...] = prod

            @pl.when(jnp.logical_and(k > 0, k < nk - 1))
            def _():
                acc_ref[...] += prod

            @pl.when(k == nk - 1)
            def _():
                store(o_ref, (acc_ref[...] + prod).astype(BF16))

    m = jax.eval_shape(a_pick, jax.ShapeDtypeStruct(a_block, a.dtype)).shape[-1]
    nn = jax.eval_shape(b_picks[0], jax.ShapeDtypeStruct(b_block, b.dtype)).shape[-1]
    acc_shape = (m, nn)
    return pl.pallas_call(
        body, grid=(nj, nk), name=name,
        in_specs=[pl.BlockSpec(a_block, a_index), pl.BlockSpec(b_block, b_index)],
        out_specs=pl.BlockSpec(out_block, out_index),
        out_shape=jax.ShapeDtypeStruct(out_shape, BF16),
        scratch_shapes=[pltpu.VMEM(acc_shape, F32) for _ in range(n_prod)],
        compiler_params=_params("arbitrary", "arbitrary"),
    )(a, b)


def _pick_all(r):
    return r[...]


def _store_all(r, v):
    r[...] = v


def _grad_ffn_down(act, dx):
    s, f = act.shape
    d = dx.shape[-1]
    tk = _grad_tile(s)
    tn = f // 2
    return _grad_matmul(
        "grad_ffn_down", act, (tk, tn), lambda j, k: (k, j), _pick_all,
        dx, (tk, d), lambda j, k: (k, 0), [lambda r: 0.5 * r[...]],
        (f, d), (tn, d), lambda j, k: (j, 0), [_store_all], 2)


def _grad_ffn_gu(h, dgate, dup, side=None):
    s, d = h.shape
    f = dgate.shape[-1]
    tn = f // 2
    tk = min(1024, s)
    nk = s // tk

    def body(h_ref, dgate_ref, dup_ref, o_ref, acc_ref):
        j, k = pl.program_id(0), pl.program_id(1)

        def accumulate(a_ref):
            prod = _dot_tn(a_ref[...], h_ref[...])
            if nk == 1:
                o_ref[...] = prod.astype(BF16)
                return

            @pl.when(k == 0)
            def _():
                acc_ref[...] = prod

            @pl.when(jnp.logical_and(k > 0, k < nk - 1))
            def _():
                acc_ref[...] += prod

            @pl.when(k == nk - 1)
            def _():
                o_ref[...] = (acc_ref[...] + prod).astype(BF16)

        @pl.when(j < 2)
        def _():
            accumulate(dgate_ref)

        @pl.when(j >= 2)
        def _():
            accumulate(dup_ref)

    (out,), parts = _host_call(
        body, (h, dgate, dup), side, grid=(4, nk), name="grad_ffn_gu",
        in_specs=[
            pl.BlockSpec((tk, d), lambda j, k: (k, 0)),
            pl.BlockSpec((tk, tn), lambda j, k: (jnp.where(j < 2, k, 0), jnp.minimum(j, 1))),
            pl.BlockSpec((tk, tn), lambda j, k: (jnp.where(j >= 2, k, 0), jnp.maximum(j - 2, 0))),
        ],
        out_specs=[pl.BlockSpec((tn, d), lambda j, k: (j, 0))],
        out_shape=[jax.ShapeDtypeStruct((2 * f, d), BF16)],
        scratch_shapes=[pltpu.VMEM((tn, d), F32)],
        compiler_params=_params("arbitrary", "arbitrary"),
    )
    return out, parts


def _grad_square(name, a, b):
    s, m = a.shape
    n = b.shape[-1]
    tk = _grad_tile(s)
    tn = min(n, D_MODEL)
    return _grad_matmul(
        name, a, (tk, m), lambda j, k: (k, 0), _pick_all,
        b, (tk, tn), lambda j, k: (k, j), [_pick_all],
        (m, n), (m, tn), lambda j, k: (0, j), [_store_all], n // tn)


def _column_shards(w):
    m, n = w.shape
    return w.reshape(m, N_DEV, n // N_DEV).transpose(1, 0, 2)


def _from_column_shards(w):
    nd, m, n = w.shape
    return w.transpose(1, 0, 2).reshape(m, nd * n)


def _norm_matmul(name, x, g, w, w_block, w_index, w_pick, n_total, tn, nj):
    s, d = x.shape
    tm = _row_tile(s)

    def body(x_ref, g_ref, w_ref, h_ref, y_ref):
        @pl.when(pl.program_id(1) == 0)
        def _():
            h_ref[...] = _rms(x_ref[...], g_ref[...]).astype(BF16)

        y_ref[...] = _dot(h_ref[...], w_pick(w_ref))

    return pl.pallas_call(
        body, grid=(s // tm, nj), name=name,
        in_specs=[
            pl.BlockSpec((tm, d), lambda i, j: (i, 0)),
            pl.BlockSpec((1, d), lambda i, j: (0, 0)),
            pl.BlockSpec(w_block, w_index),
        ],
        out_specs=[pl.BlockSpec((tm, d), lambda i, j: (i, 0)), pl.BlockSpec((tm, tn), lambda i, j: (i, j))],
        out_shape=[jax.ShapeDtypeStruct((s, d), BF16), jax.ShapeDtypeStruct((s, n_total), F32)],
        compiler_params=_params("arbitrary", "arbitrary"),
    )(x, g, w)


def _head_mean_matrix():
    r = lax.broadcasted_iota(jnp.int32, (128, 128), 0) // HEAD_DIM
    c = lax.broadcasted_iota(jnp.int32, (128, 128), 1) // HEAD_DIM
    return jnp.where(r == c, 1.0 / HEAD_DIM, 0.0).astype(F32)


def _qk_norm(qkv, qg, kg):
    s = qkv.shape[0]
    d = D_MODEL
    tm = _row_tile(s)

    def body(q_ref, k_ref, v_ref, qg_ref, kg_ref, qo_ref, ko_ref, vo_ref):
        mean_m = _head_mean_matrix()
        for u in range(N_UNITS):
            cols = slice(128 * u, 128 * (u + 1))
            for src, gain, dst, scale in ((q_ref, qg_ref, qo_ref, ATTN_SCALE), (k_ref, kg_ref, ko_ref, 1.0)):
                xs = src[:, cols]
                r = lax.rsqrt(_dot_3x(xs * xs, mean_m) + EPS)
                y = xs * r * gain[:, cols]
                dst[:, cols] = (y * scale).astype(BF16) if scale != 1.0 else y.astype(BF16)
        vo_ref[...] = v_ref[...].astype(BF16)

    blk = lambda c: pl.BlockSpec((tm, d), lambda i: (i, c))
    vec = pl.BlockSpec((1, d), lambda i: (0, 0))
    return pl.pallas_call(
        body, grid=(s // tm,), name="qk_norm",
        in_specs=[blk(0), blk(1), blk(2), vec, vec],
        out_specs=[blk(0)] * 3,
        out_shape=[jax.ShapeDtypeStruct((s, d), BF16)] * 3,
        compiler_params=_params("arbitrary"),
    )(qkv, qkv, qkv, qg, kg)


def _qk_norm_bwd(qkv, dq, dk, dv, qg, kg):
    s = qkv.shape[0]
    d = D_MODEL
    tm = _row_tile(s)
    nsteps = s // tm

    def body(q_ref, k_ref, dq_ref, dk_ref, dv_ref, qg_ref, kg_ref, o_ref, dqg_ref, dkg_ref, acc_ref):
        i = pl.program_id(0)

        @pl.when(i == 0)
        def _():
            acc_ref[...] = jnp.zeros_like(acc_ref)

        mean_m = _head_mean_matrix()
        for u in range(N_UNITS):
            cols = slice(128 * u, 128 * (u + 1))
            for n, (src, dsrc, gain) in enumerate(((q_ref, dq_ref, qg_ref), (k_ref, dk_ref, kg_ref))):
                xs = src[:, cols]
                dy = dsrc[:, cols]
                r = lax.rsqrt(_dot_3x(xs * xs, mean_m) + EPS)
                xh = xs * r
                acc_ref[n:n + 1, :] += jnp.sum(dy * xh, axis=0, keepdims=True)
                dxh = dy * gain[:, cols]
                dx = r * (dxh - xh * _dot_3x(dxh * xh, mean_m))
                o_ref[:, 128 * (N_UNITS * n + u):128 * (N_UNITS * n + u + 1)] = dx.astype(BF16)
        o_ref[:, 2 * d:3 * d] = dv_ref[...].astype(BF16)

        @pl.when(i == nsteps - 1)
        def _():
            r = lax.broadcasted_iota(jnp.int32, (128, 128), 0) % HEAD_DIM
            c = lax.broadcasted_iota(jnp.int32, (128, 128), 1) % HEAD_DIM
            fold = jnp.where(r == c, 1.0, 0.0).astype(F32)
            folded = _dot_f32(acc_ref[...], fold)
            dqg_ref[...] = jnp.broadcast_to(folded[0:1], (8, 128))
            dkg_ref[...] = jnp.broadcast_to(folded[1:2], (8, 128))

    blk = lambda c: pl.BlockSpec((tm, d), lambda i: (i, c))
    row = pl.BlockSpec((tm, d), lambda i: (i, 0))
    vec = pl.BlockSpec((1, d), lambda i: (0, 0))
    small = pl.BlockSpec((8, 128), lambda i: (0, 0))
    return pl.pallas_call(
        body, grid=(nsteps,), name="qk_norm_bwd",
        in_specs=[blk(0), blk(1), row, row, row, vec, vec],
        out_specs=[pl.BlockSpec((tm, 3 * d), lambda i: (i, 0)), small, small],
        out_shape=[jax.ShapeDtypeStruct((s, 3 * d), BF16), jax.ShapeDtypeStruct((8, 128), F32), jax.ShapeDtypeStruct((8, 128), F32)],
        scratch_shapes=[pltpu.VMEM((8, 128), F32)],
        compiler_params=_params("arbitrary"),
    )(qkv, qkv, dq, dk, dv, qg, kg)


def _split_dot(x, m):
    hi = x.astype(BF16)
    lo = (x - hi.astype(F32)).astype(BF16)
    return _dot(hi, m) + _dot(lo, m)


def _stack_heads(x):
    lane = lax.broadcasted_iota(jnp.int32, x.shape, 1)
    zero = jnp.zeros_like(x)
    return jnp.concatenate([jnp.where(lane < HEAD_DIM, x, zero), jnp.where(lane < HEAD_DIM, zero, x)], axis=0)


def _unstack_heads(x2, t):
    lane = lax.broadcasted_iota(jnp.int32, (t, 128), 1)
    return jnp.where(lane < HEAD_DIM, x2[:t], x2[t:])


def _attn_masks(t):
    r = lax.broadcasted_iota(jnp.int32, (t, t), 0)
    c = lax.broadcasted_iota(jnp.int32, (t, t), 1)
    row = lax.broadcasted_iota(jnp.int32, (2 * t, t), 0)
    col = lax.broadcasted_iota(jnp.int32, (2 * t, t), 1)
    causal = col < jnp.where(row >= t, row - t, row)
    return (r > c).astype(BF16), (r >= c).astype(BF16), causal


def _attn_sweep_cond(st):
    return jnp.logical_and(st[0] >= 0, st[1] > LOG_ZERO)


def _attn_scores(q2, kblk, after, causal):
    z = _dot_nt(q2, kblk)
    sp = jnp.maximum(z, 0.0) + jnp.log(1.0 + jnp.exp(-jnp.abs(z)))
    log_stay = -sp
    if causal is not None:
        log_stay = jnp.where(causal, log_stay, 0.0)
    return log_stay, z - sp, _split_dot(log_stay, after)


def _attention(q, k, v, side=None):
    s, d = q.shape
    t = min(ATTN_BLOCK, s)

    def body(q_ref, k_ref, v_ref, o_ref):
        i = pl.program_id(1)
        after, _, causal = _attn_masks(t)
        q2 = _stack_heads(q_ref[...])

        def step(kb, carry, acc, mask):
            start = pl.multiple_of(kb * t, t)
            kblk = k_ref[pl.ds(start, t), :]
            vblk = v_ref[pl.ds(start, t), :]
            log_stay, log_beta, later = _attn_scores(q2, kblk, after, mask)
            w = jnp.exp(log_beta + later + carry)
            if mask is not None:
                w = jnp.where(mask, w, 0.0)
            return carry + jnp.sum(log_stay, axis=1, keepdims=True), acc + _split_dot(w, vblk)

        carry, acc = step(i, jnp.zeros((2 * t, 1), F32), jnp.zeros((2 * t, 128), F32), causal)
        carry, acc = step(jnp.maximum(i - 1, 0), carry, acc, jnp.broadcast_to(i > 0, causal.shape))

        def loop(st):
            c, a = step(st[0], st[2], st[3], None)
            return st[0] - 1, jnp.max(c), c, a

        acc = lax.while_loop(_attn_sweep_cond, loop, (i - 2, jnp.max(carry), carry, acc))[3]
        o_ref[...] = _unstack_heads(acc, t)

    (o,), parts = _host_call(
        body, (q, k, v), side, grid=(N_UNITS, s // t), name="attention",
        in_specs=[
            pl.BlockSpec((t, 128), lambda h, i: (i, h)),
            pl.BlockSpec((s, 128), lambda h, i: (0, h)),
            pl.BlockSpec((s, 128), lambda h, i: (0, h)),
        ],
        out_specs=[pl.BlockSpec((t, 128), lambda h, i: (i, h))],
        out_shape=[jax.ShapeDtypeStruct((s, d), F32)],
        compiler_params=_params("arbitrary", "arbitrary"),
    )
    return o, parts


def _attention_bwd(q, k, v, do, o, side=None):
    s, d = q.shape
    t = min(ATTN_BLOCK, s)

    def body(q_ref, k_ref, v_ref, do_ref, o_ref, dq_ref, dk_ref, dv_ref):
        i = pl.program_id(1)

        @pl.when(i == 0)
        def _():
            dk_ref[...] = jnp.zeros_like(dk_ref)
            dv_ref[...] = jnp.zeros_like(dv_ref)

        after, from_here, causal = _attn_masks(t)
        q2 = _stack_heads(q_ref[...])
        do2 = _stack_heads(do_ref[...])

        def weights(kb, carry, mask):
            start = pl.multiple_of(kb * t, t)
            kblk = k_ref[pl.ds(start, t), :]
            vblk = v_ref[pl.ds(start, t), :]
            log_stay, log_beta, later = _attn_scores(q2, kblk, after, mask)
            w = jnp.exp(log_beta + later + carry)
            if mask is not None:
                w = jnp.where(mask, w, 0.0)
            g = w * _dot_nt(do2, vblk)
            return start, kblk, log_stay, log_beta, w, g

        lane = lax.broadcasted_iota(jnp.int32, (t, 128), 1)
        prod = do_ref[...].astype(F32) * o_ref[...]
        total = jnp.concatenate([
            jnp.sum(jnp.where(lane < HEAD_DIM, prod, 0.0), axis=1, keepdims=True),
            jnp.sum(jnp.where(lane < HEAD_DIM, 0.0, prod), axis=1, keepdims=True)], axis=0)
        zero = jnp.zeros((2 * t, 1), F32)

        def grad_step(kb, carry, seen, dq, mask):
            start, kblk, log_stay, log_beta, w, g = weights(kb, carry, mask)
            before = total - (_split_dot(g, from_here) + seen)
            beta = jnp.exp(log_beta)
            da = g * (1.0 - beta) - before * beta
            if mask is not None:
                da = jnp.where(mask, da, 0.0)
            dab = da.astype(BF16)
            dk_ref[pl.ds(start, t), :] += _dot_tn(dab, q2)
            dv_ref[pl.ds(start, t), :] += _dot_tn(w.astype(BF16), do2)
            return (carry + jnp.sum(log_stay, axis=1, keepdims=True), seen + jnp.sum(g, axis=1, keepdims=True),
                    dq + _dot(dab, kblk))

        carry, seen, dq = grad_step(i, zero, zero, jnp.zeros((2 * t, 128), F32), causal)
        carry, seen, dq = grad_step(jnp.maximum(i - 1, 0), carry, seen, dq, jnp.broadcast_to(i > 0, causal.shape))

        def grad_loop(st):
            c, sn, a = grad_step(st[0], st[2], st[3], st[4], None)
            return st[0] - 1, jnp.max(c), c, sn, a

        dq = lax.while_loop(_attn_sweep_cond, grad_loop, (i - 2, jnp.max(carry), carry, seen, dq))[4]
        dq_ref[...] = ATTN_SCALE * _unstack_heads(dq, t)

    blk = pl.BlockSpec((t, 128), lambda h, i: (i, h))
    full = pl.BlockSpec((s, 128), lambda h, i: (0, h))
    return _host_call(
        body, (q, k, v, do, o), side, grid=(N_UNITS, s // t), name="attention_bwd",
        in_specs=[blk, full, full, blk, blk],
        out_specs=[blk, full, full],
        out_shape=[jax.ShapeDtypeStruct((s, d), F32)] * 3,
        compiler_params=_params("arbitrary", "arbitrary"),
    )


def _matmul_res(name, a, w, x, alpha):
    s, kd = a.shape
    d = w.shape[-1]
    tm = _row_tile(s)

    def body(a_ref, w_ref, x_ref, o_ref):
        o_ref[...] = x_ref[...] + alpha * _dot(a_ref[...].astype(BF16), w_ref[...])

    return pl.pallas_call(
        body, grid=(s // tm,), name=name,
        in_specs=[pl.BlockSpec((tm, kd), lambda i: (i, 0)), pl.BlockSpec((kd, d), lambda i: (0, 0)), pl.BlockSpec((tm, d), lambda i: (i, 0))],
        out_specs=pl.BlockSpec((tm, d), lambda i: (i, 0)),
        out_shape=jax.ShapeDtypeStruct((s, d), F32),
        compiler_params=_params("arbitrary"),
    )(a, w, x)


def _matmul_nt(name, a, w):
    s, n = a.shape
    kd = w.shape[0]
    tm = _row_tile(s)

    def body(a_ref, w_ref, o_ref):
        o_ref[...] = _dot_nt(a_ref[...].astype(BF16), w_ref[...]).astype(BF16)

    return pl.pallas_call(
        body, grid=(s // tm,), name=name,
        in_specs=[pl.BlockSpec((tm, n), lambda i: (i, 0)), pl.BlockSpec((kd, n), lambda i: (0, 0))],
        out_specs=pl.BlockSpec((tm, kd), lambda i: (i, 0)),
        out_shape=jax.ShapeDtypeStruct((s, kd), BF16),
        compiler_params=_params("arbitrary"),
    )(a, w)


def _pool_matrix(rows0, cols0, nr, nc, window, transpose):
    r = rows0 + lax.broadcasted_iota(jnp.int32, (nr, nc), 0)
    c = cols0 + lax.broadcasted_iota(jnp.int32, (nr, nc), 1)
    tt, ss = (c, r) if transpose else (r, c)
    inside = jnp.logical_and(tt - ss >= 0, tt - ss < window)
    cnt = jnp.minimum(tt + 1, window).astype(F32)
    return jnp.where(inside, 1.0 / cnt, 0.0) - jnp.where(tt == ss, 1.0, 0.0)


def _pool_tile(s):
    return min(256, s)


def _pool_fwd(u, wgrp, scale, x, side=None):
    s, d = u.shape
    tm = _pool_tile(s)
    halo = min(POOL_HALO, tm)
    ratio = tm // halo

    def body(u_ref, prev_ref, w_ref, sc_ref, x_ref, o_ref, p_ref):
        i = pl.program_id(0)
        t0 = i * tm
        for gi, window in enumerate(POOL_WINDOWS):
            cols = slice(POOL_GROUP * gi, POOL_GROUP * (gi + 1))
            pooled = _dot_3x(_pool_matrix(t0, t0, tm, tm, window, False), u_ref[:, cols])
            prev = jnp.where(i > 0, prev_ref[:, cols], 0.0)
            pooled += _dot_3x(_pool_matrix(t0, t0 - halo, tm, halo, window, False), prev)
            pb = pooled.astype(BF16)
            p_ref[:, cols] = pb
            o_ref[:, cols] = x_ref[:, cols] + _dot(pb, w_ref[gi]) * sc_ref[:, cols]

    row = pl.BlockSpec((tm, d), lambda i: (i, 0))
    return _host_call(
        body, (u, u, wgrp, scale, x), side, grid=(s // tm,), name="pool_fwd",
        in_specs=[
            row,
            pl.BlockSpec((halo, d), lambda i: (jnp.maximum(i * ratio - 1, 0), 0)),
            pl.BlockSpec((4, POOL_GROUP, POOL_GROUP), lambda i: (0, 0, 0)),
            pl.BlockSpec((1, d), lambda i: (0, 0)),
            row,
        ],
        out_specs=[row, row],
        out_shape=[jax.ShapeDtypeStruct((s, d), F32), jax.ShapeDtypeStruct((s, d), BF16)],
        compiler_params=_params("arbitrary"),
    )


def _pool_bwd_group(dx, pooled, wgrp, scale):
    s, d = dx.shape
    tm = _pool_tile(s)
    nsteps = s // tm

    def body(dx_ref, p_ref, w_ref, sc_ref, dp_ref, dw_ref, dsc_ref, acc_ref):
        i = pl.program_id(0)

        @pl.when(i == 0)
        def _():
            acc_ref[...] = jnp.zeros_like(acc_ref)
            dsc_ref[...] = jnp.zeros_like(dsc_ref)

        for gi in range(len(POOL_WINDOWS)):
            cols = slice(POOL_GROUP * gi, POOL_GROUP * (gi + 1))
            pb = p_ref[:, cols]
            dxg = dx_ref[:, cols]
            y = _dot(pb, w_ref[gi])
            dsc_ref[:, cols] += jnp.sum(dxg * y, axis=0, keepdims=True)
            dyb = (dxg * sc_ref[:, cols]).astype(BF16)
            dp_ref[:, cols] = _dot_nt(dyb, w_ref[gi])
            acc_ref[gi] += _dot_tn(pb, dyb)

        @pl.when(i == nsteps - 1)
        def _():
            dw_ref[...] = acc_ref[...].astype(BF16)

    row = pl.BlockSpec((tm, d), lambda i: (i, 0))
    grp = pl.BlockSpec((4, POOL_GROUP, POOL_GROUP), lambda i: (0, 0, 0))
    vec = pl.BlockSpec((1, d), lambda i: (0, 0))
    return pl.pallas_call(
        body, grid=(nsteps,), name="pool_bwd_group",
        in_specs=[row, row, grp, vec],
        out_specs=[row, grp, vec],
        out_shape=[jax.ShapeDtypeStruct((s, d), F32), jax.ShapeDtypeStruct((4, POOL_GROUP, POOL_GROUP), BF16), jax.ShapeDtypeStruct((1, d), F32)],
        scratch_shapes=[pltpu.VMEM((4, POOL_GROUP, POOL_GROUP), F32)],
        compiler_params=_params("arbitrary"),
    )(dx, pooled, wgrp, scale)


def _pool_bwd_window(dp):
    s, d = dp.shape
    tm = _pool_tile(s)
    halo = min(POOL_HALO, tm)
    ratio = tm // halo
    nsteps = s // tm

    def body(dp_ref, next_ref, o_ref):
        i = pl.program_id(0)
        t0 = i * tm
        for gi, window in enumerate(POOL_WINDOWS):
            cols = slice(POOL_GROUP * gi, POOL_GROUP * (gi + 1))
            du = _dot_3x(_pool_matrix(t0, t0, tm, tm, window, True), dp_ref[:, cols])
            nxt = jnp.where(i < nsteps - 1, next_ref[:, cols], 0.0)
            du += _dot_3x(_pool_matrix(t0, t0 + tm, tm, halo, window, True), nxt)
            o_ref[:, cols] = du.astype(BF16)

    row = pl.BlockSpec((tm, d), lambda i: (i, 0))
    return pl.pallas_call(
        body, grid=(nsteps,), name="pool_bwd_window",
        in_specs=[row, pl.BlockSpec((halo, d), lambda i: (jnp.minimum((i + 1) * ratio, s // halo - 1), 0))],
        out_specs=row,
        out_shape=jax.ShapeDtypeStruct((s, d), BF16),
        compiler_params=_params("arbitrary"),
    )(dp, dp)


def _ple_fwd(x, g, wgate, p, wproj, side=None):
    s, d = x.shape
    pd = p.shape[-1]
    tm = _row_tile(s)

    def body(x_ref, g_ref, wg_ref, p_ref, wp_ref, o_ref, h_ref, sig_ref, proj_ref):
        x = x_ref[...]
        h = _rms(x, g_ref[...]).astype(BF16)
        sig = jax.nn.sigmoid(_dot(h, wg_ref[...]))
        proj = _dot(p_ref[...].astype(BF16), wp_ref[...])
        o_ref[...] = x + sig * proj
        h_ref[...] = h
        sig_ref[...] = sig.astype(BF16)
        proj_ref[...] = proj.astype(BF16)

    row = pl.BlockSpec((tm, d), lambda i: (i, 0))
    return _host_call(
        body, (x, g, wgate, p, wproj), side, grid=(s // tm,), name="ple_fwd",
        in_specs=[
            row,
            pl.BlockSpec((1, d), lambda i: (0, 0)),
            pl.BlockSpec((d, d), lambda i: (0, 0)),
            pl.BlockSpec((tm, pd), lambda i: (i, 0)),
            pl.BlockSpec((pd, d), lambda i: (0, 0)),
        ],
        out_specs=[row] * 4,
        out_shape=[jax.ShapeDtypeStruct((s, d), F32)] + [jax.ShapeDtypeStruct((s, d), BF16)] * 3,
        compiler_params=_params("arbitrary"),
    )


def _ple_bwd_gate(dx, sig, proj):
    s, d = dx.shape
    tm = _row_tile(s)

    def body(dx_ref, sig_ref, proj_ref, dg_ref, dp_ref):
        dx = dx_ref[...]
        sig = sig_ref[...].astype(F32)
        dg_ref[...] = (dx * proj_ref[...].astype(F32) * (sig * (1.0 - sig))).astype(BF16)
        dp_ref[...] = (dx * sig).astype(BF16)

    row = pl.BlockSpec((tm, d), lambda i: (i, 0))
    return pl.pallas_call(
        body, grid=(s // tm,), name="ple_bwd_gate",
        in_specs=[row, row, row], out_specs=[row, row],
        out_shape=[jax.ShapeDtypeStruct((s, d), BF16)] * 2,
        compiler_params=_params("arbitrary"),
    )(dx, sig, proj)


def _loss_head(y, target):
    s, d = y.shape
    tm = _row_tile(s)
    nsteps = s // tm

    def body(y_ref, t_ref, dy_ref, loss_ref, acc_ref):
        i = pl.program_id(0)

        @pl.when(i == 0)
        def _():
            acc_ref[...] = jnp.zeros_like(acc_ref)

        err = y_ref[...] - t_ref[...]
        dy_ref[...] = err * (1.0 / d)
        acc_ref[...] += jnp.sum(jnp.mean(err * err, axis=-1, keepdims=True), axis=0, keepdims=True)

        @pl.when(i == nsteps - 1)
        def _():
            loss_ref[...] = 0.5 * acc_ref[...]

    row = pl.BlockSpec((tm, d), lambda i: (i, 0))
    return pl.pallas_call(
        body, grid=(nsteps,), name="loss_head",
        in_specs=[row, row], out_specs=[row, pl.BlockSpec((8, 128), lambda i: (0, 0))],
        out_shape=[jax.ShapeDtypeStruct((s, d), F32), jax.ShapeDtypeStruct((8, 128), F32)],
        scratch_shapes=[pltpu.VMEM((8, 128), F32)],
        compiler_params=_params("arbitrary"),
    )(y, target)


def _adamw_math(w, g, m, v):
    m = ADAM_B1 * m + (1.0 - ADAM_B1) * g
    v = ADAM_B2 * v + (1.0 - ADAM_B2) * (g * g)
    m_hat = m / (1.0 - ADAM_B1 ** ADAM_STEP)
    v_hat = v / (1.0 - ADAM_B2 ** ADAM_STEP)
    delta = -ADAM_LR * (m_hat / (jnp.sqrt(v_hat) + ADAM_EPS) + ADAM_WD * w)
    return delta, m, v


def _adamw_layer(parts, w, m, v, layer, outs):
    nl, r, c = w.shape
    tr = max(t for t in range(16, r + 1, 16) if r % t == 0 and t * c <= ADAMW_BLOCK_ELEMS)

    def body(p_ref, w_ref, m_ref, v_ref, *rest):
        g_ref, d_ref, nm_ref, nv_ref = rest[-4:]
        g = p_ref[0].astype(F32)
        for dev in range(1, N_DEV):
            g = g + p_ref[dev].astype(F32)
        delta, nm, nv = _adamw_math(w_ref[0], g, m_ref[0], v_ref[0])
        g_ref[0] = g
        d_ref[0] = delta
        nm_ref[0] = nm
        nv_ref[0] = nv

    slab = pl.BlockSpec((1, tr, c), lambda i: (layer, i, 0))
    any_spec = pl.BlockSpec(memory_space=pl.ANY)
    shape = jax.ShapeDtypeStruct(w.shape, F32)
    carried = [] if outs is None else list(outs)
    return pl.pallas_call(
        body, grid=(r // tr,), name="adamw",
        in_specs=[pl.BlockSpec((N_DEV, tr, c), lambda i: (0, i, 0)), slab, slab, slab] + [any_spec] * len(carried),
        out_specs=[slab] * 4,
        out_shape=[shape] * 4,
        input_output_aliases={4 + n: n for n in range(len(carried))},
        compiler_params=_params("arbitrary"),
    )(parts, w, m, v, *carried)


def _adamw_small(parts, w, m, v):
    r, c = w.shape

    def body(p_ref, w_ref, m_ref, v_ref, g_ref, d_ref, nm_ref, nv_ref):
        g = p_ref[0:r, :]
        for dev in range(1, N_DEV):
            g = g + p_ref[dev * r:(dev + 1) * r, :]
        delta, nm, nv = _adamw_math(w_ref[...], g, m_ref[...], v_ref[...])
        g_ref[...] = g
        d_ref[...] = delta
        nm_ref[...] = nm
        nv_ref[...] = nv

    shape = jax.ShapeDtypeStruct((r, c), F32)
    return pl.pallas_call(body, name="adamw_small", out_shape=[shape] * 4)(parts, w, m, v)


def _my_place():
    return lax.axis_index("x"), lax.axis_index("y"), lax.axis_index("c")


def _peer(k):
    x, y, c = _my_place()
    return (x ^ (k >> 2), y ^ ((k >> 1) & 1), c ^ (k & 1))


def _block_of(place):
    x, y, c = place
    return 4 * x + 2 * y + c


def _gather_small(block):
    m_per, n = block.shape

    def body(x_ref, out_ref, send_sems, recv_sems, local_sem):
        me = _block_of(_my_place())

        def rows(b):
            return out_ref.at[pl.ds(b * m_per, m_per), :]

        mine = pltpu.make_async_copy(x_ref, rows(me), local_sem)
        mine.start()
        sends = []
        for k in range(1, N_DEV):
            cp = pltpu.make_async_remote_copy(
                src_ref=x_ref, dst_ref=rows(me), send_sem=send_sems.at[k - 1], recv_sem=recv_sems.at[k - 1],
                device_id=_peer(k), device_id_type=MESH)
            cp.start()
            sends.append(cp)
        for k in range(1, N_DEV):
            src = rows(_block_of(_peer(k)))
            pltpu.make_async_remote_copy(
                src_ref=src, dst_ref=src, send_sem=send_sems.at[k - 1], recv_sem=recv_sems.at[k - 1],
                device_id=_peer(k), device_id_type=MESH).wait_recv()
        for cp in sends:
            cp.wait_send()
        mine.wait()

    return pl.pallas_call(
        body, name="gather_small",
        out_shape=jax.ShapeDtypeStruct((N_DEV * m_per, n), block.dtype),
        in_specs=[pl.BlockSpec(memory_space=pltpu.VMEM)],
        out_specs=pl.BlockSpec(memory_space=pltpu.VMEM),
        scratch_shapes=[pltpu.SemaphoreType.DMA((N_DEV - 1,)), pltpu.SemaphoreType.DMA((N_DEV - 1,)), pltpu.SemaphoreType.DMA],
    )(block)


def _gather_weights(name, stacks, layers):
    n_t = len(stacks)

    def body(*refs):
        srcs, outs = refs[:n_t], refs[n_t:2 * n_t]
        send_sems, recv_sems, local_sems = refs[2 * n_t:]
        x, y, c = _my_place()
        me, sibling = (x, y, c), (x, y, 1 - c)
        chips = [(1 - x, y), (x, 1 - y), (1 - x, 1 - y)]

        def copy(t, k, block, to, src=None):
            slot = outs[t].at[_block_of(block)]
            return pltpu.make_async_remote_copy(
                src_ref=slot if src is None else src, dst_ref=slot,
                send_sem=send_sems.at[t, k], recv_sem=recv_sems.at[t, k], device_id=to, device_id_type=MESH)

        local, sends = [], []
        for t in range(n_t):
            src = srcs[t].at[layers[t]]
            own = pltpu.make_async_copy(src, outs[t].at[_block_of(me)], local_sems.at[t])
            own.start()
            local.append(own)
            sends.append(copy(t, 0, me, sibling, src=src))
            sends += [copy(t, 1 + j, me, (*chip, c), src=src) for j, chip in enumerate(chips)]
        for cp in sends:
            cp.start()
        for t in range(n_t):
            for j, chip in enumerate(chips):
                copy(t, 1 + j, (*chip, c), me).wait_recv()
                passed = copy(t, 4 + j, (*chip, c), sibling)
                passed.start()
                sends.append(passed)
        for t in range(n_t):
            copy(t, 0, sibling, me).wait_recv()
            for j, chip in enumerate(chips):
                copy(t, 4 + j, (*chip, 1 - c), me).wait_recv()
        for cp in sends:
            cp.wait_send()
        for cp in local:
            cp.wait()

    any_spec = pl.BlockSpec(memory_space=pl.ANY)
    return pl.pallas_call(
        body, name=name,
        out_shape=[jax.ShapeDtypeStruct((N_DEV,) + st.shape[1:], st.dtype) for st in stacks],
        in_specs=[any_spec] * n_t, out_specs=[any_spec] * n_t,
        scratch_shapes=[pltpu.SemaphoreType.DMA((n_t, N_DEV - 1)), pltpu.SemaphoreType.DMA((n_t, N_DEV - 1)), pltpu.SemaphoreType.DMA((n_t,))],
    )(*stacks)


class _ScatterSide:
    def __init__(self, grads):
        self.operands = list(grads)
        self.n = len(self.operands)

    def out_shape(self):
        return [jax.ShapeDtypeStruct(g.shape, g.dtype) for g in self.operands]

    def scratch(self):
        return [pltpu.SemaphoreType.DMA((self.n, N_DEV - 1)), pltpu.SemaphoreType.DMA((self.n, N_DEV - 1)), pltpu.SemaphoreType.DMA((self.n,))]

    def _copies(self, srcs, outs, send_sems, recv_sems, local_sems):
        me = _block_of(_my_place())
        local, sends, arrivals = [], [], []
        for t in range(self.n):
            local.append(pltpu.make_async_copy(srcs[t].at[me], outs[t].at[me], local_sems.at[t]))
            for k in range(1, N_DEV):
                sems = dict(send_sem=send_sems.at[t, k - 1], recv_sem=recv_sems.at[t, k - 1], device_id=_peer(k), device_id_type=MESH)
                sends.append(pltpu.make_async_remote_copy(src_ref=srcs[t].at[_block_of(_peer(k))], dst_ref=outs[t].at[me], **sems))
                slot = outs[t].at[_block_of(_peer(k))]
                arrivals.append(pltpu.make_async_remote_copy(src_ref=slot, dst_ref=slot, **sems))
        return local, sends, arrivals

    def start(self, *refs):
        local, sends, _ = self._copies(*refs)
        for cp in local + sends:
            cp.start()

    def finish(self, *refs):
        local, sends, arrivals = self._copies(*refs)
        for cp in arrivals:
            cp.wait_recv()
        for cp in sends:
            cp.wait_send()
        for cp in local:
            cp.wait()


class _GatherSide(_ScatterSide):
    def __init__(self, stacks, layers):
        super().__init__(stacks)
        self.layers = list(layers)

    def out_shape(self):
        return [jax.ShapeDtypeStruct((N_DEV,) + st.shape[1:], st.dtype) for st in self.operands]

    def _copies(self, srcs, outs, send_sems, recv_sems, local_sems):
        me = _block_of(_my_place())
        local, sends, arrivals = [], [], []
        for t in range(self.n):
            src = srcs[t].at[self.layers[t]]
            local.append(pltpu.make_async_copy(src, outs[t].at[me], local_sems.at[t]))
            for k in range(1, N_DEV):
                sems = dict(send_sem=send_sems.at[t, k - 1], recv_sem=recv_sems.at[t, k - 1], device_id=_peer(k), device_id_type=MESH)
                sends.append(pltpu.make_async_remote_copy(src_ref=src, dst_ref=outs[t].at[me], **sems))
                slot = outs[t].at[_block_of(_peer(k))]
                arrivals.append(pltpu.make_async_remote_copy(src_ref=slot, dst_ref=slot, **sems))
        return local, sends, arrivals


def _scatter_grads(name, grads):
    side = _ScatterSide(grads)
    n_t = side.n

    def body(*refs):
        parts = (refs[:n_t], refs[n_t:2 * n_t]) + tuple(refs[2 * n_t:])
        side.start(*parts)
        side.finish(*parts)

    any_spec = pl.BlockSpec(memory_space=pl.ANY)
    return pl.pallas_call(
        body, name=name, out_shape=side.out_shape(), in_specs=[any_spec] * n_t, out_specs=[any_spec] * n_t,
        scratch_shapes=side.scratch(),
    )(*grads)


def _host_call(body, args, side, *, grid, in_specs, out_specs, out_shape, scratch_shapes=(), **kw):
    if side is None:
        return pl.pallas_call(body, grid=grid, in_specs=in_specs, out_specs=out_specs, out_shape=out_shape,
                              scratch_shapes=list(scratch_shapes), **kw)(*args), None
    n_in, n_out, n_scr, n_side = len(in_specs), len(out_specs), len(scratch_shapes), side.n

    def hosted(*refs):
        cuts = [n_in, n_side, n_out, n_side, n_scr, 3]
        groups, at = [], 0
        for c in cuts:
            groups.append(refs[at:at + c])
            at += c
        ins, side_in, outs, side_out, scr, sems = groups
        first, last = None, None
        for axis, size in enumerate(grid):
            at_start, at_end = pl.program_id(axis) == 0, pl.program_id(axis) == size - 1
            first = at_start if first is None else jnp.logical_and(first, at_start)
            last = at_end if last is None else jnp.logical_and(last, at_end)

        @pl.when(first)
        def _():
            side.start(side_in, side_out, *sems)

        body(*ins, *outs, *scr)

        @pl.when(last)
        def _():
            side.finish(side_in, side_out, *sems)

    any_spec = pl.BlockSpec(memory_space=pl.ANY)
    res = pl.pallas_call(
        hosted, grid=grid, in_specs=list(in_specs) + [any_spec] * n_side, out_specs=list(out_specs) + [any_spec] * n_side,
        out_shape=list(out_shape) + side.out_shape(), scratch_shapes=list(scratch_shapes) + side.scratch(), **kw,
    )(*args, *side.operands)
    return res[:n_out], res[n_out:]


def _ffn_forward(x, g, wt, wd, carried):
    sides = {k: _GatherSide(*v) if v[0] else None for k, v in carried.items()}
    parts = {}
    (h, gate, up, act), parts["up"] = _ffn_up(x, g, wt, sides.get("up"))
    (x_new,), parts["down"] = _ffn_down(act, wd, x, sides.get("down"))
    return x_new, (x, g, h, gate, up, act), parts


def _ffn_backward(dx, saved, wt, wd, carried):
    x, g, h, gate, up, act = saved
    sides = {k: _ScatterSide(v) if v else None for k, v in carried.items()}
    parts = {}
    (dgate, dup), parts["act"] = _ffn_bwd_act(dx, wd, gate, up, sides.get("act"))
    d_wd = _grad_ffn_down(act, dx)
    (dx_in, dg), parts["dh"] = _ffn_dh(dgate, dup, wt, x, g, dx, sides.get("dh"))
    d_wt, parts["gu"] = _grad_ffn_gu(h, dgate, dup, sides.get("gu"))
    return dx_in, dg, d_wt.reshape(N_DEV, -1, D_MODEL), d_wd.reshape(N_DEV, -1, D_MODEL), parts


def kernel(x, p, norm_ffn1, w_ffn1_gu, w_ffn1_down, norm_mix, w_qkv, q_norm, k_norm, w_o, w_pool_in, w_pool_grp, pool_scale, norm_ffn2, w_ffn2_gu, w_ffn2_down, norm_ple, w_ple_gate, w_ple_proj, loss_target, m_norm_ffn1, m_w_ffn1_gu, m_w_ffn1_down, m_norm_mix, m_w_qkv, m_q_norm, m_k_norm, m_w_o, m_w_pool_in, m_w_pool_grp, m_pool_scale, m_norm_ffn2, m_w_ffn2_gu, m_w_ffn2_down, m_norm_ple, m_w_ple_gate, m_w_ple_proj, v_norm_ffn1, v_w_ffn1_gu, v_w_ffn1_down, v_norm_mix, v_w_qkv, v_q_norm, v_k_norm, v_w_o, v_w_pool_in, v_w_pool_grp, v_pool_scale, v_norm_ffn2, v_w_ffn2_gu, v_w_ffn2_down, v_norm_ple, v_w_ple_gate, v_w_ple_proj):
    d = D_MODEL
    xs = x[0]
    target = loss_target[0]
    me = _block_of(_my_place())

    def tr(w):
        return jnp.swapaxes(w, 1, 2)

    big = dict(w_ffn1_gu=tr(w_ffn1_gu), w_ffn1_down=w_ffn1_down, w_qkv=w_qkv, w_o=w_o, w_pool_in=w_pool_in,
               w_pool_grp=w_pool_grp.reshape(2, 4 * 32, POOL_GROUP), w_ffn2_gu=tr(w_ffn2_gu), w_ffn2_down=w_ffn2_down,
               w_ple_gate=w_ple_gate, w_ple_proj=w_ple_proj)
    moments = dict(
        w_ffn1_gu=(tr(m_w_ffn1_gu), tr(v_w_ffn1_gu)), w_ffn1_down=(m_w_ffn1_down, v_w_ffn1_down), w_qkv=(m_w_qkv, v_w_qkv),
        w_o=(m_w_o, v_w_o), w_pool_in=(m_w_pool_in, v_w_pool_in),
        w_pool_grp=(m_w_pool_grp.reshape(2, 4 * 32, POOL_GROUP), v_w_pool_grp.reshape(2, 4 * 32, POOL_GROUP)),
        w_ffn2_gu=(tr(m_w_ffn2_gu), tr(v_w_ffn2_gu)), w_ffn2_down=(m_w_ffn2_down, v_w_ffn2_down),
        w_ple_gate=(m_w_ple_gate, v_w_ple_gate), w_ple_proj=(m_w_ple_proj, v_w_ple_proj))
    half = {name: _cast_bf16(w) for name, w in big.items()}

    def layer_names(i):
        mixer = ["w_qkv", "w_o"] if i % 2 == 0 else ["w_pool_in", "w_pool_grp"]
        return ["w_ffn1_gu", "w_ffn1_down"] + mixer + ["w_ffn2_gu", "w_ffn2_down", "w_ple_gate", "w_ple_proj"]

    def layer_index(name, i):
        return i // 2 if name in ("w_qkv", "w_o", "w_pool_in", "w_pool_grp") else i

    scale_all = _gather_small(jnp.pad(pool_scale, ((0, 6), (0, 0))))
    scale_full = scale_all.reshape(N_DEV, 8, 128)[:, :2].transpose(1, 0, 2).reshape(2, d)

    saved = []
    weights = []
    cur = xs

    def fwd_carriers(i):
        if i % 2 == 0:
            return {
                ("ffn1", "up"): ["w_ffn1_gu"], ("ffn2", "up"): ["w_ffn2_gu"], ("ffn1", "down"): [], ("ffn2", "down"): [],
                ("mix", "attn"): ["w_ffn1_down", "w_ffn2_down"],
                ("ple", "ple"): ["w_ple_gate", "w_ple_proj", "w_pool_in", "w_pool_grp"],
            }
        return {
            ("ffn1", "up"): ["w_ffn1_gu"], ("ffn2", "up"): ["w_ffn2_gu"],
            ("ffn1", "down"): ["w_ffn1_down"], ("ffn2", "down"): ["w_ffn2_down"],
            ("mix", "pool"): ["w_qkv", "w_o"],
            ("ple", "ple"): ["w_ple_gate", "w_ple_proj"],
        }

    def next_weights(i, stage):
        nxt = layer_names(i + 1) if i + 1 < DEPTH else []
        out = {}
        for (st, host), names in fwd_carriers(i).items():
            if st == stage:
                take = [n for n in names if n in nxt]
                out[host] = ([half[n] for n in take], [layer_index(n, i + 1) for n in take])
        return out

    def arrived_weights(i, stage, parts, into):
        nxt = layer_names(i + 1) if i + 1 < DEPTH else []
        for (st, host), names in fwd_carriers(i).items():
            if st == stage and parts.get(host) is not None:
                into.update(zip([n for n in names if n in nxt], parts[host]))

    names = layer_names(0)
    coming = dict(zip(names, _gather_weights("gather_weights_0", [half[n] for n in names], [layer_index(n, 0) for n in names])))
    for i in range(DEPTH):
        wl, coming = coming, {}
        assert set(wl) == set(layer_names(i)), sorted(wl)
        for n in ("w_ffn1_gu", "w_ffn2_gu", "w_ffn1_down", "w_ffn2_down", "w_o", "w_pool_in", "w_ple_gate"):
            if n in wl:
                wl[n] = wl[n].reshape(-1, d)
        if "w_pool_grp" in wl:
            wl["w_pool_grp"] = wl["w_pool_grp"].reshape(N_DEV, 4, 32, POOL_GROUP).transpose(1, 0, 2, 3).reshape(4, POOL_GROUP, POOL_GROUP)
        for n in ("w_qkv", "w_ple_proj"):
            if n in wl:
                wl[n] = _from_column_shards(wl[n])
        weights.append(wl)
        j = i // 2
        rec = {}
        cur, rec["ffn1"], parts = _ffn_forward(cur, norm_ffn1[i][None], wl["w_ffn1_gu"], wl["w_ffn1_down"], next_weights(i, "ffn1"))
        arrived_weights(i, "ffn1", parts, coming)
        x1 = cur
        gm = norm_mix[i][None]
        if i % 2 == 0:
            qg = jnp.tile(q_norm[j], d // HEAD_DIM)[None]
            kg = jnp.tile(k_norm[j], d // HEAD_DIM)[None]
            hm, qkv = _norm_matmul("qkv_proj", x1, gm, wl["w_qkv"], (d, d), lambda a, b: (0, b), _pick_all, 3 * d, d, 3)
            qn, kn, vb = _qk_norm(qkv, qg, kg)
            stacks, layers = next_weights(i, "mix")["attn"]
            o, parts = _attention(qn, kn, vb, _GatherSide(stacks, layers) if stacks else None)
            arrived_weights(i, "mix", {"attn": parts}, coming)
            cur = _matmul_res("attn_out", o, wl["w_o"], x1, 1.0)
            rec["mix"] = (x1, gm, hm, qkv, qg, kg, qn, kn, vb, o)
        else:
            sc = scale_full[j][None]
            hm, u = _norm_matmul("pool_in", x1, gm, wl["w_pool_in"], (d, d), lambda a, b: (0, 0), _pick_all, d, d, 1)
            stacks, layers = next_weights(i, "mix")["pool"]
            (cur, pooled), parts = _pool_fwd(u, wl["w_pool_grp"], sc, x1, _GatherSide(stacks, layers) if stacks else None)
            arrived_weights(i, "mix", {"pool": parts}, coming)
            rec["mix"] = (x1, gm, hm, sc, pooled)
        cur, rec["ffn2"], parts = _ffn_forward(cur, norm_ffn2[i][None], wl["w_ffn2_gu"], wl["w_ffn2_down"], next_weights(i, "ffn2"))
        arrived_weights(i, "ffn2", parts, coming)
        x3 = cur
        gp = norm_ple[i][None]
        stacks, layers = next_weights(i, "ple")["ple"]
        (cur, hp, sig, proj), parts = _ple_fwd(x3, gp, wl["w_ple_gate"], p[i, 0], wl["w_ple_proj"],
                                               _GatherSide(stacks, layers) if stacks else None)
        arrived_weights(i, "ple", {"ple": parts}, coming)
        rec["ple"] = (x3, gp, hp, sig, proj)
        saved.append(rec)

    dy, loss_part = _loss_head(cur, target)
    loss = lax.psum(loss_part[0, 0], ("x", "y", "c"))

    small = {n: [None] * DEPTH for n in ("norm_ffn1", "norm_mix", "norm_ffn2", "norm_ple")}
    small.update(q_norm=[None] * 2, k_norm=[None] * 2, pool_scale=[None] * 2)
    results = {name: None for name in big}

    def update(layer, parts):
        for n, part in parts.items():
            mm, vv = moments[n]
            results[n] = _adamw_layer(part, big[n], mm, vv, layer_index(n, layer), results[n])

    carriers = {
        ("ffn2", "gu"): ["w_ffn1_gu"], ("ffn1", "gu"): ["w_ffn2_gu"],
        ("ffn2", "dh"): ["w_ffn1_down", "w_ple_gate", "w_ple_proj"],
        ("ffn1", "dh"): ["w_ffn2_down", "w_o", "w_pool_in", "w_pool_grp"],
        ("ffn2", "act"): ["w_qkv"],
    }
    above = {}
    own = {}

    def carried(ffn):
        return {host: [above[n] for n in names if n in above] for (f, host), names in carriers.items() if f == ffn}

    def received(ffn, parts, into):
        for (f, host), names in carriers.items():
            if f == ffn and parts.get(host) is not None:
                into.update(zip([n for n in names if n in above], parts[host]))

    dcur = dy
    for i in reversed(range(DEPTH)):
        wl, rec, j = weights[i], saved[i], i // 2
        grads = {}
        arrived = {}
        x3, gp, hp, sig, proj = rec["ple"]
        dgate, dproj = _ple_bwd_gate(dcur, sig, proj)
        dx3, small["norm_ple"][i] = _square_dh("ple_dh", dgate, wl["w_ple_gate"], x3, gp, dcur)
        grads["w_ple_gate"] = _grad_square("grad_ple_gate", hp, dgate).reshape(N_DEV, d // N_DEV, d)
        grads["w_ple_proj"] = _column_shards(_grad_square("grad_ple_proj", p[i, 0], dproj))
        dx2, small["norm_ffn2"][i], grads["w_ffn2_gu"], grads["w_ffn2_down"], parts = _ffn_backward(
            dx3, rec["ffn2"], wl["w_ffn2_gu"], wl["w_ffn2_down"], carried("ffn2"))
        received("ffn2", parts, arrived)
        if i % 2 == 0:
            x1, gm, hm, qkv, qg, kg, qn, kn, vb, o = rec["mix"]
            do = _matmul_nt("attn_out_bwd", dx2, wl["w_o"])
            grads["w_o"] = _grad_square("grad_attn_out", o, dx2).reshape(N_DEV, d // N_DEV, d)
            early = ["w_ple_gate", "w_ple_proj", "w_ffn2_gu", "w_ffn2_down"] if i == 0 else []
            (dq, dk, dv), parts = _attention_bwd(qn, kn, vb, do, o, _ScatterSide([grads[n] for n in early]) if early else None)
            own.update(zip(early, parts or []))
            dqkv, dqg, dkg = _qk_norm_bwd(qkv, dq, dk, dv, qg, kg)
            small["q_norm"][j], small["k_norm"][j] = dqg[0:1], dkg[0:1]
            dx1, small["norm_mix"][i] = _qkv_dh(dqkv, wl["w_qkv"], x1, gm, dx2)
            grads["w_qkv"] = _column_shards(_grad_square("grad_qkv", hm, dqkv))
        else:
            x1, gm, hm, sc, pooled = rec["mix"]
            dpool, dgrp, small["pool_scale"][j] = _pool_bwd_group(dx2, pooled, wl["w_pool_grp"], sc)
            grads["w_pool_grp"] = dgrp.reshape(4, N_DEV, 32, POOL_GROUP).transpose(1, 0, 2, 3).reshape(N_DEV, 4 * 32, POOL_GROUP)
            du = _pool_bwd_window(dpool)
            dx1, small["norm_mix"][i] = _square_dh("pool_dh", du, wl["w_pool_in"], x1, gm, dx2)
            grads["w_pool_in"] = _grad_square("grad_pool_in", hm, du).reshape(N_DEV, d // N_DEV, d)
        ffn1_carried = carried("ffn1")
        late = ["w_qkv", "w_o"] if i == 0 else []
        ffn1_carried["act"] = [grads[n] for n in late]
        dcur, small["norm_ffn1"][i], grads["w_ffn1_gu"], grads["w_ffn1_down"], parts = _ffn_backward(
            dx1, rec["ffn1"], wl["w_ffn1_gu"], wl["w_ffn1_down"], ffn1_carried)
        received("ffn1", parts, arrived)
        own.update(zip(late, parts["act"] or []))
        if above:
            assert set(arrived) == set(above), (sorted(arrived), sorted(above))
            update(i + 1, arrived)
        above = grads

    rest = [n for n in layer_names(0) if n not in own]
    own.update(zip(rest, _scatter_grads("scatter_grads_0", [above[n] for n in rest])))
    update(0, own)

    def lanes(a):
        return jnp.pad(a, ((0, 0), (0, d - a.shape[-1])))

    order = [("norm_ffn1", DEPTH), ("norm_mix", DEPTH), ("norm_ffn2", DEPTH), ("norm_ple", DEPTH), ("pool_scale", 2), ("q_norm", 2), ("k_norm", 2)]
    rows = jnp.concatenate([lanes(g) for name, _ in order for g in small[name]], axis=0)
    n_rows = rows.shape[0]
    pad_rows = -n_rows % 8
    rows = jnp.pad(rows, ((0, pad_rows), (0, 0)))
    gathered = _gather_small(rows)

    def own_lanes(a):
        return lax.dynamic_update_slice(jnp.zeros((a.shape[0], d), F32), a, (0, me * 128))

    def pack(values):
        mats = [own_lanes(values[name]) if name == "pool_scale" else lanes(values[name]) for name, _ in order]
        return jnp.pad(jnp.concatenate(mats, axis=0), ((0, pad_rows), (0, 0)))

    small_w = dict(norm_ffn1=norm_ffn1, norm_mix=norm_mix, norm_ffn2=norm_ffn2, norm_ple=norm_ple, pool_scale=pool_scale, q_norm=q_norm, k_norm=k_norm)
    small_m = dict(norm_ffn1=m_norm_ffn1, norm_mix=m_norm_mix, norm_ffn2=m_norm_ffn2, norm_ple=m_norm_ple, pool_scale=m_pool_scale, q_norm=m_q_norm, k_norm=m_k_norm)
    small_v = dict(norm_ffn1=v_norm_ffn1, norm_mix=v_norm_mix, norm_ffn2=v_norm_ffn2, norm_ple=v_norm_ple, pool_scale=v_pool_scale, q_norm=v_q_norm, k_norm=v_k_norm)
    packed = _adamw_small(gathered, pack(small_w), pack(small_m), pack(small_v))

    def unpack(mat):
        out, at = {}, 0
        for name, n in order:
            blk = mat[at:at + n]
            at += n
            if name == "pool_scale":
                out[name] = lax.dynamic_slice(blk, (0, me * 128), (n, 128))
            elif name in ("q_norm", "k_norm"):
                out[name] = blk[:, :HEAD_DIM]
            else:
                out[name] = blk
        return out

    small_out = [unpack(mat) for mat in packed]

    def result(kind, name):
        if name in small_w:
            return small_out[kind][name]
        r = results[name][kind]
        if name in ("w_ffn1_gu", "w_ffn2_gu"):
            return tr(r)
        return r.reshape(w_pool_grp.shape) if name == "w_pool_grp" else r

    weight_names = ["norm_ffn1", "w_ffn1_gu", "w_ffn1_down", "norm_mix", "w_qkv", "q_norm", "k_norm", "w_o", "w_pool_in", "w_pool_grp",
                    "pool_scale", "norm_ffn2", "w_ffn2_gu", "w_ffn2_down", "norm_ple", "w_ple_gate", "w_ple_proj"]
    outs = [loss, dcur[None]]
    for kind in range(4):
        outs += [result(kind, name) for name in weight_names]
    return tuple(outs)
```

```python
import math

import jax
import jax.numpy as jnp
from jax import lax
from jax.experimental import pallas as pl
from jax.experimental.pallas import tpu as pltpu

F32 = jnp.float32
BF16 = jnp.bfloat16

N_DEV = 8
DEPTH = 4
D_MODEL = 1024
N_UNITS = D_MODEL // 128
HEAD_DIM = 64
POOL_WINDOWS = (2, 4, 8, 16)
POOL_GROUP = 256
POOL_HALO = 128
EPS = 1e-6
ATTN_SCALE = 1.0 / math.sqrt(HEAD_DIM)
ATTN_BLOCK = 256
LOG_ZERO = -104.0

ADAM_LR = 0.001
ADAM_B1 = 0.9
ADAM_B2 = 0.999
ADAM_EPS = 1e-08
ADAM_WD = 0.01
ADAM_STEP = 10
ADAMW_BLOCK_ELEMS = 128 * 1024

VMEM_LIMIT = 56 * 1024 * 1024
MESH = pl.DeviceIdType.MESH

NT_DIMS = (((1,), (1,)), ((), ()))
TN_DIMS = (((0,), (0,)), ((), ()))


def _params(*sem):
    return pltpu.CompilerParams(dimension_semantics=sem, vmem_limit_bytes=VMEM_LIMIT)


def _row_tile(s):
    return min(512, s)


def _grad_tile(s):
    return min(2048, s)


def _block_spec(block, index, mode=None):
    return pl.BlockSpec(block, index) if mode is None else pl.BlockSpec(block, index, pipeline_mode=mode)


def _dot(a, b):
    return jnp.dot(a, b, preferred_element_type=F32)


def _dot_nt(a, b):
    return lax.dot_general(a, b, NT_DIMS, preferred_element_type=F32)


def _dot_tn(a, b):
    return lax.dot_general(a, b, TN_DIMS, preferred_element_type=F32)


def _dot_f32(a, b):
    return jnp.dot(a, b, precision=lax.Precision.HIGHEST, preferred_element_type=F32)


def _dot_3x(a, b):
    return jnp.dot(a, b, precision=lax.Precision.HIGH, preferred_element_type=F32)


def _rms(x, g):
    r = lax.rsqrt(jnp.mean(x * x, axis=-1, keepdims=True) + EPS)
    return x * r * g


def _rms_bwd(dy, x, g):
    r = lax.rsqrt(jnp.mean(x * x, axis=-1, keepdims=True) + EPS)
    xh = x * r
    dg = jnp.sum(dy * xh, axis=0, keepdims=True)
    dxh = dy * g
    dx = r * (dxh - xh * jnp.mean(dxh * xh, axis=-1, keepdims=True))
    return dx, dg


def _cast_bf16(w):
    l, r, c = w.shape

    def body(w_ref, o_ref):
        o_ref[...] = w_ref[...].astype(BF16)

    return pl.pallas_call(
        body, grid=(l,), name="cast_bf16",
        in_specs=[pl.BlockSpec((1, r, c), lambda i: (i, 0, 0))],
        out_specs=pl.BlockSpec((1, r, c), lambda i: (i, 0, 0)),
        out_shape=jax.ShapeDtypeStruct(w.shape, BF16),
        compiler_params=_params("arbitrary"),
    )(w)


def _ffn_up(x, g, wt, side=None):
    s, d = x.shape
    f = wt.shape[0] // 2
    tn = f // 2
    tm = _row_tile(s)

    def body(x_ref, g_ref, w_ref, h_ref, gate_ref, up_ref):
        h = _rms(x_ref[...], g_ref[...]).astype(BF16)
        h_ref[...] = h
        for half in range(2):
            cols = slice(half * tn, (half + 1) * tn)
            gate_ref[:, cols] = _dot_nt(h, w_ref[half * tn:(half + 1) * tn, :]).astype(BF16)
            up_ref[:, cols] = _dot_nt(h, w_ref[f + half * tn:f + (half + 1) * tn, :]).astype(BF16)

    wide = pl.BlockSpec((tm, f), lambda i: (i, 0))
    hidden = jax.ShapeDtypeStruct((s, f), BF16)
    return _host_call(
        body, (x, g, wt), side, grid=(s // tm,), name="ffn_up",
        in_specs=[
            pl.BlockSpec((tm, d), lambda i: (i, 0)),
            pl.BlockSpec((1, d), lambda i: (0, 0)),
            _block_spec((2 * f, d), lambda i: (0, 0), pl.Buffered(1)),
        ],
        out_specs=[pl.BlockSpec((tm, d), lambda i: (i, 0)), wide, wide],
        out_shape=[jax.ShapeDtypeStruct((s, d), BF16), hidden, hidden],
        compiler_params=_params("arbitrary"),
    )


def _ffn_down(gate, up, wd, x, side=None):
    s, f = gate.shape
    d = wd.shape[-1]
    tn = f // 2
    tm = _row_tile(s)

    def body(gate_ref, up_ref, w_ref, x_ref, o_ref):
        y = None
        for half in range(2):
            cols = slice(half * tn, (half + 1) * tn)
            part = _dot(_swiglu(gate_ref[:, cols], up_ref[:, cols]), w_ref[half * tn:(half + 1) * tn, :])
            y = part if y is None else y + part
        o_ref[...] = x_ref[...] + 0.5 * y

    row = pl.BlockSpec((tm, d), lambda i: (i, 0))
    wide = pl.BlockSpec((tm, f), lambda i: (i, 0))
    return _host_call(
        body, (gate, up, wd, x), side, grid=(s // tm,), name="ffn_down",
        in_specs=[wide, wide, _block_spec((f, d), lambda i: (0, 0), pl.Buffered(1)), row],
        out_specs=[row],
        out_shape=[jax.ShapeDtypeStruct((s, d), F32)],
        compiler_params=_params("arbitrary"),
    )


def _ffn_bwd_act(dx, wd, gate, up, side=None):
    s, d = dx.shape
    f = gate.shape[-1]
    tn = f // 2
    tm = _row_tile(s)

    def body(dx_ref, w_ref, gate_ref, up_ref, dgate_ref, dup_ref):
        dact = _dot_nt((0.5 * dx_ref[...]).astype(BF16), w_ref[...])
        gate = gate_ref[...].astype(F32)
        sig = jax.nn.sigmoid(gate)
        silu = gate * sig
        dgate_ref[...] = (dact * up_ref[...].astype(F32) * (sig + silu * (1.0 - sig))).astype(BF16)
        dup_ref[...] = (dact * silu).astype(BF16)

    col = pl.BlockSpec((tm, tn), lambda j, i: (i, j))
    hidden = jax.ShapeDtypeStruct((s, f), BF16)
    return _host_call(
        body, (dx, wd, gate, up), side, grid=(2, s // tm), name="ffn_bwd_act",
        in_specs=[pl.BlockSpec((tm, d), lambda j, i: (i, 0)), pl.BlockSpec((tn, d), lambda j, i: (j, 0)), col, col],
        out_specs=[col, col],
        out_shape=[hidden, hidden],
        compiler_params=_params("arbitrary", "arbitrary"),
    )


def _norm_bwd_matmul(name, operands, products, nk, x, g, dres, w_rows_contract=False, side=None):
    s, d = x.shape
    tm = _row_tile(s)
    n = len(operands)

    def body(*refs):
        x_ref, g_ref, dres_ref, dx_ref, dg_ref, acc_ref = refs[n:]
        i, k = pl.program_id(0), pl.program_id(1)

        @pl.when(jnp.logical_and(i == 0, k == 0))
        def _():
            dg_ref[...] = jnp.zeros_like(dg_ref)

        total = None
        for a_at, a_pick, w_at, w_pick in products:
            prod = (_dot if w_rows_contract else _dot_nt)(a_pick(refs[a_at]), w_pick(refs[w_at]))
            total = prod if total is None else total + prod

        if nk > 1:
            @pl.when(k == 0)
            def _():
                acc_ref[...] = total

            @pl.when(jnp.logical_and(k > 0, k < nk - 1))
            def _():
                acc_ref[...] += total

        @pl.when(k == nk - 1)
        def _():
            dy = total if nk == 1 else acc_ref[...] + total
            dx, dg = _rms_bwd(dy, x_ref[...], g_ref[...])
            dx_ref[...] = dres_ref[...] + dx
            dg_ref[...] += dg

    row = pl.BlockSpec((tm, d), lambda i, k: (i, 0))
    vec = pl.BlockSpec((1, d), lambda i, k: (0, 0))
    return _host_call(
        body, [op[0] for op in operands] + [x, g, dres], side, grid=(s // tm, nk), name=name,
        in_specs=[_block_spec(*op[1:]) for op in operands] + [row, vec, row],
        out_specs=[row, vec],
        out_shape=[jax.ShapeDtypeStruct((s, d), F32), jax.ShapeDtypeStruct((1, d), F32)],
        scratch_shapes=[pltpu.VMEM((tm, d), F32)],
        compiler_params=_params("arbitrary", "arbitrary"),
    )


def _ffn_dh(dgate, dup, wt, x, g, dres, side=None):
    s, d = x.shape
    f = dgate.shape[-1]
    tm = _row_tile(s)
    once = pl.Buffered(1)
    operands = [
        (dgate, (tm, f), lambda i, k: (i, 0)),
        (dup, (tm, f), lambda i, k: (i, 0)),
        (wt, (f, d), lambda i, k: (0, 0), once),
        (wt, (f, d), lambda i, k: (1, 0), once),
    ]
    products = [(0, _pick_all, 2, _pick_all), (1, _pick_all, 3, _pick_all)]
    return _norm_bwd_matmul("ffn_dh", operands, products, 1, x, g, dres, w_rows_contract=True, side=side)


def _square_dh(name, a, w, x, g, dres):
    s, d = x.shape
    tm = _row_tile(s)
    operands = [(a, (tm, d), lambda i, k: (i, 0)), (w, (d, d), lambda i, k: (0, 0))]
    return _norm_bwd_matmul(name, operands, [(0, _pick_all, 1, _pick_all)], 1, x, g, dres)[0]


def _qkv_dh(dqkv, wqkv, x, g, dres):
    s, d = x.shape
    tm = _row_tile(s)
    operands = [(dqkv, (tm, 3 * d), lambda i, k: (i, 0)), (wqkv, (d, 3 * d), lambda i, k: (0, 0), pl.Buffered(1))]
    return _norm_bwd_matmul("qkv_dh", operands, [(0, _pick_all, 1, _pick_all)], 1, x, g, dres)[0]


def _grad_matmul(name, a, a_block, a_index, a_pick, b, b_block, b_index, b_picks, out_shape, out_block, out_index, out_stores, nj, scale=1.0):
    s = a.shape[-2]
    nk = s // _grad_tile(s)
    n_prod = len(b_picks)

    def body(a_ref, b_ref, o_ref, *acc_refs):
        k = pl.program_id(1)
        av = a_pick(a_ref)
        if av.dtype != BF16:
            av = (scale * av).astype(BF16)
        for b_pick, store, acc_ref in zip(b_picks, out_stores, acc_refs):
            bv = b_pick(b_ref)
            if bv.dtype != BF16:
                bv = bv.astype(BF16)
            prod = _dot_tn(av, bv)
            if nk == 1:
                store(o_ref, prod.astype(BF16))
                continue

            @pl.when(k == 0)
            def _():
                acc_ref[...] = prod

            @pl.when(jnp.logical_and(k > 0, k < nk - 1))
            def _():
                acc_ref[...] += prod

            @pl.when(k == nk - 1)
            def _():
                store(o_ref, (acc_ref[...] + prod).astype(BF16))

    m = jax.eval_shape(a_pick, jax.ShapeDtypeStruct(a_block, a.dtype)).shape[-1]
    nn = jax.eval_shape(b_picks[0], jax.ShapeDtypeStruct(b_block, b.dtype)).shape[-1]
    acc_shape = (m, nn)
    return pl.pallas_call(
        body, grid=(nj, nk), name=name,
        in_specs=[pl.BlockSpec(a_block, a_index), pl.BlockSpec(b_block, b_index)],
        out_specs=pl.BlockSpec(out_block, out_index),
        out_shape=jax.ShapeDtypeStruct(out_shape, BF16),
        scratch_shapes=[pltpu.VMEM(acc_shape, F32) for _ in range(n_prod)],
        compiler_params=_params("arbitrary", "arbitrary"),
    )(a, b)


def _pick_all(r):
    return r[...]


def _store_all(r, v):
    r[...] = v


def _swiglu(gate, up):
    g = gate.astype(F32)
    return (g * jax.nn.sigmoid(g) * up.astype(F32)).astype(BF16)


def _grad_ffn_down(gate, up, dx):
    s, f = gate.shape
    d = dx.shape[-1]
    tk = min(1024, s)
    nk = s // tk
    tn = f // 2

    def body(gate_ref, up_ref, dx_ref, o_ref, acc_ref):
        k = pl.program_id(1)
        prod = _dot_tn(_swiglu(gate_ref[...], up_ref[...]), (0.5 * dx_ref[...]).astype(BF16))
        if nk == 1:
            o_ref[...] = prod.astype(BF16)
            return

        @pl.when(k == 0)
        def _():
            acc_ref[...] = prod

        @pl.when(jnp.logical_and(k > 0, k < nk - 1))
        def _():
            acc_ref[...] += prod

        @pl.when(k == nk - 1)
        def _():
            o_ref[...] = (acc_ref[...] + prod).astype(BF16)

    col = pl.BlockSpec((tk, tn), lambda j, k: (k, j))
    return pl.pallas_call(
        body, grid=(2, nk), name="grad_ffn_down",
        in_specs=[col, col, pl.BlockSpec((tk, d), lambda j, k: (k, 0))],
        out_specs=pl.BlockSpec((tn, d), lambda j, k: (j, 0)),
        out_shape=jax.ShapeDtypeStruct((f, d), BF16),
        scratch_shapes=[pltpu.VMEM((tn, d), F32)],
        compiler_params=_params("arbitrary", "arbitrary"),
    )(gate, up, dx)


def _grad_ffn_gu(h, dgate, dup, side=None):
    s, d = h.shape
    f = dgate.shape[-1]
    tn = f // 2
    tk = min(1024, s)
    nk = s // tk

    def body(h_ref, dgate_ref, dup_ref, o_ref, acc_ref):
        j, k = pl.program_id(0), pl.program_id(1)

        def accumulate(a_ref):
            prod = _dot_tn(a_ref[...], h_ref[...])
            if nk == 1:
                o_ref[...] = prod.astype(BF16)
                return

            @pl.when(k == 0)
            def _():
                acc_ref[...] = prod

            @pl.when(jnp.logical_and(k > 0, k < nk - 1))
            def _():
                acc_ref[...] += prod

            @pl.when(k == nk - 1)
            def _():
                o_ref[...] = (acc_ref[...] + prod).astype(BF16)

        @pl.when(j < 2)
        def _():
            accumulate(dgate_ref)

        @pl.when(j >= 2)
        def _():
            accumulate(dup_ref)

    (out,), parts = _host_call(
        body, (h, dgate, dup), side, grid=(4, nk), name="grad_ffn_gu",
        in_specs=[
            pl.BlockSpec((tk, d), lambda j, k: (k, 0)),
            pl.BlockSpec((tk, tn), lambda j, k: (jnp.where(j < 2, k, 0), jnp.minimum(j, 1))),
            pl.BlockSpec((tk, tn), lambda j, k: (jnp.where(j >= 2, k, 0), jnp.maximum(j - 2, 0))),
        ],
        out_specs=[pl.BlockSpec((tn, d), lambda j, k: (j, 0))],
        out_shape=[jax.ShapeDtypeStruct((2 * f, d), BF16)],
        scratch_shapes=[pltpu.VMEM((tn, d), F32)],
        compiler_params=_params("arbitrary", "arbitrary"),
    )
    return out, parts


def _grad_square(name, a, b):
    s, m = a.shape
    n = b.shape[-1]
    tk = _grad_tile(s)
    tn = min(n, D_MODEL)
    return _grad_matmul(
        name, a, (tk, m), lambda j, k: (k, 0), _pick_all,
        b, (tk, tn), lambda j, k: (k, j), [_pick_all],
        (m, n), (m, tn), lambda j, k: (0, j), [_store_all], n // tn)


def _column_shards(w):
    m, n = w.shape
    return w.reshape(m, N_DEV, n // N_DEV).transpose(1, 0, 2)


def _from_column_shards(w):
    nd, m, n = w.shape
    return w.transpose(1, 0, 2).reshape(m, nd * n)


def _norm_matmul(name, x, g, w, w_block, w_index, w_pick, n_total, tn, nj):
    s, d = x.shape
    tm = _row_tile(s)

    def body(x_ref, g_ref, w_ref, h_ref, y_ref):
        @pl.when(pl.program_id(1) == 0)
        def _():
            h_ref[...] = _rms(x_ref[...], g_ref[...]).astype(BF16)

        y_ref[...] = _dot(h_ref[...], w_pick(w_ref))

    return pl.pallas_call(
        body, grid=(s // tm, nj), name=name,
        in_specs=[
            pl.BlockSpec((tm, d), lambda i, j: (i, 0)),
            pl.BlockSpec((1, d), lambda i, j: (0, 0)),
            pl.BlockSpec(w_block, w_index),
        ],
        out_specs=[pl.BlockSpec((tm, d), lambda i, j: (i, 0)), pl.BlockSpec((tm, tn), lambda i, j: (i, j))],
        out_shape=[jax.ShapeDtypeStruct((s, d), BF16), jax.ShapeDtypeStruct((s, n_total), F32)],
        compiler_params=_params("arbitrary", "arbitrary"),
    )(x, g, w)


def _head_mean_matrix():
    r = lax.broadcasted_iota(jnp.int32, (128, 128), 0) // HEAD_DIM
    c = lax.broadcasted_iota(jnp.int32, (128, 128), 1) // HEAD_DIM
    return jnp.where(r == c, 1.0 / HEAD_DIM, 0.0).astype(F32)


def _qk_norm(qkv, qg, kg):
    s = qkv.shape[0]
    d = D_MODEL
    tm = _row_tile(s)

    def body(q_ref, k_ref, v_ref, qg_ref, kg_ref, qo_ref, ko_ref, vo_ref):
        mean_m = _head_mean_matrix()
        for u in range(N_UNITS):
            cols = slice(128 * u, 128 * (u + 1))
            for src, gain, dst, scale in ((q_ref, qg_ref, qo_ref, ATTN_SCALE), (k_ref, kg_ref, ko_ref, 1.0)):
                xs = src[:, cols]
                r = lax.rsqrt(_dot_3x(xs * xs, mean_m) + EPS)
                y = xs * r * gain[:, cols]
                dst[:, cols] = (y * scale).astype(BF16) if scale != 1.0 else y.astype(BF16)
        vo_ref[...] = v_ref[...].astype(BF16)

    blk = lambda c: pl.BlockSpec((tm, d), lambda i: (i, c))
    vec = pl.BlockSpec((1, d), lambda i: (0, 0))
    return pl.pallas_call(
        body, grid=(s // tm,), name="qk_norm",
        in_specs=[blk(0), blk(1), blk(2), vec, vec],
        out_specs=[blk(0)] * 3,
        out_shape=[jax.ShapeDtypeStruct((s, d), BF16)] * 3,
        compiler_params=_params("arbitrary"),
    )(qkv, qkv, qkv, qg, kg)


def _qk_norm_bwd(qkv, dq, dk, dv, qg, kg):
    s = qkv.shape[0]
    d = D_MODEL
    tm = _row_tile(s)
    nsteps = s // tm

    def body(q_ref, k_ref, dq_ref, dk_ref, dv_ref, qg_ref, kg_ref, o_ref, dqg_ref, dkg_ref, acc_ref):
        i = pl.program_id(0)

        @pl.when(i == 0)
        def _():
            acc_ref[...] = jnp.zeros_like(acc_ref)

        mean_m = _head_mean_matrix()
        for u in range(N_UNITS):
            cols = slice(128 * u, 128 * (u + 1))
            for n, (src, dsrc, gain) in enumerate(((q_ref, dq_ref, qg_ref), (k_ref, dk_ref, kg_ref))):
                xs = src[:, cols]
                dy = dsrc[:, cols]
                r = lax.rsqrt(_dot_3x(xs * xs, mean_m) + EPS)
                xh = xs * r
                acc_ref[n:n + 1, :] += jnp.sum(dy * xh, axis=0, keepdims=True)
                dxh = dy * gain[:, cols]
                dx = r * (dxh - xh * _dot_3x(dxh * xh, mean_m))
                o_ref[:, 128 * (N_UNITS * n + u):128 * (N_UNITS * n + u + 1)] = dx.astype(BF16)
        o_ref[:, 2 * d:3 * d] = dv_ref[...].astype(BF16)

        @pl.when(i == nsteps - 1)
        def _():
            r = lax.broadcasted_iota(jnp.int32, (128, 128), 0) % HEAD_DIM
            c = lax.broadcasted_iota(jnp.int32, (128, 128), 1) % HEAD_DIM
            fold = jnp.where(r == c, 1.0, 0.0).astype(F32)
            folded = _dot_f32(acc_ref[...], fold)
            dqg_ref[...] = jnp.broadcast_to(folded[0:1], (8, 128))
            dkg_ref[...] = jnp.broadcast_to(folded[1:2], (8, 128))

    blk = lambda c: pl.BlockSpec((tm, d), lambda i: (i, c))
    row = pl.BlockSpec((tm, d), lambda i: (i, 0))
    vec = pl.BlockSpec((1, d), lambda i: (0, 0))
    small = pl.BlockSpec((8, 128), lambda i: (0, 0))
    return pl.pallas_call(
        body, grid=(nsteps,), name="qk_norm_bwd",
        in_specs=[blk(0), blk(1), row, row, row, vec, vec],
        out_specs=[pl.BlockSpec((tm, 3 * d), lambda i: (i, 0)), small, small],
        out_shape=[jax.ShapeDtypeStruct((s, 3 * d), BF16), jax.ShapeDtypeStruct((8, 128), F32), jax.ShapeDtypeStruct((8, 128), F32)],
        scratch_shapes=[pltpu.VMEM((8, 128), F32)],
        compiler_params=_params("arbitrary"),
    )(qkv, qkv, dq, dk, dv, qg, kg)


def _split_dot(x, m):
    hi = x.astype(BF16)
    lo = (x - hi.astype(F32)).astype(BF16)
    return _dot(hi, m) + _dot(lo, m)


def _stack_heads(x):
    lane = lax.broadcasted_iota(jnp.int32, x.shape, 1)
    zero = jnp.zeros_like(x)
    return jnp.concatenate([jnp.where(lane < HEAD_DIM, x, zero), jnp.where(lane < HEAD_DIM, zero, x)], axis=0)


def _unstack_heads(x2, t):
    lane = lax.broadcasted_iota(jnp.int32, (t, 128), 1)
    return jnp.where(lane < HEAD_DIM, x2[:t], x2[t:])


def _attn_masks(t):
    r = lax.broadcasted_iota(jnp.int32, (t, t), 0)
    c = lax.broadcasted_iota(jnp.int32, (t, t), 1)
    row = lax.broadcasted_iota(jnp.int32, (2 * t, t), 0)
    col = lax.broadcasted_iota(jnp.int32, (2 * t, t), 1)
    causal = col < jnp.where(row >= t, row - t, row)
    return (r > c).astype(BF16), (r >= c).astype(BF16), causal


def _attn_sweep_cond(st):
    return jnp.logical_and(st[0] >= 0, st[1] > LOG_ZERO)


def _attn_scores(q2, kblk, after, causal):
    z = _dot_nt(q2, kblk)
    sp = jnp.maximum(z, 0.0) + jnp.log(1.0 + jnp.exp(-jnp.abs(z)))
    log_stay = -sp
    if causal is not None:
        log_stay = jnp.where(causal, log_stay, 0.0)
    return log_stay, z - sp, _split_dot(log_stay, after)


def _attention(q, k, v, side=None):
    s, d = q.shape
    t = min(ATTN_BLOCK, s)

    def body(q_ref, k_ref, v_ref, o_ref):
        i = pl.program_id(1)
        after, _, causal = _attn_masks(t)
        q2 = _stack_heads(q_ref[...])

        def step(kb, carry, acc, mask):
            start = pl.multiple_of(kb * t, t)
            kblk = k_ref[pl.ds(start, t), :]
            vblk = v_ref[pl.ds(start, t), :]
            log_stay, log_beta, later = _attn_scores(q2, kblk, after, mask)
            w = jnp.exp(log_beta + later + carry)
            if mask is not None:
                w = jnp.where(mask, w, 0.0)
            return carry + jnp.sum(log_stay, axis=1, keepdims=True), acc + _split_dot(w, vblk)

        carry, acc = step(i, jnp.zeros((2 * t, 1), F32), jnp.zeros((2 * t, 128), F32), causal)
        carry, acc = step(jnp.maximum(i - 1, 0), carry, acc, jnp.broadcast_to(i > 0, causal.shape))

        def loop(st):
            c, a = step(st[0], st[2], st[3], None)
            return st[0] - 1, jnp.max(c), c, a

        acc = lax.while_loop(_attn_sweep_cond, loop, (i - 2, jnp.max(carry), carry, acc))[3]
        o_ref[...] = _unstack_heads(acc, t)

    (o,), parts = _host_call(
        body, (q, k, v), side, grid=(N_UNITS, s // t), name="attention",
        in_specs=[
            pl.BlockSpec((t, 128), lambda h, i: (i, h)),
            pl.BlockSpec((s, 128), lambda h, i: (0, h)),
            pl.BlockSpec((s, 128), lambda h, i: (0, h)),
        ],
        out_specs=[pl.BlockSpec((t, 128), lambda h, i: (i, h))],
        out_shape=[jax.ShapeDtypeStruct((s, d), F32)],
        compiler_params=_params("arbitrary", "arbitrary"),
    )
    return o, parts


def _attention_bwd(q, k, v, do, o, side=None):
    s, d = q.shape
    t = min(ATTN_BLOCK, s)

    def body(q_ref, k_ref, v_ref, do_ref, o_ref, dq_ref, dk_ref, dv_ref):
        i = pl.program_id(1)

        @pl.when(i == 0)
        def _():
            dk_ref[...] = jnp.zeros_like(dk_ref)
            dv_ref[...] = jnp.zeros_like(dv_ref)

        after, from_here, causal = _attn_masks(t)
        q2 = _stack_heads(q_ref[...])
        do2 = _stack_heads(do_ref[...])

        def weights(kb, carry, mask):
            start = pl.multiple_of(kb * t, t)
            kblk = k_ref[pl.ds(start, t), :]
            vblk = v_ref[pl.ds(start, t), :]
            log_stay, log_beta, later = _attn_scores(q2, kblk, after, mask)
            w = jnp.exp(log_beta + later + carry)
            if mask is not None:
                w = jnp.where(mask, w, 0.0)
            g = w * _dot_nt(do2, vblk)
            return start, kblk, log_stay, log_beta, w, g

        lane = lax.broadcasted_iota(jnp.int32, (t, 128), 1)
        prod = do_ref[...].astype(F32) * o_ref[...]
        total = jnp.concatenate([
            jnp.sum(jnp.where(lane < HEAD_DIM, prod, 0.0), axis=1, keepdims=True),
            jnp.sum(jnp.where(lane < HEAD_DIM, 0.0, prod), axis=1, keepdims=True)], axis=0)
        zero = jnp.zeros((2 * t, 1), F32)

        def grad_step(kb, carry, seen, dq, mask):
            start, kblk, log_stay, log_beta, w, g = weights(kb, carry, mask)
            before = total - (_split_dot(g, from_here) + seen)
            beta = jnp.exp(log_beta)
            da = g * (1.0 - beta) - before * beta
            if mask is not None:
                da = jnp.where(mask, da, 0.0)
            dab = da.astype(BF16)
            dk_ref[pl.ds(start, t), :] += _dot_tn(dab, q2)
            dv_ref[pl.ds(start, t), :] += _dot_tn(w.astype(BF16), do2)
            return (carry + jnp.sum(log_stay, axis=1, keepdims=True), seen + jnp.sum(g, axis=1, keepdims=True),
                    dq + _dot(dab, kblk))

        carry, seen, dq = grad_step(i, zero, zero, jnp.zeros((2 * t, 128), F32), causal)
        carry, seen, dq = grad_step(jnp.maximum(i - 1, 0), carry, seen, dq, jnp.broadcast_to(i > 0, causal.shape))

        def grad_loop(st):
            c, sn, a = grad_step(st[0], st[2], st[3], st[4], None)
            return st[0] - 1, jnp.max(c), c, sn, a

        dq = lax.while_loop(_attn_sweep_cond, grad_loop, (i - 2, jnp.max(carry), carry, seen, dq))[4]
        dq_ref[...] = ATTN_SCALE * _unstack_heads(dq, t)

    blk = pl.BlockSpec((t, 128), lambda h, i: (i, h))
    full = pl.BlockSpec((s, 128), lambda h, i: (0, h))
    return _host_call(
        body, (q, k, v, do, o), side, grid=(N_UNITS, s // t), name="attention_bwd",
        in_specs=[blk, full, full, blk, blk],
        out_specs=[blk, full, full],
        out_shape=[jax.ShapeDtypeStruct((s, d), F32)] * 3,
        compiler_params=_params("arbitrary", "arbitrary"),
    )


def _matmul_res(name, a, w, x, alpha):
    s, kd = a.shape
    d = w.shape[-1]
    tm = _row_tile(s)

    def body(a_ref, w_ref, x_ref, o_ref):
        o_ref[...] = x_ref[...] + alpha * _dot(a_ref[...].astype(BF16), w_ref[...])

    return pl.pallas_call(
        body, grid=(s // tm,), name=name,
        in_specs=[pl.BlockSpec((tm, kd), lambda i: (i, 0)), pl.BlockSpec((kd, d), lambda i: (0, 0)), pl.BlockSpec((tm, d), lambda i: (i, 0))],
        out_specs=pl.BlockSpec((tm, d), lambda i: (i, 0)),
        out_shape=jax.ShapeDtypeStruct((s, d), F32),
        compiler_params=_params("arbitrary"),
    )(a, w, x)


def _matmul_nt(name, a, w):
    s, n = a.shape
    kd = w.shape[0]
    tm = _row_tile(s)

    def body(a_ref, w_ref, o_ref):
        o_ref[...] = _dot_nt(a_ref[...].astype(BF16), w_ref[...]).astype(BF16)

    return pl.pallas_call(
        body, grid=(s // tm,), name=name,
        in_specs=[pl.BlockSpec((tm, n), lambda i: (i, 0)), pl.BlockSpec((kd, n), lambda i: (0, 0))],
        out_specs=pl.BlockSpec((tm, kd), lambda i: (i, 0)),
        out_shape=jax.ShapeDtypeStruct((s, kd), BF16),
        compiler_params=_params("arbitrary"),
    )(a, w)


def _pool_matrix(rows0, cols0, nr, nc, window, transpose):
    r = rows0 + lax.broadcasted_iota(jnp.int32, (nr, nc), 0)
    c = cols0 + lax.broadcasted_iota(jnp.int32, (nr, nc), 1)
    tt, ss = (c, r) if transpose else (r, c)
    inside = jnp.logical_and(tt - ss >= 0, tt - ss < window)
    cnt = jnp.minimum(tt + 1, window).astype(F32)
    return jnp.where(inside, 1.0 / cnt, 0.0) - jnp.where(tt == ss, 1.0, 0.0)


def _pool_tile(s):
    return min(256, s)


def _pool_fwd(u, wgrp, scale, x, side=None):
    s, d = u.shape
    tm = _pool_tile(s)
    halo = min(POOL_HALO, tm)
    ratio = tm // halo

    def body(u_ref, prev_ref, w_ref, sc_ref, x_ref, o_ref, p_ref):
        i = pl.program_id(0)
        t0 = i * tm
        for gi, window in enumerate(POOL_WINDOWS):
            cols = slice(POOL_GROUP * gi, POOL_GROUP * (gi + 1))
            pooled = _dot_3x(_pool_matrix(t0, t0, tm, tm, window, False), u_ref[:, cols])
            prev = jnp.where(i > 0, prev_ref[:, cols], 0.0)
            pooled += _dot_3x(_pool_matrix(t0, t0 - halo, tm, halo, window, False), prev)
            pb = pooled.astype(BF16)
            p_ref[:, cols] = pb
            o_ref[:, cols] = x_ref[:, cols] + _dot(pb, w_ref[gi]) * sc_ref[:, cols]

    row = pl.BlockSpec((tm, d), lambda i: (i, 0))
    return _host_call(
        body, (u, u, wgrp, scale, x), side, grid=(s // tm,), name="pool_fwd",
        in_specs=[
            row,
            pl.BlockSpec((halo, d), lambda i: (jnp.maximum(i * ratio - 1, 0), 0)),
            pl.BlockSpec((4, POOL_GROUP, POOL_GROUP), lambda i: (0, 0, 0)),
            pl.BlockSpec((1, d), lambda i: (0, 0)),
            row,
        ],
        out_specs=[row, row],
        out_shape=[jax.ShapeDtypeStruct((s, d), F32), jax.ShapeDtypeStruct((s, d), BF16)],
        compiler_params=_params("arbitrary"),
    )


def _pool_bwd_group(dx, pooled, wgrp, scale):
    s, d = dx.shape
    tm = _pool_tile(s)
    nsteps = s // tm

    def body(dx_ref, p_ref, w_ref, sc_ref, dp_ref, dw_ref, dsc_ref, acc_ref):
        i = pl.program_id(0)

        @pl.when(i == 0)
        def _():
            acc_ref[...] = jnp.zeros_like(acc_ref)
            dsc_ref[...] = jnp.zeros_like(dsc_ref)

        for gi in range(len(POOL_WINDOWS)):
            cols = slice(POOL_GROUP * gi, POOL_GROUP * (gi + 1))
            pb = p_ref[:, cols]
            dxg = dx_ref[:, cols]
            y = _dot(pb, w_ref[gi])
            dsc_ref[:, cols] += jnp.sum(dxg * y, axis=0, keepdims=True)
            dyb = (dxg * sc_ref[:, cols]).astype(BF16)
            dp_ref[:, cols] = _dot_nt(dyb, w_ref[gi])
            acc_ref[gi] += _dot_tn(pb, dyb)

        @pl.when(i == nsteps - 1)
        def _():
            dw_ref[...] = acc_ref[...].astype(BF16)

    row = pl.BlockSpec((tm, d), lambda i: (i, 0))
    grp = pl.BlockSpec((4, POOL_GROUP, POOL_GROUP), lambda i: (0, 0, 0))
    vec = pl.BlockSpec((1, d), lambda i: (0, 0))
    return pl.pallas_call(
        body, grid=(nsteps,), name="pool_bwd_group",
        in_specs=[row, row, grp, vec],
        out_specs=[row, grp, vec],
        out_shape=[jax.ShapeDtypeStruct((s, d), F32), jax.ShapeDtypeStruct((4, POOL_GROUP, POOL_GROUP), BF16), jax.ShapeDtypeStruct((1, d), F32)],
        scratch_shapes=[pltpu.VMEM((4, POOL_GROUP, POOL_GROUP), F32)],
        compiler_params=_params("arbitrary"),
    )(dx, pooled, wgrp, scale)


def _pool_bwd_window(dp):
    s, d = dp.shape
    tm = _pool_tile(s)
    halo = min(POOL_HALO, tm)
    ratio = tm // halo
    nsteps = s // tm

    def body(dp_ref, next_ref, o_ref):
        i = pl.program_id(0)
        t0 = i * tm
        for gi, window in enumerate(POOL_WINDOWS):
            cols = slice(POOL_GROUP * gi, POOL_GROUP * (gi + 1))
            du = _dot_3x(_pool_matrix(t0, t0, tm, tm, window, True), dp_ref[:, cols])
            nxt = jnp.where(i < nsteps - 1, next_ref[:, cols], 0.0)
            du += _dot_3x(_pool_matrix(t0, t0 + tm, tm, halo, window, True), nxt)
            o_ref[:, cols] = du.astype(BF16)

    row = pl.BlockSpec((tm, d), lambda i: (i, 0))
    return pl.pallas_call(
        body, grid=(nsteps,), name="pool_bwd_window",
        in_specs=[row, pl.BlockSpec((halo, d), lambda i: (jnp.minimum((i + 1) * ratio, s // halo - 1), 0))],
        out_specs=row,
        out_shape=jax.ShapeDtypeStruct((s, d), BF16),
        compiler_params=_params("arbitrary"),
    )(dp, dp)


def _ple_fwd(x, g, wgate, p, wproj, side=None):
    s, d = x.shape
    pd = p.shape[-1]
    tm = _row_tile(s)

    def body(x_ref, g_ref, wg_ref, p_ref, wp_ref, o_ref, h_ref, sig_ref, proj_ref):
        x = x_ref[...]
        h = _rms(x, g_ref[...]).astype(BF16)
        sig = jax.nn.sigmoid(_dot(h, wg_ref[...]))
        proj = _dot(p_ref[...].astype(BF16), wp_ref[...])
        o_ref[...] = x + sig * proj
        h_ref[...] = h
        sig_ref[...] = sig.astype(BF16)
        proj_ref[...] = proj.astype(BF16)

    row = pl.BlockSpec((tm, d), lambda i: (i, 0))
    return _host_call(
        body, (x, g, wgate, p, wproj), side, grid=(s // tm,), name="ple_fwd",
        in_specs=[
            row,
            pl.BlockSpec((1, d), lambda i: (0, 0)),
            pl.BlockSpec((d, d), lambda i: (0, 0)),
            pl.BlockSpec((tm, pd), lambda i: (i, 0)),
            pl.BlockSpec((pd, d), lambda i: (0, 0)),
        ],
        out_specs=[row] * 4,
        out_shape=[jax.ShapeDtypeStruct((s, d), F32)] + [jax.ShapeDtypeStruct((s, d), BF16)] * 3,
        compiler_params=_params("arbitrary"),
    )


def _ple_bwd_gate(dx, sig, proj):
    s, d = dx.shape
    tm = _row_tile(s)

    def body(dx_ref, sig_ref, proj_ref, dg_ref, dp_ref):
        dx = dx_ref[...]
        sig = sig_ref[...].astype(F32)
        dg_ref[...] = (dx * proj_ref[...].astype(F32) * (sig * (1.0 - sig))).astype(BF16)
        dp_ref[...] = (dx * sig).astype(BF16)

    row = pl.BlockSpec((tm, d), lambda i: (i, 0))
    return pl.pallas_call(
        body, grid=(s // tm,), name="ple_bwd_gate",
        in_specs=[row, row, row], out_specs=[row, row],
        out_shape=[jax.ShapeDtypeStruct((s, d), BF16)] * 2,
        compiler_params=_params("arbitrary"),
    )(dx, sig, proj)


def _loss_head(y, target):
    s, d = y.shape
    tm = _row_tile(s)
    nsteps = s // tm

    def body(y_ref, t_ref, dy_ref, loss_ref, acc_ref):
        i = pl.program_id(0)

        @pl.when(i == 0)
        def _():
            acc_ref[...] = jnp.zeros_like(acc_ref)

        err = y_ref[...] - t_ref[...]
        dy_ref[...] = err * (1.0 / d)
        acc_ref[...] += jnp.sum(jnp.mean(err * err, axis=-1, keepdims=True), axis=0, keepdims=True)

        @pl.when(i == nsteps - 1)
        def _():
            loss_ref[...] = 0.5 * acc_ref[...]

    row = pl.BlockSpec((tm, d), lambda i: (i, 0))
    return pl.pallas_call(
        body, grid=(nsteps,), name="loss_head",
        in_specs=[row, row], out_specs=[row, pl.BlockSpec((8, 128), lambda i: (0, 0))],
        out_shape=[jax.ShapeDtypeStruct((s, d), F32), jax.ShapeDtypeStruct((8, 128), F32)],
        scratch_shapes=[pltpu.VMEM((8, 128), F32)],
        compiler_params=_params("arbitrary"),
    )(y, target)


def _adamw_math(w, g, m, v):
    m = ADAM_B1 * m + (1.0 - ADAM_B1) * g
    v = ADAM_B2 * v + (1.0 - ADAM_B2) * (g * g)
    m_hat = m / (1.0 - ADAM_B1 ** ADAM_STEP)
    v_hat = v / (1.0 - ADAM_B2 ** ADAM_STEP)
    delta = -ADAM_LR * (m_hat / (jnp.sqrt(v_hat) + ADAM_EPS) + ADAM_WD * w)
    return delta, m, v


def _adamw_layer(parts, w, m, v, layer, outs):
    nl, r, c = w.shape
    tr = max(t for t in range(16, r + 1, 16) if r % t == 0 and t * c <= ADAMW_BLOCK_ELEMS)

    def body(p_ref, w_ref, m_ref, v_ref, *rest):
        g_ref, d_ref, nm_ref, nv_ref = rest[-4:]
        g = p_ref[0].astype(F32)
        for dev in range(1, N_DEV):
            g = g + p_ref[dev].astype(F32)
        delta, nm, nv = _adamw_math(w_ref[0], g, m_ref[0], v_ref[0])
        g_ref[0] = g
        d_ref[0] = delta
        nm_ref[0] = nm
        nv_ref[0] = nv

    slab = pl.BlockSpec((1, tr, c), lambda i: (layer, i, 0))
    any_spec = pl.BlockSpec(memory_space=pl.ANY)
    shape = jax.ShapeDtypeStruct(w.shape, F32)
    carried = [] if outs is None else list(outs)
    return pl.pallas_call(
        body, grid=(r // tr,), name="adamw",
        in_specs=[pl.BlockSpec((N_DEV, tr, c), lambda i: (0, i, 0)), slab, slab, slab] + [any_spec] * len(carried),
        out_specs=[slab] * 4,
        out_shape=[shape] * 4,
        input_output_aliases={4 + n: n for n in range(len(carried))},
        compiler_params=_params("arbitrary"),
    )(parts, w, m, v, *carried)


def _adamw_small(parts, w, m, v):
    r, c = w.shape

    def body(p_ref, w_ref, m_ref, v_ref, g_ref, d_ref, nm_ref, nv_ref):
        g = p_ref[0:r, :]
        for dev in range(1, N_DEV):
            g = g + p_ref[dev * r:(dev + 1) * r, :]
        delta, nm, nv = _adamw_math(w_ref[...], g, m_ref[...], v_ref[...])
        g_ref[...] = g
        d_ref[...] = delta
        nm_ref[...] = nm
        nv_ref[...] = nv

    shape = jax.ShapeDtypeStruct((r, c), F32)
    return pl.pallas_call(body, name="adamw_small", out_shape=[shape] * 4)(parts, w, m, v)


def _my_place():
    return lax.axis_index("x"), lax.axis_index("y"), lax.axis_index("c")


def _peer(k):
    x, y, c = _my_place()
    return (x ^ (k >> 2), y ^ ((k >> 1) & 1), c ^ (k & 1))


def _block_of(place):
    x, y, c = place
    return 4 * x + 2 * y + c


def _gather_small(block):
    m_per, n = block.shape

    def body(x_ref, out_ref, send_sems, recv_sems, local_sem):
        me = _block_of(_my_place())

        def rows(b):
            return out_ref.at[pl.ds(b * m_per, m_per), :]

        mine = pltpu.make_async_copy(x_ref, rows(me), local_sem)
        mine.start()
        sends = []
        for k in range(1, N_DEV):
            cp = pltpu.make_async_remote_copy(
                src_ref=x_ref, dst_ref=rows(me), send_sem=send_sems.at[k - 1], recv_sem=recv_sems.at[k - 1],
                device_id=_peer(k), device_id_type=MESH)
            cp.start()
            sends.append(cp)
        for k in range(1, N_DEV):
            src = rows(_block_of(_peer(k)))
            pltpu.make_async_remote_copy(
                src_ref=src, dst_ref=src, send_sem=send_sems.at[k - 1], recv_sem=recv_sems.at[k - 1],
                device_id=_peer(k), device_id_type=MESH).wait_recv()
        for cp in sends:
            cp.wait_send()
        mine.wait()

    return pl.pallas_call(
        body, name="gather_small",
        out_shape=jax.ShapeDtypeStruct((N_DEV * m_per, n), block.dtype),
        in_specs=[pl.BlockSpec(memory_space=pltpu.VMEM)],
        out_specs=pl.BlockSpec(memory_space=pltpu.VMEM),
        scratch_shapes=[pltpu.SemaphoreType.DMA((N_DEV - 1,)), pltpu.SemaphoreType.DMA((N_DEV - 1,)), pltpu.SemaphoreType.DMA],
    )(block)


def _gather_weights(name, stacks, layers):
    n_t = len(stacks)

    def body(*refs):
        srcs, outs = refs[:n_t], refs[n_t:2 * n_t]
        send_sems, recv_sems, local_sems = refs[2 * n_t:]
        x, y, c = _my_place()
        me, sibling = (x, y, c), (x, y, 1 - c)
        chips = [(1 - x, y), (x, 1 - y), (1 - x, 1 - y)]

        def copy(t, k, block, to, src=None):
            slot = outs[t].at[_block_of(block)]
            return pltpu.make_async_remote_copy(
                src_ref=slot if src is None else src, dst_ref=slot,
                send_sem=send_sems.at[t, k], recv_sem=recv_sems.at[t, k], device_id=to, device_id_type=MESH)

        local, sends = [], []
        for t in range(n_t):
            src = srcs[t].at[layers[t]]
            own = pltpu.make_async_copy(src, outs[t].at[_block_of(me)], local_sems.at[t])
            own.start()
            local.append(own)
            sends.append(copy(t, 0, me, sibling, src=src))
            sends += [copy(t, 1 + j, me, (*chip, c), src=src) for j, chip in enumerate(chips)]
        for cp in sends:
            cp.start()
        for t in range(n_t):
            for j, chip in enumerate(chips):
                copy(t, 1 + j, (*chip, c), me).wait_recv()
                passed = copy(t, 4 + j, (*chip, c), sibling)
                passed.start()
                sends.append(passed)
        for t in range(n_t):
            copy(t, 0, sibling, me).wait_recv()
            for j, chip in enumerate(chips):
                copy(t, 4 + j, (*chip, 1 - c), me).wait_recv()
        for cp in sends:
            cp.wait_send()
        for cp in local:
            cp.wait()

    any_spec = pl.BlockSpec(memory_space=pl.ANY)
    return pl.pallas_call(
        body, name=name,
        out_shape=[jax.ShapeDtypeStruct((N_DEV,) + st.shape[1:], st.dtype) for st in stacks],
        in_specs=[any_spec] * n_t, out_specs=[any_spec] * n_t,
        scratch_shapes=[pltpu.SemaphoreType.DMA((n_t, N_DEV - 1)), pltpu.SemaphoreType.DMA((n_t, N_DEV - 1)), pltpu.SemaphoreType.DMA((n_t,))],
    )(*stacks)


class _ScatterSide:
    def __init__(self, grads):
        self.operands = list(grads)
        self.n = len(self.operands)

    def out_shape(self):
        return [jax.ShapeDtypeStruct(g.shape, g.dtype) for g in self.operands]

    def scratch(self):
        return [pltpu.SemaphoreType.DMA((self.n, N_DEV - 1)), pltpu.SemaphoreType.DMA((self.n, N_DEV - 1)), pltpu.SemaphoreType.DMA((self.n,))]

    def _copies(self, srcs, outs, send_sems, recv_sems, local_sems):
        me = _block_of(_my_place())
        local, sends, arrivals = [], [], []
        for t in range(self.n):
            local.append(pltpu.make_async_copy(srcs[t].at[me], outs[t].at[me], local_sems.at[t]))
            for k in range(1, N_DEV):
                sems = dict(send_sem=send_sems.at[t, k - 1], recv_sem=recv_sems.at[t, k - 1], device_id=_peer(k), device_id_type=MESH)
                sends.append(pltpu.make_async_remote_copy(src_ref=srcs[t].at[_block_of(_peer(k))], dst_ref=outs[t].at[me], **sems))
                slot = outs[t].at[_block_of(_peer(k))]
                arrivals.append(pltpu.make_async_remote_copy(src_ref=slot, dst_ref=slot, **sems))
        return local, sends, arrivals

    def start(self, *refs):
        local, sends, _ = self._copies(*refs)
        for cp in local + sends:
            cp.start()

    def finish(self, *refs):
        local, sends, arrivals = self._copies(*refs)
        for cp in arrivals:
            cp.wait_recv()
        for cp in sends:
            cp.wait_send()
        for cp in local:
            cp.wait()


class _GatherSide(_ScatterSide):
    def __init__(self, stacks, layers):
        super().__init__(stacks)
        self.layers = list(layers)

    def out_shape(self):
        return [jax.ShapeDtypeStruct((N_DEV,) + st.shape[1:], st.dtype) for st in self.operands]

    def _copies(self, srcs, outs, send_sems, recv_sems, local_sems):
        me = _block_of(_my_place())
        local, sends, arrivals = [], [], []
        for t in range(self.n):
            src = srcs[t].at[self.layers[t]]
            local.append(pltpu.make_async_copy(src, outs[t].at[me], local_sems.at[t]))
            for k in range(1, N_DEV):
                sems = dict(send_sem=send_sems.at[t, k - 1], recv_sem=recv_sems.at[t, k - 1], device_id=_peer(k), device_id_type=MESH)
                sends.append(pltpu.make_async_remote_copy(src_ref=src, dst_ref=outs[t].at[me], **sems))
                slot = outs[t].at[_block_of(_peer(k))]
                arrivals.append(pltpu.make_async_remote_copy(src_ref=slot, dst_ref=slot, **sems))
        return local, sends, arrivals


def _scatter_grads(name, grads):
    side = _ScatterSide(grads)
    n_t = side.n

    def body(*refs):
        parts = (refs[:n_t], refs[n_t:2 * n_t]) + tuple(refs[2 * n_t:])
        side.start(*parts)
        side.finish(*parts)

    any_spec = pl.BlockSpec(memory_space=pl.ANY)
    return pl.pallas_call(
        body, name=name, out_shape=side.out_shape(), in_specs=[any_spec] * n_t, out_specs=[any_spec] * n_t,
        scratch_shapes=side.scratch(),
    )(*grads)


def _host_call(body, args, side, *, grid, in_specs, out_specs, out_shape, scratch_shapes=(), **kw):
    if side is None:
        return pl.pallas_call(body, grid=grid, in_specs=in_specs, out_specs=out_specs, out_shape=out_shape,
                              scratch_shapes=list(scratch_shapes), **kw)(*args), None
    n_in, n_out, n_scr, n_side = len(in_specs), len(out_specs), len(scratch_shapes), side.n

    def hosted(*refs):
        cuts = [n_in, n_side, n_out, n_side, n_scr, 3]
        groups, at = [], 0
        for c in cuts:
            groups.append(refs[at:at + c])
            at += c
        ins, side_in, outs, side_out, scr, sems = groups
        first, last = None, None
        for axis, size in enumerate(grid):
            at_start, at_end = pl.program_id(axis) == 0, pl.program_id(axis) == size - 1
            first = at_start if first is None else jnp.logical_and(first, at_start)
            last = at_end if last is None else jnp.logical_and(last, at_end)

        @pl.when(first)
        def _():
            side.start(side_in, side_out, *sems)

        body(*ins, *outs, *scr)

        @pl.when(last)
        def _():
            side.finish(side_in, side_out, *sems)

    any_spec = pl.BlockSpec(memory_space=pl.ANY)
    res = pl.pallas_call(
        hosted, grid=grid, in_specs=list(in_specs) + [any_spec] * n_side, out_specs=list(out_specs) + [any_spec] * n_side,
        out_shape=list(out_shape) + side.out_shape(), scratch_shapes=list(scratch_shapes) + side.scratch(), **kw,
    )(*args, *side.operands)
    return res[:n_out], res[n_out:]


def _ffn_forward(x, g, wt, wd, carried):
    sides = {k: _GatherSide(*v) if v[0] else None for k, v in carried.items()}
    parts = {}
    (h, gate, up), parts["up"] = _ffn_up(x, g, wt, sides.get("up"))
    (x_new,), parts["down"] = _ffn_down(gate, up, wd, x, sides.get("down"))
    return x_new, (x, g, h, gate, up), parts


def _ffn_backward(dx, saved, wt, wd, carried):
    x, g, h, gate, up = saved
    sides = {k: _ScatterSide(v) if v else None for k, v in carried.items()}
    parts = {}
    (dgate, dup), parts["act"] = _ffn_bwd_act(dx, wd, gate, up, sides.get("act"))
    d_wd = _grad_ffn_down(gate, up, dx)
    (dx_in, dg), parts["dh"] = _ffn_dh(dgate, dup, wt, x, g, dx, sides.get("dh"))
    d_wt, parts["gu"] = _grad_ffn_gu(h, dgate, dup, sides.get("gu"))
    return dx_in, dg, d_wt.reshape(N_DEV, -1, D_MODEL), d_wd.reshape(N_DEV, -1, D_MODEL), parts


def kernel(x, p, norm_ffn1, w_ffn1_gu, w_ffn1_down, norm_mix, w_qkv, q_norm, k_norm, w_o, w_pool_in, w_pool_grp, pool_scale, norm_ffn2, w_ffn2_gu, w_ffn2_down, norm_ple, w_ple_gate, w_ple_proj, loss_target, m_norm_ffn1, m_w_ffn1_gu, m_w_ffn1_down, m_norm_mix, m_w_qkv, m_q_norm, m_k_norm, m_w_o, m_w_pool_in, m_w_pool_grp, m_pool_scale, m_norm_ffn2, m_w_ffn2_gu, m_w_ffn2_down, m_norm_ple, m_w_ple_gate, m_w_ple_proj, v_norm_ffn1, v_w_ffn1_gu, v_w_ffn1_down, v_norm_mix, v_w_qkv, v_q_norm, v_k_norm, v_w_o, v_w_pool_in, v_w_pool_grp, v_pool_scale, v_norm_ffn2, v_w_ffn2_gu, v_w_ffn2_down, v_norm_ple, v_w_ple_gate, v_w_ple_proj):
    d = D_MODEL
    xs = x[0]
    target = loss_target[0]
    me = _block_of(_my_place())

    def tr(w):
        return jnp.swapaxes(w, 1, 2)

    big = dict(w_ffn1_gu=tr(w_ffn1_gu), w_ffn1_down=w_ffn1_down, w_qkv=w_qkv, w_o=w_o, w_pool_in=w_pool_in,
               w_pool_grp=w_pool_grp.reshape(2, 4 * 32, POOL_GROUP), w_ffn2_gu=tr(w_ffn2_gu), w_ffn2_down=w_ffn2_down,
               w_ple_gate=w_ple_gate, w_ple_proj=w_ple_proj)
    moments = dict(
        w_ffn1_gu=(tr(m_w_ffn1_gu), tr(v_w_ffn1_gu)), w_ffn1_down=(m_w_ffn1_down, v_w_ffn1_down), w_qkv=(m_w_qkv, v_w_qkv),
        w_o=(m_w_o, v_w_o), w_pool_in=(m_w_pool_in, v_w_pool_in),
        w_pool_grp=(m_w_pool_grp.reshape(2, 4 * 32, POOL_GROUP), v_w_pool_grp.reshape(2, 4 * 32, POOL_GROUP)),
        w_ffn2_gu=(tr(m_w_ffn2_gu), tr(v_w_ffn2_gu)), w_ffn2_down=(m_w_ffn2_down, v_w_ffn2_down),
        w_ple_gate=(m_w_ple_gate, v_w_ple_gate), w_ple_proj=(m_w_ple_proj, v_w_ple_proj))
    half = {name: _cast_bf16(w) for name, w in big.items()}

    def layer_names(i):
        mixer = ["w_qkv", "w_o"] if i % 2 == 0 else ["w_pool_in", "w_pool_grp"]
        return ["w_ffn1_gu", "w_ffn1_down"] + mixer + ["w_ffn2_gu", "w_ffn2_down", "w_ple_gate", "w_ple_proj"]

    def layer_index(name, i):
        return i // 2 if name in ("w_qkv", "w_o", "w_pool_in", "w_pool_grp") else i

    scale_all = _gather_small(jnp.pad(pool_scale, ((0, 6), (0, 0))))
    scale_full = scale_all.reshape(N_DEV, 8, 128)[:, :2].transpose(1, 0, 2).reshape(2, d)

    saved = []
    weights = []
    cur = xs

    def fwd_carriers(i):
        if i % 2 == 0:
            return {
                ("ffn1", "up"): ["w_ffn1_gu"], ("ffn2", "up"): ["w_ffn2_gu"], ("ffn1", "down"): [], ("ffn2", "down"): [],
                ("mix", "attn"): ["w_ffn1_down", "w_ffn2_down"],
                ("ple", "ple"): ["w_ple_gate", "w_ple_proj", "w_pool_in", "w_pool_grp"],
            }
        return {
            ("ffn1", "up"): ["w_ffn1_gu"], ("ffn2", "up"): ["w_ffn2_gu"],
            ("ffn1", "down"): ["w_ffn1_down"], ("ffn2", "down"): ["w_ffn2_down"],
            ("mix", "pool"): ["w_qkv", "w_o"],
            ("ple", "ple"): ["w_ple_gate", "w_ple_proj"],
        }

    def next_weights(i, stage):
        nxt = layer_names(i + 1) if i + 1 < DEPTH else []
        out = {}
        for (st, host), names in fwd_carriers(i).items():
            if st == stage:
                take = [n for n in names if n in nxt]
                out[host] = ([half[n] for n in take], [layer_index(n, i + 1) for n in take])
        return out

    def arrived_weights(i, stage, parts, into):
        nxt = layer_names(i + 1) if i + 1 < DEPTH else []
        for (st, host), names in fwd_carriers(i).items():
            if st == stage and parts.get(host) is not None:
                into.update(zip([n for n in names if n in nxt], parts[host]))

    names = layer_names(0)
    coming = dict(zip(names, _gather_weights("gather_weights_0", [half[n] for n in names], [layer_index(n, 0) for n in names])))
    for i in range(DEPTH):
        wl, coming = coming, {}
        assert set(wl) == set(layer_names(i)), sorted(wl)
        for n in ("w_ffn1_gu", "w_ffn2_gu", "w_ffn1_down", "w_ffn2_down", "w_o", "w_pool_in", "w_ple_gate"):
            if n in wl:
                wl[n] = wl[n].reshape(-1, d)
        if "w_pool_grp" in wl:
            wl["w_pool_grp"] = wl["w_pool_grp"].reshape(N_DEV, 4, 32, POOL_GROUP).transpose(1, 0, 2, 3).reshape(4, POOL_GROUP, POOL_GROUP)
        for n in ("w_qkv", "w_ple_proj"):
            if n in wl:
                wl[n] = _from_column_shards(wl[n])
        weights.append(wl)
        j = i // 2
        rec = {}
        cur, rec["ffn1"], parts = _ffn_forward(cur, norm_ffn1[i][None], wl["w_ffn1_gu"], wl["w_ffn1_down"], next_weights(i, "ffn1"))
        arrived_weights(i, "ffn1", parts, coming)
        x1 = cur
        gm = norm_mix[i][None]
        if i % 2 == 0:
            qg = jnp.tile(q_norm[j], d // HEAD_DIM)[None]
            kg = jnp.tile(k_norm[j], d // HEAD_DIM)[None]
            hm, qkv = _norm_matmul("qkv_proj", x1, gm, wl["w_qkv"], (d, d), lambda a, b: (0, b), _pick_all, 3 * d, d, 3)
            qn, kn, vb = _qk_norm(qkv, qg, kg)
            stacks, layers = next_weights(i, "mix")["attn"]
            o, parts = _attention(qn, kn, vb, _GatherSide(stacks, layers) if stacks else None)
            arrived_weights(i, "mix", {"attn": parts}, coming)
            cur = _matmul_res("attn_out", o, wl["w_o"], x1, 1.0)
            rec["mix"] = (x1, gm, hm, qkv, qg, kg, qn, kn, vb, o)
        else:
            sc = scale_full[j][None]
            hm, u = _norm_matmul("pool_in", x1, gm, wl["w_pool_in"], (d, d), lambda a, b: (0, 0), _pick_all, d, d, 1)
            stacks, layers = next_weights(i, "mix")["pool"]
            (cur, pooled), parts = _pool_fwd(u, wl["w_pool_grp"], sc, x1, _GatherSide(stacks, layers) if stacks else None)
            arrived_weights(i, "mix", {"pool": parts}, coming)
            rec["mix"] = (x1, gm, hm, sc, pooled)
        cur, rec["ffn2"], parts = _ffn_forward(cur, norm_ffn2[i][None], wl["w_ffn2_gu"], wl["w_ffn2_down"], next_weights(i, "ffn2"))
        arrived_weights(i, "ffn2", parts, coming)
        x3 = cur
        gp = norm_ple[i][None]
        stacks, layers = next_weights(i, "ple")["ple"]
        (cur, hp, sig, proj), parts = _ple_fwd(x3, gp, wl["w_ple_gate"], p[i, 0], wl["w_ple_proj"],
                                               _GatherSide(stacks, layers) if stacks else None)
        arrived_weights(i, "ple", {"ple": parts}, coming)
        rec["ple"] = (x3, gp, hp, sig, proj)
        saved.append(rec)

    dy, loss_part = _loss_head(cur, target)
    loss = lax.psum(loss_part[0, 0], ("x", "y", "c"))

    small = {n: [None] * DEPTH for n in ("norm_ffn1", "norm_mix", "norm_ffn2", "norm_ple")}
    small.update(q_norm=[None] * 2, k_norm=[None] * 2, pool_scale=[None] * 2)
    results = {name: None for name in big}

    def update(layer, parts):
        for n, part in parts.items():
            mm, vv = moments[n]
            results[n] = _adamw_layer(part, big[n], mm, vv, layer_index(n, layer), results[n])

    carriers = {
        ("ffn2", "gu"): ["w_ffn1_gu"], ("ffn1", "gu"): ["w_ffn2_gu"],
        ("ffn2", "dh"): ["w_ffn1_down", "w_ple_gate", "w_ple_proj"],
        ("ffn1", "dh"): ["w_ffn2_down", "w_o", "w_pool_in", "w_pool_grp"],
        ("ffn2", "act"): ["w_qkv"],
    }
    above = {}
    own = {}

    def carried(ffn):
        return {host: [above[n] for n in names if n in above] for (f, host), names in carriers.items() if f == ffn}

    def received(ffn, parts, into):
        for (f, host), names in carriers.items():
            if f == ffn and parts.get(host) is not None:
                into.update(zip([n for n in names if n in above], parts[host]))

    dcur = dy
    for i in reversed(range(DEPTH)):
        wl, rec, j = weights[i], saved[i], i // 2
        grads = {}
        arrived = {}
        x3, gp, hp, sig, proj = rec["ple"]
        dgate, dproj = _ple_bwd_gate(dcur, sig, proj)
        dx3, small["norm_ple"][i] = _square_dh("ple_dh", dgate, wl["w_ple_gate"], x3, gp, dcur)
        grads["w_ple_gate"] = _grad_square("grad_ple_gate", hp, dgate).reshape(N_DEV, d // N_DEV, d)
        grads["w_ple_proj"] = _column_shards(_grad_square("grad_ple_proj", p[i, 0], dproj))
        dx2, small["norm_ffn2"][i], grads["w_ffn2_gu"], grads["w_ffn2_down"], parts = _ffn_backward(
            dx3, rec["ffn2"], wl["w_ffn2_gu"], wl["w_ffn2_down"], carried("ffn2"))
        received("ffn2", parts, arrived)
        if i % 2 == 0:
            x1, gm, hm, qkv, qg, kg, qn, kn, vb, o = rec["mix"]
            do = _matmul_nt("attn_out_bwd", dx2, wl["w_o"])
            grads["w_o"] = _grad_square("grad_attn_out", o, dx2).reshape(N_DEV, d // N_DEV, d)
            early = ["w_ple_gate", "w_ple_proj", "w_ffn2_gu", "w_ffn2_down"] if i == 0 else []
            (dq, dk, dv), parts = _attention_bwd(qn, kn, vb, do, o, _ScatterSide([grads[n] for n in early]) if early else None)
            own.update(zip(early, parts or []))
            dqkv, dqg, dkg = _qk_norm_bwd(qkv, dq, dk, dv, qg, kg)
            small["q_norm"][j], small["k_norm"][j] = dqg[0:1], dkg[0:1]
            dx1, small["norm_mix"][i] = _qkv_dh(dqkv, wl["w_qkv"], x1, gm, dx2)
            grads["w_qkv"] = _column_shards(_grad_square("grad_qkv", hm, dqkv))
        else:
            x1, gm, hm, sc, pooled = rec["mix"]
            dpool, dgrp, small["pool_scale"][j] = _pool_bwd_group(dx2, pooled, wl["w_pool_grp"], sc)
            grads["w_pool_grp"] = dgrp.reshape(4, N_DEV, 32, POOL_GROUP).transpose(1, 0, 2, 3).reshape(N_DEV, 4 * 32, POOL_GROUP)
            du = _pool_bwd_window(dpool)
            dx1, small["norm_mix"][i] = _square_dh("pool_dh", du, wl["w_pool_in"], x1, gm, dx2)
            grads["w_pool_in"] = _grad_square("grad_pool_in", hm, du).reshape(N_DEV, d // N_DEV, d)
        ffn1_carried = carried("ffn1")
        late = ["w_qkv", "w_o"] if i == 0 else []
        ffn1_carried["act"] = [grads[n] for n in late]
        dcur, small["norm_ffn1"][i], grads["w_ffn1_gu"], grads["w_ffn1_down"], parts = _ffn_backward(
            dx1, rec["ffn1"], wl["w_ffn1_gu"], wl["w_ffn1_down"], ffn1_carried)
        received("ffn1", parts, arrived)
        own.update(zip(late, parts["act"] or []))
        if above:
            assert set(arrived) == set(above), (sorted(arrived), sorted(above))
            update(i + 1, arrived)
        above = grads

    rest = [n for n in layer_names(0) if n not in own]
    own.update(zip(rest, _scatter_grads("scatter_grads_0", [above[n] for n in rest])))
    update(0, own)

    def lanes(a):
        return jnp.pad(a, ((0, 0), (0, d - a.shape[-1])))

    order = [("norm_ffn1", DEPTH), ("norm_mix", DEPTH), ("norm_ffn2", DEPTH), ("norm_ple", DEPTH), ("pool_scale", 2), ("q_norm", 2), ("k_norm", 2)]
    rows = jnp.concatenate([lanes(g) for name, _ in order for g in small[name]], axis=0)
    n_rows = rows.shape[0]
    pad_rows = -n_rows % 8
    rows = jnp.pad(rows, ((0, pad_rows), (0, 0)))
    gathered = _gather_small(rows)

    def own_lanes(a):
        return lax.dynamic_update_slice(jnp.zeros((a.shape[0], d), F32), a, (0, me * 128))

    def pack(values):
        mats = [own_lanes(values[name]) if name == "pool_scale" else lanes(values[name]) for name, _ in order]
        return jnp.pad(jnp.concatenate(mats, axis=0), ((0, pad_rows), (0, 0)))

    small_w = dict(norm_ffn1=norm_ffn1, norm_mix=norm_mix, norm_ffn2=norm_ffn2, norm_ple=norm_ple, pool_scale=pool_scale, q_norm=q_norm, k_norm=k_norm)
    small_m = dict(norm_ffn1=m_norm_ffn1, norm_mix=m_norm_mix, norm_ffn2=m_norm_ffn2, norm_ple=m_norm_ple, pool_scale=m_pool_scale, q_norm=m_q_norm, k_norm=m_k_norm)
    small_v = dict(norm_ffn1=v_norm_ffn1, norm_mix=v_norm_mix, norm_ffn2=v_norm_ffn2, norm_ple=v_norm_ple, pool_scale=v_pool_scale, q_norm=v_q_norm, k_norm=v_k_norm)
    packed = _adamw_small(gathered, pack(small_w), pack(small_m), pack(small_v))

    def unpack(mat):
        out, at = {}, 0
        for name, n in order:
            blk = mat[at:at + n]
            at += n
            if name == "pool_scale":
                out[name] = lax.dynamic_slice(blk, (0, me * 128), (n, 128))
            elif name in ("q_norm", "k_norm"):
                out[name] = blk[:, :HEAD_DIM]
            else:
                out[name] = blk
        return out

    small_out = [unpack(mat) for mat in packed]

    def result(kind, name):
        if name in small_w:
            return small_out[kind][name]
        r = results[name][kind]
        if name in ("w_ffn1_gu", "w_ffn2_gu"):
            return tr(r)
        return r.reshape(w_pool_grp.shape) if name == "w_pool_grp" else r

    weight_names = ["norm_ffn1", "w_ffn1_gu", "w_ffn1_down", "norm_mix", "w_qkv", "q_norm", "k_norm", "w_o", "w_pool_in", "w_pool_grp",
                    "pool_scale", "norm_ffn2", "w_ffn2_gu", "w_ffn2_down", "norm_ple", "w_ple_gate", "w_ple_proj"]
    outs = [loss, dcur[None]]
    for kind in range(4):
        outs += [result(kind, name) for name in weight_names]
    return tuple(outs)
```

```python
import math

import jax
import jax.numpy as jnp
from jax import lax
from jax.experimental import pallas as pl
from jax.experimental.pallas import tpu as pltpu

F32 = jnp.float32
BF16 = jnp.bfloat16

N_DEV = 8
DEPTH = 4
D_MODEL = 1024
N_UNITS = D_MODEL // 128
HEAD_DIM = 64
POOL_WINDOWS = (2, 4, 8, 16)
POOL_GROUP = 256
POOL_HALO = 128
EPS = 1e-6
ATTN_SCALE = 1.0 / math.sqrt(HEAD_DIM)
ATTN_BLOCK = 128
ATTN_AHEAD = 2
LOG_ZERO = -104.0

ADAM_LR = 0.001
ADAM_B1 = 0.9
ADAM_B2 = 0.999
ADAM_EPS = 1e-08
ADAM_WD = 0.01
ADAM_STEP = 10
ADAMW_BLOCK_ELEMS = 128 * 1024

VMEM_LIMIT = 56 * 1024 * 1024
MESH = pl.DeviceIdType.MESH

NT_DIMS = (((1,), (1,)), ((), ()))
TN_DIMS = (((0,), (0,)), ((), ()))


def _params(*sem):
    return pltpu.CompilerParams(dimension_semantics=sem, vmem_limit_bytes=VMEM_LIMIT)


def _row_tile(s):
    return min(512, s)


def _grad_tile(s):
    return min(2048, s)


def _block_spec(block, index, mode=None):
    return pl.BlockSpec(block, index) if mode is None else pl.BlockSpec(block, index, pipeline_mode=mode)


def _dot(a, b):
    return jnp.dot(a, b, preferred_element_type=F32)


def _dot_nt(a, b):
    return lax.dot_general(a, b, NT_DIMS, preferred_element_type=F32)


def _dot_tn(a, b):
    return lax.dot_general(a, b, TN_DIMS, preferred_element_type=F32)


def _dot_f32(a, b):
    return jnp.dot(a, b, precision=lax.Precision.HIGHEST, preferred_element_type=F32)


def _dot_3x(a, b):
    return jnp.dot(a, b, precision=lax.Precision.HIGH, preferred_element_type=F32)


def _rms(x, g):
    r = lax.rsqrt(jnp.mean(x * x, axis=-1, keepdims=True) + EPS)
    return x * r * g


def _rms_bwd(dy, x, g):
    r = lax.rsqrt(jnp.mean(x * x, axis=-1, keepdims=True) + EPS)
    xh = x * r
    dg = jnp.sum(dy * xh, axis=0, keepdims=True)
    dxh = dy * g
    dx = r * (dxh - xh * jnp.mean(dxh * xh, axis=-1, keepdims=True))
    return dx, dg


def _cast_bf16(w):
    l, r, c = w.shape

    def body(w_ref, o_ref):
        o_ref[...] = w_ref[...].astype(BF16)

    return pl.pallas_call(
        body, grid=(l,), name="cast_bf16",
        in_specs=[pl.BlockSpec((1, r, c), lambda i: (i, 0, 0))],
        out_specs=pl.BlockSpec((1, r, c), lambda i: (i, 0, 0)),
        out_shape=jax.ShapeDtypeStruct(w.shape, BF16),
        compiler_params=_params("arbitrary"),
    )(w)


def _ffn_up(x, g, wt, side=None):
    s, d = x.shape
    f = wt.shape[0] // 2
    tn = f // 2
    tm = _row_tile(s)

    def body(x_ref, g_ref, w_ref, h_ref, gate_ref, up_ref, act_ref):
        h = _rms(x_ref[...], g_ref[...]).astype(BF16)
        h_ref[...] = h
        for half in range(2):
            cols = slice(half * tn, (half + 1) * tn)
            gate = _dot_nt(h, w_ref[half * tn:(half + 1) * tn, :])
            up = _dot_nt(h, w_ref[f + half * tn:f + (half + 1) * tn, :])
            gate_ref[:, cols] = gate.astype(BF16)
            up_ref[:, cols] = up.astype(BF16)
            act_ref[:, cols] = (gate * jax.nn.sigmoid(gate) * up).astype(BF16)

    wide = pl.BlockSpec((tm, f), lambda i: (i, 0))
    hidden = jax.ShapeDtypeStruct((s, f), BF16)
    return _host_call(
        body, (x, g, wt), side, grid=(s // tm,), name="ffn_up",
        in_specs=[
            pl.BlockSpec((tm, d), lambda i: (i, 0)),
            pl.BlockSpec((1, d), lambda i: (0, 0)),
            _block_spec((2 * f, d), lambda i: (0, 0), pl.Buffered(1)),
        ],
        out_specs=[pl.BlockSpec((tm, d), lambda i: (i, 0)), wide, wide, wide],
        out_shape=[jax.ShapeDtypeStruct((s, d), BF16), hidden, hidden, hidden],
        compiler_params=_params("arbitrary"),
    )


def _ffn_down(act, wd, x, side=None):
    s, f = act.shape
    d = wd.shape[-1]
    tm = _row_tile(s)

    def body(a_ref, w_ref, x_ref, o_ref):
        o_ref[...] = x_ref[...] + 0.5 * _dot(a_ref[...], w_ref[...])

    row = pl.BlockSpec((tm, d), lambda i: (i, 0))
    return _host_call(
        body, (act, wd, x), side, grid=(s // tm,), name="ffn_down",
        in_specs=[pl.BlockSpec((tm, f), lambda i: (i, 0)), _block_spec((f, d), lambda i: (0, 0), pl.Buffered(1)), row],
        out_specs=[row],
        out_shape=[jax.ShapeDtypeStruct((s, d), F32)],
        compiler_params=_params("arbitrary"),
    )


def _ffn_bwd_act(dx, wd, gate, up, side=None):
    s, d = dx.shape
    f = gate.shape[-1]
    tn = f // 2
    tm = _row_tile(s)

    def body(dx_ref, w_ref, gate_ref, up_ref, dgate_ref, dup_ref):
        dact = _dot_nt((0.5 * dx_ref[...]).astype(BF16), w_ref[...])
        gate = gate_ref[...].astype(F32)
        sig = jax.nn.sigmoid(gate)
        silu = gate * sig
        dgate_ref[...] = (dact * up_ref[...].astype(F32) * (sig + silu * (1.0 - sig))).astype(BF16)
        dup_ref[...] = (dact * silu).astype(BF16)

    col = pl.BlockSpec((tm, tn), lambda j, i: (i, j))
    hidden = jax.ShapeDtypeStruct((s, f), BF16)
    return _host_call(
        body, (dx, wd, gate, up), side, grid=(2, s // tm), name="ffn_bwd_act",
        in_specs=[pl.BlockSpec((tm, d), lambda j, i: (i, 0)), pl.BlockSpec((tn, d), lambda j, i: (j, 0)), col, col],
        out_specs=[col, col],
        out_shape=[hidden, hidden],
        compiler_params=_params("arbitrary", "arbitrary"),
    )


def _norm_bwd_matmul(name, operands, products, nk, x, g, dres, w_rows_contract=False, side=None):
    s, d = x.shape
    tm = _row_tile(s)
    n = len(operands)

    def body(*refs):
        x_ref, g_ref, dres_ref, dx_ref, dg_ref, acc_ref = refs[n:]
        i, k = pl.program_id(0), pl.program_id(1)

        @pl.when(jnp.logical_and(i == 0, k == 0))
        def _():
            dg_ref[...] = jnp.zeros_like(dg_ref)

        total = None
        for a_at, a_pick, w_at, w_pick in products:
            prod = (_dot if w_rows_contract else _dot_nt)(a_pick(refs[a_at]), w_pick(refs[w_at]))
            total = prod if total is None else total + prod

        if nk > 1:
            @pl.when(k == 0)
            def _():
                acc_ref[...] = total

            @pl.when(jnp.logical_and(k > 0, k < nk - 1))
            def _():
                acc_ref[...] += total

        @pl.when(k == nk - 1)
        def _():
            dy = total if nk == 1 else acc_ref[...] + total
            dx, dg = _rms_bwd(dy, x_ref[...], g_ref[...])
            dx_ref[...] = dres_ref[...] + dx
            dg_ref[...] += dg

    row = pl.BlockSpec((tm, d), lambda i, k: (i, 0))
    vec = pl.BlockSpec((1, d), lambda i, k: (0, 0))
    return _host_call(
        body, [op[0] for op in operands] + [x, g, dres], side, grid=(s // tm, nk), name=name,
        in_specs=[_block_spec(*op[1:]) for op in operands] + [row, vec, row],
        out_specs=[row, vec],
        out_shape=[jax.ShapeDtypeStruct((s, d), F32), jax.ShapeDtypeStruct((1, d), F32)],
        scratch_shapes=[pltpu.VMEM((tm, d), F32)],
        compiler_params=_params("arbitrary", "arbitrary"),
    )


def _ffn_dh(dgate, dup, wt, x, g, dres, side=None):
    s, d = x.shape
    f = dgate.shape[-1]
    tm = _row_tile(s)
    once = pl.Buffered(1)
    operands = [
        (dgate, (tm, f), lambda i, k: (i, 0)),
        (dup, (tm, f), lambda i, k: (i, 0)),
        (wt, (f, d), lambda i, k: (0, 0), once),
        (wt, (f, d), lambda i, k: (1, 0), once),
    ]
    products = [(0, _pick_all, 2, _pick_all), (1, _pick_all, 3, _pick_all)]
    return _norm_bwd_matmul("ffn_dh", operands, products, 1, x, g, dres, w_rows_contract=True, side=side)


def _square_dh(name, a, w, x, g, dres):
    s, d = x.shape
    tm = _row_tile(s)
    operands = [(a, (tm, d), lambda i, k: (i, 0)), (w, (d, d), lambda i, k: (0, 0))]
    return _norm_bwd_matmul(name, operands, [(0, _pick_all, 1, _pick_all)], 1, x, g, dres)[0]


def _qkv_dh(dqkv, wqkv, x, g, dres):
    s, d = x.shape
    tm = _row_tile(s)
    operands = [(dqkv, (tm, 3 * d), lambda i, k: (i, 0)), (wqkv, (d, 3 * d), lambda i, k: (0, 0), pl.Buffered(1))]
    return _norm_bwd_matmul("qkv_dh", operands, [(0, _pick_all, 1, _pick_all)], 1, x, g, dres)[0]


def _grad_matmul(name, a, a_block, a_index, a_pick, b, b_block, b_index, b_picks, out_shape, out_block, out_index, out_stores, nj, scale=1.0):
    s = a.shape[-2]
    nk = s // _grad_tile(s)
    n_prod = len(b_picks)

    def body(a_ref, b_ref, o_ref, *acc_refs):
        k = pl.program_id(1)
        av = a_pick(a_ref)
        if av.dtype != BF16:
            av = (scale * av).astype(BF16)
        for b_pick, store, acc_ref in zip(b_picks, out_stores, acc_refs):
            bv = b_pick(b_ref)
            if bv.dtype != BF16:
                bv = bv.astype(BF16)
            prod = _dot_tn(av, bv)
            if nk == 1:
                store(o_ref, prod.astype(BF16))
                continue

            @pl.when(k == 0)
            def _():
                acc_ref[...] = prod

            @pl.when(jnp.logical_and(k > 0, k < nk - 1))
            def _():
                acc_ref[...] += prod

            @pl.when(k == nk - 1)
            def _():
                store(o_ref, (acc_ref[...] + prod).astype(BF16))

    m = jax.eval_shape(a_pick, jax.ShapeDtypeStruct(a_block, a.dtype)).shape[-1]
    nn = jax.eval_shape(b_picks[0], jax.ShapeDtypeStruct(b_block, b.dtype)).shape[-1]
    acc_shape = (m, nn)
    return pl.pallas_call(
        body, grid=(nj, nk), name=name,
        in_specs=[pl.BlockSpec(a_block, a_index), pl.BlockSpec(b_block, b_index)],
        out_specs=pl.BlockSpec(out_block, out_index),
        out_shape=jax.ShapeDtypeStruct(out_shape, BF16),
        scratch_shapes=[pltpu.VMEM(acc_shape, F32) for _ in range(n_prod)],
        compiler_params=_params("arbitrary", "arbitrary"),
    )(a, b)


def _pick_all(r):
    return r[...]


def _store_all(r, v):
    r[...] = v


def _grad_ffn_down(act, dx):
    s, f = act.shape
    d = dx.shape[-1]
    tk = _grad_tile(s)
    tn = f // 2
    return _grad_matmul(
        "grad_ffn_down", act, (tk, tn), lambda j, k: (k, j), _pick_all,
        dx, (tk, d), lambda j, k: (k, 0), [lambda r: 0.5 * r[...]],
        (f, d), (tn, d), lambda j, k: (j, 0), [_store_all], 2)


def _grad_ffn_gu(h, dgate, dup, side=None):
    s, d = h.shape
    f = dgate.shape[-1]
    tn = f // 2
    tk = min(1024, s)
    nk = s // tk

    def body(h_ref, dgate_ref, dup_ref, o_ref, acc_ref):
        j, k = pl.program_id(0), pl.program_id(1)

        def accumulate(a_ref):
            prod = _dot_tn(a_ref[...], h_ref[...])
            if nk == 1:
                o_ref[...] = prod.astype(BF16)
                return

            @pl.when(k == 0)
            def _():
                acc_ref[...] = prod

            @pl.when(jnp.logical_and(k > 0, k < nk - 1))
            def _():
                acc_ref[...] += prod

            @pl.when(k == nk - 1)
            def _():
                o_ref[...] = (acc_ref[...] + prod).astype(BF16)

        @pl.when(j < 2)
        def _():
            accumulate(dgate_ref)

        @pl.when(j >= 2)
        def _():
            accumulate(dup_ref)

    (out,), parts = _host_call(
        body, (h, dgate, dup), side, grid=(4, nk), name="grad_ffn_gu",
        in_specs=[
            pl.BlockSpec((tk, d), lambda j, k: (k, 0)),
            pl.BlockSpec((tk, tn), lambda j, k: (jnp.where(j < 2, k, 0), jnp.minimum(j, 1))),
            pl.BlockSpec((tk, tn), lambda j, k: (jnp.where(j >= 2, k, 0), jnp.maximum(j - 2, 0))),
        ],
        out_specs=[pl.BlockSpec((tn, d), lambda j, k: (j, 0))],
        out_shape=[jax.ShapeDtypeStruct((2 * f, d), BF16)],
        scratch_shapes=[pltpu.VMEM((tn, d), F32)],
        compiler_params=_params("arbitrary", "arbitrary"),
    )
    return out, parts


def _grad_square(name, a, b):
    s, m = a.shape
    n = b.shape[-1]
    tk = _grad_tile(s)
    tn = min(n, D_MODEL)
    return _grad_matmul(
        name, a, (tk, m), lambda j, k: (k, 0), _pick_all,
        b, (tk, tn), lambda j, k: (k, j), [_pick_all],
        (m, n), (m, tn), lambda j, k: (0, j), [_store_all], n // tn)


def _column_shards(w):
    m, n = w.shape
    return w.reshape(m, N_DEV, n // N_DEV).transpose(1, 0, 2)


def _from_column_shards(w):
    nd, m, n = w.shape
    return w.transpose(1, 0, 2).reshape(m, nd * n)


def _norm_matmul(name, x, g, w, w_block, w_index, w_pick, n_total, tn, nj):
    s, d = x.shape
    tm = _row_tile(s)

    def body(x_ref, g_ref, w_ref, h_ref, y_ref):
        @pl.when(pl.program_id(1) == 0)
        def _():
            h_ref[...] = _rms(x_ref[...], g_ref[...]).astype(BF16)

        y_ref[...] = _dot(h_ref[...], w_pick(w_ref))

    return pl.pallas_call(
        body, grid=(s // tm, nj), name=name,
        in_specs=[
            pl.BlockSpec((tm, d), lambda i, j: (i, 0)),
            pl.BlockSpec((1, d), lambda i, j: (0, 0)),
            pl.BlockSpec(w_block, w_index),
        ],
        out_specs=[pl.BlockSpec((tm, d), lambda i, j: (i, 0)), pl.BlockSpec((tm, tn), lambda i, j: (i, j))],
        out_shape=[jax.ShapeDtypeStruct((s, d), BF16), jax.ShapeDtypeStruct((s, n_total), F32)],
        compiler_params=_params("arbitrary", "arbitrary"),
    )(x, g, w)


def _head_mean_matrix():
    r = lax.broadcasted_iota(jnp.int32, (128, 128), 0) // HEAD_DIM
    c = lax.broadcasted_iota(jnp.int32, (128, 128), 1) // HEAD_DIM
    return jnp.where(r == c, 1.0 / HEAD_DIM, 0.0).astype(F32)


def _qk_norm(qkv, qg, kg):
    s = qkv.shape[0]
    d = D_MODEL
    tm = _row_tile(s)

    def body(q_ref, k_ref, v_ref, qg_ref, kg_ref, qo_ref, ko_ref, vo_ref):
        mean_m = _head_mean_matrix()
        for u in range(N_UNITS):
            cols = slice(128 * u, 128 * (u + 1))
            for src, gain, dst, scale in ((q_ref, qg_ref, qo_ref, ATTN_SCALE), (k_ref, kg_ref, ko_ref, 1.0)):
                xs = src[:, cols]
                r = lax.rsqrt(_dot_3x(xs * xs, mean_m) + EPS)
                y = xs * r * gain[:, cols]
                dst[:, cols] = (y * scale).astype(BF16) if scale != 1.0 else y.astype(BF16)
        vo_ref[...] = v_ref[...].astype(BF16)

    blk = lambda c: pl.BlockSpec((tm, d), lambda i: (i, c))
    vec = pl.BlockSpec((1, d), lambda i: (0, 0))
    return pl.pallas_call(
        body, grid=(s // tm,), name="qk_norm",
        in_specs=[blk(0), blk(1), blk(2), vec, vec],
        out_specs=[blk(0)] * 3,
        out_shape=[jax.ShapeDtypeStruct((s, d), BF16)] * 3,
        compiler_params=_params("arbitrary"),
    )(qkv, qkv, qkv, qg, kg)


def _qk_norm_bwd(qkv, dq, dk, dv, qg, kg):
    s = qkv.shape[0]
    d = D_MODEL
    tm = _row_tile(s)
    nsteps = s // tm

    def body(q_ref, k_ref, dq_ref, dk_ref, dv_ref, qg_ref, kg_ref, o_ref, dqg_ref, dkg_ref, acc_ref):
        i = pl.program_id(0)

        @pl.when(i == 0)
        def _():
            acc_ref[...] = jnp.zeros_like(acc_ref)

        mean_m = _head_mean_matrix()
        for u in range(N_UNITS):
            cols = slice(128 * u, 128 * (u + 1))
            for n, (src, dsrc, gain) in enumerate(((q_ref, dq_ref, qg_ref), (k_ref, dk_ref, kg_ref))):
                xs = src[:, cols]
                dy = dsrc[:, cols]
                r = lax.rsqrt(_dot_3x(xs * xs, mean_m) + EPS)
                xh = xs * r
                acc_ref[n:n + 1, :] += jnp.sum(dy * xh, axis=0, keepdims=True)
                dxh = dy * gain[:, cols]
                dx = r * (dxh - xh * _dot_3x(dxh * xh, mean_m))
                o_ref[:, 128 * (N_UNITS * n + u):128 * (N_UNITS * n + u + 1)] = dx.astype(BF16)
        o_ref[:, 2 * d:3 * d] = dv_ref[...].astype(BF16)

        @pl.when(i == nsteps - 1)
        def _():
            r = lax.broadcasted_iota(jnp.int32, (128, 128), 0) % HEAD_DIM
            c = lax.broadcasted_iota(jnp.int32, (128, 128), 1) % HEAD_DIM
            fold = jnp.where(r == c, 1.0, 0.0).astype(F32)
            folded = _dot_f32(acc_ref[...], fold)
            dqg_ref[...] = jnp.broadcast_to(folded[0:1], (8, 128))
            dkg_ref[...] = jnp.broadcast_to(folded[1:2], (8, 128))

    blk = lambda c: pl.BlockSpec((tm, d), lambda i: (i, c))
    row = pl.BlockSpec((tm, d), lambda i: (i, 0))
    vec = pl.BlockSpec((1, d), lambda i: (0, 0))
    small = pl.BlockSpec((8, 128), lambda i: (0, 0))
    return pl.pallas_call(
        body, grid=(nsteps,), name="qk_norm_bwd",
        in_specs=[blk(0), blk(1), row, row, row, vec, vec],
        out_specs=[pl.BlockSpec((tm, 3 * d), lambda i: (i, 0)), small, small],
        out_shape=[jax.ShapeDtypeStruct((s, 3 * d), BF16), jax.ShapeDtypeStruct((8, 128), F32), jax.ShapeDtypeStruct((8, 128), F32)],
        scratch_shapes=[pltpu.VMEM((8, 128), F32)],
        compiler_params=_params("arbitrary"),
    )(qkv, qkv, dq, dk, dv, qg, kg)


def _split_dot(x, m):
    hi = x.astype(BF16)
    lo = (x - hi.astype(F32)).astype(BF16)
    return _dot(hi, m) + _dot(lo, m)


def _stack_heads(x):
    lane = lax.broadcasted_iota(jnp.int32, x.shape, 1)
    zero = jnp.zeros_like(x)
    return jnp.concatenate([jnp.where(lane < HEAD_DIM, x, zero), jnp.where(lane < HEAD_DIM, zero, x)], axis=0)


def _unstack_heads(x2, t):
    lane = lax.broadcasted_iota(jnp.int32, (t, 128), 1)
    return jnp.where(lane < HEAD_DIM, x2[:t], x2[t:])


def _attn_masks(t):
    r = lax.broadcasted_iota(jnp.int32, (t, t), 0)
    c = lax.broadcasted_iota(jnp.int32, (t, t), 1)
    row = lax.broadcasted_iota(jnp.int32, (2 * t, t), 0)
    col = lax.broadcasted_iota(jnp.int32, (2 * t, t), 1)
    causal = col < jnp.where(row >= t, row - t, row)
    return (r > c).astype(BF16), (r >= c).astype(BF16), causal


def _attn_sweep_cond(st):
    return jnp.logical_and(st[0] >= 0, st[1] > LOG_ZERO)


def _attn_scores(q2, kblk, after, causal):
    z = _dot_nt(q2, kblk)
    sp = jnp.maximum(z, 0.0) + jnp.log(1.0 + jnp.exp(-jnp.abs(z)))
    log_stay = -sp
    if causal is not None:
        log_stay = jnp.where(causal, log_stay, 0.0)
    return log_stay, z - sp, _split_dot(log_stay, after)


def _attention(q, k, v, side=None):
    s, d = q.shape
    t = min(ATTN_BLOCK, s)

    def body(q_ref, k_ref, v_ref, o_ref):
        i = pl.program_id(1)
        after, _, causal = _attn_masks(t)
        q2 = _stack_heads(q_ref[...])

        def step(kb, carry, acc, mask):
            start = pl.multiple_of(kb * t, t)
            kblk = k_ref[pl.ds(start, t), :]
            vblk = v_ref[pl.ds(start, t), :]
            log_stay, log_beta, later = _attn_scores(q2, kblk, after, mask)
            w = jnp.exp(log_beta + later + carry)
            if mask is not None:
                w = jnp.where(mask, w, 0.0)
            return carry + jnp.sum(log_stay, axis=1, keepdims=True), acc + _split_dot(w, vblk)

        carry, acc = step(i, jnp.zeros((2 * t, 1), F32), jnp.zeros((2 * t, 128), F32), causal)
        for back in range(1, ATTN_AHEAD + 1):
            carry, acc = step(jnp.maximum(i - back, 0), carry, acc, jnp.broadcast_to(i >= back, causal.shape))

        def loop(st):
            c, a = step(st[0], st[2], st[3], None)
            return st[0] - 1, jnp.max(c), c, a

        acc = lax.while_loop(_attn_sweep_cond, loop, (i - 1 - ATTN_AHEAD, jnp.max(carry), carry, acc))[3]
        o_ref[...] = _unstack_heads(acc, t)

    (o,), parts = _host_call(
        body, (q, k, v), side, grid=(N_UNITS, s // t), name="attention",
        in_specs=[
            pl.BlockSpec((t, 128), lambda h, i: (i, h)),
            pl.BlockSpec((s, 128), lambda h, i: (0, h)),
            pl.BlockSpec((s, 128), lambda h, i: (0, h)),
        ],
        out_specs=[pl.BlockSpec((t, 128), lambda h, i: (i, h))],
        out_shape=[jax.ShapeDtypeStruct((s, d), F32)],
        compiler_params=_params("arbitrary", "arbitrary"),
    )
    return o, parts


def _attention_bwd(q, k, v, do, o, side=None):
    s, d = q.shape
    t = min(ATTN_BLOCK, s)

    def body(q_ref, k_ref, v_ref, do_ref, o_ref, dq_ref, dk_ref, dv_ref):
        i = pl.program_id(1)

        @pl.when(i == 0)
        def _():
            dk_ref[...] = jnp.zeros_like(dk_ref)
            dv_ref[...] = jnp.zeros_like(dv_ref)

        after, from_here, causal = _attn_masks(t)
        q2 = _stack_heads(q_ref[...])
        do2 = _stack_heads(do_ref[...])

        def weights(kb, carry, mask):
            start = pl.multiple_of(kb * t, t)
            kblk = k_ref[pl.ds(start, t), :]
            vblk = v_ref[pl.ds(start, t), :]
            log_stay, log_beta, later = _attn_scores(q2, kblk, after, mask)
            w = jnp.exp(log_beta + later + carry)
            if mask is not None:
                w = jnp.where(mask, w, 0.0)
            g = w * _dot_nt(do2, vblk)
            return start, kblk, log_stay, log_beta, w, g

        lane = lax.broadcasted_iota(jnp.int32, (t, 128), 1)
        prod = do_ref[...].astype(F32) * o_ref[...]
        total = jnp.concatenate([
            jnp.sum(jnp.where(lane < HEAD_DIM, prod, 0.0), axis=1, keepdims=True),
            jnp.sum(jnp.where(lane < HEAD_DIM, 0.0, prod), axis=1, keepdims=True)], axis=0)
        zero = jnp.zeros((2 * t, 1), F32)

        def grad_step(kb, carry, seen, dq, mask):
            start, kblk, log_stay, log_beta, w, g = weights(kb, carry, mask)
            before = total - (_split_dot(g, from_here) + seen)
            beta = jnp.exp(log_beta)
            da = g * (1.0 - beta) - before * beta
            if mask is not None:
                da = jnp.where(mask, da, 0.0)
            dab = da.astype(BF16)
            dk_ref[pl.ds(start, t), :] += _dot_tn(dab, q2)
            dv_ref[pl.ds(start, t), :] += _dot_tn(w.astype(BF16), do2)
            return (carry + jnp.sum(log_stay, axis=1, keepdims=True), seen + jnp.sum(g, axis=1, keepdims=True),
                    dq + _dot(dab, kblk))

        carry, seen, dq = grad_step(i, zero, zero, jnp.zeros((2 * t, 128), F32), causal)
        for back in range(1, ATTN_AHEAD + 1):
            carry, seen, dq = grad_step(jnp.maximum(i - back, 0), carry, seen, dq, jnp.broadcast_to(i >= back, causal.shape))

        def grad_loop(st):
            c, sn, a = grad_step(st[0], st[2], st[3], st[4], None)
            return st[0] - 1, jnp.max(c), c, sn, a

        dq = lax.while_loop(_attn_sweep_cond, grad_loop, (i - 1 - ATTN_AHEAD, jnp.max(carry), carry, seen, dq))[4]
        dq_ref[...] = ATTN_SCALE * _unstack_heads(dq, t)

    blk = pl.BlockSpec((t, 128), lambda h, i: (i, h))
    full = pl.BlockSpec((s, 128), lambda h, i: (0, h))
    return _host_call(
        body, (q, k, v, do, o), side, grid=(N_UNITS, s // t), name="attention_bwd",
        in_specs=[blk, full, full, blk, blk],
        out_specs=[blk, full, full],
        out_shape=[jax.ShapeDtypeStruct((s, d), F32)] * 3,
        compiler_params=_params("arbitrary", "arbitrary"),
    )


def _matmul_res(name, a, w, x, alpha):
    s, kd = a.shape
    d = w.shape[-1]
    tm = _row_tile(s)

    def body(a_ref, w_ref, x_ref, o_ref):
        o_ref[...] = x_ref[...] + alpha * _dot(a_ref[...].astype(BF16), w_ref[...])

    return pl.pallas_call(
        body, grid=(s // tm,), name=name,
        in_specs=[pl.BlockSpec((tm, kd), lambda i: (i, 0)), pl.BlockSpec((kd, d), lambda i: (0, 0)), pl.BlockSpec((tm, d), lambda i: (i, 0))],
        out_specs=pl.BlockSpec((tm, d), lambda i: (i, 0)),
        out_shape=jax.ShapeDtypeStruct((s, d), F32),
        compiler_params=_params("arbitrary"),
    )(a, w, x)


def _matmul_nt(name, a, w):
    s, n = a.shape
    kd = w.shape[0]
    tm = _row_tile(s)

    def body(a_ref, w_ref, o_ref):
        o_ref[...] = _dot_nt(a_ref[...].astype(BF16), w_ref[...]).astype(BF16)

    return pl.pallas_call(
        body, grid=(s // tm,), name=name,
        in_specs=[pl.BlockSpec((tm, n), lambda i: (i, 0)), pl.BlockSpec((kd, n), lambda i: (0, 0))],
        out_specs=pl.BlockSpec((tm, kd), lambda i: (i, 0)),
        out_shape=jax.ShapeDtypeStruct((s, kd), BF16),
        compiler_params=_params("arbitrary"),
    )(a, w)


def _pool_matrix(rows0, cols0, nr, nc, window, transpose):
    r = rows0 + lax.broadcasted_iota(jnp.int32, (nr, nc), 0)
    c = cols0 + lax.broadcasted_iota(jnp.int32, (nr, nc), 1)
    tt, ss = (c, r) if transpose else (r, c)
    inside = jnp.logical_and(tt - ss >= 0, tt - ss < window)
    cnt = jnp.minimum(tt + 1, window).astype(F32)
    return jnp.where(inside, 1.0 / cnt, 0.0) - jnp.where(tt == ss, 1.0, 0.0)


def _pool_tile(s):
    return min(256, s)


def _pool_fwd(u, wgrp, scale, x, side=None):
    s, d = u.shape
    tm = _pool_tile(s)
    halo = min(POOL_HALO, tm)
    ratio = tm // halo

    def body(u_ref, prev_ref, w_ref, sc_ref, x_ref, o_ref, p_ref):
        i = pl.program_id(0)
        t0 = i * tm
        for gi, window in enumerate(POOL_WINDOWS):
            cols = slice(POOL_GROUP * gi, POOL_GROUP * (gi + 1))
            pooled = _dot_3x(_pool_matrix(t0, t0, tm, tm, window, False), u_ref[:, cols])
            prev = jnp.where(i > 0, prev_ref[:, cols], 0.0)
            pooled += _dot_3x(_pool_matrix(t0, t0 - halo, tm, halo, window, False), prev)
            pb = pooled.astype(BF16)
            p_ref[:, cols] = pb
            o_ref[:, cols] = x_ref[:, cols] + _dot(pb, w_ref[gi]) * sc_ref[:, cols]

    row = pl.BlockSpec((tm, d), lambda i: (i, 0))
    return _host_call(
        body, (u, u, wgrp, scale, x), side, grid=(s // tm,), name="pool_fwd",
        in_specs=[
            row,
            pl.BlockSpec((halo, d), lambda i: (jnp.maximum(i * ratio - 1, 0), 0)),
            pl.BlockSpec((4, POOL_GROUP, POOL_GROUP), lambda i: (0, 0, 0)),
            pl.BlockSpec((1, d), lambda i: (0, 0)),
            row,
        ],
        out_specs=[row, row],
        out_shape=[jax.ShapeDtypeStruct((s, d), F32), jax.ShapeDtypeStruct((s, d), BF16)],
        compiler_params=_params("arbitrary"),
    )


def _pool_bwd_group(dx, pooled, wgrp, scale):
    s, d = dx.shape
    tm = _pool_tile(s)
    nsteps = s // tm

    def body(dx_ref, p_ref, w_ref, sc_ref, dp_ref, dw_ref, dsc_ref, acc_ref):
        i = pl.program_id(0)

        @pl.when(i == 0)
        def _():
            acc_ref[...] = jnp.zeros_like(acc_ref)
            dsc_ref[...] = jnp.zeros_like(dsc_ref)

        for gi in range(len(POOL_WINDOWS)):
            cols = slice(POOL_GROUP * gi, POOL_GROUP * (gi + 1))
            pb = p_ref[:, cols]
            dxg = dx_ref[:, cols]
            y = _dot(pb, w_ref[gi])
            dsc_ref[:, cols] += jnp.sum(dxg * y, axis=0, keepdims=True)
            dyb = (dxg * sc_ref[:, cols]).astype(BF16)
            dp_ref[:, cols] = _dot_nt(dyb, w_ref[gi])
            acc_ref[gi] += _dot_tn(pb, dyb)

        @pl.when(i == nsteps - 1)
        def _():
            dw_ref[...] = acc_ref[...].astype(BF16)

    row = pl.BlockSpec((tm, d), lambda i: (i, 0))
    grp = pl.BlockSpec((4, POOL_GROUP, POOL_GROUP), lambda i: (0, 0, 0))
    vec = pl.BlockSpec((1, d), lambda i: (0, 0))
    return pl.pallas_call(
        body, grid=(nsteps,), name="pool_bwd_group",
        in_specs=[row, row, grp, vec],
        out_specs=[row, grp, vec],
        out_shape=[jax.ShapeDtypeStruct((s, d), F32), jax.ShapeDtypeStruct((4, POOL_GROUP, POOL_GROUP), BF16), jax.ShapeDtypeStruct((1, d), F32)],
        scratch_shapes=[pltpu.VMEM((4, POOL_GROUP, POOL_GROUP), F32)],
        compiler_params=_params("arbitrary"),
    )(dx, pooled, wgrp, scale)


def _pool_bwd_window(dp):
    s, d = dp.shape
    tm = _pool_tile(s)
    halo = min(POOL_HALO, tm)
    ratio = tm // halo
    nsteps = s // tm

    def body(dp_ref, next_ref, o_ref):
        i = pl.program_id(0)
        t0 = i * tm
        for gi, window in enumerate(POOL_WINDOWS):
            cols = slice(POOL_GROUP * gi, POOL_GROUP * (gi + 1))
            du = _dot_3x(_pool_matrix(t0, t0, tm, tm, window, True), dp_ref[:, cols])
            nxt = jnp.where(i < nsteps - 1, next_ref[:, cols], 0.0)
            du += _dot_3x(_pool_matrix(t0, t0 + tm, tm, halo, window, True), nxt)
            o_ref[:, cols] = du.astype(BF16)

    row = pl.BlockSpec((tm, d), lambda i: (i, 0))
    return pl.pallas_call(
        body, grid=(nsteps,), name="pool_bwd_window",
        in_specs=[row, pl.BlockSpec((halo, d), lambda i: (jnp.minimum((i + 1) * ratio, s // halo - 1), 0))],
        out_specs=row,
        out_shape=jax.ShapeDtypeStruct((s, d), BF16),
        compiler_params=_params("arbitrary"),
    )(dp, dp)


def _ple_fwd(x, g, wgate, p, wproj, side=None):
    s, d = x.shape
    pd = p.shape[-1]
    tm = _row_tile(s)

    def body(x_ref, g_ref, wg_ref, p_ref, wp_ref, o_ref, h_ref, sig_ref, proj_ref):
        x = x_ref[...]
        h = _rms(x, g_ref[...]).astype(BF16)
        sig = jax.nn.sigmoid(_dot(h, wg_ref[...]))
        proj = _dot(p_ref[...].astype(BF16), wp_ref[...])
        o_ref[...] = x + sig * proj
        h_ref[...] = h
        sig_ref[...] = sig.astype(BF16)
        proj_ref[...] = proj.astype(BF16)

    row = pl.BlockSpec((tm, d), lambda i: (i, 0))
    return _host_call(
        body, (x, g, wgate, p, wproj), side, grid=(s // tm,), name="ple_fwd",
        in_specs=[
            row,
            pl.BlockSpec((1, d), lambda i: (0, 0)),
            pl.BlockSpec((d, d), lambda i: (0, 0)),
            pl.BlockSpec((tm, pd), lambda i: (i, 0)),
            pl.BlockSpec((pd, d), lambda i: (0, 0)),
        ],
        out_specs=[row] * 4,
        out_shape=[jax.ShapeDtypeStruct((s, d), F32)] + [jax.ShapeDtypeStruct((s, d), BF16)] * 3,
        compiler_params=_params("arbitrary"),
    )


def _ple_bwd_gate(dx, sig, proj):
    s, d = dx.shape
    tm = _row_tile(s)

    def body(dx_ref, sig_ref, proj_ref, dg_ref, dp_ref):
        dx = dx_ref[...]
        sig = sig_ref[...].astype(F32)
        dg_ref[...] = (dx * proj_ref[...].astype(F32) * (sig * (1.0 - sig))).astype(BF16)
        dp_ref[...] = (dx * sig).astype(BF16)

    row = pl.BlockSpec((tm, d), lambda i: (i, 0))
    return pl.pallas_call(
        body, grid=(s // tm,), name="ple_bwd_gate",
        in_specs=[row, row, row], out_specs=[row, row],
        out_shape=[jax.ShapeDtypeStruct((s, d), BF16)] * 2,
        compiler_params=_params("arbitrary"),
    )(dx, sig, proj)


def _loss_head(y, target):
    s, d = y.shape
    tm = _row_tile(s)
    nsteps = s // tm

    def body(y_ref, t_ref, dy_ref, loss_ref, acc_ref):
        i = pl.program_id(0)

        @pl.when(i == 0)
        def _():
            acc_ref[...] = jnp.zeros_like(acc_ref)

        err = y_ref[...] - t_ref[...]
        dy_ref[...] = err * (1.0 / d)
        acc_ref[...] += jnp.sum(jnp.mean(err * err, axis=-1, keepdims=True), axis=0, keepdims=True)

        @pl.when(i == nsteps - 1)
        def _():
            loss_ref[...] = 0.5 * acc_ref[...]

    row = pl.BlockSpec((tm, d), lambda i: (i, 0))
    return pl.pallas_call(
        body, grid=(nsteps,), name="loss_head",
        in_specs=[row, row], out_specs=[row, pl.BlockSpec((8, 128), lambda i: (0, 0))],
        out_shape=[jax.ShapeDtypeStruct((s, d), F32), jax.ShapeDtypeStruct((8, 128), F32)],
        scratch_shapes=[pltpu.VMEM((8, 128), F32)],
        compiler_params=_params("arbitrary"),
    )(y, target)


def _adamw_math(w, g, m, v):
    m = ADAM_B1 * m + (1.0 - ADAM_B1) * g
    v = ADAM_B2 * v + (1.0 - ADAM_B2) * (g * g)
    m_hat = m / (1.0 - ADAM_B1 ** ADAM_STEP)
    v_hat = v / (1.0 - ADAM_B2 ** ADAM_STEP)
    delta = -ADAM_LR * (m_hat / (jnp.sqrt(v_hat) + ADAM_EPS) + ADAM_WD * w)
    return delta, m, v


def _adamw_layer(parts, w, m, v, layer, outs):
    nl, r, c = w.shape
    tr = max(t for t in range(16, r + 1, 16) if r % t == 0 and t * c <= ADAMW_BLOCK_ELEMS)

    def body(p_ref, w_ref, m_ref, v_ref, *rest):
        g_ref, d_ref, nm_ref, nv_ref = rest[-4:]
        g = p_ref[0].astype(F32)
        for dev in range(1, N_DEV):
            g = g + p_ref[dev].astype(F32)
        delta, nm, nv = _adamw_math(w_ref[0], g, m_ref[0], v_ref[0])
        g_ref[0] = g
        d_ref[0] = delta
        nm_ref[0] = nm
        nv_ref[0] = nv

    slab = pl.BlockSpec((1, tr, c), lambda i: (layer, i, 0))
    any_spec = pl.BlockSpec(memory_space=pl.ANY)
    shape = jax.ShapeDtypeStruct(w.shape, F32)
    carried = [] if outs is None else list(outs)
    return pl.pallas_call(
        body, grid=(r // tr,), name="adamw",
        in_specs=[pl.BlockSpec((N_DEV, tr, c), lambda i: (0, i, 0)), slab, slab, slab] + [any_spec] * len(carried),
        out_specs=[slab] * 4,
        out_shape=[shape] * 4,
        input_output_aliases={4 + n: n for n in range(len(carried))},
        compiler_params=_params("arbitrary"),
    )(parts, w, m, v, *carried)


def _adamw_small(parts, w, m, v):
    r, c = w.shape

    def body(p_ref, w_ref, m_ref, v_ref, g_ref, d_ref, nm_ref, nv_ref):
        g = p_ref[0:r, :]
        for dev in range(1, N_DEV):
            g = g + p_ref[dev * r:(dev + 1) * r, :]
        delta, nm, nv = _adamw_math(w_ref[...], g, m_ref[...], v_ref[...])
        g_ref[...] = g
        d_ref[...] = delta
        nm_ref[...] = nm
        nv_ref[...] = nv

    shape = jax.ShapeDtypeStruct((r, c), F32)
    return pl.pallas_call(body, name="adamw_small", out_shape=[shape] * 4)(parts, w, m, v)


def _my_place():
    return lax.axis_index("x"), lax.axis_index("y"), lax.axis_index("c")


def _peer(k):
    x, y, c = _my_place()
    return (x ^ (k >> 2), y ^ ((k >> 1) & 1), c ^ (k & 1))


def _block_of(place):
    x, y, c = place
    return 4 * x + 2 * y + c


def _gather_small(block):
    m_per, n = block.shape

    def body(x_ref, out_ref, send_sems, recv_sems, local_sem):
        me = _block_of(_my_place())

        def rows(b):
            return out_ref.at[pl.ds(b * m_per, m_per), :]

        mine = pltpu.make_async_copy(x_ref, rows(me), local_sem)
        mine.start()
        sends = []
        for k in range(1, N_DEV):
            cp = pltpu.make_async_remote_copy(
                src_ref=x_ref, dst_ref=rows(me), send_sem=send_sems.at[k - 1], recv_sem=recv_sems.at[k - 1],
                device_id=_peer(k), device_id_type=MESH)
            cp.start()
            sends.append(cp)
        for k in range(1, N_DEV):
            src = rows(_block_of(_peer(k)))
            pltpu.make_async_remote_copy(
                src_ref=src, dst_ref=src, send_sem=send_sems.at[k - 1], recv_sem=recv_sems.at[k - 1],
                device_id=_peer(k), device_id_type=MESH).wait_recv()
        for cp in sends:
            cp.wait_send()
        mine.wait()

    return pl.pallas_call(
        body, name="gather_small",
        out_shape=jax.ShapeDtypeStruct((N_DEV * m_per, n), block.dtype),
        in_specs=[pl.BlockSpec(memory_space=pltpu.VMEM)],
        out_specs=pl.BlockSpec(memory_space=pltpu.VMEM),
        scratch_shapes=[pltpu.SemaphoreType.DMA((N_DEV - 1,)), pltpu.SemaphoreType.DMA((N_DEV - 1,)), pltpu.SemaphoreType.DMA],
    )(block)


def _gather_weights(name, stacks, layers):
    n_t = len(stacks)

    def body(*refs):
        srcs, outs = refs[:n_t], refs[n_t:2 * n_t]
        send_sems, recv_sems, local_sems = refs[2 * n_t:]
        x, y, c = _my_place()
        me, sibling = (x, y, c), (x, y, 1 - c)
        chips = [(1 - x, y), (x, 1 - y), (1 - x, 1 - y)]

        def copy(t, k, block, to, src=None):
            slot = outs[t].at[_block_of(block)]
            return pltpu.make_async_remote_copy(
                src_ref=slot if src is None else src, dst_ref=slot,
                send_sem=send_sems.at[t, k], recv_sem=recv_sems.at[t, k], device_id=to, device_id_type=MESH)

        local, sends = [], []
        for t in range(n_t):
            src = srcs[t].at[layers[t]]
            own = pltpu.make_async_copy(src, outs[t].at[_block_of(me)], local_sems.at[t])
            own.start()
            local.append(own)
            sends.append(copy(t, 0, me, sibling, src=src))
            sends += [copy(t, 1 + j, me, (*chip, c), src=src) for j, chip in enumerate(chips)]
        for cp in sends:
            cp.start()
        for t in range(n_t):
            for j, chip in enumerate(chips):
                copy(t, 1 + j, (*chip, c), me).wait_recv()
                passed = copy(t, 4 + j, (*chip, c), sibling)
                passed.start()
                sends.append(passed)
        for t in range(n_t):
            copy(t, 0, sibling, me).wait_recv()
            for j, chip in enumerate(chips):
                copy(t, 4 + j, (*chip, 1 - c), me).wait_recv()
        for cp in sends:
            cp.wait_send()
        for cp in local:
            cp.wait()

    any_spec = pl.BlockSpec(memory_space=pl.ANY)
    return pl.pallas_call(
        body, name=name,
        out_shape=[jax.ShapeDtypeStruct((N_DEV,) + st.shape[1:], st.dtype) for st in stacks],
        in_specs=[any_spec] * n_t, out_specs=[any_spec] * n_t,
        scratch_shapes=[pltpu.SemaphoreType.DMA((n_t, N_DEV - 1)), pltpu.SemaphoreType.DMA((n_t, N_DEV - 1)), pltpu.SemaphoreType.DMA((n_t,))],
    )(*stacks)


class _ScatterSide:
    def __init__(self, grads):
        self.operands = list(grads)
        self.n = len(self.operands)

    def out_shape(self):
        return [jax.ShapeDtypeStruct(g.shape, g.dtype) for g in self.operands]

    def scratch(self):
        return [pltpu.SemaphoreType.DMA((self.n, N_DEV - 1)), pltpu.SemaphoreType.DMA((self.n, N_DEV - 1)), pltpu.SemaphoreType.DMA((self.n,))]

    def _copies(self, srcs, outs, send_sems, recv_sems, local_sems):
        me = _block_of(_my_place())
        local, sends, arrivals = [], [], []
        for t in range(self.n):
            local.append(pltpu.make_async_copy(srcs[t].at[me], outs[t].at[me], local_sems.at[t]))
            for k in range(1, N_DEV):
                sems = dict(send_sem=send_sems.at[t, k - 1], recv_sem=recv_sems.at[t, k - 1], device_id=_peer(k), device_id_type=MESH)
                sends.append(pltpu.make_async_remote_copy(src_ref=srcs[t].at[_block_of(_peer(k))], dst_ref=outs[t].at[me], **sems))
                slot = outs[t].at[_block_of(_peer(k))]
                arrivals.append(pltpu.make_async_remote_copy(src_ref=slot, dst_ref=slot, **sems))
        return local, sends, arrivals

    def start(self, *refs):
        local, sends, _ = self._copies(*refs)
        for cp in local + sends:
            cp.start()

    def finish(self, *refs):
        local, sends, arrivals = self._copies(*refs)
        for cp in arrivals:
            cp.wait_recv()
        for cp in sends:
            cp.wait_send()
        for cp in local:
            cp.wait()


class _GatherSide(_ScatterSide):
    def __init__(self, stacks, layers):
        super().__init__(stacks)
        self.layers = list(layers)

    def out_shape(self):
        return [jax.ShapeDtypeStruct((N_DEV,) + st.shape[1:], st.dtype) for st in self.operands]

    def _copies(self, srcs, outs, send_sems, recv_sems, local_sems):
        me = _block_of(_my_place())
        local, sends, arrivals = [], [], []
        for t in range(self.n):
            src = srcs[t].at[self.layers[t]]
            local.append(pltpu.make_async_copy(src, outs[t].at[me], local_sems.at[t]))
            for k in range(1, N_DEV):
                sems = dict(send_sem=send_sems.at[t, k - 1], recv_sem=recv_sems.at[t, k - 1], device_id=_peer(k), device_id_type=MESH)
                sends.append(pltpu.make_async_remote_copy(src_ref=src, dst_ref=outs[t].at[me], **sems))
                slot = outs[t].at[_block_of(_peer(k))]
                arrivals.append(pltpu.make_async_remote_copy(src_ref=slot, dst_ref=slot, **sems))
        return local, sends, arrivals


def _scatter_grads(name, grads):
    side = _ScatterSide(grads)
    n_t = side.n

    def body(*refs):
        parts = (refs[:n_t], refs[n_t:2 * n_t]) + tuple(refs[2 * n_t:])
        side.start(*parts)
        side.finish(*parts)

    any_spec = pl.BlockSpec(memory_space=pl.ANY)
    return pl.pallas_call(
        body, name=name, out_shape=side.out_shape(), in_specs=[any_spec] * n_t, out_specs=[any_spec] * n_t,
        scratch_shapes=side.scratch(),
    )(*grads)


def _host_call(body, args, side, *, grid, in_specs, out_specs, out_shape, scratch_shapes=(), **kw):
    if side is None:
        return pl.pallas_call(body, grid=grid, in_specs=in_specs, out_specs=out_specs, out_shape=out_shape,
                              scratch_shapes=list(scratch_shapes), **kw)(*args), None
    n_in, n_out, n_scr, n_side = len(in_specs), len(out_specs), len(scratch_shapes), side.n

    def hosted(*refs):
        cuts = [n_in, n_side, n_out, n_side, n_scr, 3]
        groups, at = [], 0
        for c in cuts:
            groups.append(refs[at:at + c])
            at += c
        ins, side_in, outs, side_out, scr, sems = groups
        first, last = None, None
        for axis, size in enumerate(grid):
            at_start, at_end = pl.program_id(axis) == 0, pl.program_id(axis) == size - 1
            first = at_start if first is None else jnp.logical_and(first, at_start)
            last = at_end if last is None else jnp.logical_and(last, at_end)

        @pl.when(first)
        def _():
            side.start(side_in, side_out, *sems)

        body(*ins, *outs, *scr)

        @pl.when(last)
        def _():
            side.finish(side_in, side_out, *sems)

    any_spec = pl.BlockSpec(memory_space=pl.ANY)
    res = pl.pallas_call(
        hosted, grid=grid, in_specs=list(in_specs) + [any_spec] * n_side, out_specs=list(out_specs) + [any_spec] * n_side,
        out_shape=list(out_shape) + side.out_shape(), scratch_shapes=list(scratch_shapes) + side.scratch(), **kw,
    )(*args, *side.operands)
    return res[:n_out], res[n_out:]


def _ffn_forward(x, g, wt, wd, carried):
    sides = {k: _GatherSide(*v) if v[0] else None for k, v in carried.items()}
    parts = {}
    (h, gate, up, act), parts["up"] = _ffn_up(x, g, wt, sides.get("up"))
    (x_new,), parts["down"] = _ffn_down(act, wd, x, sides.get("down"))
    return x_new, (x, g, h, gate, up, act), parts


def _ffn_backward(dx, saved, wt, wd, carried):
    x, g, h, gate, up, act = saved
    sides = {k: _ScatterSide(v) if v else None for k, v in carried.items()}
    parts = {}
    (dgate, dup), parts["act"] = _ffn_bwd_act(dx, wd, gate, up, sides.get("act"))
    d_wd = _grad_ffn_down(act, dx)
    (dx_in, dg), parts["dh"] = _ffn_dh(dgate, dup, wt, x, g, dx, sides.get("dh"))
    d_wt, parts["gu"] = _grad_ffn_gu(h, dgate, dup, sides.get("gu"))
    return dx_in, dg, d_wt.reshape(N_DEV, -1, D_MODEL), d_wd.reshape(N_DEV, -1, D_MODEL), parts


def kernel(x, p, norm_ffn1, w_ffn1_gu, w_ffn1_down, norm_mix, w_qkv, q_norm, k_norm, w_o, w_pool_in, w_pool_grp, pool_scale, norm_ffn2, w_ffn2_gu, w_ffn2_down, norm_ple, w_ple_gate, w_ple_proj, loss_target, m_norm_ffn1, m_w_ffn1_gu, m_w_ffn1_down, m_norm_mix, m_w_qkv, m_q_norm, m_k_norm, m_w_o, m_w_pool_in, m_w_pool_grp, m_pool_scale, m_norm_ffn2, m_w_ffn2_gu, m_w_ffn2_down, m_norm_ple, m_w_ple_gate, m_w_ple_proj, v_norm_ffn1, v_w_ffn1_gu, v_w_ffn1_down, v_norm_mix, v_w_qkv, v_q_norm, v_k_norm, v_w_o, v_w_pool_in, v_w_pool_grp, v_pool_scale, v_norm_ffn2, v_w_ffn2_gu, v_w_ffn2_down, v_norm_ple, v_w_ple_gate, v_w_ple_proj):
    d = D_MODEL
    xs = x[0]
    target = loss_target[0]
    me = _block_of(_my_place())

    def tr(w):
        return jnp.swapaxes(w, 1, 2)

    big = dict(w_ffn1_gu=tr(w_ffn1_gu), w_ffn1_down=w_ffn1_down, w_qkv=w_qkv, w_o=w_o, w_pool_in=w_pool_in,
               w_pool_grp=w_pool_grp.reshape(2, 4 * 32, POOL_GROUP), w_ffn2_gu=tr(w_ffn2_gu), w_ffn2_down=w_ffn2_down,
               w_ple_gate=w_ple_gate, w_ple_proj=w_ple_proj)
    moments = dict(
        w_ffn1_gu=(tr(m_w_ffn1_gu), tr(v_w_ffn1_gu)), w_ffn1_down=(m_w_ffn1_down, v_w_ffn1_down), w_qkv=(m_w_qkv, v_w_qkv),
        w_o=(m_w_o, v_w_o), w_pool_in=(m_w_pool_in, v_w_pool_in),
        w_pool_grp=(m_w_pool_grp.reshape(2, 4 * 32, POOL_GROUP), v_w_pool_grp.reshape(2, 4 * 32, POOL_GROUP)),
        w_ffn2_gu=(tr(m_w_ffn2_gu), tr(v_w_ffn2_gu)), w_ffn2_down=(m_w_ffn2_down, v_w_ffn2_down),
        w_ple_gate=(m_w_ple_gate, v_w_ple_gate), w_ple_proj=(m_w_ple_proj, v_w_ple_proj))
    half = {name: _cast_bf16(w) for name, w in big.items()}

    def layer_names(i):
        mixer = ["w_qkv", "w_o"] if i % 2 == 0 else ["w_pool_in", "w_pool_grp"]
        return ["w_ffn1_gu", "w_ffn1_down"] + mixer + ["w_ffn2_gu", "w_ffn2_down", "w_ple_gate", "w_ple_proj"]

    def layer_index(name, i):
        return i // 2 if name in ("w_qkv", "w_o", "w_pool_in", "w_pool_grp") else i

    scale_all = _gather_small(jnp.pad(pool_scale, ((0, 6), (0, 0))))
    scale_full = scale_all.reshape(N_DEV, 8, 128)[:, :2].transpose(1, 0, 2).reshape(2, d)

    saved = []
    weights = []
    cur = xs

    def fwd_carriers(i):
        if i % 2 == 0:
            return {
                ("ffn1", "up"): ["w_ffn1_gu"], ("ffn2", "up"): ["w_ffn2_gu"], ("ffn1", "down"): [], ("ffn2", "down"): [],
                ("mix", "attn"): ["w_ffn1_down", "w_ffn2_down"],
                ("ple", "ple"): ["w_ple_gate", "w_ple_proj", "w_pool_in", "w_pool_grp"],
            }
        return {
            ("ffn1", "up"): ["w_ffn1_gu"], ("ffn2", "up"): ["w_ffn2_gu"],
            ("ffn1", "down"): ["w_ffn1_down"], ("ffn2", "down"): ["w_ffn2_down"],
            ("mix", "pool"): ["w_qkv", "w_o"],
            ("ple", "ple"): ["w_ple_gate", "w_ple_proj"],
        }

    def next_weights(i, stage):
        nxt = layer_names(i + 1) if i + 1 < DEPTH else []
        out = {}
        for (st, host), names in fwd_carriers(i).items():
            if st == stage:
                take = [n for n in names if n in nxt]
                out[host] = ([half[n] for n in take], [layer_index(n, i + 1) for n in take])
        return out

    def arrived_weights(i, stage, parts, into):
        nxt = layer_names(i + 1) if i + 1 < DEPTH else []
        for (st, host), names in fwd_carriers(i).items():
            if st == stage and parts.get(host) is not None:
                into.update(zip([n for n in names if n in nxt], parts[host]))

    names = layer_names(0)
    coming = dict(zip(names, _gather_weights("gather_weights_0", [half[n] for n in names], [layer_index(n, 0) for n in names])))
    for i in range(DEPTH):
        wl, coming = coming, {}
        assert set(wl) == set(layer_names(i)), sorted(wl)
        for n in ("w_ffn1_gu", "w_ffn2_gu", "w_ffn1_down", "w_ffn2_down", "w_o", "w_pool_in", "w_ple_gate"):
            if n in wl:
                wl[n] = wl[n].reshape(-1, d)
        if "w_pool_grp" in wl:
            wl["w_pool_grp"] = wl["w_pool_grp"].reshape(N_DEV, 4, 32, POOL_GROUP).transpose(1, 0, 2, 3).reshape(4, POOL_GROUP, POOL_GROUP)
        for n in ("w_qkv", "w_ple_proj"):
            if n in wl:
                wl[n] = _from_column_shards(wl[n])
        weights.append(wl)
        j = i // 2
        rec = {}
        cur, rec["ffn1"], parts = _ffn_forward(cur, norm_ffn1[i][None], wl["w_ffn1_gu"], wl["w_ffn1_down"], next_weights(i, "ffn1"))
        arrived_weights(i, "ffn1", parts, coming)
        x1 = cur
        gm = norm_mix[i][None]
        if i % 2 == 0:
            qg = jnp.tile(q_norm[j], d // HEAD_DIM)[None]
            kg = jnp.tile(k_norm[j], d // HEAD_DIM)[None]
            hm, qkv = _norm_matmul("qkv_proj", x1, gm, wl["w_qkv"], (d, d), lambda a, b: (0, b), _pick_all, 3 * d, d, 3)
            qn, kn, vb = _qk_norm(qkv, qg, kg)
            stacks, layers = next_weights(i, "mix")["attn"]
            o, parts = _attention(qn, kn, vb, _GatherSide(stacks, layers) if stacks else None)
            arrived_weights(i, "mix", {"attn": parts}, coming)
            cur = _matmul_res("attn_out", o, wl["w_o"], x1, 1.0)
            rec["mix"] = (x1, gm, hm, qkv, qg, kg, qn, kn, vb, o)
        else:
            sc = scale_full[j][None]
            hm, u = _norm_matmul("pool_in", x1, gm, wl["w_pool_in"], (d, d), lambda a, b: (0, 0), _pick_all, d, d, 1)
            stacks, layers = next_weights(i, "mix")["pool"]
            (cur, pooled), parts = _pool_fwd(u, wl["w_pool_grp"], sc, x1, _GatherSide(stacks, layers) if stacks else None)
            arrived_weights(i, "mix", {"pool": parts}, coming)
            rec["mix"] = (x1, gm, hm, sc, pooled)
        cur, rec["ffn2"], parts = _ffn_forward(cur, norm_ffn2[i][None], wl["w_ffn2_gu"], wl["w_ffn2_down"], next_weights(i, "ffn2"))
        arrived_weights(i, "ffn2", parts, coming)
        x3 = cur
        gp = norm_ple[i][None]
        stacks, layers = next_weights(i, "ple")["ple"]
        (cur, hp, sig, proj), parts = _ple_fwd(x3, gp, wl["w_ple_gate"], p[i, 0], wl["w_ple_proj"],
                                               _GatherSide(stacks, layers) if stacks else None)
        arrived_weights(i, "ple", {"ple": parts}, coming)
        rec["ple"] = (x3, gp, hp, sig, proj)
        saved.append(rec)

    dy, loss_part = _loss_head(cur, target)
    loss = lax.psum(loss_part[0, 0], ("x", "y", "c"))

    small = {n: [None] * DEPTH for n in ("norm_ffn1", "norm_mix", "norm_ffn2", "norm_ple")}
    small.update(q_norm=[None] * 2, k_norm=[None] * 2, pool_scale=[None] * 2)
    results = {name: None for name in big}

    def update(layer, parts):
        for n, part in parts.items():
            mm, vv = moments[n]
            results[n] = _adamw_layer(part, big[n], mm, vv, layer_index(n, layer), results[n])

    carriers = {
        ("ffn2", "gu"): ["w_ffn1_gu"], ("ffn1", "gu"): ["w_ffn2_gu"],
        ("ffn2", "dh"): ["w_ffn1_down", "w_ple_gate", "w_ple_proj"],
        ("ffn1", "dh"): ["w_ffn2_down", "w_o", "w_pool_in", "w_pool_grp"],
        ("ffn2", "act"): ["w_qkv"],
    }
    above = {}
    own = {}

    def carried(ffn):
        return {host: [above[n] for n in names if n in above] for (f, host), names in carriers.items() if f == ffn}

    def received(ffn, parts, into):
        for (f, host), names in carriers.items():
            if f == ffn and parts.get(host) is not None:
                into.update(zip([n for n in names if n in above], parts[host]))

    dcur = dy
    for i in reversed(range(DEPTH)):
        wl, rec, j = weights[i], saved[i], i // 2
        grads = {}
        arrived = {}
        x3, gp, hp, sig, proj = rec["ple"]
        dgate, dproj = _ple_bwd_gate(dcur, sig, proj)
        dx3, small["norm_ple"][i] = _square_dh("ple_dh", dgate, wl["w_ple_gate"], x3, gp, dcur)
        grads["w_ple_gate"] = _grad_square("grad_ple_gate", hp, dgate).reshape(N_DEV, d // N_DEV, d)
        grads["w_ple_proj"] = _column_shards(_grad_square("grad_ple_proj", p[i, 0], dproj))
        dx2, small["norm_ffn2"][i], grads["w_ffn2_gu"], grads["w_ffn2_down"], parts = _ffn_backward(
            dx3, rec["ffn2"], wl["w_ffn2_gu"], wl["w_ffn2_down"], carried("ffn2"))
        received("ffn2", parts, arrived)
        if i % 2 == 0:
            x1, gm, hm, qkv, qg, kg, qn, kn, vb, o = rec["mix"]
            do = _matmul_nt("attn_out_bwd", dx2, wl["w_o"])
            grads["w_o"] = _grad_square("grad_attn_out", o, dx2).reshape(N_DEV, d // N_DEV, d)
            early = ["w_ple_gate", "w_ple_proj", "w_ffn2_gu", "w_ffn2_down"] if i == 0 else []
            (dq, dk, dv), parts = _attention_bwd(qn, kn, vb, do, o, _ScatterSide([grads[n] for n in early]) if early else None)
            own.update(zip(early, parts or []))
            dqkv, dqg, dkg = _qk_norm_bwd(qkv, dq, dk, dv, qg, kg)
            small["q_norm"][j], small["k_norm"][j] = dqg[0:1], dkg[0:1]
            dx1, small["norm_mix"][i] = _qkv_dh(dqkv, wl["w_qkv"], x1, gm, dx2)
            grads["w_qkv"] = _column_shards(_grad_square("grad_qkv", hm, dqkv))
        else:
            x1, gm, hm, sc, pooled = rec["mix"]
            dpool, dgrp, small["pool_scale"][j] = _pool_bwd_group(dx2, pooled, wl["w_pool_grp"], sc)
            grads["w_pool_grp"] = dgrp.reshape(4, N_DEV, 32, POOL_GROUP).transpose(1, 0, 2, 3).reshape(N_DEV, 4 * 32, POOL_GROUP)
            du = _pool_bwd_window(dpool)
            dx1, small["norm_mix"][i] = _square_dh("pool_dh", du, wl["w_pool_in"], x1, gm, dx2)
            grads["w_pool_in"] = _grad_square("grad_pool_in", hm, du).reshape(N_DEV, d // N_DEV, d)
        ffn1_carried = carried("ffn1")
        late = ["w_qkv", "w_o"] if i == 0 else []
        ffn1_carried["act"] = [grads[n] for n in late]
        dcur, small["norm_ffn1"][i], grads["w_ffn1_gu"], grads["w_ffn1_down"], parts = _ffn_backward(
            dx1, rec["ffn1"], wl["w_ffn1_gu"], wl["w_ffn1_down"], ffn1_carried)
        received("ffn1", parts, arrived)
        own.update(zip(late, parts["act"] or []))
        if above:
            assert set(arrived) == set(above), (sorted(arrived), sorted(above))
            update(i + 1, arrived)
        above = grads

    rest = [n for n in layer_names(0) if n not in own]
    own.update(zip(rest, _scatter_grads("scatter_grads_0", [above[n] for n in rest])))
    update(0, own)

    def lanes(a):
        return jnp.pad(a, ((0, 0), (0, d - a.shape[-1])))

    order = [("norm_ffn1", DEPTH), ("norm_mix", DEPTH), ("norm_ffn2", DEPTH), ("norm_ple", DEPTH), ("pool_scale", 2), ("q_norm", 2), ("k_norm", 2)]
    rows = jnp.concatenate([lanes(g) for name, _ in order for g in small[name]], axis=0)
    n_rows = rows.shape[0]
    pad_rows = -n_rows % 8
    rows = jnp.pad(rows, ((0, pad_rows), (0, 0)))
    gathered = _gather_small(rows)

    def own_lanes(a):
        return lax.dynamic_update_slice(jnp.zeros((a.shape[0], d), F32), a, (0, me * 128))

    def pack(values):
        mats = [own_lanes(values[name]) if name == "pool_scale" else lanes(values[name]) for name, _ in order]
        return jnp.pad(jnp.concatenate(mats, axis=0), ((0, pad_rows), (0, 0)))

    small_w = dict(norm_ffn1=norm_ffn1, norm_mix=norm_mix, norm_ffn2=norm_ffn2, norm_ple=norm_ple, pool_scale=pool_scale, q_norm=q_norm, k_norm=k_norm)
    small_m = dict(norm_ffn1=m_norm_ffn1, norm_mix=m_norm_mix, norm_ffn2=m_norm_ffn2, norm_ple=m_norm_ple, pool_scale=m_pool_scale, q_norm=m_q_norm, k_norm=m_k_norm)
    small_v = dict(norm_ffn1=v_norm_ffn1, norm_mix=v_norm_mix, norm_ffn2=v_norm_ffn2, norm_ple=v_norm_ple, pool_scale=v_pool_scale, q_norm=v_q_norm, k_norm=v_k_norm)
    packed = _adamw_small(gathered, pack(small_w), pack(small_m), pack(small_v))

    def unpack(mat):
        out, at = {}, 0
        for name, n in order:
            blk = mat[at:at + n]
            at += n
            if name == "pool_scale":
                out[name] = lax.dynamic_slice(blk, (0, me * 128), (n, 128))
            elif name in ("q_norm", "k_norm"):
                out[name] = blk[:, :HEAD_DIM]
            else:
                out[name] = blk
        return out

    small_out = [unpack(mat) for mat in packed]

    def result(kind, name):
        if name in small_w:
            return small_out[kind][name]
        r = results[name][kind]
        if name in ("w_ffn1_gu", "w_ffn2_gu"):
            return tr(r)
        return r.reshape(w_pool_grp.shape) if name == "w_pool_grp" else r

    weight_names = ["norm_ffn1", "w_ffn1_gu", "w_ffn1_down", "norm_mix", "w_qkv", "q_norm", "k_norm", "w_o", "w_pool_in", "w_pool_grp",
                    "pool_scale", "norm_ffn2", "w_ffn2_gu", "w_ffn2_down", "norm_ple", "w_ple_gate", "w_ple_proj"]
    outs = [loss, dcur[None]]
    for kind in range(4):
        outs += [result(kind, name) for name in weight_names]
    return tuple(outs)
```

```python
import math

import jax
import jax.numpy as jnp
from jax import lax
from jax.experimental import pallas as pl
from jax.experimental.pallas import tpu as pltpu

F32 = jnp.float32
BF16 = jnp.bfloat16

N_DEV = 8
DEPTH = 4
D_MODEL = 1024
N_UNITS = D_MODEL // 128
HEAD_DIM = 64
POOL_WINDOWS = (2, 4, 8, 16)
POOL_GROUP = 256
POOL_HALO = 128
EPS = 1e-6
ATTN_SCALE = 1.0 / math.sqrt(HEAD_DIM)
ATTN_BLOCK = 256
LOG_ZERO = -104.0

ADAM_LR = 0.001
ADAM_B1 = 0.9
ADAM_B2 = 0.999
ADAM_EPS = 1e-08
ADAM_WD = 0.01
ADAM_STEP = 10
ADAMW_BLOCK_ELEMS = 128 * 1024

VMEM_LIMIT = 56 * 1024 * 1024
MESH = pl.DeviceIdType.MESH

NT_DIMS = (((1,), (1,)), ((), ()))
TN_DIMS = (((0,), (0,)), ((), ()))


def _params(*sem):
    return pltpu.CompilerParams(dimension_semantics=sem, vmem_limit_bytes=VMEM_LIMIT)


def _row_tile(s):
    return min(512, s)


def _grad_tile(s):
    return min(2048, s)


def _block_spec(block, index, mode=None):
    return pl.BlockSpec(block, index) if mode is None else pl.BlockSpec(block, index, pipeline_mode=mode)


def _dot(a, b):
    return jnp.dot(a, b, preferred_element_type=F32)


def _dot_nt(a, b):
    return lax.dot_general(a, b, NT_DIMS, preferred_element_type=F32)


def _dot_tn(a, b):
    return lax.dot_general(a, b, TN_DIMS, preferred_element_type=F32)


def _dot_f32(a, b):
    return jnp.dot(a, b, precision=lax.Precision.HIGHEST, preferred_element_type=F32)


def _dot_3x(a, b):
    return jnp.dot(a, b, precision=lax.Precision.HIGH, preferred_element_type=F32)


def _rms(x, g):
    r = lax.rsqrt(jnp.mean(x * x, axis=-1, keepdims=True) + EPS)
    return x * r * g


def _rms_bwd(dy, x, g):
    r = lax.rsqrt(jnp.mean(x * x, axis=-1, keepdims=True) + EPS)
    xh = x * r
    dg = jnp.sum(dy * xh, axis=0, keepdims=True)
    dxh = dy * g
    dx = r * (dxh - xh * jnp.mean(dxh * xh, axis=-1, keepdims=True))
    return dx, dg


def _cast_bf16(w):
    l, r, c = w.shape

    def body(w_ref, o_ref):
        o_ref[...] = w_ref[...].astype(BF16)

    return pl.pallas_call(
        body, grid=(l,), name="cast_bf16",
        in_specs=[pl.BlockSpec((1, r, c), lambda i: (i, 0, 0))],
        out_specs=pl.BlockSpec((1, r, c), lambda i: (i, 0, 0)),
        out_shape=jax.ShapeDtypeStruct(w.shape, BF16),
        compiler_params=_params("arbitrary"),
    )(w)


def _ffn_up(x, g, wt, side=None):
    s, d = x.shape
    f = wt.shape[0] // 2
    tn = f // 2
    tm = _row_tile(s)

    def body(x_ref, g_ref, w_ref, h_ref, gate_ref, up_ref, act_ref):
        h = _rms(x_ref[...], g_ref[...]).astype(BF16)
        h_ref[...] = h
        for half in range(2):
            cols = slice(half * tn, (half + 1) * tn)
            gate = _dot_nt(h, w_ref[half * tn:(half + 1) * tn, :])
            up = _dot_nt(h, w_ref[f + half * tn:f + (half + 1) * tn, :])
            gate_ref[:, cols] = gate.astype(BF16)
            up_ref[:, cols] = up.astype(BF16)
            act_ref[:, cols] = (gate * jax.nn.sigmoid(gate) * up).astype(BF16)

    wide = pl.BlockSpec((tm, f), lambda i: (i, 0))
    hidden = jax.ShapeDtypeStruct((s, f), BF16)
    return _host_call(
        body, (x, g, wt), side, grid=(s // tm,), name="ffn_up",
        in_specs=[
            pl.BlockSpec((tm, d), lambda i: (i, 0)),
            pl.BlockSpec((1, d), lambda i: (0, 0)),
            _block_spec((2 * f, d), lambda i: (0, 0), pl.Buffered(1)),
        ],
        out_specs=[pl.BlockSpec((tm, d), lambda i: (i, 0)), wide, wide, wide],
        out_shape=[jax.ShapeDtypeStruct((s, d), BF16), hidden, hidden, hidden],
        compiler_params=_params("arbitrary"),
    )


def _ffn_down(act, wd, x, side=None):
    s, f = act.shape
    d = wd.shape[-1]
    tm = _row_tile(s)

    def body(a_ref, w_ref, x_ref, o_ref):
        o_ref[...] = x_ref[...] + 0.5 * _dot(a_ref[...], w_ref[...])

    row = pl.BlockSpec((tm, d), lambda i: (i, 0))
    return _host_call(
        body, (act, wd, x), side, grid=(s // tm,), name="ffn_down",
        in_specs=[pl.BlockSpec((tm, f), lambda i: (i, 0)), _block_spec((f, d), lambda i: (0, 0), pl.Buffered(1)), row],
        out_specs=[row],
        out_shape=[jax.ShapeDtypeStruct((s, d), F32)],
        compiler_params=_params("arbitrary"),
    )


def _ffn_bwd_act(dx, wd, gate, up, side=None):
    s, d = dx.shape
    f = gate.shape[-1]
    tn = f // 2
    tm = _row_tile(s)

    def body(dx_ref, w_ref, gate_ref, up_ref, dgate_ref, dup_ref):
        dact = _dot_nt((0.5 * dx_ref[...]).astype(BF16), w_ref[...])
        gate = gate_ref[...].astype(F32)
        sig = jax.nn.sigmoid(gate)
        silu = gate * sig
        dgate_ref[...] = (dact * up_ref[...].astype(F32) * (sig + silu * (1.0 - sig))).astype(BF16)
        dup_ref[...] = (dact * silu).astype(BF16)

    col = pl.BlockSpec((tm, tn), lambda j, i: (i, j))
    hidden = jax.ShapeDtypeStruct((s, f), BF16)
    return _host_call(
        body, (dx, wd, gate, up), side, grid=(2, s // tm), name="ffn_bwd_act",
        in_specs=[pl.BlockSpec((tm, d), lambda j, i: (i, 0)), pl.BlockSpec((tn, d), lambda j, i: (j, 0)), col, col],
        out_specs=[col, col],
        out_shape=[hidden, hidden],
        compiler_params=_params("arbitrary", "arbitrary"),
    )


def _norm_bwd_matmul(name, operands, products, nk, x, g, dres, w_rows_contract=False, side=None):
    s, d = x.shape
    tm = _row_tile(s)
    n = len(operands)

    def body(*refs):
        x_ref, g_ref, dres_ref, dx_ref, dg_ref, acc_ref = refs[n:]
        i, k = pl.program_id(0), pl.program_id(1)

        @pl.when(jnp.logical_and(i == 0, k == 0))
        def _():
            dg_ref[...] = jnp.zeros_like(dg_ref)

        total = None
        for a_at, a_pick, w_at, w_pick in products:
            prod = (_dot if w_rows_contract else _dot_nt)(a_pick(refs[a_at]), w_pick(refs[w_at]))
            total = prod if total is None else total + prod

        if nk > 1:
            @pl.when(k == 0)
            def _():
                acc_ref[...] = total

            @pl.when(jnp.logical_and(k > 0, k < nk - 1))
            def _():
                acc_ref[...] += total

        @pl.when(k == nk - 1)
        def _():
            dy = total if nk == 1 else acc_ref[...] + total
            dx, dg = _rms_bwd(dy, x_ref[...], g_ref[...])
            dx_ref[...] = dres_ref[...] + dx
            dg_ref[...] += dg

    row = pl.BlockSpec((tm, d), lambda i, k: (i, 0))
    vec = pl.BlockSpec((1, d), lambda i, k: (0, 0))
    return _host_call(
        body, [op[0] for op in operands] + [x, g, dres], side, grid=(s // tm, nk), name=name,
        in_specs=[_block_spec(*op[1:]) for op in operands] + [row, vec, row],
        out_specs=[row, vec],
        out_shape=[jax.ShapeDtypeStruct((s, d), F32), jax.ShapeDtypeStruct((1, d), F32)],
        scratch_shapes=[pltpu.VMEM((tm, d), F32)],
        compiler_params=_params("arbitrary", "arbitrary"),
    )


def _ffn_dh(dgate, dup, wt, x, g, dres, side=None):
    s, d = x.shape
    f = dgate.shape[-1]
    tm = _row_tile(s)
    once = pl.Buffered(1)
    operands = [
        (dgate, (tm, f), lambda i, k: (i, 0)),
        (dup, (tm, f), lambda i, k: (i, 0)),
        (wt, (f, d), lambda i, k: (0, 0), once),
        (wt, (f, d), lambda i, k: (1, 0), once),
    ]
    products = [(0, _pick_all, 2, _pick_all), (1, _pick_all, 3, _pick_all)]
    return _norm_bwd_matmul("ffn_dh", operands, products, 1, x, g, dres, w_rows_contract=True, side=side)


def _square_dh(name, a, w, x, g, dres):
    s, d = x.shape
    tm = _row_tile(s)
    operands = [(a, (tm, d), lambda i, k: (i, 0)), (w, (d, d), lambda i, k: (0, 0))]
    return _norm_bwd_matmul(name, operands, [(0, _pick_all, 1, _pick_all)], 1, x, g, dres)[0]


def _qkv_dh(dqkv, wqkv, x, g, dres):
    s, d = x.shape
    tm = _row_tile(s)
    operands = [(dqkv, (tm, 3 * d), lambda i, k: (i, 0)), (wqkv, (d, 3 * d), lambda i, k: (0, 0), pl.Buffered(1))]
    return _norm_bwd_matmul("qkv_dh", operands, [(0, _pick_all, 1, _pick_all)], 1, x, g, dres)[0]


def _grad_matmul(name, a, a_block, a_index, a_pick, b, b_block, b_index, b_picks, out_shape, out_block, out_index, out_stores, nj, scale=1.0):
    s = a.shape[-2]
    nk = s // _grad_tile(s)
    n_prod = len(b_picks)

    def body(a_ref, b_ref, o_ref, *acc_refs):
        k = pl.program_id(1)
        av = a_pick(a_ref)
        if av.dtype != BF16:
            av = (scale * av).astype(BF16)
        for b_pick, store, acc_ref in zip(b_picks, out_stores, acc_refs):
            bv = b_pick(b_ref)
            if bv.dtype != BF16:
                bv = bv.astype(BF16)
            prod = _dot_tn(av, bv)
            if nk == 1:
                store(o_ref, prod.astype(BF16))
                continue

            @pl.when(k == 0)
            def _():
                acc_ref[...] = prod

            @pl.when(jnp.logical_and(k > 0, k < nk - 1))
            def _():
                acc_ref[...] += prod

            @pl.when(k == nk - 1)
            def _():
                store(o_ref, (acc_ref[...] + prod).astype(BF16))

    m = jax.eval_shape(a_pick, jax.ShapeDtypeStruct(a_block, a.dtype)).shape[-1]
    nn = jax.eval_shape(b_picks[0], jax.ShapeDtypeStruct(b_block, b.dtype)).shape[-1]
    acc_shape = (m, nn)
    return pl.pallas_call(
        body, grid=(nj, nk), name=name,
        in_specs=[pl.BlockSpec(a_block, a_index), pl.BlockSpec(b_block, b_index)],
        out_specs=pl.BlockSpec(out_block, out_index),
        out_shape=jax.ShapeDtypeStruct(out_shape, BF16),
        scratch_shapes=[pltpu.VMEM(acc_shape, F32) for _ in range(n_prod)],
        compiler_params=_params("arbitrary", "arbitrary"),
    )(a, b)


def _pick_all(r):
    return r[...]


def _store_all(r, v):
    r[...] = v


def _grad_ffn_down(act, dx):
    s, f = act.shape
    d = dx.shape[-1]
    tk = _grad_tile(s)
    tn = f // 2
    return _grad_matmul(
        "grad_ffn_down", act, (tk, tn), lambda j, k: (k, j), _pick_all,
        dx, (tk, d), lambda j, k: (k, 0), [lambda r: 0.5 * r[...]],
        (f, d), (tn, d), lambda j, k: (j, 0), [_store_all], 2)


def _grad_ffn_gu(h, dgate, dup, side=None):
    s, d = h.shape
    f = dgate.shape[-1]
    tn = f // 2
    tk = min(1024, s)
    nk = s // tk

    def body(h_ref, dgate_ref, dup_ref, o_ref, acc_ref):
        j, k = pl.program_id(0), pl.program_id(1)

        def accumulate(a_ref):
            prod = _dot_tn(a_ref[...], h_ref[...])
            if nk == 1:
                o_ref[...] = prod.astype(BF16)
                return

            @pl.when(k == 0)
            def _():
                acc_ref[...] = prod

            @pl.when(jnp.logical_and(k > 0, k < nk - 1))
            def _():
                acc_ref[...] += prod

            @pl.when(k == nk - 1)
            def _():
                o_ref[...] = (acc_ref[...] + prod).astype(BF16)

        @pl.when(j < 2)
        def _():
            accumulate(dgate_ref)

        @pl.when(j >= 2)
        def _():
            accumulate(dup_ref)

    (out,), parts = _host_call(
        body, (h, dgate, dup), side, grid=(4, nk), name="grad_ffn_gu",
        in_specs=[
            pl.BlockSpec((tk, d), lambda j, k: (k, 0)),
            pl.BlockSpec((tk, tn), lambda j, k: (jnp.where(j < 2, k, 0), jnp.minimum(j, 1))),
            pl.BlockSpec((tk, tn), lambda j, k: (jnp.where(j >= 2, k, 0), jnp.maximum(j - 2, 0))),
        ],
        out_specs=[pl.BlockSpec((tn, d), lambda j, k: (j, 0))],
        out_shape=[jax.ShapeDtypeStruct((2 * f, d), BF16)],
        scratch_shapes=[pltpu.VMEM((tn, d), F32)],
        compiler_params=_params("arbitrary", "arbitrary"),
    )
    return out, parts


def _grad_square(name, a, b):
    s, m = a.shape
    n = b.shape[-1]
    tk = _grad_tile(s)
    tn = min(n, D_MODEL)
    return _grad_matmul(
        name, a, (tk, m), lambda j, k: (k, 0), _pick_all,
        b, (tk, tn), lambda j, k: (k, j), [_pick_all],
        (m, n), (m, tn), lambda j, k: (0, j), [_store_all], n // tn)


def _column_shards(w):
    m, n = w.shape
    return w.reshape(m, N_DEV, n // N_DEV).transpose(1, 0, 2)


def _from_column_shards(w):
    nd, m, n = w.shape
    return w.transpose(1, 0, 2).reshape(m, nd * n)


def _norm_matmul(name, x, g, w, w_block, w_index, w_pick, n_total, tn, nj):
    s, d = x.shape
    tm = _row_tile(s)

    def body(x_ref, g_ref, w_ref, h_ref, y_ref):
        @pl.when(pl.program_id(1) == 0)
        def _():
            h_ref[...] = _rms(x_ref[...], g_ref[...]).astype(BF16)

        y_ref[...] = _dot(h_ref[...], w_pick(w_ref))

    return pl.pallas_call(
        body, grid=(s // tm, nj), name=name,
        in_specs=[
            pl.BlockSpec((tm, d), lambda i, j: (i, 0)),
            pl.BlockSpec((1, d), lambda i, j: (0, 0)),
            pl.BlockSpec(w_block, w_index),
        ],
        out_specs=[pl.BlockSpec((tm, d), lambda i, j: (i, 0)), pl.BlockSpec((tm, tn), lambda i, j: (i, j))],
        out_shape=[jax.ShapeDtypeStruct((s, d), BF16), jax.ShapeDtypeStruct((s, n_total), F32)],
        compiler_params=_params("arbitrary", "arbitrary"),
    )(x, g, w)


def _head_mean_matrix():
    r = lax.broadcasted_iota(jnp.int32, (128, 128), 0) // HEAD_DIM
    c = lax.broadcasted_iota(jnp.int32, (128, 128), 1) // HEAD_DIM
    return jnp.where(r == c, 1.0 / HEAD_DIM, 0.0).astype(F32)


def _qk_norm(qkv, qg, kg):
    s = qkv.shape[0]
    d = D_MODEL
    tm = _row_tile(s)

    def body(q_ref, k_ref, v_ref, qg_ref, kg_ref, qo_ref, ko_ref, vo_ref):
        mean_m = _head_mean_matrix()
        for u in range(N_UNITS):
            cols = slice(128 * u, 128 * (u + 1))
            for src, gain, dst, scale in ((q_ref, qg_ref, qo_ref, ATTN_SCALE), (k_ref, kg_ref, ko_ref, 1.0)):
                xs = src[:, cols]
                r = lax.rsqrt(_dot_3x(xs * xs, mean_m) + EPS)
                y = xs * r * gain[:, cols]
                dst[:, cols] = (y * scale).astype(BF16) if scale != 1.0 else y.astype(BF16)
        vo_ref[...] = v_ref[...].astype(BF16)

    blk = lambda c: pl.BlockSpec((tm, d), lambda i: (i, c))
    vec = pl.BlockSpec((1, d), lambda i: (0, 0))
    return pl.pallas_call(
        body, grid=(s // tm,), name="qk_norm",
        in_specs=[blk(0), blk(1), blk(2), vec, vec],
        out_specs=[blk(0)] * 3,
        out_shape=[jax.ShapeDtypeStruct((s, d), BF16)] * 3,
        compiler_params=_params("arbitrary"),
    )(qkv, qkv, qkv, qg, kg)


def _qk_norm_bwd(qkv, dq, dk, dv, qg, kg):
    s = qkv.shape[0]
    d = D_MODEL
    tm = _row_tile(s)
    nsteps = s // tm

    def body(q_ref, k_ref, dq_ref, dk_ref, dv_ref, qg_ref, kg_ref, o_ref, dqg_ref, dkg_ref, acc_ref):
        i = pl.program_id(0)

        @pl.when(i == 0)
        def _():
            acc_ref[...] = jnp.zeros_like(acc_ref)

        mean_m = _head_mean_matrix()
        for u in range(N_UNITS):
            cols = slice(128 * u, 128 * (u + 1))
            for n, (src, dsrc, gain) in enumerate(((q_ref, dq_ref, qg_ref), (k_ref, dk_ref, kg_ref))):
                xs = src[:, cols]
                dy = dsrc[:, cols]
                r = lax.rsqrt(_dot_3x(xs * xs, mean_m) + EPS)
                xh = xs * r
                acc_ref[n:n + 1, :] += jnp.sum(dy * xh, axis=0, keepdims=True)
                dxh = dy * gain[:, cols]
                dx = r * (dxh - xh * _dot_3x(dxh * xh, mean_m))
                o_ref[:, 128 * (N_UNITS * n + u):128 * (N_UNITS * n + u + 1)] = dx.astype(BF16)
        o_ref[:, 2 * d:3 * d] = dv_ref[...].astype(BF16)

        @pl.when(i == nsteps - 1)
        def _():
            r = lax.broadcasted_iota(jnp.int32, (128, 128), 0) % HEAD_DIM
            c = lax.broadcasted_iota(jnp.int32, (128, 128), 1) % HEAD_DIM
            fold = jnp.where(r == c, 1.0, 0.0).astype(F32)
            folded = _dot_f32(acc_ref[...], fold)
            dqg_ref[...] = jnp.broadcast_to(folded[0:1], (8, 128))
            dkg_ref[...] = jnp.broadcast_to(folded[1:2], (8, 128))

    blk = lambda c: pl.BlockSpec((tm, d), lambda i: (i, c))
    row = pl.BlockSpec((tm, d), lambda i: (i, 0))
    vec = pl.BlockSpec((1, d), lambda i: (0, 0))
    small = pl.BlockSpec((8, 128), lambda i: (0, 0))
    return pl.pallas_call(
        body, grid=(nsteps,), name="qk_norm_bwd",
        in_specs=[blk(0), blk(1), row, row, row, vec, vec],
        out_specs=[pl.BlockSpec((tm, 3 * d), lambda i: (i, 0)), small, small],
        out_shape=[jax.ShapeDtypeStruct((s, 3 * d), BF16), jax.ShapeDtypeStruct((8, 128), F32), jax.ShapeDtypeStruct((8, 128), F32)],
        scratch_shapes=[pltpu.VMEM((8, 128), F32)],
        compiler_params=_params("arbitrary"),
    )(qkv, qkv, dq, dk, dv, qg, kg)


def _split_dot(x, m):
    hi = x.astype(BF16)
    lo = (x - hi.astype(F32)).astype(BF16)
    return _dot(hi, m) + _dot(lo, m)


def _stack_heads(x):
    lane = lax.broadcasted_iota(jnp.int32, x.shape, 1)
    zero = jnp.zeros_like(x)
    return jnp.concatenate([jnp.where(lane < HEAD_DIM, x, zero), jnp.where(lane < HEAD_DIM, zero, x)], axis=0)


def _unstack_heads(x2, t):
    lane = lax.broadcasted_iota(jnp.int32, (t, 128), 1)
    return jnp.where(lane < HEAD_DIM, x2[:t], x2[t:])


def _attn_masks(t):
    r = lax.broadcasted_iota(jnp.int32, (t, t), 0)
    c = lax.broadcasted_iota(jnp.int32, (t, t), 1)
    row = lax.broadcasted_iota(jnp.int32, (2 * t, t), 0)
    col = lax.broadcasted_iota(jnp.int32, (2 * t, t), 1)
    causal = col < jnp.where(row >= t, row - t, row)
    return (r > c).astype(BF16), (r >= c).astype(BF16), causal


def _attn_sweep_cond(st):
    return jnp.logical_and(st[0] >= 0, st[1] > LOG_ZERO)


def _attn_scores(q2, kblk, after, causal):
    z = _dot_nt(q2, kblk)
    sp = jnp.maximum(z, 0.0) + jnp.log(1.0 + jnp.exp(-jnp.abs(z)))
    log_stay = -sp
    if causal is not None:
        log_stay = jnp.where(causal, log_stay, 0.0)
    return log_stay, z - sp, _split_dot(log_stay, after)


def _attention(q, k, v, side=None):
    s, d = q.shape
    t = min(ATTN_BLOCK, s)

    def body(q_ref, k_ref, v_ref, o_ref):
        i = pl.program_id(1)
        after, _, causal = _attn_masks(t)
        q2 = _stack_heads(q_ref[...])

        def step(kb, carry, acc, mask):
            start = pl.multiple_of(kb * t, t)
            kblk = k_ref[pl.ds(start, t), :]
            vblk = v_ref[pl.ds(start, t), :]
            log_stay, log_beta, later = _attn_scores(q2, kblk, after, mask)
            w = jnp.exp(log_beta + later + carry)
            if mask is not None:
                w = jnp.where(mask, w, 0.0)
            return carry + jnp.sum(log_stay, axis=1, keepdims=True), acc + _dot(w.astype(BF16), vblk)

        carry, acc = step(i, jnp.zeros((2 * t, 1), F32), jnp.zeros((2 * t, 128), F32), causal)
        carry, acc = step(jnp.maximum(i - 1, 0), carry, acc, jnp.broadcast_to(i > 0, causal.shape))

        def loop(st):
            c, a = step(st[0], st[2], st[3], None)
            return st[0] - 1, jnp.max(c), c, a

        acc = lax.while_loop(_attn_sweep_cond, loop, (i - 2, jnp.max(carry), carry, acc))[3]
        o_ref[...] = _unstack_heads(acc, t)

    (o,), parts = _host_call(
        body, (q, k, v), side, grid=(N_UNITS, s // t), name="attention",
        in_specs=[
            pl.BlockSpec((t, 128), lambda h, i: (i, h)),
            pl.BlockSpec((s, 128), lambda h, i: (0, h)),
            pl.BlockSpec((s, 128), lambda h, i: (0, h)),
        ],
        out_specs=[pl.BlockSpec((t, 128), lambda h, i: (i, h))],
        out_shape=[jax.ShapeDtypeStruct((s, d), F32)],
        compiler_params=_params("arbitrary", "arbitrary"),
    )
    return o, parts


def _attention_bwd(q, k, v, do, o, side=None):
    s, d = q.shape
    t = min(ATTN_BLOCK, s)

    def body(q_ref, k_ref, v_ref, do_ref, o_ref, dq_ref, dk_ref, dv_ref):
        i = pl.program_id(1)

        @pl.when(i == 0)
        def _():
            dk_ref[...] = jnp.zeros_like(dk_ref)
            dv_ref[...] = jnp.zeros_like(dv_ref)

        after, from_here, causal = _attn_masks(t)
        q2 = _stack_heads(q_ref[...])
        do2 = _stack_heads(do_ref[...])

        def weights(kb, carry, mask):
            start = pl.multiple_of(kb * t, t)
            kblk = k_ref[pl.ds(start, t), :]
            vblk = v_ref[pl.ds(start, t), :]
            log_stay, log_beta, later = _attn_scores(q2, kblk, after, mask)
            w = jnp.exp(log_beta + later + carry)
            if mask is not None:
                w = jnp.where(mask, w, 0.0)
            w = w.astype(BF16)
            g = w.astype(F32) * _dot_nt(do2, vblk)
            return start, kblk, log_stay, log_beta, w, g

        lane = lax.broadcasted_iota(jnp.int32, (t, 128), 1)
        prod = do_ref[...].astype(F32) * o_ref[...]
        total = jnp.concatenate([
            jnp.sum(jnp.where(lane < HEAD_DIM, prod, 0.0), axis=1, keepdims=True),
            jnp.sum(jnp.where(lane < HEAD_DIM, 0.0, prod), axis=1, keepdims=True)], axis=0)
        zero = jnp.zeros((2 * t, 1), F32)

        def grad_step(kb, carry, seen, dq, mask):
            start, kblk, log_stay, log_beta, w, g = weights(kb, carry, mask)
            before = total - (_split_dot(g, from_here) + seen)
            beta = jnp.exp(log_beta)
            da = g * (1.0 - beta) - before * beta
            if mask is not None:
                da = jnp.where(mask, da, 0.0)
            dab = da.astype(BF16)
            dk_ref[pl.ds(start, t), :] += _dot_tn(dab, q2)
            dv_ref[pl.ds(start, t), :] += _dot_tn(w.astype(BF16), do2)
            return (carry + jnp.sum(log_stay, axis=1, keepdims=True), seen + jnp.sum(g, axis=1, keepdims=True),
                    dq + _dot(dab, kblk))

        carry, seen, dq = grad_step(i, zero, zero, jnp.zeros((2 * t, 128), F32), causal)
        carry, seen, dq = grad_step(jnp.maximum(i - 1, 0), carry, seen, dq, jnp.broadcast_to(i > 0, causal.shape))

        def grad_loop(st):
            c, sn, a = grad_step(st[0], st[2], st[3], st[4], None)
            return st[0] - 1, jnp.max(c), c, sn, a

        dq = lax.while_loop(_attn_sweep_cond, grad_loop, (i - 2, jnp.max(carry), carry, seen, dq))[4]
        dq_ref[...] = ATTN_SCALE * _unstack_heads(dq, t)

    blk = pl.BlockSpec((t, 128), lambda h, i: (i, h))
    full = pl.BlockSpec((s, 128), lambda h, i: (0, h))
    return _host_call(
        body, (q, k, v, do, o), side, grid=(N_UNITS, s // t), name="attention_bwd",
        in_specs=[blk, full, full, blk, blk],
        out_specs=[blk, full, full],
        out_shape=[jax.ShapeDtypeStruct((s, d), F32)] * 3,
        compiler_params=_params("arbitrary", "arbitrary"),
    )


def _matmul_res(name, a, w, x, alpha):
    s, kd = a.shape
    d = w.shape[-1]
    tm = _row_tile(s)

    def body(a_ref, w_ref, x_ref, o_ref):
        o_ref[...] = x_ref[...] + alpha * _dot(a_ref[...].astype(BF16), w_ref[...])

    return pl.pallas_call(
        body, grid=(s // tm,), name=name,
        in_specs=[pl.BlockSpec((tm, kd), lambda i: (i, 0)), pl.BlockSpec((kd, d), lambda i: (0, 0)), pl.BlockSpec((tm, d), lambda i: (i, 0))],
        out_specs=pl.BlockSpec((tm, d), lambda i: (i, 0)),
        out_shape=jax.ShapeDtypeStruct((s, d), F32),
        compiler_params=_params("arbitrary"),
    )(a, w, x)


def _matmul_nt(name, a, w):
    s, n = a.shape
    kd = w.shape[0]
    tm = _row_tile(s)

    def body(a_ref, w_ref, o_ref):
        o_ref[...] = _dot_nt(a_ref[...].astype(BF16), w_ref[...]).astype(BF16)

    return pl.pallas_call(
        body, grid=(s // tm,), name=name,
        in_specs=[pl.BlockSpec((tm, n), lambda i: (i, 0)), pl.BlockSpec((kd, n), lambda i: (0, 0))],
        out_specs=pl.BlockSpec((tm, kd), lambda i: (i, 0)),
        out_shape=jax.ShapeDtypeStruct((s, kd), BF16),
        compiler_params=_params("arbitrary"),
    )(a, w)


def _pool_matrix(rows0, cols0, nr, nc, window, transpose):
    r = rows0 + lax.broadcasted_iota(jnp.int32, (nr, nc), 0)
    c = cols0 + lax.broadcasted_iota(jnp.int32, (nr, nc), 1)
    tt, ss = (c, r) if transpose else (r, c)
    inside = jnp.logical_and(tt - ss >= 0, tt - ss < window)
    cnt = jnp.minimum(tt + 1, window).astype(F32)
    return jnp.where(inside, 1.0 / cnt, 0.0) - jnp.where(tt == ss, 1.0, 0.0)


def _pool_tile(s):
    return min(256, s)


def _pool_fwd(u, wgrp, scale, x, side=None):
    s, d = u.shape
    tm = _pool_tile(s)
    halo = min(POOL_HALO, tm)
    ratio = tm // halo

    def body(u_ref, prev_ref, w_ref, sc_ref, x_ref, o_ref, p_ref):
        i = pl.program_id(0)
        t0 = i * tm
        for gi, window in enumerate(POOL_WINDOWS):
            cols = slice(POOL_GROUP * gi, POOL_GROUP * (gi + 1))
            pooled = _dot_3x(_pool_matrix(t0, t0, tm, tm, window, False), u_ref[:, cols])
            prev = jnp.where(i > 0, prev_ref[:, cols], 0.0)
            pooled += _dot_3x(_pool_matrix(t0, t0 - halo, tm, halo, window, False), prev)
            pb = pooled.astype(BF16)
            p_ref[:, cols] = pb
            o_ref[:, cols] = x_ref[:, cols] + _dot(pb, w_ref[gi]) * sc_ref[:, cols]

    row = pl.BlockSpec((tm, d), lambda i: (i, 0))
    return _host_call(
        body, (u, u, wgrp, scale, x), side, grid=(s // tm,), name="pool_fwd",
        in_specs=[
            row,
            pl.BlockSpec((halo, d), lambda i: (jnp.maximum(i * ratio - 1, 0), 0)),
            pl.BlockSpec((4, POOL_GROUP, POOL_GROUP), lambda i: (0, 0, 0)),
            pl.BlockSpec((1, d), lambda i: (0, 0)),
            row,
        ],
        out_specs=[row, row],
        out_shape=[jax.ShapeDtypeStruct((s, d), F32), jax.ShapeDtypeStruct((s, d), BF16)],
        compiler_params=_params("arbitrary"),
    )


def _pool_bwd_group(dx, pooled, wgrp, scale):
    s, d = dx.shape
    tm = _pool_tile(s)
    nsteps = s // tm

    def body(dx_ref, p_ref, w_ref, sc_ref, dp_ref, dw_ref, dsc_ref, acc_ref):
        i = pl.program_id(0)

        @pl.when(i == 0)
        def _():
            acc_ref[...] = jnp.zeros_like(acc_ref)
            dsc_ref[...] = jnp.zeros_like(dsc_ref)

        for gi in range(len(POOL_WINDOWS)):
            cols = slice(POOL_GROUP * gi, POOL_GROUP * (gi + 1))
            pb = p_ref[:, cols]
            dxg = dx_ref[:, cols]
            y = _dot(pb, w_ref[gi])
            dsc_ref[:, cols] += jnp.sum(dxg * y, axis=0, keepdims=True)
            dyb = (dxg * sc_ref[:, cols]).astype(BF16)
            dp_ref[:, cols] = _dot_nt(dyb, w_ref[gi])
            acc_ref[gi] += _dot_tn(pb, dyb)

        @pl.when(i == nsteps - 1)
        def _():
            dw_ref[...] = acc_ref[...].astype(BF16)

    row = pl.BlockSpec((tm, d), lambda i: (i, 0))
    grp = pl.BlockSpec((4, POOL_GROUP, POOL_GROUP), lambda i: (0, 0, 0))
    vec = pl.BlockSpec((1, d), lambda i: (0, 0))
    return pl.pallas_call(
        body, grid=(nsteps,), name="pool_bwd_group",
        in_specs=[row, row, grp, vec],
        out_specs=[row, grp, vec],
        out_shape=[jax.ShapeDtypeStruct((s, d), F32), jax.ShapeDtypeStruct((4, POOL_GROUP, POOL_GROUP), BF16), jax.ShapeDtypeStruct((1, d), F32)],
        scratch_shapes=[pltpu.VMEM((4, POOL_GROUP, POOL_GROUP), F32)],
        compiler_params=_params("arbitrary"),
    )(dx, pooled, wgrp, scale)


def _pool_bwd_window(dp):
    s, d = dp.shape
    tm = _pool_tile(s)
    halo = min(POOL_HALO, tm)
    ratio = tm // halo
    nsteps = s // tm

    def body(dp_ref, next_ref, o_ref):
        i = pl.program_id(0)
        t0 = i * tm
        for gi, window in enumerate(POOL_WINDOWS):
            cols = slice(POOL_GROUP * gi, POOL_GROUP * (gi + 1))
            du = _dot_3x(_pool_matrix(t0, t0, tm, tm, window, True), dp_ref[:, cols])
            nxt = jnp.where(i < nsteps - 1, next_ref[:, cols], 0.0)
            du += _dot_3x(_pool_matrix(t0, t0 + tm, tm, halo, window, True), nxt)
            o_ref[:, cols] = du.astype(BF16)

    row = pl.BlockSpec((tm, d), lambda i: (i, 0))
    return pl.pallas_call(
        body, grid=(nsteps,), name="pool_bwd_window",
        in_specs=[row, pl.BlockSpec((halo, d), lambda i: (jnp.minimum((i + 1) * ratio, s // halo - 1), 0))],
        out_specs=row,
        out_shape=jax.ShapeDtypeStruct((s, d), BF16),
        compiler_params=_params("arbitrary"),
    )(dp, dp)


def _ple_fwd(x, g, wgate, p, wproj, side=None):
    s, d = x.shape
    pd = p.shape[-1]
    tm = _row_tile(s)

    def body(x_ref, g_ref, wg_ref, p_ref, wp_ref, o_ref, h_ref, sig_ref, proj_ref):
        x = x_ref[...]
        h = _rms(x, g_ref[...]).astype(BF16)
        sig = jax.nn.sigmoid(_dot(h, wg_ref[...]))
        proj = _dot(p_ref[...].astype(BF16), wp_ref[...])
        o_ref[...] = x + sig * proj
        h_ref[...] = h
        sig_ref[...] = sig.astype(BF16)
        proj_ref[...] = proj.astype(BF16)

    row = pl.BlockSpec((tm, d), lambda i: (i, 0))
    return _host_call(
        body, (x, g, wgate, p, wproj), side, grid=(s // tm,), name="ple_fwd",
        in_specs=[
            row,
            pl.BlockSpec((1, d), lambda i: (0, 0)),
            pl.BlockSpec((d, d), lambda i: (0, 0)),
            pl.BlockSpec((tm, pd), lambda i: (i, 0)),
            pl.BlockSpec((pd, d), lambda i: (0, 0)),
        ],
        out_specs=[row] * 4,
        out_shape=[jax.ShapeDtypeStruct((s, d), F32)] + [jax.ShapeDtypeStruct((s, d), BF16)] * 3,
        compiler_params=_params("arbitrary"),
    )


def _ple_bwd_gate(dx, sig, proj):
    s, d = dx.shape
    tm = _row_tile(s)

    def body(dx_ref, sig_ref, proj_ref, dg_ref, dp_ref):
        dx = dx_ref[...]
        sig = sig_ref[...].astype(F32)
        dg_ref[...] = (dx * proj_ref[...].astype(F32) * (sig * (1.0 - sig))).astype(BF16)
        dp_ref[...] = (dx * sig).astype(BF16)

    row = pl.BlockSpec((tm, d), lambda i: (i, 0))
    return pl.pallas_call(
        body, grid=(s // tm,), name="ple_bwd_gate",
        in_specs=[row, row, row], out_specs=[row, row],
        out_shape=[jax.ShapeDtypeStruct((s, d), BF16)] * 2,
        compiler_params=_params("arbitrary"),
    )(dx, sig, proj)


def _loss_head(y, target):
    s, d = y.shape
    tm = _row_tile(s)
    nsteps = s // tm

    def body(y_ref, t_ref, dy_ref, loss_ref, acc_ref):
        i = pl.program_id(0)

        @pl.when(i == 0)
        def _():
            acc_ref[...] = jnp.zeros_like(acc_ref)

        err = y_ref[...] - t_ref[...]
        dy_ref[...] = err * (1.0 / d)
        acc_ref[...] += jnp.sum(jnp.mean(err * err, axis=-1, keepdims=True), axis=0, keepdims=True)

        @pl.when(i == nsteps - 1)
        def _():
            loss_ref[...] = 0.5 * acc_ref[...]

    row = pl.BlockSpec((tm, d), lambda i: (i, 0))
    return pl.pallas_call(
        body, grid=(nsteps,), name="loss_head",
        in_specs=[row, row], out_specs=[row, pl.BlockSpec((8, 128), lambda i: (0, 0))],
        out_shape=[jax.ShapeDtypeStruct((s, d), F32), jax.ShapeDtypeStruct((8, 128), F32)],
        scratch_shapes=[pltpu.VMEM((8, 128), F32)],
        compiler_params=_params("arbitrary"),
    )(y, target)


def _adamw_math(w, g, m, v):
    m = ADAM_B1 * m + (1.0 - ADAM_B1) * g
    v = ADAM_B2 * v + (1.0 - ADAM_B2) * (g * g)
    m_hat = m / (1.0 - ADAM_B1 ** ADAM_STEP)
    v_hat = v / (1.0 - ADAM_B2 ** ADAM_STEP)
    delta = -ADAM_LR * (m_hat / (jnp.sqrt(v_hat) + ADAM_EPS) + ADAM_WD * w)
    return delta, m, v


def _adamw_layer(parts, w, m, v, layer, outs):
    nl, r, c = w.shape
    tr = max(t for t in range(16, r + 1, 16) if r % t == 0 and t * c <= ADAMW_BLOCK_ELEMS)

    def body(p_ref, w_ref, m_ref, v_ref, *rest):
        g_ref, d_ref, nm_ref, nv_ref = rest[-4:]
        g = p_ref[0].astype(F32)
        for dev in range(1, N_DEV):
            g = g + p_ref[dev].astype(F32)
        delta, nm, nv = _adamw_math(w_ref[0], g, m_ref[0], v_ref[0])
        g_ref[0] = g
        d_ref[0] = delta
        nm_ref[0] = nm
        nv_ref[0] = nv

    slab = pl.BlockSpec((1, tr, c), lambda i: (layer, i, 0))
    any_spec = pl.BlockSpec(memory_space=pl.ANY)
    shape = jax.ShapeDtypeStruct(w.shape, F32)
    carried = [] if outs is None else list(outs)
    return pl.pallas_call(
        body, grid=(r // tr,), name="adamw",
        in_specs=[pl.BlockSpec((N_DEV, tr, c), lambda i: (0, i, 0)), slab, slab, slab] + [any_spec] * len(carried),
        out_specs=[slab] * 4,
        out_shape=[shape] * 4,
        input_output_aliases={4 + n: n for n in range(len(carried))},
        compiler_params=_params("arbitrary"),
    )(parts, w, m, v, *carried)


def _adamw_small(parts, w, m, v):
    r, c = w.shape

    def body(p_ref, w_ref, m_ref, v_ref, g_ref, d_ref, nm_ref, nv_ref):
        g = p_ref[0:r, :]
        for dev in range(1, N_DEV):
            g = g + p_ref[dev * r:(dev + 1) * r, :]
        delta, nm, nv = _adamw_math(w_ref[...], g, m_ref[...], v_ref[...])
        g_ref[...] = g
        d_ref[...] = delta
        nm_ref[...] = nm
        nv_ref[...] = nv

    shape = jax.ShapeDtypeStruct((r, c), F32)
    return pl.pallas_call(body, name="adamw_small", out_shape=[shape] * 4)(parts, w, m, v)


def _my_place():
    return lax.axis_index("x"), lax.axis_index("y"), lax.axis_index("c")


def _peer(k):
    x, y, c = _my_place()
    return (x ^ (k >> 2), y ^ ((k >> 1) & 1), c ^ (k & 1))


def _block_of(place):
    x, y, c = place
    return 4 * x + 2 * y + c


def _gather_small(block):
    m_per, n = block.shape

    def body(x_ref, out_ref, send_sems, recv_sems, local_sem):
        me = _block_of(_my_place())

        def rows(b):
            return out_ref.at[pl.ds(b * m_per, m_per), :]

        mine = pltpu.make_async_copy(x_ref, rows(me), local_sem)
        mine.start()
        sends = []
        for k in range(1, N_DEV):
            cp = pltpu.make_async_remote_copy(
                src_ref=x_ref, dst_ref=rows(me), send_sem=send_sems.at[k - 1], recv_sem=recv_sems.at[k - 1],
                device_id=_peer(k), device_id_type=MESH)
            cp.start()
            sends.append(cp)
        for k in range(1, N_DEV):
            src = rows(_block_of(_peer(k)))
            pltpu.make_async_remote_copy(
                src_ref=src, dst_ref=src, send_sem=send_sems.at[k - 1], recv_sem=recv_sems.at[k - 1],
                device_id=_peer(k), device_id_type=MESH).wait_recv()
        for cp in sends:
            cp.wait_send()
        mine.wait()

    return pl.pallas_call(
        body, name="gather_small",
        out_shape=jax.ShapeDtypeStruct((N_DEV * m_per, n), block.dtype),
        in_specs=[pl.BlockSpec(memory_space=pltpu.VMEM)],
        out_specs=pl.BlockSpec(memory_space=pltpu.VMEM),
        scratch_shapes=[pltpu.SemaphoreType.DMA((N_DEV - 1,)), pltpu.SemaphoreType.DMA((N_DEV - 1,)), pltpu.SemaphoreType.DMA],
    )(block)


def _gather_weights(name, stacks, layers):
    n_t = len(stacks)

    def body(*refs):
        srcs, outs = refs[:n_t], refs[n_t:2 * n_t]
        send_sems, recv_sems, local_sems = refs[2 * n_t:]
        x, y, c = _my_place()
        me, sibling = (x, y, c), (x, y, 1 - c)
        chips = [(1 - x, y), (x, 1 - y), (1 - x, 1 - y)]

        def copy(t, k, block, to, src=None):
            slot = outs[t].at[_block_of(block)]
            return pltpu.make_async_remote_copy(
                src_ref=slot if src is None else src, dst_ref=slot,
                send_sem=send_sems.at[t, k], recv_sem=recv_sems.at[t, k], device_id=to, device_id_type=MESH)

        local, sends = [], []
        for t in range(n_t):
            src = srcs[t].at[layers[t]]
            own = pltpu.make_async_copy(src, outs[t].at[_block_of(me)], local_sems.at[t])
            own.start()
            local.append(own)
            sends.append(copy(t, 0, me, sibling, src=src))
            sends += [copy(t, 1 + j, me, (*chip, c), src=src) for j, chip in enumerate(chips)]
        for cp in sends:
            cp.start()
        for t in range(n_t):
            for j, chip in enumerate(chips):
                copy(t, 1 + j, (*chip, c), me).wait_recv()
                passed = copy(t, 4 + j, (*chip, c), sibling)
                passed.start()
                sends.append(passed)
        for t in range(n_t):
            copy(t, 0, sibling, me).wait_recv()
            for j, chip in enumerate(chips):
                copy(t, 4 + j, (*chip, 1 - c), me).wait_recv()
        for cp in sends:
            cp.wait_send()
        for cp in local:
            cp.wait()

    any_spec = pl.BlockSpec(memory_space=pl.ANY)
    return pl.pallas_call(
        body, name=name,
        out_shape=[jax.ShapeDtypeStruct((N_DEV,) + st.shape[1:], st.dtype) for st in stacks],
        in_specs=[any_spec] * n_t, out_specs=[any_spec] * n_t,
        scratch_shapes=[pltpu.SemaphoreType.DMA((n_t, N_DEV - 1)), pltpu.SemaphoreType.DMA((n_t, N_DEV - 1)), pltpu.SemaphoreType.DMA((n_t,))],
    )(*stacks)


class _ScatterSide:
    def __init__(self, grads):
        self.operands = list(grads)
        self.n = len(self.operands)

    def out_shape(self):
        return [jax.ShapeDtypeStruct(g.shape, g.dtype) for g in self.operands]

    def scratch(self):
        return [pltpu.SemaphoreType.DMA((self.n, N_DEV - 1)), pltpu.SemaphoreType.DMA((self.n, N_DEV - 1)), pltpu.SemaphoreType.DMA((self.n,))]

    def _copies(self, srcs, outs, send_sems, recv_sems, local_sems):
        me = _block_of(_my_place())
        local, sends, arrivals = [], [], []
        for t in range(self.n):
            local.append(pltpu.make_async_copy(srcs[t].at[me], outs[t].at[me], local_sems.at[t]))
            for k in range(1, N_DEV):
                sems = dict(send_sem=send_sems.at[t, k - 1], recv_sem=recv_sems.at[t, k - 1], device_id=_peer(k), device_id_type=MESH)
                sends.append(pltpu.make_async_remote_copy(src_ref=srcs[t].at[_block_of(_peer(k))], dst_ref=outs[t].at[me], **sems))
                slot = outs[t].at[_block_of(_peer(k))]
                arrivals.append(pltpu.make_async_remote_copy(src_ref=slot, dst_ref=slot, **sems))
        return local, sends, arrivals

    def start(self, *refs):
        local, sends, _ = self._copies(*refs)
        for cp in local + sends:
            cp.start()

    def finish(self, *refs):
        local, sends, arrivals = self._copies(*refs)
        for cp in arrivals:
            cp.wait_recv()
        for cp in sends:
            cp.wait_send()
        for cp in local:
            cp.wait()


class _GatherSide(_ScatterSide):
    def __init__(self, stacks, layers):
        super().__init__(stacks)
        self.layers = list(layers)

    def out_shape(self):
        return [jax.ShapeDtypeStruct((N_DEV,) + st.shape[1:], st.dtype) for st in self.operands]

    def _copies(self, srcs, outs, send_sems, recv_sems, local_sems):
        me = _block_of(_my_place())
        local, sends, arrivals = [], [], []
        for t in range(self.n):
            src = srcs[t].at[self.layers[t]]
            local.append(pltpu.make_async_copy(src, outs[t].at[me], local_sems.at[t]))
            for k in range(1, N_DEV):
                sems = dict(send_sem=send_sems.at[t, k - 1], recv_sem=recv_sems.at[t, k - 1], device_id=_peer(k), device_id_type=MESH)
                sends.append(pltpu.make_async_remote_copy(src_ref=src, dst_ref=outs[t].at[me], **sems))
                slot = outs[t].at[_block_of(_peer(k))]
                arrivals.append(pltpu.make_async_remote_copy(src_ref=slot, dst_ref=slot, **sems))
        return local, sends, arrivals


def _scatter_grads(name, grads):
    side = _ScatterSide(grads)
    n_t = side.n

    def body(*refs):
        parts = (refs[:n_t], refs[n_t:2 * n_t]) + tuple(refs[2 * n_t:])
        side.start(*parts)
        side.finish(*parts)

    any_spec = pl.BlockSpec(memory_space=pl.ANY)
    return pl.pallas_call(
        body, name=name, out_shape=side.out_shape(), in_specs=[any_spec] * n_t, out_specs=[any_spec] * n_t,
        scratch_shapes=side.scratch(),
    )(*grads)


def _host_call(body, args, side, *, grid, in_specs, out_specs, out_shape, scratch_shapes=(), **kw):
    if side is None:
        return pl.pallas_call(body, grid=grid, in_specs=in_specs, out_specs=out_specs, out_shape=out_shape,
                              scratch_shapes=list(scratch_shapes), **kw)(*args), None
    n_in, n_out, n_scr, n_side = len(in_specs), len(out_specs), len(scratch_shapes), side.n

    def hosted(*refs):
        cuts = [n_in, n_side, n_out, n_side, n_scr, 3]
        groups, at = [], 0
        for c in cuts:
            groups.append(refs[at:at + c])
            at += c
        ins, side_in, outs, side_out, scr, sems = groups
        first, last = None, None
        for axis, size in enumerate(grid):
            at_start, at_end = pl.program_id(axis) == 0, pl.program_id(axis) == size - 1
            first = at_start if first is None else jnp.logical_and(first, at_start)
            last = at_end if last is None else jnp.logical_and(last, at_end)

        @pl.when(first)
        def _():
            side.start(side_in, side_out, *sems)

        body(*ins, *outs, *scr)

        @pl.when(last)
        def _():
            side.finish(side_in, side_out, *sems)

    any_spec = pl.BlockSpec(memory_space=pl.ANY)
    res = pl.pallas_call(
        hosted, grid=grid, in_specs=list(in_specs) + [any_spec] * n_side, out_specs=list(out_specs) + [any_spec] * n_side,
        out_shape=list(out_shape) + side.out_shape(), scratch_shapes=list(scratch_shapes) + side.scratch(), **kw,
    )(*args, *side.operands)
    return res[:n_out], res[n_out:]


def _ffn_forward(x, g, wt, wd, carried):
    sides = {k: _GatherSide(*v) if v[0] else None for k, v in carried.items()}
    parts = {}
    (h, gate, up, act), parts["up"] = _ffn_up(x, g, wt, sides.get("up"))
    (x_new,), parts["down"] = _ffn_down(act, wd, x, sides.get("down"))
    return x_new, (x, g, h, gate, up, act), parts


def _ffn_backward(dx, saved, wt, wd, carried):
    x, g, h, gate, up, act = saved
    sides = {k: _ScatterSide(v) if v else None for k, v in carried.items()}
    parts = {}
    (dgate, dup), parts["act"] = _ffn_bwd_act(dx, wd, gate, up, sides.get("act"))
    d_wd = _grad_ffn_down(act, dx)
    (dx_in, dg), parts["dh"] = _ffn_dh(dgate, dup, wt, x, g, dx, sides.get("dh"))
    d_wt, parts["gu"] = _grad_ffn_gu(h, dgate, dup, sides.get("gu"))
    return dx_in, dg, d_wt.reshape(N_DEV, -1, D_MODEL), d_wd.reshape(N_DEV, -1, D_MODEL), parts


def kernel(x, p, norm_ffn1, w_ffn1_gu, w_ffn1_down, norm_mix, w_qkv, q_norm, k_norm, w_o, w_pool_in, w_pool_grp, pool_scale, norm_ffn2, w_ffn2_gu, w_ffn2_down, norm_ple, w_ple_gate, w_ple_proj, loss_target, m_norm_ffn1, m_w_ffn1_gu, m_w_ffn1_down, m_norm_mix, m_w_qkv, m_q_norm, m_k_norm, m_w_o, m_w_pool_in, m_w_pool_grp, m_pool_scale, m_norm_ffn2, m_w_ffn2_gu, m_w_ffn2_down, m_norm_ple, m_w_ple_gate, m_w_ple_proj, v_norm_ffn1, v_w_ffn1_gu, v_w_ffn1_down, v_norm_mix, v_w_qkv, v_q_norm, v_k_norm, v_w_o, v_w_pool_in, v_w_pool_grp, v_pool_scale, v_norm_ffn2, v_w_ffn2_gu, v_w_ffn2_down, v_norm_ple, v_w_ple_gate, v_w_ple_proj):
    d = D_MODEL
    xs = x[0]
    target = loss_target[0]
    me = _block_of(_my_place())

    def tr(w):
        return jnp.swapaxes(w, 1, 2)

    big = dict(w_ffn1_gu=tr(w_ffn1_gu), w_ffn1_down=w_ffn1_down, w_qkv=w_qkv, w_o=w_o, w_pool_in=w_pool_in,
               w_pool_grp=w_pool_grp.reshape(2, 4 * 32, POOL_GROUP), w_ffn2_gu=tr(w_ffn2_gu), w_ffn2_down=w_ffn2_down,
               w_ple_gate=w_ple_gate, w_ple_proj=w_ple_proj)
    moments = dict(
        w_ffn1_gu=(tr(m_w_ffn1_gu), tr(v_w_ffn1_gu)), w_ffn1_down=(m_w_ffn1_down, v_w_ffn1_down), w_qkv=(m_w_qkv, v_w_qkv),
        w_o=(m_w_o, v_w_o), w_pool_in=(m_w_pool_in, v_w_pool_in),
        w_pool_grp=(m_w_pool_grp.reshape(2, 4 * 32, POOL_GROUP), v_w_pool_grp.reshape(2, 4 * 32, POOL_GROUP)),
        w_ffn2_gu=(tr(m_w_ffn2_gu), tr(v_w_ffn2_gu)), w_ffn2_down=(m_w_ffn2_down, v_w_ffn2_down),
        w_ple_gate=(m_w_ple_gate, v_w_ple_gate), w_ple_proj=(m_w_ple_proj, v_w_ple_proj))
    half = {name: _cast_bf16(w) for name, w in big.items()}

    def layer_names(i):
        mixer = ["w_qkv", "w_o"] if i % 2 == 0 else ["w_pool_in", "w_pool_grp"]
        return ["w_ffn1_gu", "w_ffn1_down"] + mixer + ["w_ffn2_gu", "w_ffn2_down", "w_ple_gate", "w_ple_proj"]

    def layer_index(name, i):
        return i // 2 if name in ("w_qkv", "w_o", "w_pool_in", "w_pool_grp") else i

    scale_all = _gather_small(jnp.pad(pool_scale, ((0, 6), (0, 0))))
    scale_full = scale_all.reshape(N_DEV, 8, 128)[:, :2].transpose(1, 0, 2).reshape(2, d)

    saved = []
    weights = []
    cur = xs

    def fwd_carriers(i):
        if i % 2 == 0:
            return {
                ("ffn1", "up"): ["w_ffn1_gu"], ("ffn2", "up"): ["w_ffn2_gu"], ("ffn1", "down"): [], ("ffn2", "down"): [],
                ("mix", "attn"): ["w_ffn1_down", "w_ffn2_down"],
                ("ple", "ple"): ["w_ple_gate", "w_ple_proj", "w_pool_in", "w_pool_grp"],
            }
        return {
            ("ffn1", "up"): ["w_ffn1_gu"], ("ffn2", "up"): ["w_ffn2_gu"],
            ("ffn1", "down"): ["w_ffn1_down"], ("ffn2", "down"): ["w_ffn2_down"],
            ("mix", "pool"): ["w_qkv", "w_o"],
            ("ple", "ple"): ["w_ple_gate", "w_ple_proj"],
        }

    def next_weights(i, stage):
        nxt = layer_names(i + 1) if i + 1 < DEPTH else []
        out = {}
        for (st, host), names in fwd_carriers(i).items():
            if st == stage:
                take = [n for n in names if n in nxt]
                out[host] = ([half[n] for n in take], [layer_index(n, i + 1) for n in take])
        return out

    def arrived_weights(i, stage, parts, into):
        nxt = layer_names(i + 1) if i + 1 < DEPTH else []
        for (st, host), names in fwd_carriers(i).items():
            if st == stage and parts.get(host) is not None:
                into.update(zip([n for n in names if n in nxt], parts[host]))

    names = layer_names(0)
    coming = dict(zip(names, _gather_weights("gather_weights_0", [half[n] for n in names], [layer_index(n, 0) for n in names])))
    for i in range(DEPTH):
        wl, coming = coming, {}
        assert set(wl) == set(layer_names(i)), sorted(wl)
        for n in ("w_ffn1_gu", "w_ffn2_gu", "w_ffn1_down", "w_ffn2_down", "w_o", "w_pool_in", "w_ple_gate"):
            if n in wl:
                wl[n] = wl[n].reshape(-1, d)
        if "w_pool_grp" in wl:
            wl["w_pool_grp"] = wl["w_pool_grp"].reshape(N_DEV, 4, 32, POOL_GROUP).transpose(1, 0, 2, 3).reshape(4, POOL_GROUP, POOL_GROUP)
        for n in ("w_qkv", "w_ple_proj"):
            if n in wl:
                wl[n] = _from_column_shards(wl[n])
        weights.append(wl)
        j = i // 2
        rec = {}
        cur, rec["ffn1"], parts = _ffn_forward(cur, norm_ffn1[i][None], wl["w_ffn1_gu"], wl["w_ffn1_down"], next_weights(i, "ffn1"))
        arrived_weights(i, "ffn1", parts, coming)
        x1 = cur
        gm = norm_mix[i][None]
        if i % 2 == 0:
            qg = jnp.tile(q_norm[j], d // HEAD_DIM)[None]
            kg = jnp.tile(k_norm[j], d // HEAD_DIM)[None]
            hm, qkv = _norm_matmul("qkv_proj", x1, gm, wl["w_qkv"], (d, d), lambda a, b: (0, b), _pick_all, 3 * d, d, 3)
            qn, kn, vb = _qk_norm(qkv, qg, kg)
            stacks, layers = next_weights(i, "mix")["attn"]
            o, parts = _attention(qn, kn, vb, _GatherSide(stacks, layers) if stacks else None)
            arrived_weights(i, "mix", {"attn": parts}, coming)
            cur = _matmul_res("attn_out", o, wl["w_o"], x1, 1.0)
            rec["mix"] = (x1, gm, hm, qkv, qg, kg, qn, kn, vb, o)
        else:
            sc = scale_full[j][None]
            hm, u = _norm_matmul("pool_in", x1, gm, wl["w_pool_in"], (d, d), lambda a, b: (0, 0), _pick_all, d, d, 1)
            stacks, layers = next_weights(i, "mix")["pool"]
            (cur, pooled), parts = _pool_fwd(u, wl["w_pool_grp"], sc, x1, _GatherSide(stacks, layers) if stacks else None)
            arrived_weights(i, "mix", {"pool": parts}, coming)
            rec["mix"] = (x1, gm, hm, sc, pooled)
        cur, rec["ffn2"], parts = _ffn_forward(cur, norm_ffn2[i][None], wl["w_ffn2_gu"], wl["w_ffn2_down"], next_weights(i, "ffn2"))
        arrived_weights(i, "ffn2", parts, coming)
        x3 = cur
        gp = norm_ple[i][None]
        stacks, layers = next_weights(i, "ple")["ple"]
        (cur, hp, sig, proj), parts = _ple_fwd(x3, gp, wl["w_ple_gate"], p[i, 0], wl["w_ple_proj"],
                                               _GatherSide(stacks, layers) if stacks else None)
        arrived_weights(i, "ple", {"ple": parts}, coming)
        rec["ple"] = (x3, gp, hp, sig, proj)
        saved.append(rec)

    dy, loss_part = _loss_head(cur, target)
    loss = lax.psum(loss_part[0, 0], ("x", "y", "c"))

    small = {n: [None] * DEPTH for n in ("norm_ffn1", "norm_mix", "norm_ffn2", "norm_ple")}
    small.update(q_norm=[None] * 2, k_norm=[None] * 2, pool_scale=[None] * 2)
    results = {name: None for name in big}

    def update(layer, parts):
        for n, part in parts.items():
            mm, vv = moments[n]
            results[n] = _adamw_layer(part, big[n], mm, vv, layer_index(n, layer), results[n])

    carriers = {
        ("ffn2", "gu"): ["w_ffn1_gu"], ("ffn1", "gu"): ["w_ffn2_gu"],
        ("ffn2", "dh"): ["w_ffn1_down", "w_ple_gate", "w_ple_proj"],
        ("ffn1", "dh"): ["w_ffn2_down", "w_o", "w_pool_in", "w_pool_grp"],
        ("ffn2", "act"): ["w_qkv"],
    }
    above = {}
    own = {}

    def carried(ffn):
        return {host: [above[n] for n in names if n in above] for (f, host), names in carriers.items() if f == ffn}

    def received(ffn, parts, into):
        for (f, host), names in carriers.items():
            if f == ffn and parts.get(host) is not None:
                into.update(zip([n for n in names if n in above], parts[host]))

    dcur = dy
    for i in reversed(range(DEPTH)):
        wl, rec, j = weights[i], saved[i], i // 2
        grads = {}
        arrived = {}
        x3, gp, hp, sig, proj = rec["ple"]
        dgate, dproj = _ple_bwd_gate(dcur, sig, proj)
        dx3, small["norm_ple"][i] = _square_dh("ple_dh", dgate, wl["w_ple_gate"], x3, gp, dcur)
        grads["w_ple_gate"] = _grad_square("grad_ple_gate", hp, dgate).reshape(N_DEV, d // N_DEV, d)
        grads["w_ple_proj"] = _column_shards(_grad_square("grad_ple_proj", p[i, 0], dproj))
        dx2, small["norm_ffn2"][i], grads["w_ffn2_gu"], grads["w_ffn2_down"], parts = _ffn_backward(
            dx3, rec["ffn2"], wl["w_ffn2_gu"], wl["w_ffn2_down"], carried("ffn2"))
        received("ffn2", parts, arrived)
        if i % 2 == 0:
            x1, gm, hm, qkv, qg, kg, qn, kn, vb, o = rec["mix"]
            do = _matmul_nt("attn_out_bwd", dx2, wl["w_o"])
            grads["w_o"] = _grad_square("grad_attn_out", o, dx2).reshape(N_DEV, d // N_DEV, d)
            early = ["w_ple_gate", "w_ple_proj", "w_ffn2_gu", "w_ffn2_down"] if i == 0 else []
            (dq, dk, dv), parts = _attention_bwd(qn, kn, vb, do, o, _ScatterSide([grads[n] for n in early]) if early else None)
            own.update(zip(early, parts or []))
            dqkv, dqg, dkg = _qk_norm_bwd(qkv, dq, dk, dv, qg, kg)
            small["q_norm"][j], small["k_norm"][j] = dqg[0:1], dkg[0:1]
            dx1, small["norm_mix"][i] = _qkv_dh(dqkv, wl["w_qkv"], x1, gm, dx2)
            grads["w_qkv"] = _column_shards(_grad_square("grad_qkv", hm, dqkv))
        else:
            x1, gm, hm, sc, pooled = rec["mix"]
            dpool, dgrp, small["pool_scale"][j] = _pool_bwd_group(dx2, pooled, wl["w_pool_grp"], sc)
            grads["w_pool_grp"] = dgrp.reshape(4, N_DEV, 32, POOL_GROUP).transpose(1, 0, 2, 3).reshape(N_DEV, 4 * 32, POOL_GROUP)
            du = _pool_bwd_window(dpool)
            dx1, small["norm_mix"][i] = _square_dh("pool_dh", du, wl["w_pool_in"], x1, gm, dx2)
            grads["w_pool_in"] = _grad_square("grad_pool_in", hm, du).reshape(N_DEV, d // N_DEV, d)
        ffn1_carried = carried("ffn1")
        late = ["w_qkv", "w_o"] if i == 0 else []
        ffn1_carried["act"] = [grads[n] for n in late]
        dcur, small["norm_ffn1"][i], grads["w_ffn1_gu"], grads["w_ffn1_down"], parts = _ffn_backward(
            dx1, rec["ffn1"], wl["w_ffn1_gu"], wl["w_ffn1_down"], ffn1_carried)
        received("ffn1", parts, arrived)
        own.update(zip(late, parts["act"] or []))
        if above:
            assert set(arrived) == set(above), (sorted(arrived), sorted(above))
            update(i + 1, arrived)
        above = grads

    rest = [n for n in layer_names(0) if n not in own]
    own.update(zip(rest, _scatter_grads("scatter_grads_0", [above[n] for n in rest])))
    update(0, own)

    def lanes(a):
        return jnp.pad(a, ((0, 0), (0, d - a.shape[-1])))

    order = [("norm_ffn1", DEPTH), ("norm_mix", DEPTH), ("norm_ffn2", DEPTH), ("norm_ple", DEPTH), ("pool_scale", 2), ("q_norm", 2), ("k_norm", 2)]
    rows = jnp.concatenate([lanes(g) for name, _ in order for g in small[name]], axis=0)
    n_rows = rows.shape[0]
    pad_rows = -n_rows % 8
    rows = jnp.pad(rows, ((0, pad_rows), (0, 0)))
    gathered = _gather_small(rows)

    def own_lanes(a):
        return lax.dynamic_update_slice(jnp.zeros((a.shape[0], d), F32), a, (0, me * 128))

    def pack(values):
        mats = [own_lanes(values[name]) if name == "pool_scale" else lanes(values[name]) for name, _ in order]
        return jnp.pad(jnp.concatenate(mats, axis=0), ((0, pad_rows), (0, 0)))

    small_w = dict(norm_ffn1=norm_ffn1, norm_mix=norm_mix, norm_ffn2=norm_ffn2, norm_ple=norm_ple, pool_scale=pool_scale, q_norm=q_norm, k_norm=k_norm)
    small_m = dict(norm_ffn1=m_norm_ffn1, norm_mix=m_norm_mix, norm_ffn2=m_norm_ffn2, norm_ple=m_norm_ple, pool_scale=m_pool_scale, q_norm=m_q_norm, k_norm=m_k_norm)
    small_v = dict(norm_ffn1=v_norm_ffn1, norm_mix=v_norm_mix, norm_ffn2=v_norm_ffn2, norm_ple=v_norm_ple, pool_scale=v_pool_scale, q_norm=v_q_norm, k_norm=v_k_norm)
    packed = _adamw_small(gathered, pack(small_w), pack(small_m), pack(small_v))

    def unpack(mat):
        out, at = {}, 0
        for name, n in order:
            blk = mat[at:at + n]
            at += n
            if name == "pool_scale":
                out[name] = lax.dynamic_slice(blk, (0, me * 128), (n, 128))
            elif name in ("q_norm", "k_norm"):
                out[name] = blk[:, :HEAD_DIM]
            else:
                out[name] = blk
        return out

    small_out = [unpack(mat) for mat in packed]

    def result(kind, name):
        if name in small_w:
            return small_out[kind][name]
        r = results[name][kind]
        if name in ("w_ffn1_gu", "w_ffn2_gu"):
            return tr(r)
        return r.reshape(w_pool_grp.shape) if name == "w_pool_grp" else r

    weight_names = ["norm_ffn1", "w_ffn1_gu", "w_ffn1_down", "norm_mix", "w_qkv", "q_norm", "k_norm", "w_o", "w_pool_in", "w_pool_grp",
                    "pool_scale", "norm_ffn2", "w_ffn2_gu", "w_ffn2_down", "norm_ple", "w_ple_gate", "w_ple_proj"]
    outs = [loss, dcur[None]]
    for kind in range(4):
        outs += [result(kind, name) for name in weight_names]
    return tuple(outs)
```
